```python
import math
import jax
import jax.numpy as jnp
from jax import lax
import numpy as np

D_MODEL = 2048
BATCH = 4
SEQ = 2048
DEPTH = 2
DEC_BATCH = 8
DEC_SEQ = 8
PAST_LEN = 16384
PAGE_SIZE = 128

HEAD_DIM = 128
N_HEADS = D_MODEL // HEAD_DIM
H_A = N_HEADS // 2
DH_A = HEAD_DIM // 2
H_B = N_HEADS - H_A
N_IDX_HEADS = 16
IDX_DIM = 64
DSA_TOPK = 256
H_C = N_HEADS // 2
H_D = N_HEADS - H_C
MOBA_BLOCK = 256
MOBA_TOPK = 3
N_MEM = 256
H_X = 4
DH_X = HEAD_DIM
D_FF = 5632
N_BUCKETS = 32
T5_MAX_EXACT = N_BUCKETS // 2
T5_MAX_DIST = 128
Q_BLOCK = 128
MOBA_Q_BLOCK = 16
N_EVEN = (DEPTH + 1) // 2
N_ODD = DEPTH // 2
RMS_EPS = 1e-6
FORGET_BIAS_INIT = 2.0
EVEN_SPLITS = (H_A * 2 * DH_A, H_A * 2 * DH_A, H_A * HEAD_DIM, H_B * HEAD_DIM, HEAD_DIM, HEAD_DIM, N_IDX_HEADS * IDX_DIM, IDX_DIM, N_IDX_HEADS)
ODD_SPLITS = (H_C * HEAD_DIM, H_C * HEAD_DIM, H_C * HEAD_DIM, H_C, H_D * HEAD_DIM, H_D * HEAD_DIM, H_D * HEAD_DIM)
E_IN = sum(EVEN_SPLITS)
O_IN = sum(ODD_SPLITS)
NG_FFN1_PRE = 0
NG_FFN1_POST = 1
NG_MIX_PRE = 2
NG_MIX_POST = 3
NG_X_PRE = 4
NG_X_POST = 5
NG_FFN2_PRE = 6
NG_FFN2_POST = 7
NG_MEM = 8
N_NORMS = 9
F32 = jnp.float32

kernel_name = 'hybrid_diff_dsa_fox_moba_decoder_step'


def rms_norm(x, g):
    x32 = x.astype(F32)
    y = x32 * lax.rsqrt(jnp.mean(x32 * x32, axis=-1, keepdims=True) + RMS_EPS)
    return (y * g.astype(F32)).astype(x.dtype)


def swiglu(x, w_gate, w_up, w_down):
    return (jax.nn.silu(x @ w_gate) * (x @ w_up)) @ w_down


def split_cols(a, sizes):
    cuts = [int(c) for c in np.cumsum(sizes)[:-1]]
    return jnp.split(a, cuts, axis=-1)


def t5_bucket(dist):
    n = jnp.maximum(dist, 0)
    n_f = jnp.maximum(n, 1).astype(F32)
    large = T5_MAX_EXACT + (jnp.log(n_f / T5_MAX_EXACT) / math.log(T5_MAX_DIST / T5_MAX_EXACT) * (N_BUCKETS - T5_MAX_EXACT)).astype(jnp.int32)
    return jnp.where(n < T5_MAX_EXACT, n, jnp.minimum(large, N_BUCKETS - 1))


def gather_pages(pool, layer, page_table):
    pages = pool[layer, page_table]
    return pages.reshape((page_table.shape[0], page_table.shape[1] * pool.shape[2]) + pool.shape[3:])


def sweep_query_blocks(fn, qs, q_pos, blk):
    n_q = q_pos.shape[0]
    if n_q <= blk or n_q % blk:
        return fn(*qs, q_pos)
    n_blk = n_q // blk

    def to_blocks(a):
        return jnp.swapaxes(a.reshape((a.shape[0], n_blk, blk) + a.shape[2:]), 0, 1)

    xs = tuple(to_blocks(a) for a in qs) + (q_pos.reshape(n_blk, blk),)
    out = lax.map(lambda args: fn(*args), xs)
    out = jnp.swapaxes(out, 0, 1)
    return out.reshape((out.shape[0], n_q) + out.shape[3:])


def diff_attn_block(q, q_pos, k, v, lam, bias_tab):
    k_pos = jnp.arange(k.shape[1])
    s = jnp.einsum('bqhcd,bkhcd->bhcqk', q, k, preferred_element_type=F32) * DH_A ** -0.5
    bias = bias_tab[t5_bucket(q_pos[:, None] - k_pos[None, :])]
    s = s + jnp.transpose(bias, (2, 0, 1))[None, :, None].astype(F32)
    s = jnp.where(k_pos[None, :] <= q_pos[:, None], s, -jnp.inf)
    p = jax.nn.softmax(s, axis=-1)
    a = p[:, :, 0] - lam * p[:, :, 1]
    return jnp.einsum('bhqk,bkhd->bqhd', a.astype(v.dtype), v)


def dsa_block(q, iq, iw, q_pos, k, v, k_idx, n_sel, bias_tab):
    k_pos = jnp.arange(k.shape[1])
    rel = jax.nn.relu(jnp.einsum('bqnd,bkd->bqnk', iq, k_idx, preferred_element_type=F32) * IDX_DIM ** -0.5)
    score = jnp.einsum('bqn,bqnk->bqk', iw.astype(F32) * N_IDX_HEADS ** -0.5, rel)
    score = jnp.where(k_pos[None, None, :] <= q_pos[None, :, None], score, -jnp.inf)
    _, sel = lax.top_k(score, n_sel)
    take = jax.vmap(lambda a, i: a[i])
    k_sel = take(k, sel)
    v_sel = take(v, sel)
    s = jnp.einsum('bqhd,bqnd->bqhn', q, k_sel, preferred_element_type=F32) * HEAD_DIM ** -0.5
    s = s + jnp.swapaxes(bias_tab[t5_bucket(q_pos[None, :, None] - sel)], -1, -2).astype(F32)
    valid = sel <= q_pos[None, :, None]
    s = jnp.where(valid[:, :, None, :], s, -jnp.inf)
    p = jax.nn.softmax(s, axis=-1)
    return jnp.einsum('bqhn,bqnd->bqhd', p.astype(v_sel.dtype), v_sel)


def fox_block(q, cum_q, q_pos, k, v, cum_k):
    k_pos = jnp.arange(k.shape[1])
    s = jnp.einsum('bqhd,bkhd->bhqk', q, k, preferred_element_type=F32) * HEAD_DIM ** -0.5
    s = s + jnp.swapaxes(cum_q, 1, 2)[..., None] - jnp.swapaxes(cum_k, 1, 2)[:, :, None, :]
    s = jnp.where(k_pos[None, :] <= q_pos[:, None], s, -jnp.inf)
    p = jax.nn.softmax(s, axis=-1)
    return jnp.einsum('bhqk,bkhd->bqhd', p.astype(v.dtype), v)


def moba_block(q, q_pos, k_t, v_t, k_mean, bias_tab):
    b_sz, n_q, n_h, _ = q.shape
    n_keys = k_t.shape[2]
    n_full = k_mean.shape[1]
    own = q_pos // MOBA_BLOCK
    own_b = jnp.broadcast_to(own[None, :, None, None], (b_sz, n_q, n_h, 1))
    n_top = min(MOBA_TOPK, n_full)
    if n_top > 0:
        gate = jnp.einsum('bqhd,bnhd->bqhn', q.astype(F32), k_mean)
        fully_past = jnp.arange(n_full)[None, None, None, :] < own[None, :, None, None]
        gate = jnp.where(fully_past, gate, -jnp.inf)
        _, chosen = lax.top_k(gate, n_top)
        blocks = jnp.concatenate([chosen, own_b], axis=-1)
        block_ok = jnp.concatenate([chosen < own_b, jnp.ones_like(own_b, dtype=bool)], axis=-1)
    else:
        blocks = own_b
        block_ok = jnp.ones_like(own_b, dtype=bool)
    pos = blocks[..., None] * MOBA_BLOCK + jnp.arange(MOBA_BLOCK)
    ok = block_ok[..., None] & (pos <= q_pos[None, :, None, None, None])
    pos = pos.reshape(b_sz, n_q, n_h, -1)
    ok = ok.reshape(b_sz, n_q, n_h, -1)
    pos_c = jnp.minimum(pos, n_keys - 1)
    b_ix = jnp.arange(b_sz)[:, None, None, None]
    h_ix = jnp.arange(n_h)[None, None, :, None]
    k_sel = k_t[b_ix, h_ix, pos_c]
    v_sel = v_t[b_ix, h_ix, pos_c]
    s = jnp.einsum('bqhd,bqhkd->bqhk', q, k_sel, preferred_element_type=F32) * HEAD_DIM ** -0.5
    s = s + bias_tab[t5_bucket(q_pos[None, :, None, None] - pos), h_ix].astype(F32)
    s = jnp.where(ok, s, -jnp.inf)
    p = jax.nn.softmax(s, axis=-1)
    return jnp.einsum('bqhk,bqhkd->bqhd', p.astype(v_sel.dtype), v_sel)


def join_past(past, new_rows):
    if past is None:
        return new_rows
    return tuple(jnp.concatenate([p, n], axis=1) for p, n in zip(past, new_rows))


def mixer_even(h, past, w_in, w_out, lam_vec, g_subln, lam_init, t5_table):
    b_sz, n_t, _ = h.shape
    qa, ka, va, qb, kb, vb, iq, ik, iw = split_cols(h @ w_in, EVEN_SPLITS)
    qa = qa.reshape(b_sz, n_t, H_A, 2, DH_A)
    ka = ka.reshape(b_sz, n_t, H_A, 2, DH_A)
    va = va.reshape(b_sz, n_t, H_A, HEAD_DIM)
    qb = qb.reshape(b_sz, n_t, H_B, HEAD_DIM)
    iq = iq.reshape(b_sz, n_t, N_IDX_HEADS, IDX_DIM)
    new_rows = (ka, va, kb, vb, ik)
    k_a, v_a, k_b, v_b, k_idx = join_past(past, new_rows)
    n_keys = k_a.shape[1]
    q_pos = (n_keys - n_t) + jnp.arange(n_t)
    lv = lam_vec.astype(F32)
    lam = jnp.exp(jnp.sum(lv[0] * lv[1])) - jnp.exp(jnp.sum(lv[2] * lv[3])) + lam_init
    tab_a = t5_table[:, :H_A]
    tab_b = t5_table[:, H_A:]
    o_a = sweep_query_blocks(lambda q, qp: diff_attn_block(q, qp, k_a, v_a, lam, tab_a), (qa,), q_pos, Q_BLOCK)
    o_a = rms_norm(o_a, g_subln) * (1.0 - lam_init)
    n_sel = min(DSA_TOPK, n_keys // 4)
    o_b = sweep_query_blocks(lambda q, qi, wi, qp: dsa_block(q, qi, wi, qp, k_b, v_b, k_idx, n_sel, tab_b), (qb, iq, iw), q_pos, Q_BLOCK)
    o = jnp.concatenate([o_a.reshape(b_sz, n_t, -1), o_b.reshape(b_sz, n_t, -1).astype(o_a.dtype)], axis=-1)
    return o @ w_out, new_rows


def mixer_odd(h, past, w_in, w_out, b_forget, t5_table):
    b_sz, n_t, _ = h.shape
    qc, kc, vc, fc, qd, kd, vd = split_cols(h @ w_in, ODD_SPLITS)
    qc, kc, vc = (a.reshape(b_sz, n_t, H_C, HEAD_DIM) for a in (qc, kc, vc))
    qd, kd, vd = (a.reshape(b_sz, n_t, H_D, HEAD_DIM) for a in (qd, kd, vd))
    log_f = jax.nn.log_sigmoid(fc.astype(F32) + b_forget.astype(F32))
    new_rows = (kc, vc, log_f, kd, vd)
    k_c, v_c, log_f_all, k_d, v_d = join_past(past, new_rows)
    n_keys = k_c.shape[1]
    q_pos = (n_keys - n_t) + jnp.arange(n_t)
    cum = jnp.cumsum(log_f_all.astype(F32), axis=1)
    o_c = sweep_query_blocks(lambda q, cq, qp: fox_block(q, cq, qp, k_c, v_c, cum), (qc, cum[:, n_keys - n_t:]), q_pos, Q_BLOCK)
    n_full = n_keys // MOBA_BLOCK
    k_mean = k_d[:, :n_full * MOBA_BLOCK].astype(F32).reshape(b_sz, n_full, MOBA_BLOCK, H_D, HEAD_DIM).mean(axis=2)
    k_d_t = jnp.swapaxes(k_d, 1, 2)
    v_d_t = jnp.swapaxes(v_d, 1, 2)
    tab_d = t5_table[:, H_C:]
    o_d = sweep_query_blocks(lambda q, qp: moba_block(q, qp, k_d_t, v_d_t, k_mean, tab_d), (qd,), q_pos, MOBA_Q_BLOCK)
    o = jnp.concatenate([o_c.reshape(b_sz, n_t, -1), o_d.reshape(b_sz, n_t, -1).astype(o_c.dtype)], axis=-1)
    return o @ w_out, new_rows


def memory_kv(mem, g_mem, w_k, w_v):
    m = rms_norm(mem, g_mem)
    b_sz = mem.shape[0]
    return (m @ w_k).reshape(b_sz, N_MEM, H_X, DH_X), (m @ w_v).reshape(b_sz, N_MEM, H_X, DH_X)


def cross_attend(h, mem_k, mem_v, w_q, w_o):
    b_sz, n_t, _ = h.shape
    q = (h @ w_q).reshape(b_sz, n_t, H_X, DH_X)
    s = jnp.einsum('bqhd,bmhd->bhqm', q, mem_k, preferred_element_type=F32) * DH_X ** -0.5
    p = jax.nn.softmax(s, axis=-1)
    o = jnp.einsum('bhqm,bmhd->bqhd', p.astype(mem_v.dtype), mem_v)
    return o.reshape(b_sz, n_t, -1) @ w_o


def run_trunk(x, past_even, past_odd, mem_kv, prm):
    rows_even, rows_odd = [], []
    for layer in range(DEPTH):
        g = prm['norm_g'][layer]
        wg, wu, wd = prm['w_ffn_gate'][layer], prm['w_ffn_up'][layer], prm['w_ffn_down'][layer]
        x = x + 0.5 * rms_norm(swiglu(rms_norm(x, g[NG_FFN1_PRE]), wg[0], wu[0], wd[0]), g[NG_FFN1_POST])
        h = rms_norm(x, g[NG_MIX_PRE])
        if layer % 2 == 0:
            e = layer // 2
            lam_init = 0.8 - 0.6 * math.exp(-0.3 * layer)
            m, rows = mixer_even(h, None if past_even is None else past_even[e], prm['w_in_even'][e], prm['w_out_even'][e], prm['diff_lambda'][e], prm['g_subln'][e], lam_init, prm['t5_table'])
            rows_even.append(rows)
        else:
            o = layer // 2
            m, rows = mixer_odd(h, None if past_odd is None else past_odd[o], prm['w_in_odd'][o], prm['w_out_odd'][o], prm['b_forget'][o], prm['t5_table'])
            rows_odd.append(rows)
        x = x + rms_norm(m, g[NG_MIX_POST])
        mk, mv = mem_kv[layer]
        x = x + rms_norm(cross_attend(rms_norm(x, g[NG_X_PRE]), mk, mv, prm['w_xq'][layer], prm['w_xo'][layer]), g[NG_X_POST])
        x = x + 0.5 * rms_norm(swiglu(rms_norm(x, g[NG_FFN2_PRE]), wg[1], wu[1], wd[1]), g[NG_FFN2_POST])
    return x, rows_even, rows_odd


def setup_inputs(seed: int = 0) -> dict:
    key = jax.random.key(seed)
    keys = iter(jax.random.split(key, 48))

    def nrm(shape, scale):
        return jax.random.normal(next(keys), shape, F32) * scale

    n_pages = PAST_LEN // PAGE_SIZE
    n_used = DEC_BATCH * n_pages
    n_pool = (5 * n_used + 3) // 4
    mix_w = N_HEADS * HEAD_DIM
    x_prompt = nrm((BATCH, SEQ, D_MODEL), 1.0)
    x_sample = nrm((DEC_BATCH, DEC_SEQ, D_MODEL), 1.0)
    cache_a_k = nrm((N_EVEN, n_pool, PAGE_SIZE, H_A, 2, DH_A), 1.0)
    cache_a_v = nrm((N_EVEN, n_pool, PAGE_SIZE, H_A, HEAD_DIM), 1.0)
    cache_b_k = nrm((N_EVEN, n_pool, PAGE_SIZE, HEAD_DIM), 1.0)
    cache_b_v = nrm((N_EVEN, n_pool, PAGE_SIZE, HEAD_DIM), 1.0)
    cache_b_idx = nrm((N_EVEN, n_pool, PAGE_SIZE, IDX_DIM), 1.0)
    cache_c_k = nrm((N_ODD, n_pool, PAGE_SIZE, H_C, HEAD_DIM), 1.0)
    cache_c_v = nrm((N_ODD, n_pool, PAGE_SIZE, H_C, HEAD_DIM), 1.0)
    cache_c_logf = jax.nn.log_sigmoid(FORGET_BIAS_INIT + nrm((N_ODD, n_pool, PAGE_SIZE, H_C), 1.0))
    cache_d_k = nrm((N_ODD, n_pool, PAGE_SIZE, H_D, HEAD_DIM), 1.0)
    cache_d_v = nrm((N_ODD, n_pool, PAGE_SIZE, H_D, HEAD_DIM), 1.0)
    cache_mem_k = nrm((DEPTH, DEC_BATCH, N_MEM, H_X, DH_X), 1.0)
    cache_mem_v = nrm((DEPTH, DEC_BATCH, N_MEM, H_X, DH_X), 1.0)
    page_table = jax.random.permutation(next(keys), n_pool)[:n_used].reshape(DEC_BATCH, n_pages).astype(jnp.int32)
    mem_prompt = nrm((BATCH, N_MEM, D_MODEL), 1.0)
    t5_table = nrm((N_BUCKETS, N_HEADS), 0.3)
    norm_g = 1.0 + nrm((DEPTH, N_NORMS, D_MODEL), 0.05)
    w_ffn_gate = nrm((DEPTH, 2, D_MODEL, D_FF), D_MODEL ** -0.5)
    w_ffn_up = nrm((DEPTH, 2, D_MODEL, D_FF), D_MODEL ** -0.5)
    w_ffn_down = nrm((DEPTH, 2, D_FF, D_MODEL), D_FF ** -0.5)
    w_xq = nrm((DEPTH, D_MODEL, H_X * DH_X), D_MODEL ** -0.5)
    w_xk = nrm((DEPTH, D_MODEL, H_X * DH_X), D_MODEL ** -0.5)
    w_xv = nrm((DEPTH, D_MODEL, H_X * DH_X), D_MODEL ** -0.5)
    w_xo = nrm((DEPTH, H_X * DH_X, D_MODEL), (H_X * DH_X) ** -0.5)
    w_in_even = nrm((N_EVEN, D_MODEL, E_IN), D_MODEL ** -0.5)
    w_out_even = nrm((N_EVEN, mix_w, D_MODEL), mix_w ** -0.5)
    diff_lambda = nrm((N_EVEN, 4, DH_A), 0.1)
    g_subln = 1.0 + nrm((N_EVEN, HEAD_DIM), 0.05)
    w_in_odd = nrm((N_ODD, D_MODEL, O_IN), D_MODEL ** -0.5)
    w_out_odd = nrm((N_ODD, mix_w, D_MODEL), mix_w ** -0.5)
    b_forget = FORGET_BIAS_INIT + nrm((N_ODD, H_C), 0.1)
    return {'x_prompt': x_prompt, 'x_sample': x_sample,
            'cache_a_k': cache_a_k, 'cache_a_v': cache_a_v, 'cache_b_k': cache_b_k, 'cache_b_v': cache_b_v,
            'cache_b_idx': cache_b_idx, 'cache_c_k': cache_c_k, 'cache_c_v': cache_c_v, 'cache_c_logf': cache_c_logf,
            'cache_d_k': cache_d_k, 'cache_d_v': cache_d_v, 'cache_mem_k': cache_mem_k, 'cache_mem_v': cache_mem_v,
            'page_table': page_table, 'mem_prompt': mem_prompt, 't5_table': t5_table, 'norm_g': norm_g,
            'w_ffn_gate': w_ffn_gate, 'w_ffn_up': w_ffn_up, 'w_ffn_down': w_ffn_down,
            'w_xq': w_xq, 'w_xk': w_xk, 'w_xv': w_xv, 'w_xo': w_xo,
            'w_in_even': w_in_even, 'w_out_even': w_out_even, 'diff_lambda': diff_lambda, 'g_subln': g_subln,
            'w_in_odd': w_in_odd, 'w_out_odd': w_out_odd, 'b_forget': b_forget}


def reference(x_prompt, x_sample, cache_a_k, cache_a_v, cache_b_k, cache_b_v, cache_b_idx, cache_c_k, cache_c_v, cache_c_logf, cache_d_k, cache_d_v, cache_mem_k, cache_mem_v, page_table, mem_prompt, t5_table, norm_g, w_ffn_gate, w_ffn_up, w_ffn_down, w_xq, w_xk, w_xv, w_xo, w_in_even, w_out_even, diff_lambda, g_subln, w_in_odd, w_out_odd, b_forget):
    prm = {'t5_table': t5_table, 'norm_g': norm_g, 'w_ffn_gate': w_ffn_gate, 'w_ffn_up': w_ffn_up,
           'w_ffn_down': w_ffn_down, 'w_xq': w_xq, 'w_xo': w_xo, 'w_in_even': w_in_even,
           'w_out_even': w_out_even, 'diff_lambda': diff_lambda, 'g_subln': g_subln,
           'w_in_odd': w_in_odd, 'w_out_odd': w_out_odd, 'b_forget': b_forget}
    mem_kv_p = [memory_kv(mem_prompt, norm_g[l, NG_MEM], w_xk[l], w_xv[l]) for l in range(DEPTH)]
    y_prompt, ev_p, od_p = run_trunk(x_prompt, None, None, mem_kv_p, prm)
    past_even = [tuple(gather_pages(c, e, page_table) for c in (cache_a_k, cache_a_v, cache_b_k, cache_b_v, cache_b_idx)) for e in range(N_EVEN)]
    past_odd = [tuple(gather_pages(c, o, page_table) for c in (cache_c_k, cache_c_v, cache_c_logf, cache_d_k, cache_d_v)) for o in range(N_ODD)]
    mem_kv_s = [(cache_mem_k[l], cache_mem_v[l]) for l in range(DEPTH)]
    y_sample, ev_s, od_s = run_trunk(x_sample, past_even, past_odd, mem_kv_s, prm)

    def stack(rows, i):
        return jnp.stack([r[i] for r in rows])

    a_k_p, a_v_p, b_k_p, b_v_p, b_idx_p = (stack(ev_p, i) for i in range(5))
    c_k_p, c_v_p, c_logf_p, d_k_p, d_v_p = (stack(od_p, i) for i in range(5))
    mem_k_p = jnp.stack([kv[0] for kv in mem_kv_p])
    mem_v_p = jnp.stack([kv[1] for kv in mem_kv_p])
    a_k_s, a_v_s, b_k_s, b_v_s, b_idx_s = (stack(ev_s, i) for i in range(5))
    c_k_s, c_v_s, c_logf_s, d_k_s, d_v_s = (stack(od_s, i) for i in range(5))
    return (y_prompt, y_sample, a_k_p, a_v_p, b_k_p, b_v_p, b_idx_p, c_k_p, c_v_p, c_logf_p, d_k_p, d_v_p, mem_k_p, mem_v_p, a_k_s, a_v_s, b_k_s, b_v_s, b_idx_s, c_k_s, c_v_s, c_logf_s, d_k_s, d_v_s)
```

```python
import functools
import math

import numpy as np
import jax
import jax.numpy as jnp
from jax import lax
from jax.experimental import pallas as pl
from jax.experimental.pallas import tpu as pltpu

F32 = jnp.float32
BF16 = jnp.bfloat16
NEG_INF = float("-inf")

HEAD_DIM = 128
DH_A = HEAD_DIM // 2
N_IDX_HEADS = 16
IDX_DIM = 64
DSA_TOPK = 256
MOBA_BLOCK = 256
MOBA_TOPK = 3
N_BUCKETS = 32
T5_MAX_EXACT = N_BUCKETS // 2
T5_MAX_DIST = 128
RMS_EPS = 1e-6
NG_FFN1_PRE, NG_FFN1_POST, NG_MIX_PRE, NG_MIX_POST = 0, 1, 2, 3
NG_X_PRE, NG_X_POST, NG_FFN2_PRE, NG_FFN2_POST, NG_MEM = 4, 5, 6, 7, 8

LANES = 128
ROW_TILE = 512
COL_TILE = 512
VMEM_LIMIT = 56 * 1024 * 1024

NT_DIMS = (((1,), (1,)), ((), ()))


def _params(*sem):
    return pltpu.CompilerParams(dimension_semantics=sem, vmem_limit_bytes=VMEM_LIMIT)


def _rms(x, g):
    return x * lax.rsqrt(jnp.mean(x * x, axis=-1, keepdims=True) + RMS_EPS) * g


def _round_up(n, m):
    return (n + m - 1) // m * m


def _bucket_np(dist):
    n = np.maximum(dist, 0)
    n_f = np.maximum(n, 1).astype(np.float32)
    large = T5_MAX_EXACT + (np.log(n_f / np.float32(T5_MAX_EXACT)) / np.float32(math.log(T5_MAX_DIST / T5_MAX_EXACT))
                            * np.float32(N_BUCKETS - T5_MAX_EXACT)).astype(np.int32)
    return np.where(n < T5_MAX_EXACT, n, np.minimum(large, N_BUCKETS - 1)).astype(np.int32)


def _bias_tile(tab, dist, valid):
    t = tab[_bucket_np(dist)]
    t = jnp.where(jnp.asarray(valid)[..., None], t, NEG_INF)
    return jnp.transpose(t, (2, 0, 1)).astype(F32)


def _prompt_bias(tab, blk):
    assert blk + 1 >= T5_MAX_DIST
    r = np.arange(blk)[:, None]
    c = np.arange(blk)[None, :]
    tiles = [_bias_tile(tab, r - c, r >= c),
             _bias_tile(tab, blk + r - c, np.ones((blk, blk), bool)),
             _bias_tile(tab, np.full((blk, blk), 2 * blk), np.ones((blk, blk), bool))]
    return jnp.stack(tiles, axis=1)


def _sample_bias(tab, n_new, page, rep):
    q = np.arange(n_new)[:, None]
    c = np.arange(page)[None, :]
    ones = np.ones((n_new, page), bool)
    tiles = [_bias_tile(tab, np.full((n_new, page), T5_MAX_DIST + page), ones),
             _bias_tile(tab, page + q - c, ones),
             _bias_tile(tab, q - c, (c <= q) & (c < n_new))]
    t = jnp.stack(tiles)
    n_h = t.shape[1]
    t = jnp.broadcast_to(t[:, :, None], (3, n_h, rep, n_new, page))
    return t.reshape(3, n_h * rep * n_new, page)


def _rms_matmul_body(x_ref, g_ref, w_ref, *rest, emit32, emit16):
    outs, h_scr = rest[:-1], rest[-1]

    @pl.when(pl.program_id(1) == 0)
    def _():
        h_scr[...] = _rms(x_ref[...], g_ref[...]).astype(BF16)

    y = jnp.dot(h_scr[...], w_ref[...], preferred_element_type=F32)
    k = 0
    if emit32:
        outs[k][...] = y
        k += 1
    if emit16:
        outs[k][...] = y.astype(BF16)


def _rms_matmul(x, g, w16, *, emit32=True, emit16=False):
    m, d = x.shape
    n = w16.shape[1]
    tm = min(m, ROW_TILE)
    tn = min(n, COL_TILE)
    assert m % tm == 0 and n % tn == 0
    out_shape, out_specs = [], []
    for flag, dt in ((emit32, F32), (emit16, BF16)):
        if flag:
            out_shape.append(jax.ShapeDtypeStruct((m, n), dt))
            out_specs.append(pl.BlockSpec((tm, tn), lambda i, j: (i, j)))
    res = pl.pallas_call(
        functools.partial(_rms_matmul_body, emit32=emit32, emit16=emit16),
        grid=(m // tm, n // tn),
        in_specs=[pl.BlockSpec((tm, d), lambda i, j: (i, 0)),
                  pl.BlockSpec((1, d), lambda i, j: (0, 0)),
                  pl.BlockSpec((d, tn), lambda i, j: (0, j))],
        out_specs=out_specs,
        out_shape=out_shape,
        scratch_shapes=[pltpu.VMEM((tm, d), BF16)],
        compiler_params=_params("parallel", "arbitrary"),
        name="rms_matmul",
    )(x, g.reshape(1, d), w16)
    return res if len(res) > 1 else res[0]


def _ffn_body(x_ref, gpre_ref, gpost_ref, wg_ref, wu_ref, wd_ref, o_ref, h_scr, acc_scr):
    j = pl.program_id(1)

    @pl.when(j == 0)
    def _():
        h_scr[...] = _rms(x_ref[...], gpre_ref[...]).astype(BF16)
        acc_scr[...] = jnp.zeros_like(acc_scr)

    h = h_scr[...]
    gate = jnp.dot(h, wg_ref[...], preferred_element_type=F32)
    up = jnp.dot(h, wu_ref[...], preferred_element_type=F32)
    act = (gate * jax.nn.sigmoid(gate) * up).astype(BF16)
    acc_scr[...] += jnp.dot(act, wd_ref[...], preferred_element_type=F32)

    @pl.when(j == pl.num_programs(1) - 1)
    def _():
        o_ref[...] = x_ref[...] + 0.5 * _rms(acc_scr[...], gpost_ref[...])


def _ffn(x, g_pre, g_post, wg16, wu16, wd16):
    m, d = x.shape
    ff = wg16.shape[1]
    tm = min(m, ROW_TILE)
    tf = min(ff, COL_TILE)
    assert m % tm == 0 and ff % tf == 0
    return pl.pallas_call(
        _ffn_body,
        grid=(m // tm, ff // tf),
        in_specs=[pl.BlockSpec((tm, d), lambda i, j: (i, 0)),
                  pl.BlockSpec((1, d), lambda i, j: (0, 0)),
                  pl.BlockSpec((1, d), lambda i, j: (0, 0)),
                  pl.BlockSpec((d, tf), lambda i, j: (0, j)),
                  pl.BlockSpec((d, tf), lambda i, j: (0, j)),
                  pl.BlockSpec((tf, d), lambda i, j: (j, 0))],
        out_specs=pl.BlockSpec((tm, d), lambda i, j: (i, 0)),
        out_shape=jax.ShapeDtypeStruct((m, d), F32),
        scratch_shapes=[pltpu.VMEM((tm, d), BF16), pltpu.VMEM((tm, d), F32)],
        compiler_params=_params("parallel", "arbitrary"),
        name="ffn",
    )(x, g_pre.reshape(1, d), g_post.reshape(1, d), wg16, wu16, wd16)


def _out_body(*refs, n_parts):
    x_ref, g_ref = refs[0], refs[1]
    o_refs = refs[2:2 + n_parts]
    w_refs = refs[2 + n_parts:2 + 2 * n_parts]
    out_ref = refs[-1]
    y = None
    for o_ref, w_ref in zip(o_refs, w_refs):
        t = jnp.dot(o_ref[...].astype(BF16), w_ref[...], preferred_element_type=F32)
        y = t if y is None else y + t
    out_ref[...] = x_ref[...] + _rms(y, g_ref[...])


def _out_proj(x, g, parts, w16):
    m, d = x.shape
    tm = min(m, ROW_TILE)
    assert m % tm == 0
    ws, off = [], 0
    for p in parts:
        ws.append(w16[off:off + p.shape[1]])
        off += p.shape[1]
    assert off == w16.shape[0]
    in_specs = [pl.BlockSpec((tm, d), lambda i: (i, 0)), pl.BlockSpec((1, d), lambda i: (0, 0))]
    in_specs += [pl.BlockSpec((tm, p.shape[1]), lambda i: (i, 0)) for p in parts]
    in_specs += [pl.BlockSpec(w.shape, lambda i: (0, 0)) for w in ws]
    return pl.pallas_call(
        functools.partial(_out_body, n_parts=len(parts)),
        grid=(m // tm,),
        in_specs=in_specs,
        out_specs=pl.BlockSpec((tm, d), lambda i: (i, 0)),
        out_shape=jax.ShapeDtypeStruct((m, d), F32),
        compiler_params=_params("parallel"),
        name="out_proj",
    )(x, g.reshape(1, d), *parts, *ws)


def _cross_body(q_ref, k_ref, v_ref, o_ref, *, n_heads):
    q = q_ref[...].astype(BF16)
    k = k_ref[...].astype(BF16)
    v = v_ref[...].astype(BF16)
    for h in range(n_heads):
        sl = slice(h * HEAD_DIM, (h + 1) * HEAD_DIM)
        s = lax.dot_general(q[:, sl], k[:, sl], NT_DIMS, preferred_element_type=F32) * HEAD_DIM ** -0.5
        p = jnp.exp(s - jnp.max(s, axis=-1, keepdims=True))
        l = jnp.sum(p, axis=-1, keepdims=True)
        o = jnp.dot(p.astype(BF16), v[:, sl], preferred_element_type=F32) / l
        o_ref[:, sl] = o.astype(o_ref.dtype)


def _cross_attend(q, mem_k, mem_v):
    b, t, w = q.shape
    n_mem = mem_k.shape[1]
    tq = min(t, ROW_TILE)
    assert t % tq == 0
    return pl.pallas_call(
        functools.partial(_cross_body, n_heads=w // HEAD_DIM),
        grid=(b, t // tq),
        in_specs=[pl.BlockSpec((None, tq, w), lambda bi, i: (bi, i, 0)),
                  pl.BlockSpec((None, n_mem, w), lambda bi, i: (bi, 0, 0)),
                  pl.BlockSpec((None, n_mem, w), lambda bi, i: (bi, 0, 0))],
        out_specs=pl.BlockSpec((None, tq, w), lambda bi, i: (bi, i, 0)),
        out_shape=jax.ShapeDtypeStruct((b, t, w), F32),
        compiler_params=_params("parallel", "parallel"),
        name="cross_attend",
    )(q, mem_k, mem_v)


def _softmax_step(s, v16, m, l, acc):
    m_new = jnp.maximum(m, jnp.max(s, axis=-1, keepdims=True))
    m_safe = jnp.where(m_new == NEG_INF, 0.0, m_new)
    p = jnp.exp(s - m_safe)
    alpha = jnp.exp(m - m_safe)
    l = alpha * l + jnp.sum(p, axis=-1, keepdims=True)
    acc = alpha * acc + jnp.dot(p.astype(BF16), v16, preferred_element_type=F32)
    return m_new, l, acc


def _diff_lambda(lam_ref, lam_init):
    lv = lam_ref[...]
    return (jnp.exp(jnp.sum(lv[0:1] * lv[1:2], axis=-1, keepdims=True))
            - jnp.exp(jnp.sum(lv[2:3] * lv[3:4], axis=-1, keepdims=True)) + lam_init)


def _diff_prompt_body(q_ref, k_ref, v_ref, bias_ref, lam_ref, g_ref, o_ref, *, blk, lam_init):
    i = pl.program_id(2)
    q = q_ref[...]
    q0, q1 = q[:, :DH_A], q[:, DH_A:]
    scale = DH_A ** -0.5

    def body(j, carry):
        m0, l0, a0, m1, l1, a1 = carry
        rows = pl.ds(pl.multiple_of(j * blk, blk), blk)
        ks = k_ref[rows, :]
        vs = v_ref[rows, :]
        bt = bias_ref[jnp.minimum(i - j, 2)]
        s0 = lax.dot_general(q0, ks[:, :DH_A], NT_DIMS, preferred_element_type=F32) * scale + bt
        s1 = lax.dot_general(q1, ks[:, DH_A:], NT_DIMS, preferred_element_type=F32) * scale + bt
        m0, l0, a0 = _softmax_step(s0, vs, m0, l0, a0)
        m1, l1, a1 = _softmax_step(s1, vs, m1, l1, a1)
        return m0, l0, a0, m1, l1, a1

    neg = jnp.full((blk, 1), NEG_INF, F32)
    zero = jnp.zeros((blk, 1), F32)
    zacc = jnp.zeros((blk, HEAD_DIM), F32)
    _, l0, a0, _, l1, a1 = lax.fori_loop(0, i + 1, body, (neg, zero, zacc, neg, zero, zacc))
    lam = _diff_lambda(lam_ref, lam_init)
    o = a0 / l0 - lam * (a1 / l1)
    o_ref[...] = (_rms(o, g_ref[...]) * (1.0 - lam_init)).astype(o_ref.dtype)


def _diff_prompt(p16, col_q, col_k, col_v, n_heads, bias, lam_vec, g_subln, lam_init, blk):
    b, t, _ = p16.shape
    cq, ck, cv = col_q // HEAD_DIM, col_k // HEAD_DIM, col_v // HEAD_DIM
    return pl.pallas_call(
        functools.partial(_diff_prompt_body, blk=blk, lam_init=lam_init),
        grid=(b, n_heads, t // blk),
        in_specs=[pl.BlockSpec((None, blk, HEAD_DIM), lambda bi, h, i: (bi, i, cq + h)),
                  pl.BlockSpec((None, t, HEAD_DIM), lambda bi, h, i: (bi, 0, ck + h)),
                  pl.BlockSpec((None, t, HEAD_DIM), lambda bi, h, i: (bi, 0, cv + h)),
                  pl.BlockSpec((None, 3, blk, blk), lambda bi, h, i: (h, 0, 0, 0)),
                  pl.BlockSpec(lam_vec.shape, lambda bi, h, i: (0, 0)),
                  pl.BlockSpec((1, HEAD_DIM), lambda bi, h, i: (0, 0))],
        out_specs=pl.BlockSpec((None, blk, HEAD_DIM), lambda bi, h, i: (bi, i, h)),
        out_shape=jax.ShapeDtypeStruct((b, t, n_heads * HEAD_DIM), BF16),
        compiler_params=_params("parallel", "parallel", "arbitrary"),
        name="diff_prompt",
    )(p16, p16, p16, bias, lam_vec, g_subln.reshape(1, HEAD_DIM))


def _fox_prompt_body(q_ref, k_ref, v_ref, cum_ref, cumt_ref, o_ref, *, blk):
    h = pl.program_id(1)
    i = pl.program_id(2)
    q = q_ref[...]
    cum = cum_ref[...]
    lane = lax.broadcasted_iota(jnp.int32, cum.shape, 1)
    cq = jnp.sum(jnp.where(lane == h, cum, 0.0), axis=-1, keepdims=True)
    row = lax.broadcasted_iota(jnp.int32, (blk, blk), 0)
    col = lax.broadcasted_iota(jnp.int32, (blk, blk), 1)

    def body(j, carry):
        m, l, acc = carry
        start = pl.multiple_of(j * blk, blk)
        ks = k_ref[pl.ds(start, blk), :]
        vs = v_ref[pl.ds(start, blk), :]
        ck = cumt_ref[:, pl.ds(start, blk)]
        s = lax.dot_general(q, ks, NT_DIMS, preferred_element_type=F32) * HEAD_DIM ** -0.5 + (cq - ck)
        s = jnp.where((j < i) | (col <= row), s, NEG_INF)
        return _softmax_step(s, vs, m, l, acc)

    init = (jnp.full((blk, 1), NEG_INF, F32), jnp.zeros((blk, 1), F32), jnp.zeros((blk, HEAD_DIM), F32))
    _, l, acc = lax.fori_loop(0, i + 1, body, init)
    o_ref[...] = (acc / l).astype(o_ref.dtype)


def _fox_prompt(p16, col_q, col_k, col_v, n_heads, cum, blk):
    b, t, _ = p16.shape
    cq, ck, cv = col_q // HEAD_DIM, col_k // HEAD_DIM, col_v // HEAD_DIM
    cum_t = jnp.swapaxes(cum, 1, 2).reshape(b, n_heads, 1, t)
    return pl.pallas_call(
        functools.partial(_fox_prompt_body, blk=blk),
        grid=(b, n_heads, t // blk),
        in_specs=[pl.BlockSpec((None, blk, HEAD_DIM), lambda bi, h, i: (bi, i, cq + h)),
                  pl.BlockSpec((None, t, HEAD_DIM), lambda bi, h, i: (bi, 0, ck + h)),
                  pl.BlockSpec((None, t, HEAD_DIM), lambda bi, h, i: (bi, 0, cv + h)),
                  pl.BlockSpec((None, blk, n_heads), lambda bi, h, i: (bi, i, 0)),
                  pl.BlockSpec((None, None, 1, t), lambda bi, h, i: (bi, h, 0, 0))],
        out_specs=pl.BlockSpec((None, blk, HEAD_DIM), lambda bi, h, i: (bi, i, h)),
        out_shape=jax.ShapeDtypeStruct((b, t, n_heads * HEAD_DIM), BF16),
        compiler_params=_params("parallel", "parallel", "arbitrary"),
        name="fox_prompt",
    )(p16, p16, p16, cum, cum_t)


def _top_blocks_negmask(gate, n_top, limit):
    n_blk = gate.shape[1]
    lane = lax.broadcasted_iota(jnp.int32, gate.shape, 1).astype(F32)
    chosen = jnp.zeros(gate.shape, F32)
    g = gate
    for _ in range(n_top):
        mx = jnp.max(g, axis=-1, keepdims=True)
        idx = jnp.min(jnp.where(g == mx, lane, float(n_blk)), axis=-1, keepdims=True)
        pick = lane == idx
        chosen = jnp.where(pick & (idx < limit), 1.0, chosen)
        g = jnp.where(pick, NEG_INF, g)
    return jnp.where(chosen > 0.0, 0.0, NEG_INF)


def _moba_prompt_body(q_ref, k_ref, v_ref, q32_ref, k32_ref, bias_ref, o_ref, kmean_scr, *, n_blk):
    blk = MOBA_BLOCK
    i = pl.program_id(2)

    @pl.when(i == 0)
    def _():
        kmean_scr[...] = jnp.mean(k32_ref[...].reshape(n_blk, blk, HEAD_DIM), axis=1)

    gate = lax.dot_general(q32_ref[...], kmean_scr[...], NT_DIMS, preferred_element_type=F32,
                           precision=lax.Precision.HIGHEST)
    lane = lax.broadcasted_iota(jnp.int32, gate.shape, 1)
    gate = jnp.where(lane < i, gate, NEG_INF)
    sel = _top_blocks_negmask(gate, min(MOBA_TOPK, n_blk), i.astype(F32))
    q = q_ref[...]

    def scores(j):
        start = pl.multiple_of(j * blk, blk)
        ks = k_ref[pl.ds(start, blk), :]
        s = lax.dot_general(q, ks, NT_DIMS, preferred_element_type=F32) * HEAD_DIM ** -0.5
        return s + bias_ref[jnp.minimum(i - j, 2)], v_ref[pl.ds(start, blk), :]

    s_own, v_own = scores(i)
    init = _softmax_step(s_own, v_own, jnp.full((blk, 1), NEG_INF, F32), jnp.zeros((blk, 1), F32),
                         jnp.zeros((blk, HEAD_DIM), F32))

    def body(j, carry):
        s, vs = scores(j)
        s = s + jnp.min(jnp.where(lane == j, sel, 0.0), axis=-1, keepdims=True)
        return _softmax_step(s, vs, *carry)

    _, l, acc = lax.fori_loop(0, i, body, init)
    o_ref[...] = (acc / l).astype(o_ref.dtype)


def _moba_prompt(p16, p32, col_q, col_k, col_v, n_heads, bias):
    b, t, _ = p16.shape
    blk = MOBA_BLOCK
    assert t % blk == 0
    cq, ck, cv = col_q // HEAD_DIM, col_k // HEAD_DIM, col_v // HEAD_DIM
    return pl.pallas_call(
        functools.partial(_moba_prompt_body, n_blk=t // blk),
        grid=(b, n_heads, t // blk),
        in_specs=[pl.BlockSpec((None, blk, HEAD_DIM), lambda bi, h, i: (bi, i, cq + h)),
                  pl.BlockSpec((None, t, HEAD_DIM), lambda bi, h, i: (bi, 0, ck + h)),
                  pl.BlockSpec((None, t, HEAD_DIM), lambda bi, h, i: (bi, 0, cv + h)),
                  pl.BlockSpec((None, blk, HEAD_DIM), lambda bi, h, i: (bi, i, cq + h)),
                  pl.BlockSpec((None, t, HEAD_DIM), lambda bi, h, i: (bi, 0, ck + h)),
                  pl.BlockSpec((None, 3, blk, blk), lambda bi, h, i: (h, 0, 0, 0))],
        out_specs=pl.BlockSpec((None, blk, HEAD_DIM), lambda bi, h, i: (bi, i, h)),
        out_shape=jax.ShapeDtypeStruct((b, t, n_heads * HEAD_DIM), BF16),
        scratch_shapes=[pltpu.VMEM((t // blk, HEAD_DIM), F32)],
        compiler_params=_params("parallel", "parallel", "arbitrary"),
        name="moba_prompt",
    )(p16, p16, p16, p32, p32, bias)


KEY_SIGN = -2 ** 31
KEY_OF_NEG_INF = -2139095041


def _order_key(score):
    bits = pltpu.bitcast(score, jnp.int32)
    key = jnp.where(bits < 0, bits ^ 0x7FFFFFFF, bits)
    return jnp.where(score == 0.0, 0, key)


def _kth_largest_key(count_ge, n_rows, k):
    def bit_body(b, ans):
        cand = ans | jnp.left_shift(jnp.int32(1), 31 - b)
        return jnp.where(count_ge(cand ^ KEY_SIGN) >= k, cand, ans)

    ans = lax.fori_loop(0, 32, bit_body, jnp.zeros((n_rows, 1), jnp.int32))
    return ans ^ KEY_SIGN


def _dsa_prompt_body(iq_ref, ikw_ref, kidx_ref, qb_ref, kb_ref, vb_ref, bias_ref, o_ref, key_scr, nm_scr, *,
                     tq, n_heads, n_sel):
    i = pl.program_id(1)
    n_chunks = i + 1
    iq = iq_ref[...].reshape(N_IDX_HEADS * tq, IDX_DIM)
    w = ikw_ref[:, IDX_DIM:IDX_DIM + N_IDX_HEADS] * N_IDX_HEADS ** -0.5
    row = lax.broadcasted_iota(jnp.int32, (tq, tq), 0)
    col = lax.broadcasted_iota(jnp.int32, (tq, tq), 1)

    def score_body(j, _):
        kc = kidx_ref[pl.ds(pl.multiple_of(j * tq, tq), tq), :][:, :IDX_DIM]
        rel = jnp.maximum(lax.dot_general(iq, kc, NT_DIMS, preferred_element_type=F32) * IDX_DIM ** -0.5, 0.0)
        rel = rel.reshape(N_IDX_HEADS, tq, tq)
        sc = w[:, 0:1] * rel[0]
        for n in range(1, N_IDX_HEADS):
            sc = sc + w[:, n:n + 1] * rel[n]
        sc = jnp.where((j < i) | (col <= row), sc, NEG_INF)
        key_scr[j] = _order_key(sc)
        return 0

    lax.fori_loop(0, n_chunks, score_body, 0)

    def count(pred):
        def body(j, acc):
            return acc + jnp.where(pred(key_scr[j]), 1.0, 0.0)
        return jnp.sum(lax.fori_loop(0, n_chunks, body, jnp.zeros((tq, tq), F32)), axis=-1, keepdims=True)

    thr = _kth_largest_key(lambda t: count(lambda k: k >= t), tq, n_sel)
    cnt_ge = count(lambda k: k >= thr)
    tie = jnp.max(jnp.where((cnt_ge > n_sel) & (thr > KEY_OF_NEG_INF), 1.0, 0.0)) > 0.0

    @pl.when(jnp.logical_not(tie))
    def _():
        def body(j, _):
            nm_scr[j] = jnp.where(key_scr[j] >= thr, 0.0, NEG_INF)
            return 0
        lax.fori_loop(0, n_chunks, body, 0)

    @pl.when(tie)
    def _():
        allow = n_sel - count(lambda k: k > thr)
        tri = jnp.where(row <= col, 1.0, 0.0).astype(BF16)

        def body(j, before):
            k = key_scr[j]
            eq = jnp.where(k == thr, 1.0, 0.0)
            rank = jnp.dot(eq.astype(BF16), tri, preferred_element_type=F32) + before
            keep = jnp.where(k > thr, 1.0, jnp.where(rank <= allow, eq, 0.0))
            nm_scr[j] = jnp.where(keep > 0.0, 0.0, NEG_INF)
            return before + jnp.sum(eq, axis=-1, keepdims=True)
        lax.fori_loop(0, n_chunks, body, jnp.zeros((tq, 1), F32))

    qb = qb_ref[...]
    qs = jnp.concatenate([qb[:, h * HEAD_DIM:(h + 1) * HEAD_DIM] for h in range(n_heads)], axis=0)

    def att_body(j, carry):
        rows = pl.ds(pl.multiple_of(j * tq, tq), tq)
        s = lax.dot_general(qs, kb_ref[rows, :], NT_DIMS, preferred_element_type=F32) * HEAD_DIM ** -0.5
        s = s + bias_ref[jnp.minimum(i - j, 2)] + jnp.tile(nm_scr[j], (n_heads, 1))
        return _softmax_step(s, vb_ref[rows, :], *carry)

    r = n_heads * tq
    init = (jnp.full((r, 1), NEG_INF, F32), jnp.zeros((r, 1), F32), jnp.zeros((r, HEAD_DIM), F32))
    _, l, acc = lax.fori_loop(0, n_chunks, att_body, init)
    o = acc / l
    for h in range(n_heads):
        o_ref[:, h * HEAD_DIM:(h + 1) * HEAD_DIM] = o[h * tq:(h + 1) * tq].astype(o_ref.dtype)


def _dsa_prompt(p16, p32, iq_t, col_qb, col_kb, col_vb, col_ik, n_heads, bias, n_sel, tq):
    b, t, _ = p16.shape
    qw = n_heads * HEAD_DIM
    assert col_qb % qw == 0
    return pl.pallas_call(
        functools.partial(_dsa_prompt_body, tq=tq, n_heads=n_heads, n_sel=n_sel),
        grid=(b, t // tq),
        in_specs=[pl.BlockSpec((None, N_IDX_HEADS, tq, IDX_DIM), lambda bi, i: (bi, 0, i, 0)),
                  pl.BlockSpec((None, tq, LANES), lambda bi, i: (bi, i, col_ik // LANES)),
                  pl.BlockSpec((None, t, LANES), lambda bi, i: (bi, 0, col_ik // LANES)),
                  pl.BlockSpec((None, tq, qw), lambda bi, i: (bi, i, col_qb // qw)),
                  pl.BlockSpec((None, t, HEAD_DIM), lambda bi, i: (bi, 0, col_kb // HEAD_DIM)),
                  pl.BlockSpec((None, t, HEAD_DIM), lambda bi, i: (bi, 0, col_vb // HEAD_DIM)),
                  pl.BlockSpec(bias.shape, lambda bi, i: (0, 0, 0))],
        out_specs=pl.BlockSpec((None, tq, qw), lambda bi, i: (bi, i, 0)),
        out_shape=jax.ShapeDtypeStruct((b, t, qw), BF16),
        scratch_shapes=[pltpu.VMEM((t // tq, tq, tq), jnp.int32), pltpu.VMEM((t // tq, tq, tq), F32)],
        compiler_params=_params("parallel", "arbitrary"),
        name="dsa_prompt",
    )(iq_t, p32, p16, p16, p16, p16, bias)


def _block_diag_rows(q, dtype):
    b, tn, g, d = q.shape
    eye = jnp.eye(g, dtype=q.dtype)
    return jnp.einsum('btgd,gk->bgtkd', q, eye).reshape(b, g * tn, g * d).astype(dtype)


def _page_spec(layer, n_pages, page, width):
    return pl.BlockSpec((None, None, page, width),
                        lambda bi, p, pt: (layer, pt[bi, jnp.minimum(p, n_pages - 1)], 0, 0))


def _bias_index(p, n_pages):
    return jnp.clip(p - (n_pages - 2), 0, 2)


def _sample_call(body, pt, n_steps, operands, in_specs, out_shape, out_spec, scratch, name):
    return pl.pallas_call(
        body,
        grid_spec=pltpu.PrefetchScalarGridSpec(
            num_scalar_prefetch=1, grid=(pt.shape[0], n_steps),
            in_specs=in_specs, out_specs=out_spec, scratch_shapes=scratch),
        out_shape=out_shape,
        compiler_params=_params("parallel", "arbitrary"),
        name=name,
    )(pt, *operands)


def _flash_scratch(rows, width):
    return [pltpu.VMEM((rows, 1), F32), pltpu.VMEM((rows, 1), F32), pltpu.VMEM((rows, width), F32)]


def _flash_init(m_scr, l_scr, acc_scr):
    m_scr[...] = jnp.full(m_scr.shape, NEG_INF, F32)
    l_scr[...] = jnp.zeros(l_scr.shape, F32)
    acc_scr[...] = jnp.zeros(acc_scr.shape, F32)


def _flash_update(s, v16, m_scr, l_scr, acc_scr):
    m, l, acc = _softmax_step(s, v16, m_scr[...], l_scr[...], acc_scr[...])
    m_scr[...] = m
    l_scr[...] = l
    acc_scr[...] = acc


def _diff_sample_body(pt_ref, wq_ref, kc_ref, vc_ref, kn_ref, vn_ref, bias_ref, lam_ref, g_ref, o_ref,
                      m_scr, l_scr, acc_scr, *, n_pages, n_new, n_heads, lam_init):
    p = pl.program_id(1)

    @pl.when(p == 0)
    def _():
        _flash_init(m_scr, l_scr, acc_scr)

    def step(k_ref, v_ref):
        s = lax.dot_general(wq_ref[...], k_ref[...].astype(BF16), NT_DIMS, preferred_element_type=F32)
        s = s * DH_A ** -0.5 + bias_ref[_bias_index(p, n_pages)]
        _flash_update(s, v_ref[...].astype(BF16), m_scr, l_scr, acc_scr)

    @pl.when(p < n_pages)
    def _():
        step(kc_ref, vc_ref)

    @pl.when(p == n_pages)
    def _():
        step(kn_ref, vn_ref)
        lam = _diff_lambda(lam_ref, lam_init)
        on = acc_scr[...] / l_scr[...]
        for h in range(n_heads):
            r0 = (2 * h) * n_new
            cols = slice(h * HEAD_DIM, (h + 1) * HEAD_DIM)
            o = on[r0:r0 + n_new, cols] - lam * on[r0 + n_new:r0 + 2 * n_new, cols]
            o_ref[h * n_new:(h + 1) * n_new, :] = _rms(o, g_ref[...]) * (1.0 - lam_init)


def _diff_sample(pt, layer, q, cache_k, cache_v, k_new, v_new, tab, lam_vec, g_subln, lam_init):
    b, tn, n_heads = q.shape[:3]
    n_pages = pt.shape[1]
    page, width = cache_k.shape[2:]
    wq = _block_diag_rows(q.reshape(b, tn, 2 * n_heads, DH_A), BF16)
    rows = 2 * n_heads * tn
    bias = _sample_bias(tab, tn, page, 2)
    pad = ((0, 0), (0, page - tn), (0, 0))
    whole = lambda shape: pl.BlockSpec(shape, lambda bi, p, pt_: (0,) * len(shape))
    per_b = lambda r, w: pl.BlockSpec((None, r, w), lambda bi, p, pt_: (bi, 0, 0))
    out = _sample_call(
        functools.partial(_diff_sample_body, n_pages=n_pages, n_new=tn, n_heads=n_heads, lam_init=lam_init),
        pt, n_pages + 1,
        (wq, cache_k, cache_v, jnp.pad(k_new, pad), jnp.pad(v_new, pad), bias, lam_vec, g_subln.reshape(1, HEAD_DIM)),
        [per_b(rows, width), _page_spec(layer, n_pages, page, width), _page_spec(layer, n_pages, page, width),
         per_b(page, width), per_b(page, width), whole(bias.shape), whole(lam_vec.shape), whole((1, HEAD_DIM))],
        jax.ShapeDtypeStruct((b, n_heads * tn, HEAD_DIM), F32), per_b(n_heads * tn, HEAD_DIM),
        _flash_scratch(rows, width), "diff_sample")
    return jnp.swapaxes(out.reshape(b, n_heads, tn, HEAD_DIM), 1, 2).reshape(b, tn, n_heads * HEAD_DIM)


def _extract_heads(on, n_heads, n_new):
    return jnp.concatenate([on[h * n_new:(h + 1) * n_new, h * HEAD_DIM:(h + 1) * HEAD_DIM] for h in range(n_heads)],
                           axis=0)


def _fox_sample_body(pt_ref, wq_ref, kc_ref, vc_ref, kn_ref, vn_ref, cumq_ref, cumk_ref, mask_ref, o_ref,
                     m_scr, l_scr, acc_scr, *, n_pages, n_new, n_heads):
    p = pl.program_id(1)

    @pl.when(p == 0)
    def _():
        _flash_init(m_scr, l_scr, acc_scr)

    def step(k_ref, v_ref, extra):
        ck = cumk_ref[...]
        ck = jnp.broadcast_to(ck[:, None, :], (n_heads, n_new, ck.shape[1])).reshape(n_heads * n_new, ck.shape[1])
        s = lax.dot_general(wq_ref[...], k_ref[...].astype(BF16), NT_DIMS, preferred_element_type=F32)
        s = s * HEAD_DIM ** -0.5 + (cumq_ref[...] - ck)
        if extra is not None:
            s = s + extra
        _flash_update(s, v_ref[...].astype(BF16), m_scr, l_scr, acc_scr)

    @pl.when(p < n_pages)
    def _():
        step(kc_ref, vc_ref, None)

    @pl.when(p == n_pages)
    def _():
        step(kn_ref, vn_ref, mask_ref[...])
        o_ref[...] = _extract_heads(acc_scr[...] / l_scr[...], n_heads, n_new)


def _heads_first(o, b, n_heads, tn):
    return jnp.swapaxes(o.reshape(b, n_heads, tn, HEAD_DIM), 1, 2).reshape(b, tn, n_heads * HEAD_DIM)


def _fox_sample(pt, layer, q, cache_k, cache_v, k_new, v_new, cum_q, cum_k):
    b, tn, n_heads = q.shape[:3]
    n_pages = pt.shape[1]
    page, width = cache_k.shape[2:]
    wq = _block_diag_rows(q, BF16)
    rows = n_heads * tn
    cq = jnp.swapaxes(cum_q, 1, 2).reshape(b, rows, 1)
    qi = np.arange(tn)[:, None]
    ci = np.arange(page)[None, :]
    mask = np.where((ci <= qi) & (ci < tn), 0.0, NEG_INF).astype(np.float32)
    mask = jnp.asarray(np.tile(mask, (n_heads, 1)))
    pad = ((0, 0), (0, page - tn), (0, 0))
    per_b = lambda r, w: pl.BlockSpec((None, r, w), lambda bi, p, pt_: (bi, 0, 0))
    out = _sample_call(
        functools.partial(_fox_sample_body, n_pages=n_pages, n_new=tn, n_heads=n_heads),
        pt, n_pages + 1,
        (wq, cache_k, cache_v, jnp.pad(k_new, pad), jnp.pad(v_new, pad), cq, cum_k, mask),
        [per_b(rows, width), _page_spec(layer, n_pages, page, width), _page_spec(layer, n_pages, page, width),
         per_b(page, width), per_b(page, width), per_b(rows, 1),
         pl.BlockSpec((None, n_heads, page), lambda bi, p, pt_: (bi, 0, p)),
         pl.BlockSpec(mask.shape, lambda bi, p, pt_: (0, 0))],
        jax.ShapeDtypeStruct((b, rows, HEAD_DIM), F32), per_b(rows, HEAD_DIM),
        _flash_scratch(rows, width), "fox_sample")
    return _heads_first(out, b, n_heads, tn)


def _kmean_body(pt_ref, k_ref, o_ref, *, pages_per_block):
    p = pl.program_id(1)
    s = jnp.sum(k_ref[...], axis=0, keepdims=True)

    @pl.when(p % pages_per_block == 0)
    def _():
        o_ref[...] = s

    @pl.when(p % pages_per_block != 0)
    def _():
        o_ref[...] += s

    @pl.when(p % pages_per_block == pages_per_block - 1)
    def _():
        o_ref[...] = o_ref[...] * (1.0 / MOBA_BLOCK)


def _moba_gate_body(wq_ref, kmean_ref, o_ref, *, n_blk):
    gate = lax.dot_general(wq_ref[...], kmean_ref[...], NT_DIMS, preferred_element_type=F32,
                           precision=lax.Precision.HIGHEST)
    o_ref[...] = _top_blocks_negmask(gate, min(MOBA_TOPK, n_blk), float(n_blk))


def _moba_sample_body(pt_ref, wq_ref, kc_ref, vc_ref, kn_ref, vn_ref, bias_ref, sel_ref, o_ref,
                      m_scr, l_scr, acc_scr, *, n_pages, n_new, n_heads, pages_per_block):
    p = pl.program_id(1)

    @pl.when(p == 0)
    def _():
        _flash_init(m_scr, l_scr, acc_scr)

    def step(k_ref, v_ref, row_mask):
        s = lax.dot_general(wq_ref[...], k_ref[...].astype(BF16), NT_DIMS, preferred_element_type=F32)
        s = s * HEAD_DIM ** -0.5 + bias_ref[_bias_index(p, n_pages)]
        if row_mask is not None:
            s = s + row_mask
        _flash_update(s, v_ref[...].astype(BF16), m_scr, l_scr, acc_scr)

    @pl.when(p < n_pages)
    def _():
        sel = sel_ref[...]
        lane = lax.broadcasted_iota(jnp.int32, sel.shape, 1)
        step(kc_ref, vc_ref, jnp.min(jnp.where(lane == p // pages_per_block, sel, 0.0), axis=-1, keepdims=True))

    @pl.when(p == n_pages)
    def _():
        step(kn_ref, vn_ref, None)
        o_ref[...] = _extract_heads(acc_scr[...] / l_scr[...], n_heads, n_new)


def _moba_sample(pt, layer, q, cache_k, cache_v, k_new, v_new, tab):
    b, tn, n_heads = q.shape[:3]
    n_pages = pt.shape[1]
    page, width = cache_k.shape[2:]
    assert MOBA_BLOCK % page == 0 and (n_pages * page) % MOBA_BLOCK == 0 and tn < MOBA_BLOCK
    ppb = MOBA_BLOCK // page
    n_blk = n_pages // ppb
    rows = n_heads * tn
    kmean = pl.pallas_call(
        functools.partial(_kmean_body, pages_per_block=ppb),
        grid_spec=pltpu.PrefetchScalarGridSpec(
            num_scalar_prefetch=1, grid=(b, n_pages),
            in_specs=[pl.BlockSpec((None, None, page, width), lambda bi, p, pt_: (layer, pt_[bi, p], 0, 0))],
            out_specs=pl.BlockSpec((None, None, 1, width), lambda bi, p, pt_: (bi, p // ppb, 0, 0))),
        out_shape=jax.ShapeDtypeStruct((b, n_blk, 1, width), F32),
        compiler_params=_params("parallel", "arbitrary"),
        name="moba_kmean",
    )(pt, cache_k).reshape(b, n_blk, width)
    sel = pl.pallas_call(
        functools.partial(_moba_gate_body, n_blk=n_blk),
        grid=(b,),
        in_specs=[pl.BlockSpec((None, rows, width), lambda bi: (bi, 0, 0)),
                  pl.BlockSpec((None, n_blk, width), lambda bi: (bi, 0, 0))],
        out_specs=pl.BlockSpec((None, rows, n_blk), lambda bi: (bi, 0, 0)),
        out_shape=jax.ShapeDtypeStruct((b, rows, n_blk), F32),
        compiler_params=_params("parallel"),
        name="moba_gate",
    )(_block_diag_rows(q, F32), kmean)
    bias = _sample_bias(tab, tn, page, 1)
    pad = ((0, 0), (0, page - tn), (0, 0))
    per_b = lambda r, w: pl.BlockSpec((None, r, w), lambda bi, p, pt_: (bi, 0, 0))
    out = _sample_call(
        functools.partial(_moba_sample_body, n_pages=n_pages, n_new=tn, n_heads=n_heads, pages_per_block=ppb),
        pt, n_pages + 1,
        (_block_diag_rows(q, BF16), cache_k, cache_v, jnp.pad(k_new, pad), jnp.pad(v_new, pad), bias, sel),
        [per_b(rows, width), _page_spec(layer, n_pages, page, width), _page_spec(layer, n_pages, page, width),
         per_b(page, width), per_b(page, width), pl.BlockSpec(bias.shape, lambda bi, p, pt_: (0, 0, 0)),
         per_b(rows, n_blk)],
        jax.ShapeDtypeStruct((b, rows, HEAD_DIM), F32), per_b(rows, HEAD_DIM),
        _flash_scratch(rows, width), "moba_sample")
    return _heads_first(out, b, n_heads, tn)


def _dsa_score_body(pt_ref, iq_ref, w_ref, kc_ref, kn_ref, mask_ref, o_ref, *, n_pages, n_new):
    p = pl.program_id(1)

    def step(k_ref, extra):
        rel = lax.dot_general(iq_ref[...], k_ref[...].astype(BF16), NT_DIMS, preferred_element_type=F32)
        rel = jnp.maximum(rel * IDX_DIM ** -0.5, 0.0) * w_ref[...]
        sc = jnp.sum(rel.reshape(N_IDX_HEADS, n_new, rel.shape[1]), axis=0)
        o_ref[...] = sc if extra is None else sc + extra

    @pl.when(p < n_pages)
    def _():
        step(kc_ref, None)

    @pl.when(p == n_pages)
    def _():
        step(kn_ref, mask_ref[...])


def _dsa_select_body(sc_ref, o_ref, key_scr, *, n_new, n_sel, chunk):
    width = sc_ref.shape[1]
    key_scr[...] = _order_key(sc_ref[...])

    def count(pred):
        return jnp.sum(jnp.where(pred(key_scr[...]), 1.0, 0.0), axis=-1, keepdims=True)

    thr = _kth_largest_key(lambda t: count(lambda k: k >= t), n_new, n_sel)
    cnt_ge = count(lambda k: k >= thr)
    tie = jnp.max(jnp.where((cnt_ge > n_sel) & (thr > KEY_OF_NEG_INF), 1.0, 0.0)) > 0.0

    @pl.when(jnp.logical_not(tie))
    def _():
        o_ref[...] = jnp.where(key_scr[...] >= thr, 0.0, NEG_INF)

    @pl.when(tie)
    def _():
        allow = n_sel - count(lambda k: k > thr)
        r = lax.broadcasted_iota(jnp.int32, (chunk, chunk), 0)
        c = lax.broadcasted_iota(jnp.int32, (chunk, chunk), 1)
        tri = jnp.where(r <= c, 1.0, 0.0).astype(BF16)

        def body(j, before):
            cols = pl.ds(pl.multiple_of(j * chunk, chunk), chunk)
            k = key_scr[:, cols]
            eq = jnp.where(k == thr, 1.0, 0.0)
            rank = jnp.dot(eq.astype(BF16), tri, preferred_element_type=F32) + before
            keep = jnp.where(k > thr, 1.0, jnp.where(rank <= allow, eq, 0.0))
            o_ref[:, cols] = jnp.where(keep > 0.0, 0.0, NEG_INF)
            return before + jnp.sum(eq, axis=-1, keepdims=True)
        lax.fori_loop(0, width // chunk, body, jnp.zeros((n_new, 1), F32))


def _dsa_sample_body(pt_ref, q_ref, kc_ref, vc_ref, kn_ref, vn_ref, bias_ref, nm_ref, o_ref,
                     m_scr, l_scr, acc_scr, *, n_pages, n_heads):
    p = pl.program_id(1)

    @pl.when(p == 0)
    def _():
        _flash_init(m_scr, l_scr, acc_scr)

    def step(k_ref, v_ref):
        s = lax.dot_general(q_ref[...], k_ref[...].astype(BF16), NT_DIMS, preferred_element_type=F32)
        s = s * HEAD_DIM ** -0.5 + bias_ref[_bias_index(p, n_pages)] + jnp.tile(nm_ref[...], (n_heads, 1))
        _flash_update(s, v_ref[...].astype(BF16), m_scr, l_scr, acc_scr)

    @pl.when(p < n_pages)
    def _():
        step(kc_ref, vc_ref)

    @pl.when(p == n_pages)
    def _():
        step(kn_ref, vn_ref)
        o_ref[...] = acc_scr[...] / l_scr[...]


def _dsa_sample(pt, layer, qb, iq, iw, cache_k, cache_v, cache_idx, k_new, v_new, ik_new, tab):
    b, tn, n_heads = qb.shape[:3]
    n_pages = pt.shape[1]
    page = cache_k.shape[2]
    n_keys = n_pages * page + tn
    n_sel = min(DSA_TOPK, n_keys // 4)
    width = (n_pages + 1) * page
    pad = ((0, 0), (0, page - tn), (0, 0))
    per_b = lambda r, w: pl.BlockSpec((None, r, w), lambda bi, p, pt_: (bi, 0, 0))
    qi = np.arange(tn)[:, None]
    ci = np.arange(page)[None, :]
    new_mask = jnp.asarray(np.where((ci <= qi) & (ci < tn), 0.0, NEG_INF).astype(np.float32))
    iq_rows = jnp.swapaxes(iq, 1, 2).reshape(b, N_IDX_HEADS * tn, IDX_DIM).astype(BF16)
    w_rows = (jnp.swapaxes(iw, 1, 2).astype(F32) * N_IDX_HEADS ** -0.5).reshape(b, N_IDX_HEADS * tn, 1)
    scores = _sample_call(
        functools.partial(_dsa_score_body, n_pages=n_pages, n_new=tn),
        pt, n_pages + 1,
        (iq_rows, w_rows, cache_idx, jnp.pad(ik_new, pad), new_mask),
        [per_b(N_IDX_HEADS * tn, IDX_DIM), per_b(N_IDX_HEADS * tn, 1), _page_spec(layer, n_pages, page, IDX_DIM),
         per_b(page, IDX_DIM), pl.BlockSpec(new_mask.shape, lambda bi, p, pt_: (0, 0))],
        jax.ShapeDtypeStruct((b, tn, width), F32), pl.BlockSpec((None, tn, page), lambda bi, p, pt_: (bi, 0, p)),
        [], "dsa_score")
    negmask = pl.pallas_call(
        functools.partial(_dsa_select_body, n_new=tn, n_sel=n_sel, chunk=page),
        grid=(b,),
        in_specs=[pl.BlockSpec((None, tn, width), lambda bi: (bi, 0, 0))],
        out_specs=pl.BlockSpec((None, tn, width), lambda bi: (bi, 0, 0)),
        out_shape=jax.ShapeDtypeStruct((b, tn, width), F32),
        scratch_shapes=[pltpu.VMEM((tn, width), jnp.int32)],
        compiler_params=_params("parallel"),
        name="dsa_select",
    )(scores)
    q_rows = jnp.swapaxes(qb, 1, 2).reshape(b, n_heads * tn, HEAD_DIM).astype(BF16)
    bias = _sample_bias(tab, tn, page, 1)
    rows = n_heads * tn
    out = _sample_call(
        functools.partial(_dsa_sample_body, n_pages=n_pages, n_heads=n_heads),
        pt, n_pages + 1,
        (q_rows, cache_k, cache_v, jnp.pad(k_new, pad), jnp.pad(v_new, pad), bias, negmask),
        [per_b(rows, HEAD_DIM), _page_spec(layer, n_pages, page, HEAD_DIM), _page_spec(layer, n_pages, page, HEAD_DIM),
         per_b(page, HEAD_DIM), per_b(page, HEAD_DIM), pl.BlockSpec(bias.shape, lambda bi, p, pt_: (0, 0, 0)),
         pl.BlockSpec((None, tn, page), lambda bi, p, pt_: (bi, 0, p))],
        jax.ShapeDtypeStruct((b, rows, HEAD_DIM), F32), per_b(rows, HEAD_DIM),
        _flash_scratch(rows, HEAD_DIM), "dsa_sample")
    return _heads_first(out, b, n_heads, tn)


def _pad_cols(w, n):
    return jnp.pad(w, ((0, 0), (0, n - w.shape[1])))


def _mixer_even(h_in, g, w_in16, past, pt, layer_e, lam_vec, g_subln, lam_init, tab, b, t):
    d = h_in.shape[1]
    n_a = n_b = (d // HEAD_DIM) // 2
    wa, wb = n_a * HEAD_DIM, n_b * HEAD_DIM
    c_qa, c_ka, c_va, c_qb = 0, wa, 2 * wa, 3 * wa
    c_kb = c_qb + wb
    c_vb = c_kb + HEAD_DIM
    c_iq = c_vb + HEAD_DIM
    c_ik = c_iq + N_IDX_HEADS * IDX_DIM
    c_iw = c_ik + IDX_DIM
    p32, p16 = _rms_matmul(h_in, g, w_in16, emit32=True, emit16=True)
    npad = p32.shape[1]
    p32 = p32.reshape(b, t, npad)
    p16 = p16.reshape(b, t, npad)
    ka = p32[..., c_ka:c_ka + wa]
    va = p32[..., c_va:c_va + wa]
    kb = p32[..., c_kb:c_kb + HEAD_DIM]
    vb = p32[..., c_vb:c_vb + HEAD_DIM]
    ik = p32[..., c_ik:c_ik + IDX_DIM]
    rows = (ka.reshape(b, t, n_a, 2, DH_A), va.reshape(b, t, n_a, HEAD_DIM), kb, vb, ik)
    tab_a, tab_b = tab[:, :n_a], tab[:, n_a:]
    if past is None:
        blk = min(t, 256)
        assert t % blk == 0
        o_a = _diff_prompt(p16, c_qa, c_ka, c_va, n_a, _prompt_bias(tab_a, blk), lam_vec, g_subln, lam_init, blk)
        tq = min(t, 128)
        bias_b = jnp.swapaxes(_prompt_bias(tab_b, tq), 0, 1).reshape(3, n_b * tq, tq)
        iq_t = jnp.swapaxes(p16[..., c_iq:c_ik].reshape(b, t, N_IDX_HEADS, IDX_DIM), 1, 2)
        o_b = _dsa_prompt(p16, p32, iq_t, c_qb, c_kb, c_vb, c_ik, n_b, bias_b, min(DSA_TOPK, t // 4), tq)
    else:
        cache_a_k, cache_a_v, cache_b_k, cache_b_v, cache_b_idx = past
        qa = p32[..., c_qa:c_qa + wa].reshape(b, t, n_a, 2, DH_A)
        o_a = _diff_sample(pt, layer_e, qa, cache_a_k, cache_a_v, ka, va, tab_a, lam_vec, g_subln, lam_init)
        qb = p32[..., c_qb:c_qb + wb].reshape(b, t, n_b, HEAD_DIM)
        iq = p32[..., c_iq:c_ik].reshape(b, t, N_IDX_HEADS, IDX_DIM)
        iw = p32[..., c_iw:c_iw + N_IDX_HEADS]
        o_b = _dsa_sample(pt, layer_e, qb, iq, iw, cache_b_k, cache_b_v, cache_b_idx, kb, vb, ik, tab_b)
    return (o_a.reshape(b * t, wa), o_b.reshape(b * t, wb)), rows


def _mixer_odd(h_in, g, w_in16, past, pt, layer_o, b_forget, tab, b, t):
    d = h_in.shape[1]
    n_c = n_d = (d // HEAD_DIM) // 2
    wc, wd = n_c * HEAD_DIM, n_d * HEAD_DIM
    c_qc, c_kc, c_vc, c_qd = 0, wc, 2 * wc, 3 * wc
    c_kd = c_qd + wd
    c_vd = c_kd + wd
    c_fc = c_vd + wd
    p32, p16 = _rms_matmul(h_in, g, w_in16, emit32=True, emit16=True)
    npad = p32.shape[1]
    p32 = p32.reshape(b, t, npad)
    p16 = p16.reshape(b, t, npad)
    kc = p32[..., c_kc:c_kc + wc]
    vc = p32[..., c_vc:c_vc + wc]
    kd = p32[..., c_kd:c_kd + wd]
    vd = p32[..., c_vd:c_vd + wd]
    log_f = jax.nn.log_sigmoid(p32[..., c_fc:c_fc + n_c] + b_forget.astype(F32))
    rows = (kc.reshape(b, t, n_c, HEAD_DIM), vc.reshape(b, t, n_c, HEAD_DIM), log_f,
            kd.reshape(b, t, n_d, HEAD_DIM), vd.reshape(b, t, n_d, HEAD_DIM))
    tab_d = tab[:, n_c:]
    if past is None:
        blk = min(t, 256)
        o_c = _fox_prompt(p16, c_qc, c_kc, c_vc, n_c, jnp.cumsum(log_f, axis=1), blk)
        o_d = _moba_prompt(p16, p32, c_qd, c_kd, c_vd, n_d, _prompt_bias(tab_d, MOBA_BLOCK))
    else:
        cache_c_k, cache_c_v, cache_c_logf, cache_d_k, cache_d_v = past
        n_pages = pt.shape[1]
        page = cache_c_k.shape[2]
        logf_past = cache_c_logf[layer_o][pt].reshape(b, n_pages * page, n_c)
        cum = jnp.cumsum(jnp.concatenate([logf_past, log_f], axis=1).astype(F32), axis=1)
        cum_q = cum[:, n_pages * page:]
        cum_k = jnp.swapaxes(jnp.pad(cum, ((0, 0), (0, page - t), (0, 0))), 1, 2)
        qc = p32[..., c_qc:c_qc + wc].reshape(b, t, n_c, HEAD_DIM)
        o_c = _fox_sample(pt, layer_o, qc, cache_c_k, cache_c_v, kc, vc, cum_q, cum_k)
        qd = p32[..., c_qd:c_qd + wd].reshape(b, t, n_d, HEAD_DIM)
        o_d = _moba_sample(pt, layer_o, qd, cache_d_k, cache_d_v, kd, vd, tab_d)
    return (o_c.reshape(b * t, wc), o_d.reshape(b * t, wd)), rows


def _run_trunk(x, past_even, past_odd, pt, mem_kv, prm):
    b, t, d = x.shape
    x = x.reshape(b * t, d)
    depth = prm['norm_g'].shape[0]
    rows_even, rows_odd = [], []
    for layer in range(depth):
        g = prm['norm_g'][layer]
        wg, wu, wd = prm['w_ffn_gate'][layer], prm['w_ffn_up'][layer], prm['w_ffn_down'][layer]
        x = _ffn(x, g[NG_FFN1_PRE], g[NG_FFN1_POST], wg[0], wu[0], wd[0])
        if layer % 2 == 0:
            e = layer // 2
            lam_init = 0.8 - 0.6 * math.exp(-0.3 * layer)
            parts, rows = _mixer_even(x, g[NG_MIX_PRE], prm['w_in_even'][e], past_even, pt, e,
                                      prm['diff_lambda'][e].astype(F32), prm['g_subln'][e].astype(F32), lam_init,
                                      prm['t5_table'], b, t)
            rows_even.append(rows)
            x = _out_proj(x, g[NG_MIX_POST], parts, prm['w_out_even'][e])
        else:
            o = layer // 2
            parts, rows = _mixer_odd(x, g[NG_MIX_PRE], prm['w_in_odd'][o], past_odd, pt, o,
                                     prm['b_forget'][o], prm['t5_table'], b, t)
            rows_odd.append(rows)
            x = _out_proj(x, g[NG_MIX_POST], parts, prm['w_out_odd'][o])
        mk, mv = mem_kv[layer]
        q = _rms_matmul(x, g[NG_X_PRE], prm['w_xq'][layer])
        o_x = _cross_attend(q.reshape(b, t, -1), mk, mv)
        x = _out_proj(x, g[NG_X_POST], [o_x.reshape(b * t, -1)], prm['w_xo'][layer])
        x = _ffn(x, g[NG_FFN2_PRE], g[NG_FFN2_POST], wg[1], wu[1], wd[1])
    return x.reshape(b, t, d), rows_even, rows_odd


def kernel(x_prompt, x_sample, cache_a_k, cache_a_v, cache_b_k, cache_b_v, cache_b_idx, cache_c_k, cache_c_v, cache_c_logf, cache_d_k, cache_d_v, cache_mem_k, cache_mem_v, page_table, mem_prompt, t5_table, norm_g, w_ffn_gate, w_ffn_up, w_ffn_down, w_xq, w_xk, w_xv, w_xo, w_in_even, w_out_even, diff_lambda, g_subln, w_in_odd, w_out_odd, b_forget):
    depth = norm_g.shape[0]
    d_model = x_prompt.shape[-1]
    n_c = (d_model // HEAD_DIM) // 2
    e_pad = _round_up(w_in_even.shape[-1], COL_TILE)
    o_pad = _round_up(w_in_odd.shape[-1], COL_TILE)
    c_fc = 3 * n_c * HEAD_DIM
    w_in_odd_r = jnp.concatenate([w_in_odd[..., :c_fc], w_in_odd[..., c_fc + n_c:], w_in_odd[..., c_fc:c_fc + n_c]],
                                 axis=-1)
    prm = {
        't5_table': t5_table.astype(F32), 'norm_g': norm_g.astype(F32),
        'w_ffn_gate': w_ffn_gate.astype(BF16), 'w_ffn_up': w_ffn_up.astype(BF16), 'w_ffn_down': w_ffn_down.astype(BF16),
        'w_xq': w_xq.astype(BF16), 'w_xo': w_xo.astype(BF16),
        'w_in_even': [_pad_cols(w_in_even[e], e_pad).astype(BF16) for e in range(w_in_even.shape[0])],
        'w_out_even': w_out_even.astype(BF16),
        'w_in_odd': [_pad_cols(w_in_odd_r[o], o_pad).astype(BF16) for o in range(w_in_odd.shape[0])],
        'w_out_odd': w_out_odd.astype(BF16),
        'diff_lambda': diff_lambda, 'g_subln': g_subln, 'b_forget': b_forget,
    }
    b_p, n_mem, _ = mem_prompt.shape
    hx_w = w_xk.shape[-1]

    mem_kv_p, mem_k_out, mem_v_out = [], [], []
    for l in range(depth):
        w_kv = jnp.concatenate([w_xk[l], w_xv[l]], axis=-1).astype(BF16)
        kv = _rms_matmul(mem_prompt.reshape(b_p * n_mem, d_model), norm_g[l, NG_MEM].astype(F32), w_kv)
        mk = kv[:, :hx_w].reshape(b_p, n_mem, hx_w)
        mv = kv[:, hx_w:].reshape(b_p, n_mem, hx_w)
        mem_kv_p.append((mk, mv))
        mem_k_out.append(mk.reshape(b_p, n_mem, hx_w // HEAD_DIM, HEAD_DIM))
        mem_v_out.append(mv.reshape(b_p, n_mem, hx_w // HEAD_DIM, HEAD_DIM))
    y_prompt, ev_p, od_p = _run_trunk(x_prompt.astype(F32), None, None, None, mem_kv_p, prm)

    def pages(c):
        return c.reshape(c.shape[:3] + (-1,))

    past_even = (pages(cache_a_k), pages(cache_a_v), cache_b_k, cache_b_v, cache_b_idx)
    past_odd = (pages(cache_c_k), pages(cache_c_v), cache_c_logf, pages(cache_d_k), pages(cache_d_v))
    b_s = x_sample.shape[0]
    mem_kv_s = [(cache_mem_k[l].reshape(b_s, n_mem, hx_w), cache_mem_v[l].reshape(b_s, n_mem, hx_w))
                for l in range(depth)]
    y_sample, ev_s, od_s = _run_trunk(x_sample.astype(F32), past_even, past_odd, page_table.astype(jnp.int32),
                                      mem_kv_s, prm)

    def stack(rows, i):
        return jnp.stack([r[i] for r in rows])

    out = [y_prompt, y_sample]
    out += [stack(ev_p, i) for i in range(5)] + [stack(od_p, i) for i in range(5)]
    out += [jnp.stack(mem_k_out), jnp.stack(mem_v_out)]
    out += [stack(ev_s, i) for i in range(5)] + [stack(od_s, i) for i in range(5)]
    return tuple(out)
```

```python
import functools
import math

import numpy as np
import jax
import jax.numpy as jnp
from jax import lax
from jax.experimental import pallas as pl
from jax.experimental.pallas import tpu as pltpu

F32 = jnp.float32
BF16 = jnp.bfloat16
NEG_INF = float("-inf")

HEAD_DIM = 128
DH_A = HEAD_DIM // 2
N_IDX_HEADS = 16
IDX_DIM = 64
DSA_TOPK = 256
MOBA_BLOCK = 256
MOBA_TOPK = 3
N_BUCKETS = 32
T5_MAX_EXACT = N_BUCKETS // 2
T5_MAX_DIST = 128
RMS_EPS = 1e-6
NG_FFN1_PRE, NG_FFN1_POST, NG_MIX_PRE, NG_MIX_POST = 0, 1, 2, 3
NG_X_PRE, NG_X_POST, NG_FFN2_PRE, NG_FFN2_POST, NG_MEM = 4, 5, 6, 7, 8

LANES = 128
ROW_TILE = 512
COL_TILE = 512
VMEM_LIMIT = 56 * 1024 * 1024

NT_DIMS = (((1,), (1,)), ((), ()))


def _params(*sem):
    return pltpu.CompilerParams(dimension_semantics=sem, vmem_limit_bytes=VMEM_LIMIT)


def _rms(x, g):
    return x * lax.rsqrt(jnp.mean(x * x, axis=-1, keepdims=True) + RMS_EPS) * g


def _round_up(n, m):
    return (n + m - 1) // m * m


def _bucket_np(dist):
    n = np.maximum(dist, 0)
    n_f = np.maximum(n, 1).astype(np.float32)
    large = T5_MAX_EXACT + (np.log(n_f / np.float32(T5_MAX_EXACT)) / np.float32(math.log(T5_MAX_DIST / T5_MAX_EXACT))
                            * np.float32(N_BUCKETS - T5_MAX_EXACT)).astype(np.int32)
    return np.where(n < T5_MAX_EXACT, n, np.minimum(large, N_BUCKETS - 1)).astype(np.int32)


def _rel_bias(tab, dists):
    onehot = (_bucket_np(dists)[:, None] == np.arange(N_BUCKETS)[None, :]).astype(np.float32)
    rel = jnp.sum(jnp.asarray(onehot)[:, :, None] * tab[None].astype(F32), axis=1)
    return jnp.where(jnp.asarray(dists >= 0)[:, None], rel, NEG_INF).T


def _toeplitz(u, n_rows, n_cols):
    h = u.shape[0]
    period = n_rows + n_cols
    w = jnp.concatenate([u[:, :n_cols][:, ::-1], jnp.zeros((h, 1), u.dtype), u[:, n_cols:][:, ::-1]], axis=1)
    flat = jnp.tile(w, (1, n_rows))[:, :n_rows * (period - 1)]
    return flat.reshape(h, n_rows, period - 1)[:, :, :n_cols]


def _dist_tile(tab, n_rows, n_cols, offset):
    dists = np.arange(n_rows + n_cols - 1) - (n_cols - 1) + offset
    return _toeplitz(_rel_bias(tab, dists), n_rows, n_cols)


def _far_tile(tab, n_rows, n_cols):
    return jnp.broadcast_to(tab[N_BUCKETS - 1].astype(F32)[:, None, None], (tab.shape[1], n_rows, n_cols))


def _prompt_bias(tab, blk):
    assert blk + 1 >= T5_MAX_DIST
    return jnp.stack([_dist_tile(tab, blk, blk, 0), _dist_tile(tab, blk, blk, blk), _far_tile(tab, blk, blk)], axis=1)


def _sample_bias(tab, n_new, page):
    assert page + 1 >= T5_MAX_DIST
    return jnp.stack([_far_tile(tab, n_new, page), _dist_tile(tab, n_new, page, page), _dist_tile(tab, n_new, page, 0)])


def _expand_heads(tile):
    h, r, c = tile.shape
    same = jnp.asarray(np.eye(h, dtype=bool))[:, None, None, :]
    return jnp.where(same, tile[:, :, :, None], NEG_INF).reshape(h * r, c * h)


def _rms_matmul_body(x_ref, g_ref, w_ref, *rest, emit32, emit16):
    outs, h_scr = rest[:-1], rest[-1]

    @pl.when(pl.program_id(1) == 0)
    def _():
        h_scr[...] = _rms(x_ref[...], g_ref[...]).astype(BF16)

    y = jnp.dot(h_scr[...], w_ref[...], preferred_element_type=F32)
    k = 0
    if emit32:
        outs[k][...] = y
        k += 1
    if emit16:
        outs[k][...] = y.astype(BF16)


def _rms_matmul(x, g, w16, *, emit32=True, emit16=False):
    m, d = x.shape
    n = w16.shape[1]
    tm = min(m, ROW_TILE)
    tn = min(n, COL_TILE)
    assert m % tm == 0 and n % tn == 0
    out_shape, out_specs = [], []
    for flag, dt in ((emit32, F32), (emit16, BF16)):
        if flag:
            out_shape.append(jax.ShapeDtypeStruct((m, n), dt))
            out_specs.append(pl.BlockSpec((tm, tn), lambda i, j: (i, j)))
    res = pl.pallas_call(
        functools.partial(_rms_matmul_body, emit32=emit32, emit16=emit16),
        grid=(m // tm, n // tn),
        in_specs=[pl.BlockSpec((tm, d), lambda i, j: (i, 0)),
                  pl.BlockSpec((1, d), lambda i, j: (0, 0)),
                  pl.BlockSpec((d, tn), lambda i, j: (0, j))],
        out_specs=out_specs,
        out_shape=out_shape,
        scratch_shapes=[pltpu.VMEM((tm, d), BF16)],
        compiler_params=_params("parallel", "arbitrary"),
        name="rms_matmul",
    )(x, g.reshape(1, d), w16)
    return res if len(res) > 1 else res[0]


def _ffn_body(x_ref, gpre_ref, gpost_ref, wg_ref, wu_ref, wd_ref, o_ref, h_scr, acc_scr):
    j = pl.program_id(1)

    @pl.when(j == 0)
    def _():
        h_scr[...] = _rms(x_ref[...], gpre_ref[...]).astype(BF16)
        acc_scr[...] = jnp.zeros_like(acc_scr)

    h = h_scr[...]
    gate = jnp.dot(h, wg_ref[...], preferred_element_type=F32)
    up = jnp.dot(h, wu_ref[...], preferred_element_type=F32)
    act = (gate * jax.nn.sigmoid(gate) * up).astype(BF16)
    acc_scr[...] += jnp.dot(act, wd_ref[...], preferred_element_type=F32)

    @pl.when(j == pl.num_programs(1) - 1)
    def _():
        o_ref[...] = x_ref[...] + 0.5 * _rms(acc_scr[...], gpost_ref[...])


def _ffn(x, g_pre, g_post, wg16, wu16, wd16):
    m, d = x.shape
    ff = wg16.shape[1]
    tm = min(m, ROW_TILE)
    tf = min(ff, COL_TILE)
    assert m % tm == 0 and ff % tf == 0
    return pl.pallas_call(
        _ffn_body,
        grid=(m // tm, ff // tf),
        in_specs=[pl.BlockSpec((tm, d), lambda i, j: (i, 0)),
                  pl.BlockSpec((1, d), lambda i, j: (0, 0)),
                  pl.BlockSpec((1, d), lambda i, j: (0, 0)),
                  pl.BlockSpec((d, tf), lambda i, j: (0, j)),
                  pl.BlockSpec((d, tf), lambda i, j: (0, j)),
                  pl.BlockSpec((tf, d), lambda i, j: (j, 0))],
        out_specs=pl.BlockSpec((tm, d), lambda i, j: (i, 0)),
        out_shape=jax.ShapeDtypeStruct((m, d), F32),
        scratch_shapes=[pltpu.VMEM((tm, d), BF16), pltpu.VMEM((tm, d), F32)],
        compiler_params=_params("parallel", "arbitrary"),
        name="ffn",
    )(x, g_pre.reshape(1, d), g_post.reshape(1, d), wg16, wu16, wd16)


def _out_body(*refs, n_parts):
    x_ref, g_ref = refs[0], refs[1]
    o_refs = refs[2:2 + n_parts]
    w_refs = refs[2 + n_parts:2 + 2 * n_parts]
    out_ref = refs[-1]
    y = None
    for o_ref, w_ref in zip(o_refs, w_refs):
        t = jnp.dot(o_ref[...].astype(BF16), w_ref[...], preferred_element_type=F32)
        y = t if y is None else y + t
    out_ref[...] = x_ref[...] + _rms(y, g_ref[...])


def _out_proj(x, g, parts, w16):
    m, d = x.shape
    tm = min(m, ROW_TILE)
    assert m % tm == 0
    ws, off = [], 0
    for p in parts:
        ws.append(w16[off:off + p.shape[1]])
        off += p.shape[1]
    assert off == w16.shape[0]
    in_specs = [pl.BlockSpec((tm, d), lambda i: (i, 0)), pl.BlockSpec((1, d), lambda i: (0, 0))]
    in_specs += [pl.BlockSpec((tm, p.shape[1]), lambda i: (i, 0)) for p in parts]
    in_specs += [pl.BlockSpec(w.shape, lambda i: (0, 0)) for w in ws]
    return pl.pallas_call(
        functools.partial(_out_body, n_parts=len(parts)),
        grid=(m // tm,),
        in_specs=in_specs,
        out_specs=pl.BlockSpec((tm, d), lambda i: (i, 0)),
        out_shape=jax.ShapeDtypeStruct((m, d), F32),
        compiler_params=_params("parallel"),
        name="out_proj",
    )(x, g.reshape(1, d), *parts, *ws)


def _cross_body(q_ref, k_ref, v_ref, o_ref, *, n_heads):
    q = q_ref[...].astype(BF16)
    k = k_ref[...].astype(BF16)
    v = v_ref[...].astype(BF16)
    for h in range(n_heads):
        sl = slice(h * HEAD_DIM, (h + 1) * HEAD_DIM)
        s = lax.dot_general(q[:, sl], k[:, sl], NT_DIMS, preferred_element_type=F32) * HEAD_DIM ** -0.5
        p = jnp.exp(s - jnp.max(s, axis=-1, keepdims=True))
        l = jnp.sum(p, axis=-1, keepdims=True)
        o = jnp.dot(p.astype(BF16), v[:, sl], preferred_element_type=F32) / l
        o_ref[:, sl] = o.astype(o_ref.dtype)


def _cross_attend(q, mem_k, mem_v):
    b, t, w = q.shape
    n_mem = mem_k.shape[1]
    tq = min(t, ROW_TILE)
    assert t % tq == 0
    return pl.pallas_call(
        functools.partial(_cross_body, n_heads=w // HEAD_DIM),
        grid=(b, t // tq),
        in_specs=[pl.BlockSpec((None, tq, w), lambda bi, i: (bi, i, 0)),
                  pl.BlockSpec((None, n_mem, w), lambda bi, i: (bi, 0, 0)),
                  pl.BlockSpec((None, n_mem, w), lambda bi, i: (bi, 0, 0))],
        out_specs=pl.BlockSpec((None, tq, w), lambda bi, i: (bi, i, 0)),
        out_shape=jax.ShapeDtypeStruct((b, t, w), F32),
        compiler_params=_params("parallel", "parallel"),
        name="cross_attend",
    )(q, mem_k, mem_v)


def _softmax_step(s, v16, m, l, acc):
    m_new = jnp.maximum(m, jnp.max(s, axis=-1, keepdims=True))
    m_safe = jnp.where(m_new == NEG_INF, 0.0, m_new)
    p = jnp.exp(s - m_safe)
    alpha = jnp.exp(m - m_safe)
    l = alpha * l + jnp.sum(p, axis=-1, keepdims=True)
    acc = alpha * acc + jnp.dot(p.astype(BF16), v16, preferred_element_type=F32)
    return m_new, l, acc


def _diff_lambda(lam_ref, lam_init):
    lv = lam_ref[...]
    return (jnp.exp(jnp.sum(lv[0:1] * lv[1:2], axis=-1, keepdims=True))
            - jnp.exp(jnp.sum(lv[2:3] * lv[3:4], axis=-1, keepdims=True)) + lam_init)


def _diff_prompt_body(q_ref, k_ref, v_ref, bias_ref, lam_ref, g_ref, o_ref, *, blk, lam_init):
    i = pl.program_id(2)
    q = q_ref[...]
    q0, q1 = q[:, :DH_A], q[:, DH_A:]
    scale = DH_A ** -0.5

    def body(j, carry):
        m0, l0, a0, m1, l1, a1 = carry
        rows = pl.ds(pl.multiple_of(j * blk, blk), blk)
        ks = k_ref[rows, :]
        vs = v_ref[rows, :]
        bt = bias_ref[jnp.minimum(i - j, 2)]
        s0 = lax.dot_general(q0, ks[:, :DH_A], NT_DIMS, preferred_element_type=F32) * scale + bt
        s1 = lax.dot_general(q1, ks[:, DH_A:], NT_DIMS, preferred_element_type=F32) * scale + bt
        m0, l0, a0 = _softmax_step(s0, vs, m0, l0, a0)
        m1, l1, a1 = _softmax_step(s1, vs, m1, l1, a1)
        return m0, l0, a0, m1, l1, a1

    neg = jnp.full((blk, 1), NEG_INF, F32)
    zero = jnp.zeros((blk, 1), F32)
    zacc = jnp.zeros((blk, HEAD_DIM), F32)
    _, l0, a0, _, l1, a1 = lax.fori_loop(0, i + 1, body, (neg, zero, zacc, neg, zero, zacc))
    lam = _diff_lambda(lam_ref, lam_init)
    o = a0 / l0 - lam * (a1 / l1)
    o_ref[...] = (_rms(o, g_ref[...]) * (1.0 - lam_init)).astype(o_ref.dtype)


def _diff_prompt(p16, col_q, col_k, col_v, n_heads, bias, lam_vec, g_subln, lam_init, blk):
    b, t, _ = p16.shape
    cq, ck, cv = col_q // HEAD_DIM, col_k // HEAD_DIM, col_v // HEAD_DIM
    return pl.pallas_call(
        functools.partial(_diff_prompt_body, blk=blk, lam_init=lam_init),
        grid=(b, n_heads, t // blk),
        in_specs=[pl.BlockSpec((None, blk, HEAD_DIM), lambda bi, h, i: (bi, i, cq + h)),
                  pl.BlockSpec((None, t, HEAD_DIM), lambda bi, h, i: (bi, 0, ck + h)),
                  pl.BlockSpec((None, t, HEAD_DIM), lambda bi, h, i: (bi, 0, cv + h)),
                  pl.BlockSpec((None, 3, blk, blk), lambda bi, h, i: (h, 0, 0, 0)),
                  pl.BlockSpec(lam_vec.shape, lambda bi, h, i: (0, 0)),
                  pl.BlockSpec((1, HEAD_DIM), lambda bi, h, i: (0, 0))],
        out_specs=pl.BlockSpec((None, blk, HEAD_DIM), lambda bi, h, i: (bi, i, h)),
        out_shape=jax.ShapeDtypeStruct((b, t, n_heads * HEAD_DIM), BF16),
        compiler_params=_params("parallel", "parallel", "arbitrary"),
        name="diff_prompt",
    )(p16, p16, p16, bias, lam_vec, g_subln.reshape(1, HEAD_DIM))


def _fox_prompt_body(q_ref, k_ref, v_ref, cum_ref, cumt_ref, o_ref, *, blk):
    h = pl.program_id(1)
    i = pl.program_id(2)
    q = q_ref[...]
    cum = cum_ref[...]
    lane = lax.broadcasted_iota(jnp.int32, cum.shape, 1)
    cq = jnp.sum(jnp.where(lane == h, cum, 0.0), axis=-1, keepdims=True)
    row = lax.broadcasted_iota(jnp.int32, (blk, blk), 0)
    col = lax.broadcasted_iota(jnp.int32, (blk, blk), 1)

    def body(j, carry):
        m, l, acc = carry
        start = pl.multiple_of(j * blk, blk)
        ks = k_ref[pl.ds(start, blk), :]
        vs = v_ref[pl.ds(start, blk), :]
        ck = cumt_ref[:, pl.ds(start, blk)]
        s = lax.dot_general(q, ks, NT_DIMS, preferred_element_type=F32) * HEAD_DIM ** -0.5 + (cq - ck)
        s = jnp.where((j < i) | (col <= row), s, NEG_INF)
        return _softmax_step(s, vs, m, l, acc)

    init = (jnp.full((blk, 1), NEG_INF, F32), jnp.zeros((blk, 1), F32), jnp.zeros((blk, HEAD_DIM), F32))
    _, l, acc = lax.fori_loop(0, i + 1, body, init)
    o_ref[...] = (acc / l).astype(o_ref.dtype)


def _fox_prompt(p16, col_q, col_k, col_v, n_heads, cum, blk):
    b, t, _ = p16.shape
    cq, ck, cv = col_q // HEAD_DIM, col_k // HEAD_DIM, col_v // HEAD_DIM
    cum_t = jnp.swapaxes(cum, 1, 2).reshape(b, n_heads, 1, t)
    return pl.pallas_call(
        functools.partial(_fox_prompt_body, blk=blk),
        grid=(b, n_heads, t // blk),
        in_specs=[pl.BlockSpec((None, blk, HEAD_DIM), lambda bi, h, i: (bi, i, cq + h)),
                  pl.BlockSpec((None, t, HEAD_DIM), lambda bi, h, i: (bi, 0, ck + h)),
                  pl.BlockSpec((None, t, HEAD_DIM), lambda bi, h, i: (bi, 0, cv + h)),
                  pl.BlockSpec((None, blk, n_heads), lambda bi, h, i: (bi, i, 0)),
                  pl.BlockSpec((None, None, 1, t), lambda bi, h, i: (bi, h, 0, 0))],
        out_specs=pl.BlockSpec((None, blk, HEAD_DIM), lambda bi, h, i: (bi, i, h)),
        out_shape=jax.ShapeDtypeStruct((b, t, n_heads * HEAD_DIM), BF16),
        compiler_params=_params("parallel", "parallel", "arbitrary"),
        name="fox_prompt",
    )(p16, p16, p16, cum, cum_t)


def _top_blocks_negmask(gate, n_top, limit):
    n_blk = gate.shape[1]
    lane = lax.broadcasted_iota(jnp.int32, gate.shape, 1).astype(F32)
    chosen = jnp.zeros(gate.shape, F32)
    g = gate
    for _ in range(n_top):
        mx = jnp.max(g, axis=-1, keepdims=True)
        idx = jnp.min(jnp.where(g == mx, lane, float(n_blk)), axis=-1, keepdims=True)
        pick = lane == idx
        chosen = jnp.where(pick & (idx < limit), 1.0, chosen)
        g = jnp.where(pick, NEG_INF, g)
    return jnp.where(chosen > 0.0, 0.0, NEG_INF)


def _moba_prompt_body(q_ref, k_ref, v_ref, q32_ref, k32_ref, bias_ref, o_ref, kmean_scr, *, n_blk):
    blk = MOBA_BLOCK
    i = pl.program_id(2)

    @pl.when(i == 0)
    def _():
        kmean_scr[...] = jnp.mean(k32_ref[...].reshape(n_blk, blk, HEAD_DIM), axis=1)

    gate = lax.dot_general(q32_ref[...], kmean_scr[...], NT_DIMS, preferred_element_type=F32,
                           precision=lax.Precision.HIGHEST)
    lane = lax.broadcasted_iota(jnp.int32, gate.shape, 1)
    gate = jnp.where(lane < i, gate, NEG_INF)
    sel = _top_blocks_negmask(gate, min(MOBA_TOPK, n_blk), i.astype(F32))
    q = q_ref[...]

    def scores(j):
        start = pl.multiple_of(j * blk, blk)
        ks = k_ref[pl.ds(start, blk), :]
        s = lax.dot_general(q, ks, NT_DIMS, preferred_element_type=F32) * HEAD_DIM ** -0.5
        return s + bias_ref[jnp.minimum(i - j, 2)], v_ref[pl.ds(start, blk), :]

    s_own, v_own = scores(i)
    init = _softmax_step(s_own, v_own, jnp.full((blk, 1), NEG_INF, F32), jnp.zeros((blk, 1), F32),
                         jnp.zeros((blk, HEAD_DIM), F32))

    def body(j, carry):
        s, vs = scores(j)
        s = s + jnp.min(jnp.where(lane == j, sel, 0.0), axis=-1, keepdims=True)
        return _softmax_step(s, vs, *carry)

    _, l, acc = lax.fori_loop(0, i, body, init)
    o_ref[...] = (acc / l).astype(o_ref.dtype)


def _moba_prompt(p16, p32, col_q, col_k, col_v, n_heads, bias):
    b, t, _ = p16.shape
    blk = MOBA_BLOCK
    assert t % blk == 0
    cq, ck, cv = col_q // HEAD_DIM, col_k // HEAD_DIM, col_v // HEAD_DIM
    return pl.pallas_call(
        functools.partial(_moba_prompt_body, n_blk=t // blk),
        grid=(b, n_heads, t // blk),
        in_specs=[pl.BlockSpec((None, blk, HEAD_DIM), lambda bi, h, i: (bi, i, cq + h)),
                  pl.BlockSpec((None, t, HEAD_DIM), lambda bi, h, i: (bi, 0, ck + h)),
                  pl.BlockSpec((None, t, HEAD_DIM), lambda bi, h, i: (bi, 0, cv + h)),
                  pl.BlockSpec((None, blk, HEAD_DIM), lambda bi, h, i: (bi, i, cq + h)),
                  pl.BlockSpec((None, t, HEAD_DIM), lambda bi, h, i: (bi, 0, ck + h)),
                  pl.BlockSpec((None, 3, blk, blk), lambda bi, h, i: (h, 0, 0, 0))],
        out_specs=pl.BlockSpec((None, blk, HEAD_DIM), lambda bi, h, i: (bi, i, h)),
        out_shape=jax.ShapeDtypeStruct((b, t, n_heads * HEAD_DIM), BF16),
        scratch_shapes=[pltpu.VMEM((t // blk, HEAD_DIM), F32)],
        compiler_params=_params("parallel", "parallel", "arbitrary"),
        name="moba_prompt",
    )(p16, p16, p16, p32, p32, bias)


KEY_SIGN = -2 ** 31
KEY_OF_NEG_INF = -2139095041


def _order_key(score):
    bits = pltpu.bitcast(score, jnp.int32)
    key = jnp.where(bits < 0, bits ^ 0x7FFFFFFF, bits)
    return jnp.where(score == 0.0, 0, key)


def _kth_largest_key(count_ge, n_rows, k):
    def bit_body(b, ans):
        cand = ans | jnp.left_shift(jnp.int32(1), 31 - b)
        return jnp.where(count_ge(cand ^ KEY_SIGN) >= k, cand, ans)

    ans = lax.fori_loop(0, 32, bit_body, jnp.zeros((n_rows, 1), jnp.int32))
    return ans ^ KEY_SIGN


def _dsa_prompt_body(iq_ref, ikw_ref, kidx_ref, qb_ref, kb_ref, vb_ref, bias_ref, o_ref, key_scr, nm_scr, *,
                     tq, n_heads, n_sel):
    i = pl.program_id(1)
    n_chunks = i + 1
    iq = iq_ref[...].reshape(N_IDX_HEADS * tq, IDX_DIM)
    w = ikw_ref[:, IDX_DIM:IDX_DIM + N_IDX_HEADS] * N_IDX_HEADS ** -0.5
    row = lax.broadcasted_iota(jnp.int32, (tq, tq), 0)
    col = lax.broadcasted_iota(jnp.int32, (tq, tq), 1)

    def score_body(j, _):
        kc = kidx_ref[pl.ds(pl.multiple_of(j * tq, tq), tq), :][:, :IDX_DIM]
        rel = jnp.maximum(lax.dot_general(iq, kc, NT_DIMS, preferred_element_type=F32) * IDX_DIM ** -0.5, 0.0)
        rel = rel.reshape(N_IDX_HEADS, tq, tq)
        sc = w[:, 0:1] * rel[0]
        for n in range(1, N_IDX_HEADS):
            sc = sc + w[:, n:n + 1] * rel[n]
        sc = jnp.where((j < i) | (col <= row), sc, NEG_INF)
        key_scr[j] = _order_key(sc)
        return 0

    lax.fori_loop(0, n_chunks, score_body, 0)

    def count(pred):
        def body(j, acc):
            return acc + jnp.where(pred(key_scr[j]), 1.0, 0.0)
        return jnp.sum(lax.fori_loop(0, n_chunks, body, jnp.zeros((tq, tq), F32)), axis=-1, keepdims=True)

    thr = _kth_largest_key(lambda t: count(lambda k: k >= t), tq, n_sel)
    cnt_ge = count(lambda k: k >= thr)
    tie = jnp.max(jnp.where((cnt_ge > n_sel) & (thr > KEY_OF_NEG_INF), 1.0, 0.0)) > 0.0

    @pl.when(jnp.logical_not(tie))
    def _():
        def body(j, _):
            nm_scr[j] = jnp.where(key_scr[j] >= thr, 0.0, NEG_INF)
            return 0
        lax.fori_loop(0, n_chunks, body, 0)

    @pl.when(tie)
    def _():
        allow = n_sel - count(lambda k: k > thr)
        tri = jnp.where(row <= col, 1.0, 0.0).astype(BF16)

        def body(j, before):
            k = key_scr[j]
            eq = jnp.where(k == thr, 1.0, 0.0)
            rank = jnp.dot(eq.astype(BF16), tri, preferred_element_type=F32) + before
            keep = jnp.where(k > thr, 1.0, jnp.where(rank <= allow, eq, 0.0))
            nm_scr[j] = jnp.where(keep > 0.0, 0.0, NEG_INF)
            return before + jnp.sum(eq, axis=-1, keepdims=True)
        lax.fori_loop(0, n_chunks, body, jnp.zeros((tq, 1), F32))

    qb = qb_ref[...]
    qs = jnp.concatenate([qb[:, h * HEAD_DIM:(h + 1) * HEAD_DIM] for h in range(n_heads)], axis=0)

    def att_body(j, carry):
        rows = pl.ds(pl.multiple_of(j * tq, tq), tq)
        s = lax.dot_general(qs, kb_ref[rows, :], NT_DIMS, preferred_element_type=F32) * HEAD_DIM ** -0.5
        s = s + bias_ref[jnp.minimum(i - j, 2)] + jnp.tile(nm_scr[j], (n_heads, 1))
        return _softmax_step(s, vb_ref[rows, :], *carry)

    r = n_heads * tq
    init = (jnp.full((r, 1), NEG_INF, F32), jnp.zeros((r, 1), F32), jnp.zeros((r, HEAD_DIM), F32))
    _, l, acc = lax.fori_loop(0, n_chunks, att_body, init)
    o = acc / l
    for h in range(n_heads):
        o_ref[:, h * HEAD_DIM:(h + 1) * HEAD_DIM] = o[h * tq:(h + 1) * tq].astype(o_ref.dtype)


def _dsa_prompt(p16, p32, iq_t, col_qb, col_kb, col_vb, col_ik, n_heads, bias, n_sel, tq):
    b, t, _ = p16.shape
    qw = n_heads * HEAD_DIM
    assert col_qb % qw == 0
    return pl.pallas_call(
        functools.partial(_dsa_prompt_body, tq=tq, n_heads=n_heads, n_sel=n_sel),
        grid=(b, t // tq),
        in_specs=[pl.BlockSpec((None, N_IDX_HEADS, tq, IDX_DIM), lambda bi, i: (bi, 0, i, 0)),
                  pl.BlockSpec((None, tq, LANES), lambda bi, i: (bi, i, col_ik // LANES)),
                  pl.BlockSpec((None, t, LANES), lambda bi, i: (bi, 0, col_ik // LANES)),
                  pl.BlockSpec((None, tq, qw), lambda bi, i: (bi, i, col_qb // qw)),
                  pl.BlockSpec((None, t, HEAD_DIM), lambda bi, i: (bi, 0, col_kb // HEAD_DIM)),
                  pl.BlockSpec((None, t, HEAD_DIM), lambda bi, i: (bi, 0, col_vb // HEAD_DIM)),
                  pl.BlockSpec(bias.shape, lambda bi, i: (0, 0, 0))],
        out_specs=pl.BlockSpec((None, tq, qw), lambda bi, i: (bi, i, 0)),
        out_shape=jax.ShapeDtypeStruct((b, t, qw), BF16),
        scratch_shapes=[pltpu.VMEM((t // tq, tq, tq), jnp.int32), pltpu.VMEM((t // tq, tq, tq), F32)],
        compiler_params=_params("parallel", "arbitrary"),
        name="dsa_prompt",
    )(iq_t, p32, p16, p16, p16, p16, bias)


def _block_diag_rows(q, dtype):
    b, tn, g, d = q.shape
    eye = jnp.eye(g, dtype=q.dtype)
    return jnp.einsum('btgd,gk->bgtkd', q, eye).reshape(b, g * tn, g * d).astype(dtype)


def _page_specs(layer, n_pages, group, rows, width):
    def spec(g):
        return pl.BlockSpec((None, None, rows, width),
                            lambda bi, p, pt: (layer, pt[bi, jnp.minimum(p * group + g, n_pages - 1)], 0, 0))
    return [spec(g) for g in range(group)]


def _past_bias_index(page_idx, n_pages):
    return jnp.clip(page_idx - (n_pages - 2), 0, 1)


def _paged_call(body, pt, n_steps, operands, in_specs, out_shape, out_spec, scratch, name):
    return pl.pallas_call(
        body,
        grid_spec=pltpu.PrefetchScalarGridSpec(
            num_scalar_prefetch=1, grid=(pt.shape[0], n_steps),
            in_specs=in_specs, out_specs=out_spec, scratch_shapes=scratch),
        out_shape=out_shape,
        compiler_params=_params("parallel", "arbitrary"),
        name=name,
    )(pt, *operands)


def _per_batch(rows, width):
    return pl.BlockSpec((None, rows, width), lambda bi, p, pt: (bi, 0, 0))


def _whole(shape):
    return pl.BlockSpec(shape, lambda bi, p, pt: (0,) * len(shape))


def _rows_page(x, page):
    b, tn, w = x.shape
    return jnp.pad(x, ((0, 0), (0, page - tn), (0, 0))).reshape(b, page * (w // HEAD_DIM), HEAD_DIM)


def _transposed_page(x, page):
    return jnp.pad(jnp.swapaxes(x, 1, 2), ((0, 0), (0, 0), (0, page - x.shape[1])))


def _flash_scratch(rows, width):
    return [pltpu.VMEM((rows, 1), F32), pltpu.VMEM((rows, 1), F32), pltpu.VMEM((rows, width), F32)]


def _flash_init(m_scr, l_scr, acc_scr):
    m_scr[...] = jnp.full(m_scr.shape, NEG_INF, F32)
    l_scr[...] = jnp.zeros(l_scr.shape, F32)
    acc_scr[...] = jnp.zeros(acc_scr.shape, F32)


def _flash_update(s, pv, m_scr, l_scr, acc_scr):
    m = m_scr[...]
    m_new = jnp.maximum(m, jnp.max(s, axis=-1, keepdims=True))
    m_safe = jnp.where(m_new == NEG_INF, 0.0, m_new)
    p = jnp.exp(s - m_safe)
    alpha = jnp.exp(m - m_safe)
    l_scr[...] = alpha * l_scr[...] + jnp.sum(p, axis=-1, keepdims=True)
    acc_scr[...] = alpha * acc_scr[...] + pv(p.astype(BF16))
    m_scr[...] = m_new


def _cat16(refs, axis):
    parts = [r[...].astype(BF16) for r in refs]
    return parts[0] if len(parts) == 1 else jnp.concatenate(parts, axis=axis)


SAMPLE_PAGE_GROUP = 4
SMALL_PAGE_GROUP = 8


def _heads_first(o, b, n_heads, tn):
    return jnp.swapaxes(o.reshape(b, n_heads, tn, HEAD_DIM), 1, 2).reshape(b, tn, n_heads * HEAD_DIM)


def _diff_sample_body(pt_ref, wq_ref, bias_ref, lam_ref, g_ref, kn_ref, vn_ref, *rest,
                      n_pages, group, page, n_new, n_heads, lam_init):
    k_refs, v_refs = rest[:group], rest[group:2 * group]
    o_ref, m_scr, l_scr, acc_scr = rest[2 * group:]
    p = pl.program_id(1)
    rows_h = 2 * n_new

    @pl.when(p == 0)
    def _():
        _flash_init(m_scr, l_scr, acc_scr)

    def step(ks, vs, bias):
        s = jnp.dot(wq_ref[...], _cat16(ks, 1), preferred_element_type=F32) * DH_A ** -0.5 + bias

        def pv(p16):
            outs = []
            for h in range(n_heads):
                vh = [v[pl.ds(h, page, stride=n_heads), :].astype(BF16) for v in vs]
                vh = vh[0] if len(vh) == 1 else jnp.concatenate(vh, axis=0)
                outs.append(jnp.dot(p16[h * rows_h:(h + 1) * rows_h], vh, preferred_element_type=F32))
            return jnp.concatenate(outs, axis=0)

        _flash_update(s, pv, m_scr, l_scr, acc_scr)

    @pl.when(p < n_pages // group)
    def _():
        tiles = [bias_ref[_past_bias_index(p * group + g, n_pages)] for g in range(group)]
        step(k_refs, v_refs, jnp.concatenate(tiles, axis=1))

    @pl.when(p == n_pages // group)
    def _():
        step([kn_ref], [vn_ref], bias_ref[2])
        lam = _diff_lambda(lam_ref, lam_init)
        on = acc_scr[...] / l_scr[...]
        for h in range(n_heads):
            r0 = h * rows_h
            o = on[r0:r0 + n_new] - lam * on[r0 + n_new:r0 + rows_h]
            o_ref[h * n_new:(h + 1) * n_new, :] = _rms(o, g_ref[...]) * (1.0 - lam_init)


def _diff_sample(pt, layer, q, cache_kt, cache_v, k_new, v_new, tab, lam_vec, g_subln, lam_init):
    b, tn, n_heads = q.shape[:3]
    n_pages = pt.shape[1]
    width, page = cache_kt.shape[2:]
    group = min(SAMPLE_PAGE_GROUP, n_pages)
    assert n_pages % group == 0
    wq = _block_diag_rows(q.reshape(b, tn, 2 * n_heads, DH_A), BF16)
    rows = 2 * n_heads * tn
    bias = _sample_bias(tab, tn, page)
    bias = jnp.broadcast_to(bias[:, :, None], (3, n_heads, 2, tn, page)).reshape(3, rows, page)
    out = _paged_call(
        functools.partial(_diff_sample_body, n_pages=n_pages, group=group, page=page, n_new=tn, n_heads=n_heads,
                          lam_init=lam_init),
        pt, n_pages // group + 1,
        (wq, bias, lam_vec, g_subln.reshape(1, HEAD_DIM), _transposed_page(k_new, page), _rows_page(v_new, page))
        + (cache_kt,) * group + (cache_v,) * group,
        [_per_batch(rows, width), _whole(bias.shape), _whole(lam_vec.shape), _whole((1, HEAD_DIM)),
         _per_batch(width, page), _per_batch(page * n_heads, HEAD_DIM)]
        + _page_specs(layer, n_pages, group, width, page) + _page_specs(layer, n_pages, group, page * n_heads, HEAD_DIM),
        jax.ShapeDtypeStruct((b, n_heads * tn, HEAD_DIM), F32), _per_batch(n_heads * tn, HEAD_DIM),
        _flash_scratch(rows, HEAD_DIM), "diff_sample")
    return _heads_first(out, b, n_heads, tn)


def _fox_sample_body(pt_ref, q_ref, cq_ref, ckp_ref, ckn_ref, hm_ref, nm_ref, kn_ref, vn_ref, *rest,
                     n_pages, group):
    k_refs, v_refs = rest[:group], rest[group:2 * group]
    o_ref, m_scr, l_scr, acc_scr = rest[2 * group:]
    p = pl.program_id(1)

    @pl.when(p == 0)
    def _():
        _flash_init(m_scr, l_scr, acc_scr)

    def step(ks, vs, ck, mask):
        s = lax.dot_general(q_ref[...], _cat16(ks, 0), NT_DIMS, preferred_element_type=F32)
        s = s * HEAD_DIM ** -0.5 + (cq_ref[...] - ck) + mask
        _flash_update(s, lambda p16: jnp.dot(p16, _cat16(vs, 0), preferred_element_type=F32), m_scr, l_scr, acc_scr)

    @pl.when(p < n_pages // group)
    def _():
        step(k_refs, v_refs, ckp_ref[...], hm_ref[...])

    @pl.when(p == n_pages // group)
    def _():
        step([kn_ref], [vn_ref], ckn_ref[...], nm_ref[...])
        o_ref[...] = acc_scr[...] / l_scr[...]


def _fox_sample(pt, layer, q, cache_k, cache_v, k_new, v_new, cum_q, cum_past, cum_new):
    b, tn, n_heads = q.shape[:3]
    n_pages = pt.shape[1]
    prow = cache_k.shape[2]
    page = prow // n_heads
    group = min(SAMPLE_PAGE_GROUP, n_pages)
    assert n_pages % group == 0
    rows = n_heads * tn
    q_rows = jnp.swapaxes(q, 1, 2).reshape(b, rows, HEAD_DIM).astype(BF16)
    cq = jnp.swapaxes(cum_q, 1, 2).reshape(b, rows, 1)
    ck_past = cum_past.reshape(b, n_pages // group, 1, group * prow)
    ck_new = jnp.pad(cum_new, ((0, 0), (0, page - tn), (0, 0))).reshape(b, 1, prow)
    head_mask = _expand_heads(jnp.zeros((n_heads, tn, page), F32))
    qi = np.arange(tn)[:, None]
    ci = np.arange(page)[None, :]
    causal = np.broadcast_to(np.where(ci <= qi, 0.0, NEG_INF).astype(np.float32), (n_heads, tn, page))
    new_mask = _expand_heads(jnp.asarray(causal))
    out = _paged_call(
        functools.partial(_fox_sample_body, n_pages=n_pages, group=group),
        pt, n_pages // group + 1,
        (q_rows, cq, ck_past, ck_new, jnp.tile(head_mask, (1, group)), new_mask,
         _rows_page(k_new, page), _rows_page(v_new, page)) + (cache_k,) * group + (cache_v,) * group,
        [_per_batch(rows, HEAD_DIM), _per_batch(rows, 1),
         pl.BlockSpec((None, None, 1, group * prow),
                      lambda bi, p, pt_: (bi, jnp.minimum(p, n_pages // group - 1), 0, 0)),
         _per_batch(1, prow), _whole((rows, group * prow)), _whole((rows, prow)),
         _per_batch(prow, HEAD_DIM), _per_batch(prow, HEAD_DIM)]
        + _page_specs(layer, n_pages, group, prow, HEAD_DIM) + _page_specs(layer, n_pages, group, prow, HEAD_DIM),
        jax.ShapeDtypeStruct((b, rows, HEAD_DIM), F32), _per_batch(rows, HEAD_DIM),
        _flash_scratch(rows, HEAD_DIM), "fox_sample")
    return _heads_first(out, b, n_heads, tn)


def _kmean_body(pt_ref, *refs, n_heads):
    k_refs, o_ref = refs[:-1], refs[-1]
    total = None
    for k_ref in k_refs:
        k = k_ref[...]
        s = jnp.sum(k.reshape(k.shape[0] // n_heads, n_heads, HEAD_DIM), axis=0)
        total = s if total is None else total + s
    o_ref[...] = total * (1.0 / MOBA_BLOCK)


def _moba_gate_body(wq_ref, kmean_ref, o_ref, *, n_blk):
    gate = lax.dot_general(wq_ref[...], kmean_ref[...], NT_DIMS, preferred_element_type=F32,
                           precision=lax.Precision.HIGHEST)
    o_ref[...] = _top_blocks_negmask(gate, min(MOBA_TOPK, n_blk), float(n_blk))


def _moba_sample_body(pt_ref, q_ref, bias_ref, sel_ref, kn_ref, vn_ref, *rest, n_pages, group, pages_per_block):
    k_refs, v_refs = rest[:group], rest[group:2 * group]
    o_ref, m_scr, l_scr, acc_scr = rest[2 * group:]
    p = pl.program_id(1)

    @pl.when(p == 0)
    def _():
        _flash_init(m_scr, l_scr, acc_scr)

    def scores(k_ref, bias):
        s = lax.dot_general(q_ref[...], k_ref[...].astype(BF16), NT_DIMS, preferred_element_type=F32)
        return s * HEAD_DIM ** -0.5 + bias

    def update(s, vs):
        _flash_update(s, lambda p16: jnp.dot(p16, _cat16(vs, 0), preferred_element_type=F32), m_scr, l_scr, acc_scr)

    @pl.when(p < n_pages // group)
    def _():
        sel = sel_ref[...]
        lane = lax.broadcasted_iota(jnp.int32, sel.shape, 1)
        parts = []
        for g in range(group):
            page_idx = p * group + g
            row_mask = jnp.min(jnp.where(lane == page_idx // pages_per_block, sel, 0.0), axis=-1, keepdims=True)
            parts.append(scores(k_refs[g], bias_ref[_past_bias_index(page_idx, n_pages)]) + row_mask)
        update(parts[0] if group == 1 else jnp.concatenate(parts, axis=1), v_refs)

    @pl.when(p == n_pages // group)
    def _():
        update(scores(kn_ref, bias_ref[2]), [vn_ref])
        o_ref[...] = acc_scr[...] / l_scr[...]


def _moba_sample(pt, layer, q, cache_k, cache_v, k_new, v_new, tab):
    b, tn, n_heads = q.shape[:3]
    n_pages = pt.shape[1]
    prow = cache_k.shape[2]
    page = prow // n_heads
    width = n_heads * HEAD_DIM
    assert MOBA_BLOCK % page == 0 and (n_pages * page) % MOBA_BLOCK == 0 and tn < MOBA_BLOCK
    ppb = MOBA_BLOCK // page
    n_blk = n_pages // ppb
    group = min(SAMPLE_PAGE_GROUP, n_pages)
    assert n_pages % group == 0
    rows = n_heads * tn
    kmean = _paged_call(
        functools.partial(_kmean_body, n_heads=n_heads), pt, n_blk, (cache_k,) * ppb,
        _page_specs(layer, n_pages, ppb, prow, HEAD_DIM),
        jax.ShapeDtypeStruct((b, n_blk, n_heads, HEAD_DIM), F32),
        pl.BlockSpec((None, None, n_heads, HEAD_DIM), lambda bi, p, pt_: (bi, p, 0, 0)), [], "moba_kmean")
    sel = pl.pallas_call(
        functools.partial(_moba_gate_body, n_blk=n_blk),
        grid=(b,),
        in_specs=[pl.BlockSpec((None, rows, width), lambda bi: (bi, 0, 0)),
                  pl.BlockSpec((None, n_blk, width), lambda bi: (bi, 0, 0))],
        out_specs=pl.BlockSpec((None, rows, n_blk), lambda bi: (bi, 0, 0)),
        out_shape=jax.ShapeDtypeStruct((b, rows, n_blk), F32),
        compiler_params=_params("parallel"),
        name="moba_gate",
    )(_block_diag_rows(q, F32), kmean.reshape(b, n_blk, width))
    bias = _sample_bias(tab, tn, page)
    bias = jnp.stack([_expand_heads(bias[i]) for i in range(3)])
    q_rows = jnp.swapaxes(q, 1, 2).reshape(b, rows, HEAD_DIM).astype(BF16)
    out = _paged_call(
        functools.partial(_moba_sample_body, n_pages=n_pages, group=group, pages_per_block=ppb),
        pt, n_pages // group + 1,
        (q_rows, bias, sel, _rows_page(k_new, page), _rows_page(v_new, page)) + (cache_k,) * group + (cache_v,) * group,
        [_per_batch(rows, HEAD_DIM), _whole(bias.shape), _per_batch(rows, n_blk),
         _per_batch(prow, HEAD_DIM), _per_batch(prow, HEAD_DIM)]
        + _page_specs(layer, n_pages, group, prow, HEAD_DIM) + _page_specs(layer, n_pages, group, prow, HEAD_DIM),
        jax.ShapeDtypeStruct((b, rows, HEAD_DIM), F32), _per_batch(rows, HEAD_DIM),
        _flash_scratch(rows, HEAD_DIM), "moba_sample")
    return _heads_first(out, b, n_heads, tn)


def _dsa_score_body(pt_ref, iq_ref, w_ref, mask_ref, kn_ref, *rest, n_pages, group, n_new):
    k_refs, o_ref = rest[:group], rest[group]
    p = pl.program_id(1)

    def score(kt16):
        rel = jnp.dot(iq_ref[...], kt16, preferred_element_type=F32)
        rel = jnp.maximum(rel * IDX_DIM ** -0.5, 0.0) * w_ref[...]
        return jnp.sum(rel.reshape(N_IDX_HEADS, n_new, rel.shape[1]), axis=0)

    @pl.when(p < n_pages // group)
    def _():
        o_ref[...] = score(_cat16(k_refs, 1))

    @pl.when(p == n_pages // group)
    def _():
        page = kn_ref.shape[1]
        o_ref[...] = jnp.full(o_ref.shape, NEG_INF, F32)
        o_ref[:, :page] = score(kn_ref[...].astype(BF16)) + mask_ref[...]


def _dsa_select_body(sc_ref, o_ref, key_scr, *, n_new, n_sel, chunk):
    width = sc_ref.shape[1]
    key_scr[...] = _order_key(sc_ref[...])

    def count(pred):
        return jnp.sum(jnp.where(pred(key_scr[...]), 1.0, 0.0), axis=-1, keepdims=True)

    thr = _kth_largest_key(lambda t: count(lambda k: k >= t), n_new, n_sel)
    cnt_ge = count(lambda k: k >= thr)
    tie = jnp.max(jnp.where((cnt_ge > n_sel) & (thr > KEY_OF_NEG_INF), 1.0, 0.0)) > 0.0

    @pl.when(jnp.logical_not(tie))
    def _():
        o_ref[...] = jnp.where(key_scr[...] >= thr, 0.0, NEG_INF)

    @pl.when(tie)
    def _():
        allow = n_sel - count(lambda k: k > thr)
        r = lax.broadcasted_iota(jnp.int32, (chunk, chunk), 0)
        c = lax.broadcasted_iota(jnp.int32, (chunk, chunk), 1)
        tri = jnp.where(r <= c, 1.0, 0.0).astype(BF16)

        def body(j, before):
            cols = pl.ds(pl.multiple_of(j * chunk, chunk), chunk)
            k = key_scr[:, cols]
            eq = jnp.where(k == thr, 1.0, 0.0)
            rank = jnp.dot(eq.astype(BF16), tri, preferred_element_type=F32) + before
            keep = jnp.where(k > thr, 1.0, jnp.where(rank <= allow, eq, 0.0))
            o_ref[:, cols] = jnp.where(keep > 0.0, 0.0, NEG_INF)
            return before + jnp.sum(eq, axis=-1, keepdims=True)
        lax.fori_loop(0, width // chunk, body, jnp.zeros((n_new, 1), F32))


def _dsa_sample_body(pt_ref, q_ref, bias_ref, nm_ref, kn_ref, vn_ref, *rest, n_pages, group, n_heads):
    k_refs, v_refs = rest[:group], rest[group:2 * group]
    o_ref, m_scr, l_scr, acc_scr = rest[2 * group:]
    p = pl.program_id(1)

    @pl.when(p == 0)
    def _():
        _flash_init(m_scr, l_scr, acc_scr)

    def step(ks, vs, bias, nm):
        s = lax.dot_general(q_ref[...], _cat16(ks, 0), NT_DIMS, preferred_element_type=F32)
        s = s * HEAD_DIM ** -0.5 + bias + jnp.tile(nm, (n_heads, 1))
        _flash_update(s, lambda p16: jnp.dot(p16, _cat16(vs, 0), preferred_element_type=F32), m_scr, l_scr, acc_scr)

    @pl.when(p < n_pages // group)
    def _():
        tiles = [bias_ref[_past_bias_index(p * group + g, n_pages)] for g in range(group)]
        step(k_refs, v_refs, jnp.concatenate(tiles, axis=1), nm_ref[...])

    @pl.when(p == n_pages // group)
    def _():
        page = kn_ref.shape[0]
        step([kn_ref], [vn_ref], bias_ref[2], nm_ref[:, :page])
        o_ref[...] = acc_scr[...] / l_scr[...]


def _dsa_sample(pt, layer, qb, iq, iw, cache_k, cache_v, cache_idx_t, k_new, v_new, ik_new, tab):
    b, tn, n_heads = qb.shape[:3]
    n_pages = pt.shape[1]
    page = cache_k.shape[2]
    n_sel = min(DSA_TOPK, (n_pages * page + tn) // 4)
    group = min(SMALL_PAGE_GROUP, n_pages)
    assert n_pages % group == 0
    n_steps = n_pages // group + 1
    width = n_steps * group * page
    pad = ((0, 0), (0, page - tn), (0, 0))
    qi = np.arange(tn)[:, None]
    ci = np.arange(page)[None, :]
    new_mask = jnp.asarray(np.where(ci <= qi, 0.0, NEG_INF).astype(np.float32))
    n_iq = N_IDX_HEADS * tn
    iq_rows = jnp.swapaxes(iq, 1, 2).reshape(b, n_iq, IDX_DIM).astype(BF16)
    w_rows = (jnp.swapaxes(iw, 1, 2).astype(F32) * N_IDX_HEADS ** -0.5).reshape(b, n_iq, 1)
    step_cols = pl.BlockSpec((None, tn, group * page), lambda bi, p, pt_: (bi, 0, p))
    scores = _paged_call(
        functools.partial(_dsa_score_body, n_pages=n_pages, group=group, n_new=tn),
        pt, n_steps,
        (iq_rows, w_rows, new_mask, _transposed_page(ik_new, page)) + (cache_idx_t,) * group,
        [_per_batch(n_iq, IDX_DIM), _per_batch(n_iq, 1), _whole(new_mask.shape), _per_batch(IDX_DIM, page)]
        + _page_specs(layer, n_pages, group, IDX_DIM, page),
        jax.ShapeDtypeStruct((b, tn, width), F32), step_cols, [], "dsa_score")
    negmask = pl.pallas_call(
        functools.partial(_dsa_select_body, n_new=tn, n_sel=n_sel, chunk=page),
        grid=(b,),
        in_specs=[pl.BlockSpec((None, tn, width), lambda bi: (bi, 0, 0))],
        out_specs=pl.BlockSpec((None, tn, width), lambda bi: (bi, 0, 0)),
        out_shape=jax.ShapeDtypeStruct((b, tn, width), F32),
        scratch_shapes=[pltpu.VMEM((tn, width), jnp.int32)],
        compiler_params=_params("parallel"),
        name="dsa_select",
    )(scores)
    rows = n_heads * tn
    q_rows = jnp.swapaxes(qb, 1, 2).reshape(b, rows, HEAD_DIM).astype(BF16)
    bias = _sample_bias(tab, tn, page).reshape(3, rows, page)
    out = _paged_call(
        functools.partial(_dsa_sample_body, n_pages=n_pages, group=group, n_heads=n_heads),
        pt, n_steps,
        (q_rows, bias, negmask, jnp.pad(k_new, pad), jnp.pad(v_new, pad)) + (cache_k,) * group + (cache_v,) * group,
        [_per_batch(rows, HEAD_DIM), _whole(bias.shape), step_cols, _per_batch(page, HEAD_DIM),
         _per_batch(page, HEAD_DIM)]
        + _page_specs(layer, n_pages, group, page, HEAD_DIM) + _page_specs(layer, n_pages, group, page, HEAD_DIM),
        jax.ShapeDtypeStruct((b, rows, HEAD_DIM), F32), _per_batch(rows, HEAD_DIM),
        _flash_scratch(rows, HEAD_DIM), "dsa_sample")
    return _heads_first(out, b, n_heads, tn)


def _pad_cols(w, n):
    return jnp.pad(w, ((0, 0), (0, n - w.shape[1])))


def _mixer_even(h_in, g, w_in16, past, pt, layer_e, lam_vec, g_subln, lam_init, tab, b, t):
    d = h_in.shape[1]
    n_a = n_b = (d // HEAD_DIM) // 2
    wa, wb = n_a * HEAD_DIM, n_b * HEAD_DIM
    c_qa, c_ka, c_va, c_qb = 0, wa, 2 * wa, 3 * wa
    c_kb = c_qb + wb
    c_vb = c_kb + HEAD_DIM
    c_iq = c_vb + HEAD_DIM
    c_ik = c_iq + N_IDX_HEADS * IDX_DIM
    c_iw = c_ik + IDX_DIM
    p32, p16 = _rms_matmul(h_in, g, w_in16, emit32=True, emit16=True)
    npad = p32.shape[1]
    p32 = p32.reshape(b, t, npad)
    p16 = p16.reshape(b, t, npad)
    ka = p32[..., c_ka:c_ka + wa]
    va = p32[..., c_va:c_va + wa]
    kb = p32[..., c_kb:c_kb + HEAD_DIM]
    vb = p32[..., c_vb:c_vb + HEAD_DIM]
    ik = p32[..., c_ik:c_ik + IDX_DIM]
    rows = (ka.reshape(b, t, n_a, 2, DH_A), va.reshape(b, t, n_a, HEAD_DIM), kb, vb, ik)
    tab_a, tab_b = tab[:, :n_a], tab[:, n_a:]
    if past is None:
        blk = min(t, 256)
        assert t % blk == 0
        o_a = _diff_prompt(p16, c_qa, c_ka, c_va, n_a, _prompt_bias(tab_a, blk), lam_vec, g_subln, lam_init, blk)
        tq = min(t, 128)
        bias_b = jnp.swapaxes(_prompt_bias(tab_b, tq), 0, 1).reshape(3, n_b * tq, tq)
        iq_t = jnp.swapaxes(p16[..., c_iq:c_ik].reshape(b, t, N_IDX_HEADS, IDX_DIM), 1, 2)
        o_b = _dsa_prompt(p16, p32, iq_t, c_qb, c_kb, c_vb, c_ik, n_b, bias_b, min(DSA_TOPK, t // 4), tq)
    else:
        cache_a_k, cache_a_v, cache_b_k, cache_b_v, cache_b_idx = past
        qa = p32[..., c_qa:c_qa + wa].reshape(b, t, n_a, 2, DH_A)
        o_a = _diff_sample(pt, layer_e, qa, cache_a_k, cache_a_v, ka, va, tab_a, lam_vec, g_subln, lam_init)
        qb = p32[..., c_qb:c_qb + wb].reshape(b, t, n_b, HEAD_DIM)
        iq = p32[..., c_iq:c_ik].reshape(b, t, N_IDX_HEADS, IDX_DIM)
        iw = p32[..., c_iw:c_iw + N_IDX_HEADS]
        o_b = _dsa_sample(pt, layer_e, qb, iq, iw, cache_b_k, cache_b_v, cache_b_idx, kb, vb, ik, tab_b)
    return (o_a.reshape(b * t, wa), o_b.reshape(b * t, wb)), rows


def _mixer_odd(h_in, g, w_in16, past, pt, layer_o, b_forget, tab, b, t):
    d = h_in.shape[1]
    n_c = n_d = (d // HEAD_DIM) // 2
    wc, wd = n_c * HEAD_DIM, n_d * HEAD_DIM
    c_qc, c_kc, c_vc, c_qd = 0, wc, 2 * wc, 3 * wc
    c_kd = c_qd + wd
    c_vd = c_kd + wd
    c_fc = c_vd + wd
    p32, p16 = _rms_matmul(h_in, g, w_in16, emit32=True, emit16=True)
    npad = p32.shape[1]
    p32 = p32.reshape(b, t, npad)
    p16 = p16.reshape(b, t, npad)
    kc = p32[..., c_kc:c_kc + wc]
    vc = p32[..., c_vc:c_vc + wc]
    kd = p32[..., c_kd:c_kd + wd]
    vd = p32[..., c_vd:c_vd + wd]
    log_f = jax.nn.log_sigmoid(p32[..., c_fc:c_fc + n_c] + b_forget.astype(F32))
    rows = (kc.reshape(b, t, n_c, HEAD_DIM), vc.reshape(b, t, n_c, HEAD_DIM), log_f,
            kd.reshape(b, t, n_d, HEAD_DIM), vd.reshape(b, t, n_d, HEAD_DIM))
    tab_d = tab[:, n_c:]
    if past is None:
        blk = min(t, 256)
        o_c = _fox_prompt(p16, c_qc, c_kc, c_vc, n_c, jnp.cumsum(log_f, axis=1), blk)
        o_d = _moba_prompt(p16, p32, c_qd, c_kd, c_vd, n_d, _prompt_bias(tab_d, MOBA_BLOCK))
    else:
        cache_c_k, cache_c_v, cache_c_logf, cache_d_k, cache_d_v = past
        n_pages = pt.shape[1]
        page = cache_c_logf.shape[2]
        logf_past = cache_c_logf[layer_o][pt].reshape(b, n_pages * page, n_c)
        cum = jnp.cumsum(jnp.concatenate([logf_past, log_f], axis=1).astype(F32), axis=1)
        cum_q = cum[:, n_pages * page:]
        qc = p32[..., c_qc:c_qc + wc].reshape(b, t, n_c, HEAD_DIM)
        o_c = _fox_sample(pt, layer_o, qc, cache_c_k, cache_c_v, kc, vc, cum_q, cum[:, :n_pages * page], cum_q)
        qd = p32[..., c_qd:c_qd + wd].reshape(b, t, n_d, HEAD_DIM)
        o_d = _moba_sample(pt, layer_o, qd, cache_d_k, cache_d_v, kd, vd, tab_d)
    return (o_c.reshape(b * t, wc), o_d.reshape(b * t, wd)), rows


def _run_trunk(x, past_even, past_odd, pt, mem_kv, prm):
    b, t, d = x.shape
    x = x.reshape(b * t, d)
    depth = prm['norm_g'].shape[0]
    rows_even, rows_odd = [], []
    for layer in range(depth):
        g = prm['norm_g'][layer]
        wg, wu, wd = prm['w_ffn_gate'][layer], prm['w_ffn_up'][layer], prm['w_ffn_down'][layer]
        x = _ffn(x, g[NG_FFN1_PRE], g[NG_FFN1_POST], wg[0], wu[0], wd[0])
        if layer % 2 == 0:
            e = layer // 2
            lam_init = 0.8 - 0.6 * math.exp(-0.3 * layer)
            parts, rows = _mixer_even(x, g[NG_MIX_PRE], prm['w_in_even'][e], past_even, pt, e,
                                      prm['diff_lambda'][e].astype(F32), prm['g_subln'][e].astype(F32), lam_init,
                                      prm['t5_table'], b, t)
            rows_even.append(rows)
            x = _out_proj(x, g[NG_MIX_POST], parts, prm['w_out_even'][e])
        else:
            o = layer // 2
            parts, rows = _mixer_odd(x, g[NG_MIX_PRE], prm['w_in_odd'][o], past_odd, pt, o,
                                     prm['b_forget'][o], prm['t5_table'], b, t)
            rows_odd.append(rows)
            x = _out_proj(x, g[NG_MIX_POST], parts, prm['w_out_odd'][o])
        mk, mv = mem_kv[layer]
        q = _rms_matmul(x, g[NG_X_PRE], prm['w_xq'][layer])
        o_x = _cross_attend(q.reshape(b, t, -1), mk, mv)
        x = _out_proj(x, g[NG_X_POST], [o_x.reshape(b * t, -1)], prm['w_xo'][layer])
        x = _ffn(x, g[NG_FFN2_PRE], g[NG_FFN2_POST], wg[1], wu[1], wd[1])
    return x.reshape(b, t, d), rows_even, rows_odd


def kernel(x_prompt, x_sample, cache_a_k, cache_a_v, cache_b_k, cache_b_v, cache_b_idx, cache_c_k, cache_c_v, cache_c_logf, cache_d_k, cache_d_v, cache_mem_k, cache_mem_v, page_table, mem_prompt, t5_table, norm_g, w_ffn_gate, w_ffn_up, w_ffn_down, w_xq, w_xk, w_xv, w_xo, w_in_even, w_out_even, diff_lambda, g_subln, w_in_odd, w_out_odd, b_forget):
    depth = norm_g.shape[0]
    d_model = x_prompt.shape[-1]
    n_c = (d_model // HEAD_DIM) // 2
    e_pad = _round_up(w_in_even.shape[-1], COL_TILE)
    o_pad = _round_up(w_in_odd.shape[-1], COL_TILE)
    c_fc = 3 * n_c * HEAD_DIM
    w_in_odd_r = jnp.concatenate([w_in_odd[..., :c_fc], w_in_odd[..., c_fc + n_c:], w_in_odd[..., c_fc:c_fc + n_c]],
                                 axis=-1)
    prm = {
        't5_table': t5_table.astype(F32), 'norm_g': norm_g.astype(F32),
        'w_ffn_gate': w_ffn_gate.astype(BF16), 'w_ffn_up': w_ffn_up.astype(BF16), 'w_ffn_down': w_ffn_down.astype(BF16),
        'w_xq': w_xq.astype(BF16), 'w_xo': w_xo.astype(BF16),
        'w_in_even': [_pad_cols(w_in_even[e], e_pad).astype(BF16) for e in range(w_in_even.shape[0])],
        'w_out_even': w_out_even.astype(BF16),
        'w_in_odd': [_pad_cols(w_in_odd_r[o], o_pad).astype(BF16) for o in range(w_in_odd.shape[0])],
        'w_out_odd': w_out_odd.astype(BF16),
        'diff_lambda': diff_lambda, 'g_subln': g_subln, 'b_forget': b_forget,
    }
    b_p, n_mem, _ = mem_prompt.shape
    hx_w = w_xk.shape[-1]

    mem_kv_p, mem_k_out, mem_v_out = [], [], []
    for l in range(depth):
        w_kv = jnp.concatenate([w_xk[l], w_xv[l]], axis=-1).astype(BF16)
        kv = _rms_matmul(mem_prompt.reshape(b_p * n_mem, d_model), norm_g[l, NG_MEM].astype(F32), w_kv)
        mk = kv[:, :hx_w].reshape(b_p, n_mem, hx_w)
        mv = kv[:, hx_w:].reshape(b_p, n_mem, hx_w)
        mem_kv_p.append((mk, mv))
        mem_k_out.append(mk.reshape(b_p, n_mem, hx_w // HEAD_DIM, HEAD_DIM))
        mem_v_out.append(mv.reshape(b_p, n_mem, hx_w // HEAD_DIM, HEAD_DIM))
    y_prompt, ev_p, od_p = _run_trunk(x_prompt.astype(F32), None, None, None, mem_kv_p, prm)

    def rows_pages(c):
        return c.reshape(c.shape[:2] + (c.shape[2] * c.shape[3], c.shape[4]))

    def transposed_pages(c):
        c = c.reshape(c.shape[:3] + (-1,))
        return jnp.swapaxes(c, 2, 3)

    past_even = (transposed_pages(cache_a_k), rows_pages(cache_a_v), cache_b_k, cache_b_v,
                 transposed_pages(cache_b_idx))
    past_odd = (rows_pages(cache_c_k), rows_pages(cache_c_v), cache_c_logf, rows_pages(cache_d_k),
                rows_pages(cache_d_v))
    b_s = x_sample.shape[0]
    mem_kv_s = [(cache_mem_k[l].reshape(b_s, n_mem, hx_w), cache_mem_v[l].reshape(b_s, n_mem, hx_w))
                for l in range(depth)]
    y_sample, ev_s, od_s = _run_trunk(x_sample.astype(F32), past_even, past_odd, page_table.astype(jnp.int32),
                                      mem_kv_s, prm)

    def stack(rows, i):
        return jnp.stack([r[i] for r in rows])

    out = [y_prompt, y_sample]
    out += [stack(ev_p, i) for i in range(5)] + [stack(od_p, i) for i in range(5)]
    out += [jnp.stack(mem_k_out), jnp.stack(mem_v_out)]
    out += [stack(ev_s, i) for i in range(5)] + [stack(od_s, i) for i in range(5)]
    return tuple(out)
```

```python
import functools
import math

import numpy as np
import jax
import jax.numpy as jnp
from jax import lax
from jax.experimental import pallas as pl
from jax.experimental.pallas import tpu as pltpu

F32 = jnp.float32
BF16 = jnp.bfloat16
NEG_INF = float("-inf")

HEAD_DIM = 128
DH_A = HEAD_DIM // 2
N_IDX_HEADS = 16
IDX_DIM = 64
DSA_TOPK = 256
MOBA_BLOCK = 256
MOBA_TOPK = 3
N_BUCKETS = 32
T5_MAX_EXACT = N_BUCKETS // 2
T5_MAX_DIST = 128
RMS_EPS = 1e-6
NG_FFN1_PRE, NG_FFN1_POST, NG_MIX_PRE, NG_MIX_POST = 0, 1, 2, 3
NG_X_PRE, NG_X_POST, NG_FFN2_PRE, NG_FFN2_POST, NG_MEM = 4, 5, 6, 7, 8

LANES = 128
ROW_TILE = 512
COL_TILE = 512
VMEM_LIMIT = 56 * 1024 * 1024

NT_DIMS = (((1,), (1,)), ((), ()))


def _params(*sem):
    return pltpu.CompilerParams(dimension_semantics=sem, vmem_limit_bytes=VMEM_LIMIT)


def _rms(x, g):
    return x * lax.rsqrt(jnp.mean(x * x, axis=-1, keepdims=True) + RMS_EPS) * g


def _round_up(n, m):
    return (n + m - 1) // m * m


def _bucket_np(dist):
    n = np.maximum(dist, 0)
    n_f = np.maximum(n, 1).astype(np.float32)
    large = T5_MAX_EXACT + (np.log(n_f / np.float32(T5_MAX_EXACT)) / np.float32(math.log(T5_MAX_DIST / T5_MAX_EXACT))
                            * np.float32(N_BUCKETS - T5_MAX_EXACT)).astype(np.int32)
    return np.where(n < T5_MAX_EXACT, n, np.minimum(large, N_BUCKETS - 1)).astype(np.int32)


def _rel_bias(tab, dists):
    onehot = (_bucket_np(dists)[:, None] == np.arange(N_BUCKETS)[None, :]).astype(np.float32)
    rel = jnp.sum(jnp.asarray(onehot)[:, :, None] * tab[None].astype(F32), axis=1)
    return jnp.where(jnp.asarray(dists >= 0)[:, None], rel, NEG_INF).T


def _toeplitz(u, n_rows, n_cols):
    h = u.shape[0]
    period = n_rows + n_cols
    w = jnp.concatenate([u[:, :n_cols][:, ::-1], jnp.zeros((h, 1), u.dtype), u[:, n_cols:][:, ::-1]], axis=1)
    flat = jnp.tile(w, (1, n_rows))[:, :n_rows * (period - 1)]
    return flat.reshape(h, n_rows, period - 1)[:, :, :n_cols]


def _dist_tile(tab, n_rows, n_cols, offset):
    dists = np.arange(n_rows + n_cols - 1) - (n_cols - 1) + offset
    return _toeplitz(_rel_bias(tab, dists), n_rows, n_cols)


def _far_tile(tab, n_rows, n_cols):
    return jnp.broadcast_to(tab[N_BUCKETS - 1].astype(F32)[:, None, None], (tab.shape[1], n_rows, n_cols))


def _prompt_bias(tab, blk):
    assert blk + 1 >= T5_MAX_DIST
    return jnp.stack([_dist_tile(tab, blk, blk, 0), _dist_tile(tab, blk, blk, blk), _far_tile(tab, blk, blk)], axis=1)


def _sample_bias(tab, n_new, page):
    assert page + 1 >= T5_MAX_DIST
    return jnp.stack([_far_tile(tab, n_new, page), _dist_tile(tab, n_new, page, page), _dist_tile(tab, n_new, page, 0)])


def _expand_heads(tile):
    h, r, c = tile.shape
    same = jnp.asarray(np.eye(h, dtype=bool))[:, None, None, :]
    return jnp.where(same, tile[:, :, :, None], NEG_INF).reshape(h * r, c * h)


def _rms_matmul_body(x_ref, g_ref, w_ref, *rest, emit32, emit16):
    outs, h_scr = rest[:-1], rest[-1]

    @pl.when(pl.program_id(1) == 0)
    def _():
        h_scr[...] = _rms(x_ref[...], g_ref[...]).astype(BF16)

    y = jnp.dot(h_scr[...], w_ref[...], preferred_element_type=F32)
    k = 0
    if emit32:
        outs[k][...] = y
        k += 1
    if emit16:
        outs[k][...] = y.astype(BF16)


def _rms_matmul(x, g, w16, *, emit32=True, emit16=False):
    m, d = x.shape
    n = w16.shape[1]
    tm = min(m, 2 * ROW_TILE)
    tn = min(n, COL_TILE)
    assert m % tm == 0 and n % tn == 0
    out_shape, out_specs = [], []
    for flag, dt in ((emit32, F32), (emit16, BF16)):
        if flag:
            out_shape.append(jax.ShapeDtypeStruct((m, n), dt))
            out_specs.append(pl.BlockSpec((tm, tn), lambda i, j: (i, j)))
    res = pl.pallas_call(
        functools.partial(_rms_matmul_body, emit32=emit32, emit16=emit16),
        grid=(m // tm, n // tn),
        in_specs=[pl.BlockSpec((tm, d), lambda i, j: (i, 0)),
                  pl.BlockSpec((1, d), lambda i, j: (0, 0)),
                  pl.BlockSpec((d, tn), lambda i, j: (0, j))],
        out_specs=out_specs,
        out_shape=out_shape,
        scratch_shapes=[pltpu.VMEM((tm, d), BF16)],
        compiler_params=_params("parallel", "arbitrary"),
        name="rms_matmul",
    )(x, g.reshape(1, d), w16)
    return res if len(res) > 1 else res[0]


def _ffn_body(x_ref, gpre_ref, gpost_ref, wg_ref, wu_ref, wd_ref, o_ref, h_scr, acc_scr):
    j = pl.program_id(1)

    @pl.when(j == 0)
    def _():
        h_scr[...] = _rms(x_ref[...], gpre_ref[...]).astype(BF16)
        acc_scr[...] = jnp.zeros_like(acc_scr)

    h = h_scr[...]
    gate = jnp.dot(h, wg_ref[...], preferred_element_type=F32)
    up = jnp.dot(h, wu_ref[...], preferred_element_type=F32)
    act = (gate * jax.nn.sigmoid(gate) * up).astype(BF16)
    acc_scr[...] += jnp.dot(act, wd_ref[...], preferred_element_type=F32)

    @pl.when(j == pl.num_programs(1) - 1)
    def _():
        o_ref[...] = x_ref[...] + 0.5 * _rms(acc_scr[...], gpost_ref[...])


def _ffn(x, g_pre, g_post, wg16, wu16, wd16):
    m, d = x.shape
    ff = wg16.shape[1]
    tm = min(m, ROW_TILE)
    tf = min(ff, COL_TILE)
    assert m % tm == 0 and ff % tf == 0
    return pl.pallas_call(
        _ffn_body,
        grid=(m // tm, ff // tf),
        in_specs=[pl.BlockSpec((tm, d), lambda i, j: (i, 0)),
                  pl.BlockSpec((1, d), lambda i, j: (0, 0)),
                  pl.BlockSpec((1, d), lambda i, j: (0, 0)),
                  pl.BlockSpec((d, tf), lambda i, j: (0, j)),
                  pl.BlockSpec((d, tf), lambda i, j: (0, j)),
                  pl.BlockSpec((tf, d), lambda i, j: (j, 0))],
        out_specs=pl.BlockSpec((tm, d), lambda i, j: (i, 0)),
        out_shape=jax.ShapeDtypeStruct((m, d), F32),
        scratch_shapes=[pltpu.VMEM((tm, d), BF16), pltpu.VMEM((tm, d), F32)],
        compiler_params=_params("parallel", "arbitrary"),
        name="ffn",
    )(x, g_pre.reshape(1, d), g_post.reshape(1, d), wg16, wu16, wd16)


def _out_body(*refs, n_parts):
    x_ref, g_ref = refs[0], refs[1]
    o_refs = refs[2:2 + n_parts]
    w_refs = refs[2 + n_parts:2 + 2 * n_parts]
    out_ref = refs[-1]
    y = None
    for o_ref, w_ref in zip(o_refs, w_refs):
        t = jnp.dot(o_ref[...].astype(BF16), w_ref[...], preferred_element_type=F32)
        y = t if y is None else y + t
    out_ref[...] = x_ref[...] + _rms(y, g_ref[...])


def _out_proj(x, g, parts, w16):
    m, d = x.shape
    tm = min(m, ROW_TILE)
    assert m % tm == 0
    ws, off = [], 0
    for p in parts:
        ws.append(w16[off:off + p.shape[1]])
        off += p.shape[1]
    assert off == w16.shape[0]
    in_specs = [pl.BlockSpec((tm, d), lambda i: (i, 0)), pl.BlockSpec((1, d), lambda i: (0, 0))]
    in_specs += [pl.BlockSpec((tm, p.shape[1]), lambda i: (i, 0)) for p in parts]
    in_specs += [pl.BlockSpec(w.shape, lambda i: (0, 0)) for w in ws]
    return pl.pallas_call(
        functools.partial(_out_body, n_parts=len(parts)),
        grid=(m // tm,),
        in_specs=in_specs,
        out_specs=pl.BlockSpec((tm, d), lambda i: (i, 0)),
        out_shape=jax.ShapeDtypeStruct((m, d), F32),
        compiler_params=_params("parallel"),
        name="out_proj",
    )(x, g.reshape(1, d), *parts, *ws)


def _cross_body(q_ref, k_ref, v_ref, o_ref, *, n_heads):
    q = q_ref[...].astype(BF16)
    k = k_ref[...].astype(BF16)
    v = v_ref[...].astype(BF16)
    for h in range(n_heads):
        sl = slice(h * HEAD_DIM, (h + 1) * HEAD_DIM)
        s = lax.dot_general(q[:, sl], k[:, sl], NT_DIMS, preferred_element_type=F32) * HEAD_DIM ** -0.5
        p = jnp.exp(s - jnp.max(s, axis=-1, keepdims=True))
        l = jnp.sum(p, axis=-1, keepdims=True)
        o = jnp.dot(p.astype(BF16), v[:, sl], preferred_element_type=F32) / l
        o_ref[:, sl] = o.astype(o_ref.dtype)


def _cross_attend(q, mem_k, mem_v):
    b, t, w = q.shape
    n_mem = mem_k.shape[1]
    tq = min(t, ROW_TILE)
    assert t % tq == 0
    return pl.pallas_call(
        functools.partial(_cross_body, n_heads=w // HEAD_DIM),
        grid=(b, t // tq),
        in_specs=[pl.BlockSpec((None, tq, w), lambda bi, i: (bi, i, 0)),
                  pl.BlockSpec((None, n_mem, w), lambda bi, i: (bi, 0, 0)),
                  pl.BlockSpec((None, n_mem, w), lambda bi, i: (bi, 0, 0))],
        out_specs=pl.BlockSpec((None, tq, w), lambda bi, i: (bi, i, 0)),
        out_shape=jax.ShapeDtypeStruct((b, t, w), F32),
        compiler_params=_params("parallel", "parallel"),
        name="cross_attend",
    )(q, mem_k, mem_v)


def _softmax_step(s, v16, m, l, acc):
    m_new = jnp.maximum(m, jnp.max(s, axis=-1, keepdims=True))
    m_safe = jnp.where(m_new == NEG_INF, 0.0, m_new)
    p = jnp.exp(s - m_safe)
    alpha = jnp.exp(m - m_safe)
    l = alpha * l + jnp.sum(p, axis=-1, keepdims=True)
    acc = alpha * acc + jnp.dot(p.astype(BF16), v16, preferred_element_type=F32)
    return m_new, l, acc


def _diff_lambda(lam_ref, lam_init):
    lv = lam_ref[...]
    return (jnp.exp(jnp.sum(lv[0:1] * lv[1:2], axis=-1, keepdims=True))
            - jnp.exp(jnp.sum(lv[2:3] * lv[3:4], axis=-1, keepdims=True)) + lam_init)


def _fold_lanes(x, op):
    parts = [x[:, c * LANES:(c + 1) * LANES] for c in range(x.shape[1] // LANES)]
    return functools.reduce(op, parts)


def _masked_softmax_pv(n_chunks, scores, values, s_scr):
    n_maps, _, rows, _ = s_scr.shape

    def first(j, mx):
        out = []
        for a, s in enumerate(scores(j)):
            s_scr[a, j] = s
            out.append(jnp.maximum(mx[a], _fold_lanes(s, jnp.maximum)))
        return tuple(out)

    mx = lax.fori_loop(0, n_chunks, first, tuple(jnp.full((rows, LANES), NEG_INF, F32) for _ in range(n_maps)))
    m = [jnp.max(x, axis=-1, keepdims=True) for x in mx]

    def second(j, carry):
        v = values(j)
        out = []
        for a in range(n_maps):
            l, acc = carry[a]
            p = jnp.exp(s_scr[a, j] - m[a])
            out.append((l + _fold_lanes(p, jnp.add), acc + jnp.dot(p.astype(BF16), v, preferred_element_type=F32)))
        return tuple(out)

    init = tuple((jnp.zeros((rows, LANES), F32), jnp.zeros((rows, HEAD_DIM), F32)) for _ in range(n_maps))
    res = lax.fori_loop(0, n_chunks, second, init)
    return [acc / jnp.sum(l, axis=-1, keepdims=True) for l, acc in res]


def _prompt_tiles(t):
    tq = min(t, 256)
    tk = min(t, 512)
    assert t % tk == 0 and tk % tq == 0
    return tq, tk


def _chunk_bias(tab, tq, tk):
    assert tq + 1 >= T5_MAX_DIST
    return jnp.stack([_dist_tile(tab, tq, tk, k * tq) for k in range(tk // tq + 2)], axis=1)


def _diff_prompt_body(q_ref, k_ref, v_ref, bias_ref, lam_ref, g_ref, o_ref, s_scr, *, tq, tk, lam_init):
    i = pl.program_id(2)
    ratio = tk // tq
    q = q_ref[...]
    q0, q1 = q[:, :DH_A], q[:, DH_A:]
    scale = DH_A ** -0.5

    def scores(j):
        ks = k_ref[pl.ds(pl.multiple_of(j * tk, tk), tk), :]
        bt = bias_ref[jnp.minimum(i - ratio * j, ratio + 1)]
        return (lax.dot_general(q0, ks[:, :DH_A], NT_DIMS, preferred_element_type=F32) * scale + bt,
                lax.dot_general(q1, ks[:, DH_A:], NT_DIMS, preferred_element_type=F32) * scale + bt)

    def values(j):
        return v_ref[pl.ds(pl.multiple_of(j * tk, tk), tk), :]

    o0, o1 = _masked_softmax_pv(i // ratio + 1, scores, values, s_scr)
    o = o0 - _diff_lambda(lam_ref, lam_init) * o1
    o_ref[...] = (_rms(o, g_ref[...]) * (1.0 - lam_init)).astype(o_ref.dtype)


def _diff_prompt(p16, col_q, col_k, col_v, n_heads, tab, lam_vec, g_subln, lam_init):
    b, t, _ = p16.shape
    tq, tk = _prompt_tiles(t)
    bias = _chunk_bias(tab, tq, tk)
    cq, ck, cv = col_q // HEAD_DIM, col_k // HEAD_DIM, col_v // HEAD_DIM
    return pl.pallas_call(
        functools.partial(_diff_prompt_body, tq=tq, tk=tk, lam_init=lam_init),
        grid=(b, n_heads, t // tq),
        in_specs=[pl.BlockSpec((None, tq, HEAD_DIM), lambda bi, h, i: (bi, i, cq + h)),
                  pl.BlockSpec((None, t, HEAD_DIM), lambda bi, h, i: (bi, 0, ck + h)),
                  pl.BlockSpec((None, t, HEAD_DIM), lambda bi, h, i: (bi, 0, cv + h)),
                  pl.BlockSpec((None,) + bias.shape[1:], lambda bi, h, i: (h, 0, 0, 0)),
                  pl.BlockSpec(lam_vec.shape, lambda bi, h, i: (0, 0)),
                  pl.BlockSpec((1, HEAD_DIM), lambda bi, h, i: (0, 0))],
        out_specs=pl.BlockSpec((None, tq, HEAD_DIM), lambda bi, h, i: (bi, i, h)),
        out_shape=jax.ShapeDtypeStruct((b, t, n_heads * HEAD_DIM), BF16),
        scratch_shapes=[pltpu.VMEM((2, t // tk, tq, tk), F32)],
        compiler_params=_params("parallel", "parallel", "arbitrary"),
        name="diff_prompt",
    )(p16, p16, p16, bias, lam_vec, g_subln.reshape(1, HEAD_DIM))


def _fox_prompt_body(q_ref, k_ref, v_ref, cum_ref, cumt_ref, o_ref, s_scr, *, tq, tk):
    h = pl.program_id(1)
    i = pl.program_id(2)
    q = q_ref[...]
    cum = cum_ref[...]
    lane = lax.broadcasted_iota(jnp.int32, cum.shape, 1)
    cq = jnp.sum(jnp.where(lane == h, cum, 0.0), axis=-1, keepdims=True)
    ahead = lax.broadcasted_iota(jnp.int32, (tq, tk), 1) - lax.broadcasted_iota(jnp.int32, (tq, tk), 0)

    def scores(j):
        start = pl.multiple_of(j * tk, tk)
        ck = cumt_ref[:, pl.ds(start, tk)]
        s = lax.dot_general(q, k_ref[pl.ds(start, tk), :], NT_DIMS, preferred_element_type=F32)
        s = s * HEAD_DIM ** -0.5 + (cq - ck)
        return (jnp.where(ahead <= i * tq - j * tk, s, NEG_INF),)

    def values(j):
        return v_ref[pl.ds(pl.multiple_of(j * tk, tk), tk), :]

    o, = _masked_softmax_pv(i // (tk // tq) + 1, scores, values, s_scr)
    o_ref[...] = o.astype(o_ref.dtype)


def _fox_prompt(p16, col_q, col_k, col_v, n_heads, cum):
    b, t, _ = p16.shape
    tq, tk = _prompt_tiles(t)
    cq, ck, cv = col_q // HEAD_DIM, col_k // HEAD_DIM, col_v // HEAD_DIM
    cum_t = jnp.swapaxes(cum, 1, 2).reshape(b, n_heads, 1, t)
    return pl.pallas_call(
        functools.partial(_fox_prompt_body, tq=tq, tk=tk),
        grid=(b, n_heads, t // tq),
        in_specs=[pl.BlockSpec((None, tq, HEAD_DIM), lambda bi, h, i: (bi, i, cq + h)),
                  pl.BlockSpec((None, t, HEAD_DIM), lambda bi, h, i: (bi, 0, ck + h)),
                  pl.BlockSpec((None, t, HEAD_DIM), lambda bi, h, i: (bi, 0, cv + h)),
                  pl.BlockSpec((None, tq, n_heads), lambda bi, h, i: (bi, i, 0)),
                  pl.BlockSpec((None, None, 1, t), lambda bi, h, i: (bi, h, 0, 0))],
        out_specs=pl.BlockSpec((None, tq, HEAD_DIM), lambda bi, h, i: (bi, i, h)),
        out_shape=jax.ShapeDtypeStruct((b, t, n_heads * HEAD_DIM), BF16),
        scratch_shapes=[pltpu.VMEM((1, t // tk, tq, tk), F32)],
        compiler_params=_params("parallel", "parallel", "arbitrary"),
        name="fox_prompt",
    )(p16, p16, p16, cum, cum_t)


def _top_blocks_negmask(gate, n_top, limit):
    n_blk = gate.shape[1]
    lane = lax.broadcasted_iota(jnp.int32, gate.shape, 1).astype(F32)
    chosen = jnp.zeros(gate.shape, F32)
    g = gate
    for _ in range(n_top):
        mx = jnp.max(g, axis=-1, keepdims=True)
        idx = jnp.min(jnp.where(g == mx, lane, float(n_blk)), axis=-1, keepdims=True)
        pick = lane == idx
        chosen = jnp.where(pick & (idx < limit), 1.0, chosen)
        g = jnp.where(pick, NEG_INF, g)
    return jnp.where(chosen > 0.0, 0.0, NEG_INF)


def _moba_prompt_body(q_ref, k_ref, v_ref, q32_ref, k32_ref, bias_ref, o_ref, kmean_scr, s_scr, *, n_blk, tk):
    blk = MOBA_BLOCK
    ratio = tk // blk
    i = pl.program_id(2)

    @pl.when(i == 0)
    def _():
        kmean_scr[...] = jnp.mean(k32_ref[...].reshape(n_blk, blk, HEAD_DIM), axis=1)

    gate = lax.dot_general(q32_ref[...], kmean_scr[...], NT_DIMS, preferred_element_type=F32,
                           precision=lax.Precision.HIGHEST)
    lane = lax.broadcasted_iota(jnp.int32, gate.shape, 1)
    gate = jnp.where(lane < i, gate, NEG_INF)
    sel = _top_blocks_negmask(gate, min(MOBA_TOPK, n_blk), i.astype(F32))
    sel = jnp.where(lane == i, 0.0, sel)
    q = q_ref[...]

    def scores(j):
        ks = k_ref[pl.ds(pl.multiple_of(j * tk, tk), tk), :]
        s = lax.dot_general(q, ks, NT_DIMS, preferred_element_type=F32) * HEAD_DIM ** -0.5
        s = s + bias_ref[jnp.minimum(i - ratio * j, ratio + 1)]
        parts = []
        for c in range(ratio):
            keep = jnp.min(jnp.where(lane == j * ratio + c, sel, 0.0), axis=-1, keepdims=True)
            parts.append(s[:, c * blk:(c + 1) * blk] + keep)
        return (parts[0] if ratio == 1 else jnp.concatenate(parts, axis=1),)

    def values(j):
        return v_ref[pl.ds(pl.multiple_of(j * tk, tk), tk), :]

    o, = _masked_softmax_pv(i // ratio + 1, scores, values, s_scr)
    o_ref[...] = o.astype(o_ref.dtype)


def _moba_prompt(p16, p32, col_q, col_k, col_v, n_heads, tab):
    b, t, _ = p16.shape
    blk = MOBA_BLOCK
    assert t % blk == 0
    tk = min(t, 2 * blk)
    assert t % tk == 0
    bias = _chunk_bias(tab, blk, tk)
    cq, ck, cv = col_q // HEAD_DIM, col_k // HEAD_DIM, col_v // HEAD_DIM
    return pl.pallas_call(
        functools.partial(_moba_prompt_body, n_blk=t // blk, tk=tk),
        grid=(b, n_heads, t // blk),
        in_specs=[pl.BlockSpec((None, blk, HEAD_DIM), lambda bi, h, i: (bi, i, cq + h)),
                  pl.BlockSpec((None, t, HEAD_DIM), lambda bi, h, i: (bi, 0, ck + h)),
                  pl.BlockSpec((None, t, HEAD_DIM), lambda bi, h, i: (bi, 0, cv + h)),
                  pl.BlockSpec((None, blk, HEAD_DIM), lambda bi, h, i: (bi, i, cq + h)),
                  pl.BlockSpec((None, t, HEAD_DIM), lambda bi, h, i: (bi, 0, ck + h)),
                  pl.BlockSpec((None,) + bias.shape[1:], lambda bi, h, i: (h, 0, 0, 0))],
        out_specs=pl.BlockSpec((None, blk, HEAD_DIM), lambda bi, h, i: (bi, i, h)),
        out_shape=jax.ShapeDtypeStruct((b, t, n_heads * HEAD_DIM), BF16),
        scratch_shapes=[pltpu.VMEM((t // blk, HEAD_DIM), F32), pltpu.VMEM((1, t // tk, blk, tk), F32)],
        compiler_params=_params("parallel", "parallel", "arbitrary"),
        name="moba_prompt",
    )(p16, p16, p16, p32, p32, bias)


KEY_SIGN = -2 ** 31
KEY_OF_NEG_INF = -2139095041


def _order_key(score):
    bits = pltpu.bitcast(score, jnp.int32)
    key = jnp.where(bits < 0, bits ^ 0x7FFFFFFF, bits)
    return jnp.where(score == 0.0, 0, key)


def _kth_largest_key(count_ge, n_rows, k):
    def bit_body(b, ans):
        cand = ans | jnp.left_shift(jnp.int32(1), 31 - b)
        cnt, = count_ge([cand ^ KEY_SIGN])
        return jnp.where(cnt >= k, cand, ans)

    ans = lax.fori_loop(0, 32, bit_body, jnp.zeros((n_rows, 1), jnp.int32))
    return ans ^ KEY_SIGN


def _dsa_prompt_body(iq_ref, ikw_ref, kidx_ref, qb_ref, kb_ref, vb_ref, bias_ref, o_ref, key_scr, nm_scr, w_scr,
                     s_scr, *, tq, n_heads, n_sel):
    i = pl.program_id(1)
    n_chunks = i + 1
    iq = iq_ref[...].reshape(N_IDX_HEADS * tq, IDX_DIM)
    w = ikw_ref[:, IDX_DIM:IDX_DIM + N_IDX_HEADS] * N_IDX_HEADS ** -0.5 * IDX_DIM ** -0.5
    for n in range(N_IDX_HEADS):
        w_scr[n] = jnp.broadcast_to(w[:, n:n + 1], (tq, tq))
    row = lax.broadcasted_iota(jnp.int32, (tq, tq), 0)
    col = lax.broadcasted_iota(jnp.int32, (tq, tq), 1)

    def score_body(j, _):
        kc = kidx_ref[pl.ds(pl.multiple_of(j * tq, tq), tq), :][:, :IDX_DIM]
        rel = jnp.maximum(lax.dot_general(iq, kc, NT_DIMS, preferred_element_type=F32), 0.0)
        rel = rel.reshape(N_IDX_HEADS, tq, tq)
        sc = w_scr[0] * rel[0]
        for n in range(1, N_IDX_HEADS):
            sc = sc + w_scr[n] * rel[n]
        sc = jnp.where((j < i) | (col <= row), sc, NEG_INF)
        key_scr[j] = _order_key(sc)
        return 0

    lax.fori_loop(0, n_chunks, score_body, 0)

    def count_many(preds):
        def body(j, accs):
            k = key_scr[j]
            return tuple(acc + jnp.where(pred(k), 1.0, 0.0) for acc, pred in zip(accs, preds))
        accs = lax.fori_loop(0, n_chunks, body, tuple(jnp.zeros((tq, tq), F32) for _ in preds))
        return [jnp.sum(acc, axis=-1, keepdims=True) for acc in accs]

    def count(pred):
        return count_many([pred])[0]

    thr = _kth_largest_key(lambda ts: count_many([(lambda k, t=t: k >= t) for t in ts]), tq, n_sel)
    cnt_ge = count(lambda k: k >= thr)
    tie = jnp.max(jnp.where((cnt_ge > n_sel) & (thr > KEY_OF_NEG_INF), 1.0, 0.0)) > 0.0

    @pl.when(jnp.logical_not(tie))
    def _():
        def body(j, _):
            nm_scr[j] = jnp.where(key_scr[j] >= thr, 0.0, NEG_INF)
            return 0
        lax.fori_loop(0, n_chunks, body, 0)

    @pl.when(tie)
    def _():
        allow = n_sel - count(lambda k: k > thr)
        tri = jnp.where(row <= col, 1.0, 0.0).astype(BF16)

        def body(j, before):
            k = key_scr[j]
            eq = jnp.where(k == thr, 1.0, 0.0)
            rank = jnp.dot(eq.astype(BF16), tri, preferred_element_type=F32) + before
            keep = jnp.where(k > thr, 1.0, jnp.where(rank <= allow, eq, 0.0))
            nm_scr[j] = jnp.where(keep > 0.0, 0.0, NEG_INF)
            return before + jnp.sum(eq, axis=-1, keepdims=True)
        lax.fori_loop(0, n_chunks, body, jnp.zeros((tq, 1), F32))

    qb = qb_ref[...]
    qs = jnp.concatenate([qb[:, h * HEAD_DIM:(h + 1) * HEAD_DIM] for h in range(n_heads)], axis=0)

    def scores(j):
        rows = pl.ds(pl.multiple_of(j * tq, tq), tq)
        s = lax.dot_general(qs, kb_ref[rows, :], NT_DIMS, preferred_element_type=F32) * HEAD_DIM ** -0.5
        return (s + bias_ref[jnp.minimum(i - j, 2)] + jnp.tile(nm_scr[j], (n_heads, 1)),)

    def values(j):
        return vb_ref[pl.ds(pl.multiple_of(j * tq, tq), tq), :]

    o, = _masked_softmax_pv(n_chunks, scores, values, s_scr)
    for h in range(n_heads):
        o_ref[:, h * HEAD_DIM:(h + 1) * HEAD_DIM] = o[h * tq:(h + 1) * tq].astype(o_ref.dtype)


def _dsa_prompt(p16, p32, iq_t, col_qb, col_kb, col_vb, col_ik, n_heads, bias, n_sel, tq):
    b, t, _ = p16.shape
    qw = n_heads * HEAD_DIM
    assert col_qb % qw == 0
    return pl.pallas_call(
        functools.partial(_dsa_prompt_body, tq=tq, n_heads=n_heads, n_sel=n_sel),
        grid=(b, t // tq),
        in_specs=[pl.BlockSpec((None, N_IDX_HEADS, tq, IDX_DIM), lambda bi, i: (bi, 0, i, 0)),
                  pl.BlockSpec((None, tq, LANES), lambda bi, i: (bi, i, col_ik // LANES)),
                  pl.BlockSpec((None, t, LANES), lambda bi, i: (bi, 0, col_ik // LANES)),
                  pl.BlockSpec((None, tq, qw), lambda bi, i: (bi, i, col_qb // qw)),
                  pl.BlockSpec((None, t, HEAD_DIM), lambda bi, i: (bi, 0, col_kb // HEAD_DIM)),
                  pl.BlockSpec((None, t, HEAD_DIM), lambda bi, i: (bi, 0, col_vb // HEAD_DIM)),
                  pl.BlockSpec(bias.shape, lambda bi, i: (0, 0, 0))],
        out_specs=pl.BlockSpec((None, tq, qw), lambda bi, i: (bi, i, 0)),
        out_shape=jax.ShapeDtypeStruct((b, t, qw), BF16),
        scratch_shapes=[pltpu.VMEM((t // tq, tq, tq), jnp.int32), pltpu.VMEM((t // tq, tq, tq), F32),
                        pltpu.VMEM((N_IDX_HEADS, tq, tq), F32), pltpu.VMEM((1, t // tq, n_heads * tq, tq), F32)],
        compiler_params=_params("parallel", "arbitrary"),
        name="dsa_prompt",
    )(iq_t, p32, p16, p16, p16, p16, bias)


def _block_diag_rows(q, dtype):
    b, tn, g, d = q.shape
    eye = jnp.eye(g, dtype=q.dtype)
    return jnp.einsum('btgd,gk->bgtkd', q, eye).reshape(b, g * tn, g * d).astype(dtype)


def _page_specs(layer, n_pages, group, rows, width):
    def spec(g):
        return pl.BlockSpec((None, None, rows, width),
                            lambda bi, p, pt: (layer, pt[bi, jnp.minimum(p * group + g, n_pages - 1)], 0, 0))
    return [spec(g) for g in range(group)]


def _past_bias_index(page_idx, n_pages):
    return jnp.clip(page_idx - (n_pages - 2), 0, 1)


def _paged_call(body, pt, n_steps, operands, in_specs, out_shape, out_spec, scratch, name):
    return pl.pallas_call(
        body,
        grid_spec=pltpu.PrefetchScalarGridSpec(
            num_scalar_prefetch=1, grid=(pt.shape[0], n_steps),
            in_specs=in_specs, out_specs=out_spec, scratch_shapes=scratch),
        out_shape=out_shape,
        compiler_params=_params("parallel", "arbitrary"),
        name=name,
    )(pt, *operands)


def _per_batch(rows, width):
    return pl.BlockSpec((None, rows, width), lambda bi, p, pt: (bi, 0, 0))


def _whole(shape):
    return pl.BlockSpec(shape, lambda bi, p, pt: (0,) * len(shape))


def _rows_page(x, page):
    b, tn, w = x.shape
    return jnp.pad(x, ((0, 0), (0, page - tn), (0, 0))).reshape(b, page * (w // HEAD_DIM), HEAD_DIM)


def _transposed_page(x, page):
    return jnp.pad(jnp.swapaxes(x, 1, 2), ((0, 0), (0, 0), (0, page - x.shape[1])))


def _flash_scratch(rows, width):
    return [pltpu.VMEM((rows, 1), F32), pltpu.VMEM((rows, 1), F32), pltpu.VMEM((rows, width), F32)]


def _flash_init(m_scr, l_scr, acc_scr):
    m_scr[...] = jnp.full(m_scr.shape, NEG_INF, F32)
    l_scr[...] = jnp.zeros(l_scr.shape, F32)
    acc_scr[...] = jnp.zeros(acc_scr.shape, F32)


def _flash_update(s, pv, m_scr, l_scr, acc_scr):
    m = m_scr[...]
    m_new = jnp.maximum(m, jnp.max(s, axis=-1, keepdims=True))
    m_safe = jnp.where(m_new == NEG_INF, 0.0, m_new)
    p = jnp.exp(s - m_safe)
    alpha = jnp.exp(m - m_safe)
    l_scr[...] = alpha * l_scr[...] + jnp.sum(p, axis=-1, keepdims=True)
    acc_scr[...] = alpha * acc_scr[...] + pv(p.astype(BF16))
    m_scr[...] = m_new


def _cat16(refs, axis):
    parts = [r[...].astype(BF16) for r in refs]
    return parts[0] if len(parts) == 1 else jnp.concatenate(parts, axis=axis)


SAMPLE_PAGE_GROUP = 8
SMALL_PAGE_GROUP = 8


def _heads_first(o, b, n_heads, tn):
    return jnp.swapaxes(o.reshape(b, n_heads, tn, HEAD_DIM), 1, 2).reshape(b, tn, n_heads * HEAD_DIM)


def _diff_sample_body(pt_ref, wq_ref, bias_ref, lam_ref, g_ref, kn_ref, vn_ref, *rest,
                      n_pages, group, page, n_new, n_heads, lam_init):
    k_refs, v_refs = rest[:group], rest[group:2 * group]
    o_ref, m_scr, l_scr, acc_scr = rest[2 * group:]
    p = pl.program_id(1)
    rows_h = 2 * n_new

    @pl.when(p == 0)
    def _():
        _flash_init(m_scr, l_scr, acc_scr)

    def step(ks, vs, bias):
        s = jnp.dot(wq_ref[...], _cat16(ks, 1), preferred_element_type=F32) * DH_A ** -0.5 + bias

        def pv(p16):
            outs = []
            for h in range(n_heads):
                vh = [v[pl.ds(h, page, stride=n_heads), :].astype(BF16) for v in vs]
                vh = vh[0] if len(vh) == 1 else jnp.concatenate(vh, axis=0)
                outs.append(jnp.dot(p16[h * rows_h:(h + 1) * rows_h], vh, preferred_element_type=F32))
            return jnp.concatenate(outs, axis=0)

        _flash_update(s, pv, m_scr, l_scr, acc_scr)

    @pl.when(p < n_pages // group)
    def _():
        tiles = [bias_ref[_past_bias_index(p * group + g, n_pages)] for g in range(group)]
        step(k_refs, v_refs, jnp.concatenate(tiles, axis=1))

    @pl.when(p == n_pages // group)
    def _():
        step([kn_ref], [vn_ref], bias_ref[2])
        lam = _diff_lambda(lam_ref, lam_init)
        on = acc_scr[...] / l_scr[...]
        for h in range(n_heads):
            r0 = h * rows_h
            o = on[r0:r0 + n_new] - lam * on[r0 + n_new:r0 + rows_h]
            o_ref[h * n_new:(h + 1) * n_new, :] = _rms(o, g_ref[...]) * (1.0 - lam_init)


def _diff_sample(pt, layer, q, cache_kt, cache_v, k_new, v_new, tab, lam_vec, g_subln, lam_init):
    b, tn, n_heads = q.shape[:3]
    n_pages = pt.shape[1]
    width, page = cache_kt.shape[2:]
    group = min(SAMPLE_PAGE_GROUP, n_pages)
    assert n_pages % group == 0
    wq = _block_diag_rows(q.reshape(b, tn, 2 * n_heads, DH_A), BF16)
    rows = 2 * n_heads * tn
    bias = _sample_bias(tab, tn, page)
    bias = jnp.broadcast_to(bias[:, :, None], (3, n_heads, 2, tn, page)).reshape(3, rows, page)
    out = _paged_call(
        functools.partial(_diff_sample_body, n_pages=n_pages, group=group, page=page, n_new=tn, n_heads=n_heads,
                          lam_init=lam_init),
        pt, n_pages // group + 1,
        (wq, bias, lam_vec, g_subln.reshape(1, HEAD_DIM), _transposed_page(k_new, page), _rows_page(v_new, page))
        + (cache_kt,) * group + (cache_v,) * group,
        [_per_batch(rows, width), _whole(bias.shape), _whole(lam_vec.shape), _whole((1, HEAD_DIM)),
         _per_batch(width, page), _per_batch(page * n_heads, HEAD_DIM)]
        + _page_specs(layer, n_pages, group, width, page) + _page_specs(layer, n_pages, group, page * n_heads, HEAD_DIM),
        jax.ShapeDtypeStruct((b, n_heads * tn, HEAD_DIM), F32), _per_batch(n_heads * tn, HEAD_DIM),
        _flash_scratch(rows, HEAD_DIM), "diff_sample")
    return _heads_first(out, b, n_heads, tn)


def _fox_sample_body(pt_ref, q_ref, cq_ref, ckp_ref, ckn_ref, hm_ref, nm_ref, kn_ref, vn_ref, *rest,
                     n_pages, group):
    k_refs, v_refs = rest[:group], rest[group:2 * group]
    o_ref, m_scr, l_scr, acc_scr = rest[2 * group:]
    p = pl.program_id(1)

    @pl.when(p == 0)
    def _():
        _flash_init(m_scr, l_scr, acc_scr)

    def step(ks, vs, ck, mask):
        s = lax.dot_general(q_ref[...], _cat16(ks, 0), NT_DIMS, preferred_element_type=F32)
        s = s * HEAD_DIM ** -0.5 + (cq_ref[...] - ck) + mask
        _flash_update(s, lambda p16: jnp.dot(p16, _cat16(vs, 0), preferred_element_type=F32), m_scr, l_scr, acc_scr)

    @pl.when(p < n_pages // group)
    def _():
        step(k_refs, v_refs, ckp_ref[...], hm_ref[...])

    @pl.when(p == n_pages // group)
    def _():
        step([kn_ref], [vn_ref], ckn_ref[...], nm_ref[...])
        o_ref[...] = acc_scr[...] / l_scr[...]


def _fox_sample(pt, layer, q, cache_k, cache_v, k_new, v_new, cum_q, cum_past, cum_new):
    b, tn, n_heads = q.shape[:3]
    n_pages = pt.shape[1]
    prow = cache_k.shape[2]
    page = prow // n_heads
    group = min(SAMPLE_PAGE_GROUP, n_pages)
    assert n_pages % group == 0
    rows = n_heads * tn
    q_rows = jnp.swapaxes(q, 1, 2).reshape(b, rows, HEAD_DIM).astype(BF16)
    cq = jnp.swapaxes(cum_q, 1, 2).reshape(b, rows, 1)
    ck_past = cum_past.reshape(b, n_pages // group, 1, group * prow)
    ck_new = jnp.pad(cum_new, ((0, 0), (0, page - tn), (0, 0))).reshape(b, 1, prow)
    head_mask = _expand_heads(jnp.zeros((n_heads, tn, page), F32))
    qi = np.arange(tn)[:, None]
    ci = np.arange(page)[None, :]
    causal = np.broadcast_to(np.where(ci <= qi, 0.0, NEG_INF).astype(np.float32), (n_heads, tn, page))
    new_mask = _expand_heads(jnp.asarray(causal))
    out = _paged_call(
        functools.partial(_fox_sample_body, n_pages=n_pages, group=group),
        pt, n_pages // group + 1,
        (q_rows, cq, ck_past, ck_new, jnp.tile(head_mask, (1, group)), new_mask,
         _rows_page(k_new, page), _rows_page(v_new, page)) + (cache_k,) * group + (cache_v,) * group,
        [_per_batch(rows, HEAD_DIM), _per_batch(rows, 1),
         pl.BlockSpec((None, None, 1, group * prow),
                      lambda bi, p, pt_: (bi, jnp.minimum(p, n_pages // group - 1), 0, 0)),
         _per_batch(1, prow), _whole((rows, group * prow)), _whole((rows, prow)),
         _per_batch(prow, HEAD_DIM), _per_batch(prow, HEAD_DIM)]
        + _page_specs(layer, n_pages, group, prow, HEAD_DIM) + _page_specs(layer, n_pages, group, prow, HEAD_DIM),
        jax.ShapeDtypeStruct((b, rows, HEAD_DIM), F32), _per_batch(rows, HEAD_DIM),
        _flash_scratch(rows, HEAD_DIM), "fox_sample")
    return _heads_first(out, b, n_heads, tn)


def _kmean_body(pt_ref, *refs, n_heads, pages_per_block):
    k_refs, o_ref = refs[:-1], refs[-1]
    for blk in range(len(k_refs) // pages_per_block):
        total = None
        for k_ref in k_refs[blk * pages_per_block:(blk + 1) * pages_per_block]:
            k = k_ref[...]
            s = jnp.sum(k.reshape(k.shape[0] // n_heads, n_heads, HEAD_DIM), axis=0)
            total = s if total is None else total + s
        o_ref[blk] = total * (1.0 / MOBA_BLOCK)


def _moba_gate_body(wq_ref, kmean_ref, o_ref, *, n_blk):
    gate = lax.dot_general(wq_ref[...], kmean_ref[...], NT_DIMS, preferred_element_type=F32,
                           precision=lax.Precision.HIGHEST)
    o_ref[...] = _top_blocks_negmask(gate, min(MOBA_TOPK, n_blk), float(n_blk))


def _moba_sample_body(pt_ref, q_ref, bias_ref, sel_ref, kn_ref, vn_ref, *rest, n_pages, group, pages_per_block):
    k_refs, v_refs = rest[:group], rest[group:2 * group]
    o_ref, m_scr, l_scr, acc_scr = rest[2 * group:]
    p = pl.program_id(1)

    @pl.when(p == 0)
    def _():
        _flash_init(m_scr, l_scr, acc_scr)

    def scores(k_ref, bias):
        s = lax.dot_general(q_ref[...], k_ref[...].astype(BF16), NT_DIMS, preferred_element_type=F32)
        return s * HEAD_DIM ** -0.5 + bias

    def update(s, vs):
        _flash_update(s, lambda p16: jnp.dot(p16, _cat16(vs, 0), preferred_element_type=F32), m_scr, l_scr, acc_scr)

    @pl.when(p < n_pages // group)
    def _():
        sel = sel_ref[...]
        lane = lax.broadcasted_iota(jnp.int32, sel.shape, 1)
        parts = []
        for g in range(group):
            page_idx = p * group + g
            row_mask = jnp.min(jnp.where(lane == page_idx // pages_per_block, sel, 0.0), axis=-1, keepdims=True)
            parts.append(scores(k_refs[g], bias_ref[_past_bias_index(page_idx, n_pages)]) + row_mask)
        update(parts[0] if group == 1 else jnp.concatenate(parts, axis=1), v_refs)

    @pl.when(p == n_pages // group)
    def _():
        update(scores(kn_ref, bias_ref[2]), [vn_ref])
        o_ref[...] = acc_scr[...] / l_scr[...]


def _moba_sample(pt, layer, q, cache_k, cache_v, k_new, v_new, tab):
    b, tn, n_heads = q.shape[:3]
    n_pages = pt.shape[1]
    prow = cache_k.shape[2]
    page = prow // n_heads
    width = n_heads * HEAD_DIM
    assert MOBA_BLOCK % page == 0 and (n_pages * page) % MOBA_BLOCK == 0 and tn < MOBA_BLOCK
    ppb = MOBA_BLOCK // page
    n_blk = n_pages // ppb
    group = min(SAMPLE_PAGE_GROUP, n_pages)
    assert n_pages % group == 0
    rows = n_heads * tn
    bps = group // ppb if group % ppb == 0 else 1
    assert n_blk % bps == 0
    kmean = _paged_call(
        functools.partial(_kmean_body, n_heads=n_heads, pages_per_block=ppb), pt, n_blk // bps,
        (cache_k,) * (bps * ppb), _page_specs(layer, n_pages, bps * ppb, prow, HEAD_DIM),
        jax.ShapeDtypeStruct((b, n_blk, n_heads, HEAD_DIM), F32),
        pl.BlockSpec((None, bps, n_heads, HEAD_DIM), lambda bi, p, pt_: (bi, p, 0, 0)), [], "moba_kmean")
    sel = pl.pallas_call(
        functools.partial(_moba_gate_body, n_blk=n_blk),
        grid=(b,),
        in_specs=[pl.BlockSpec((None, rows, width), lambda bi: (bi, 0, 0)),
                  pl.BlockSpec((None, n_blk, width), lambda bi: (bi, 0, 0))],
        out_specs=pl.BlockSpec((None, rows, n_blk), lambda bi: (bi, 0, 0)),
        out_shape=jax.ShapeDtypeStruct((b, rows, n_blk), F32),
        compiler_params=_params("parallel"),
        name="moba_gate",
    )(_block_diag_rows(q, F32), kmean.reshape(b, n_blk, width))
    bias = _sample_bias(tab, tn, page)
    bias = jnp.stack([_expand_heads(bias[i]) for i in range(3)])
    q_rows = jnp.swapaxes(q, 1, 2).reshape(b, rows, HEAD_DIM).astype(BF16)
    out = _paged_call(
        functools.partial(_moba_sample_body, n_pages=n_pages, group=group, pages_per_block=ppb),
        pt, n_pages // group + 1,
        (q_rows, bias, sel, _rows_page(k_new, page), _rows_page(v_new, page)) + (cache_k,) * group + (cache_v,) * group,
        [_per_batch(rows, HEAD_DIM), _whole(bias.shape), _per_batch(rows, n_blk),
         _per_batch(prow, HEAD_DIM), _per_batch(prow, HEAD_DIM)]
        + _page_specs(layer, n_pages, group, prow, HEAD_DIM) + _page_specs(layer, n_pages, group, prow, HEAD_DIM),
        jax.ShapeDtypeStruct((b, rows, HEAD_DIM), F32), _per_batch(rows, HEAD_DIM),
        _flash_scratch(rows, HEAD_DIM), "moba_sample")
    return _heads_first(out, b, n_heads, tn)


def _dsa_score_body(pt_ref, iq_ref, w_ref, mask_ref, kn_ref, *rest, n_pages, group, n_new):
    k_refs, o_ref = rest[:group], rest[group]
    p = pl.program_id(1)

    def score(kt16):
        rel = jnp.dot(iq_ref[...], kt16, preferred_element_type=F32)
        rel = jnp.maximum(rel * IDX_DIM ** -0.5, 0.0) * w_ref[...]
        return jnp.sum(rel.reshape(N_IDX_HEADS, n_new, rel.shape[1]), axis=0)

    @pl.when(p < n_pages // group)
    def _():
        o_ref[...] = score(_cat16(k_refs, 1))

    @pl.when(p == n_pages // group)
    def _():
        page = kn_ref.shape[1]
        o_ref[...] = jnp.full(o_ref.shape, NEG_INF, F32)
        o_ref[:, :page] = score(kn_ref[...].astype(BF16)) + mask_ref[...]


def _dsa_select_body(sc_ref, o_ref, key_scr, *, n_new, n_sel, chunk):
    width = sc_ref.shape[1]
    key_scr[...] = _order_key(sc_ref[...])

    def count(pred):
        return jnp.sum(jnp.where(pred(key_scr[...]), 1.0, 0.0), axis=-1, keepdims=True)

    thr = _kth_largest_key(lambda ts: [count(lambda k, t=t: k >= t) for t in ts], n_new, n_sel)
    cnt_ge = count(lambda k: k >= thr)
    tie = jnp.max(jnp.where((cnt_ge > n_sel) & (thr > KEY_OF_NEG_INF), 1.0, 0.0)) > 0.0

    @pl.when(jnp.logical_not(tie))
    def _():
        o_ref[...] = jnp.where(key_scr[...] >= thr, 0.0, NEG_INF)

    @pl.when(tie)
    def _():
        allow = n_sel - count(lambda k: k > thr)
        r = lax.broadcasted_iota(jnp.int32, (chunk, chunk), 0)
        c = lax.broadcasted_iota(jnp.int32, (chunk, chunk), 1)
        tri = jnp.where(r <= c, 1.0, 0.0).astype(BF16)

        def body(j, before):
            cols = pl.ds(pl.multiple_of(j * chunk, chunk), chunk)
            k = key_scr[:, cols]
            eq = jnp.where(k == thr, 1.0, 0.0)
            rank = jnp.dot(eq.astype(BF16), tri, preferred_element_type=F32) + before
            keep = jnp.where(k > thr, 1.0, jnp.where(rank <= allow, eq, 0.0))
            o_ref[:, cols] = jnp.where(keep > 0.0, 0.0, NEG_INF)
            return before + jnp.sum(eq, axis=-1, keepdims=True)
        lax.fori_loop(0, width // chunk, body, jnp.zeros((n_new, 1), F32))


def _dsa_sample_body(pt_ref, q_ref, bias_ref, nm_ref, kn_ref, vn_ref, *rest, n_pages, group, n_heads):
    k_refs, v_refs = rest[:group], rest[group:2 * group]
    o_ref, m_scr, l_scr, acc_scr = rest[2 * group:]
    p = pl.program_id(1)

    @pl.when(p == 0)
    def _():
        _flash_init(m_scr, l_scr, acc_scr)

    def step(ks, vs, bias, nm):
        s = lax.dot_general(q_ref[...], _cat16(ks, 0), NT_DIMS, preferred_element_type=F32)
        s = s * HEAD_DIM ** -0.5 + bias + jnp.tile(nm, (n_heads, 1))
        _flash_update(s, lambda p16: jnp.dot(p16, _cat16(vs, 0), preferred_element_type=F32), m_scr, l_scr, acc_scr)

    @pl.when(p < n_pages // group)
    def _():
        tiles = [bias_ref[_past_bias_index(p * group + g, n_pages)] for g in range(group)]
        step(k_refs, v_refs, jnp.concatenate(tiles, axis=1), nm_ref[...])

    @pl.when(p == n_pages // group)
    def _():
        page = kn_ref.shape[0]
        step([kn_ref], [vn_ref], bias_ref[2], nm_ref[:, :page])
        o_ref[...] = acc_scr[...] / l_scr[...]


def _dsa_sample(pt, layer, qb, iq, iw, cache_k, cache_v, cache_idx_t, k_new, v_new, ik_new, tab):
    b, tn, n_heads = qb.shape[:3]
    n_pages = pt.shape[1]
    page = cache_k.shape[2]
    n_sel = min(DSA_TOPK, (n_pages * page + tn) // 4)
    group = min(SMALL_PAGE_GROUP, n_pages)
    assert n_pages % group == 0
    n_steps = n_pages // group + 1
    width = n_steps * group * page
    pad = ((0, 0), (0, page - tn), (0, 0))
    qi = np.arange(tn)[:, None]
    ci = np.arange(page)[None, :]
    new_mask = jnp.asarray(np.where(ci <= qi, 0.0, NEG_INF).astype(np.float32))
    n_iq = N_IDX_HEADS * tn
    iq_rows = jnp.swapaxes(iq, 1, 2).reshape(b, n_iq, IDX_DIM).astype(BF16)
    w_rows = (jnp.swapaxes(iw, 1, 2).astype(F32) * N_IDX_HEADS ** -0.5).reshape(b, n_iq, 1)
    step_cols = pl.BlockSpec((None, tn, group * page), lambda bi, p, pt_: (bi, 0, p))
    scores = _paged_call(
        functools.partial(_dsa_score_body, n_pages=n_pages, group=group, n_new=tn),
        pt, n_steps,
        (iq_rows, w_rows, new_mask, _transposed_page(ik_new, page)) + (cache_idx_t,) * group,
        [_per_batch(n_iq, IDX_DIM), _per_batch(n_iq, 1), _whole(new_mask.shape), _per_batch(IDX_DIM, page)]
        + _page_specs(layer, n_pages, group, IDX_DIM, page),
        jax.ShapeDtypeStruct((b, tn, width), F32), step_cols, [], "dsa_score")
    negmask = pl.pallas_call(
        functools.partial(_dsa_select_body, n_new=tn, n_sel=n_sel, chunk=page),
        grid=(b,),
        in_specs=[pl.BlockSpec((None, tn, width), lambda bi: (bi, 0, 0))],
        out_specs=pl.BlockSpec((None, tn, width), lambda bi: (bi, 0, 0)),
        out_shape=jax.ShapeDtypeStruct((b, tn, width), F32),
        scratch_shapes=[pltpu.VMEM((tn, width), jnp.int32)],
        compiler_params=_params("parallel"),
        name="dsa_select",
    )(scores)
    rows = n_heads * tn
    q_rows = jnp.swapaxes(qb, 1, 2).reshape(b, rows, HEAD_DIM).astype(BF16)
    bias = _sample_bias(tab, tn, page).reshape(3, rows, page)
    out = _paged_call(
        functools.partial(_dsa_sample_body, n_pages=n_pages, group=group, n_heads=n_heads),
        pt, n_steps,
        (q_rows, bias, negmask, jnp.pad(k_new, pad), jnp.pad(v_new, pad)) + (cache_k,) * group + (cache_v,) * group,
        [_per_batch(rows, HEAD_DIM), _whole(bias.shape), step_cols, _per_batch(page, HEAD_DIM),
         _per_batch(page, HEAD_DIM)]
        + _page_specs(layer, n_pages, group, page, HEAD_DIM) + _page_specs(layer, n_pages, group, page, HEAD_DIM),
        jax.ShapeDtypeStruct((b, rows, HEAD_DIM), F32), _per_batch(rows, HEAD_DIM),
        _flash_scratch(rows, HEAD_DIM), "dsa_sample")
    return _heads_first(out, b, n_heads, tn)


def _pad_cols(w, n):
    return jnp.pad(w, ((0, 0), (0, n - w.shape[1])))


def _mixer_even(h_in, g, w_in16, past, pt, layer_e, lam_vec, g_subln, lam_init, tab, b, t):
    d = h_in.shape[1]
    n_a = n_b = (d // HEAD_DIM) // 2
    wa, wb = n_a * HEAD_DIM, n_b * HEAD_DIM
    c_qa, c_ka, c_va, c_qb = 0, wa, 2 * wa, 3 * wa
    c_kb = c_qb + wb
    c_vb = c_kb + HEAD_DIM
    c_iq = c_vb + HEAD_DIM
    c_ik = c_iq + N_IDX_HEADS * IDX_DIM
    c_iw = c_ik + IDX_DIM
    p32, p16 = _rms_matmul(h_in, g, w_in16, emit32=True, emit16=True)
    npad = p32.shape[1]
    p32 = p32.reshape(b, t, npad)
    p16 = p16.reshape(b, t, npad)
    ka = p32[..., c_ka:c_ka + wa]
    va = p32[..., c_va:c_va + wa]
    kb = p32[..., c_kb:c_kb + HEAD_DIM]
    vb = p32[..., c_vb:c_vb + HEAD_DIM]
    ik = p32[..., c_ik:c_ik + IDX_DIM]
    rows = (ka.reshape(b, t, n_a, 2, DH_A), va.reshape(b, t, n_a, HEAD_DIM), kb, vb, ik)
    tab_a, tab_b = tab[:, :n_a], tab[:, n_a:]
    if past is None:
        o_a = _diff_prompt(p16, c_qa, c_ka, c_va, n_a, tab_a, lam_vec, g_subln, lam_init)
        tq = min(t, 128)
        bias_b = jnp.swapaxes(_prompt_bias(tab_b, tq), 0, 1).reshape(3, n_b * tq, tq)
        iq_t = jnp.swapaxes(p16[..., c_iq:c_ik].reshape(b, t, N_IDX_HEADS, IDX_DIM), 1, 2)
        o_b = _dsa_prompt(p16, p32, iq_t, c_qb, c_kb, c_vb, c_ik, n_b, bias_b, min(DSA_TOPK, t // 4), tq)
    else:
        cache_a_k, cache_a_v, cache_b_k, cache_b_v, cache_b_idx = past
        qa = p32[..., c_qa:c_qa + wa].reshape(b, t, n_a, 2, DH_A)
        o_a = _diff_sample(pt, layer_e, qa, cache_a_k, cache_a_v, ka, va, tab_a, lam_vec, g_subln, lam_init)
        qb = p32[..., c_qb:c_qb + wb].reshape(b, t, n_b, HEAD_DIM)
        iq = p32[..., c_iq:c_ik].reshape(b, t, N_IDX_HEADS, IDX_DIM)
        iw = p32[..., c_iw:c_iw + N_IDX_HEADS]
        o_b = _dsa_sample(pt, layer_e, qb, iq, iw, cache_b_k, cache_b_v, cache_b_idx, kb, vb, ik, tab_b)
    return (o_a.reshape(b * t, wa), o_b.reshape(b * t, wb)), rows


def _mixer_odd(h_in, g, w_in16, past, pt, layer_o, b_forget, tab, b, t):
    d = h_in.shape[1]
    n_c = n_d = (d // HEAD_DIM) // 2
    wc, wd = n_c * HEAD_DIM, n_d * HEAD_DIM
    c_qc, c_kc, c_vc, c_qd = 0, wc, 2 * wc, 3 * wc
    c_kd = c_qd + wd
    c_vd = c_kd + wd
    c_fc = c_vd + wd
    p32, p16 = _rms_matmul(h_in, g, w_in16, emit32=True, emit16=True)
    npad = p32.shape[1]
    p32 = p32.reshape(b, t, npad)
    p16 = p16.reshape(b, t, npad)
    kc = p32[..., c_kc:c_kc + wc]
    vc = p32[..., c_vc:c_vc + wc]
    kd = p32[..., c_kd:c_kd + wd]
    vd = p32[..., c_vd:c_vd + wd]
    log_f = jax.nn.log_sigmoid(p32[..., c_fc:c_fc + n_c] + b_forget.astype(F32))
    rows = (kc.reshape(b, t, n_c, HEAD_DIM), vc.reshape(b, t, n_c, HEAD_DIM), log_f,
            kd.reshape(b, t, n_d, HEAD_DIM), vd.reshape(b, t, n_d, HEAD_DIM))
    tab_d = tab[:, n_c:]
    if past is None:
        o_c = _fox_prompt(p16, c_qc, c_kc, c_vc, n_c, jnp.cumsum(log_f, axis=1))
        o_d = _moba_prompt(p16, p32, c_qd, c_kd, c_vd, n_d, tab_d)
    else:
        cache_c_k, cache_c_v, cache_c_logf, cache_d_k, cache_d_v = past
        n_pages = pt.shape[1]
        page = cache_c_logf.shape[2]
        logf_past = cache_c_logf[layer_o][pt].reshape(b, n_pages * page, n_c)
        cum = jnp.cumsum(jnp.concatenate([logf_past, log_f], axis=1).astype(F32), axis=1)
        cum_q = cum[:, n_pages * page:]
        qc = p32[..., c_qc:c_qc + wc].reshape(b, t, n_c, HEAD_DIM)
        o_c = _fox_sample(pt, layer_o, qc, cache_c_k, cache_c_v, kc, vc, cum_q, cum[:, :n_pages * page], cum_q)
        qd = p32[..., c_qd:c_qd + wd].reshape(b, t, n_d, HEAD_DIM)
        o_d = _moba_sample(pt, layer_o, qd, cache_d_k, cache_d_v, kd, vd, tab_d)
    return (o_c.reshape(b * t, wc), o_d.reshape(b * t, wd)), rows


def _run_trunk(x, past_even, past_odd, pt, mem_kv, prm):
    b, t, d = x.shape
    x = x.reshape(b * t, d)
    depth = prm['norm_g'].shape[0]
    rows_even, rows_odd = [], []
    for layer in range(depth):
        g = prm['norm_g'][layer]
        wg, wu, wd = prm['w_ffn_gate'][layer], prm['w_ffn_up'][layer], prm['w_ffn_down'][layer]
        x = _ffn(x, g[NG_FFN1_PRE], g[NG_FFN1_POST], wg[0], wu[0], wd[0])
        if layer % 2 == 0:
            e = layer // 2
            lam_init = 0.8 - 0.6 * math.exp(-0.3 * layer)
            parts, rows = _mixer_even(x, g[NG_MIX_PRE], prm['w_in_even'][e], past_even, pt, e,
                                      prm['diff_lambda'][e].astype(F32), prm['g_subln'][e].astype(F32), lam_init,
                                      prm['t5_table'], b, t)
            rows_even.append(rows)
            x = _out_proj(x, g[NG_MIX_POST], parts, prm['w_out_even'][e])
        else:
            o = layer // 2
            parts, rows = _mixer_odd(x, g[NG_MIX_PRE], prm['w_in_odd'][o], past_odd, pt, o,
                                     prm['b_forget'][o], prm['t5_table'], b, t)
            rows_odd.append(rows)
            x = _out_proj(x, g[NG_MIX_POST], parts, prm['w_out_odd'][o])
        mk, mv = mem_kv[layer]
        q = _rms_matmul(x, g[NG_X_PRE], prm['w_xq'][layer])
        o_x = _cross_attend(q.reshape(b, t, -1), mk, mv)
        x = _out_proj(x, g[NG_X_POST], [o_x.reshape(b * t, -1)], prm['w_xo'][layer])
        x = _ffn(x, g[NG_FFN2_PRE], g[NG_FFN2_POST], wg[1], wu[1], wd[1])
    return x.reshape(b, t, d), rows_even, rows_odd


def kernel(x_prompt, x_sample, cache_a_k, cache_a_v, cache_b_k, cache_b_v, cache_b_idx, cache_c_k, cache_c_v, cache_c_logf, cache_d_k, cache_d_v, cache_mem_k, cache_mem_v, page_table, mem_prompt, t5_table, norm_g, w_ffn_gate, w_ffn_up, w_ffn_down, w_xq, w_xk, w_xv, w_xo, w_in_even, w_out_even, diff_lambda, g_subln, w_in_odd, w_out_odd, b_forget):
    depth = norm_g.shape[0]
    d_model = x_prompt.shape[-1]
    n_c = (d_model // HEAD_DIM) // 2
    e_pad = _round_up(w_in_even.shape[-1], COL_TILE)
    o_pad = _round_up(w_in_odd.shape[-1], COL_TILE)
    c_fc = 3 * n_c * HEAD_DIM
    w_in_odd_r = jnp.concatenate([w_in_odd[..., :c_fc], w_in_odd[..., c_fc + n_c:], w_in_odd[..., c_fc:c_fc + n_c]],
                                 axis=-1)
    prm = {
        't5_table': t5_table.astype(F32), 'norm_g': norm_g.astype(F32),
        'w_ffn_gate': w_ffn_gate.astype(BF16), 'w_ffn_up': w_ffn_up.astype(BF16), 'w_ffn_down': w_ffn_down.astype(BF16),
        'w_xq': w_xq.astype(BF16), 'w_xo': w_xo.astype(BF16),
        'w_in_even': [_pad_cols(w_in_even[e], e_pad).astype(BF16) for e in range(w_in_even.shape[0])],
        'w_out_even': w_out_even.astype(BF16),
        'w_in_odd': [_pad_cols(w_in_odd_r[o], o_pad).astype(BF16) for o in range(w_in_odd.shape[0])],
        'w_out_odd': w_out_odd.astype(BF16),
        'diff_lambda': diff_lambda, 'g_subln': g_subln, 'b_forget': b_forget,
    }
    b_p, n_mem, _ = mem_prompt.shape
    hx_w = w_xk.shape[-1]

    mem_kv_p, mem_k_out, mem_v_out = [], [], []
    for l in range(depth):
        w_kv = jnp.concatenate([w_xk[l], w_xv[l]], axis=-1).astype(BF16)
        kv = _rms_matmul(mem_prompt.reshape(b_p * n_mem, d_model), norm_g[l, NG_MEM].astype(F32), w_kv)
        mk = kv[:, :hx_w].reshape(b_p, n_mem, hx_w)
        mv = kv[:, hx_w:].reshape(b_p, n_mem, hx_w)
        mem_kv_p.append((mk, mv))
        mem_k_out.append(mk.reshape(b_p, n_mem, hx_w // HEAD_DIM, HEAD_DIM))
        mem_v_out.append(mv.reshape(b_p, n_mem, hx_w // HEAD_DIM, HEAD_DIM))
    y_prompt, ev_p, od_p = _run_trunk(x_prompt.astype(F32), None, None, None, mem_kv_p, prm)

    def rows_pages(c):
        return c.reshape(c.shape[:2] + (c.shape[2] * c.shape[3], c.shape[4]))

    def transposed_pages(c):
        c = c.reshape(c.shape[:3] + (-1,))
        return jnp.swapaxes(c, 2, 3)

    past_even = (transposed_pages(cache_a_k), rows_pages(cache_a_v), cache_b_k, cache_b_v,
                 transposed_pages(cache_b_idx))
    past_odd = (rows_pages(cache_c_k), rows_pages(cache_c_v), cache_c_logf, rows_pages(cache_d_k),
                rows_pages(cache_d_v))
    b_s = x_sample.shape[0]
    mem_kv_s = [(cache_mem_k[l].reshape(b_s, n_mem, hx_w), cache_mem_v[l].reshape(b_s, n_mem, hx_w))
                for l in range(depth)]
    y_sample, ev_s, od_s = _run_trunk(x_sample.astype(F32), past_even, past_odd, page_table.astype(jnp.int32),
                                      mem_kv_s, prm)

    def stack(rows, i):
        return jnp.stack([r[i] for r in rows])

    out = [y_prompt, y_sample]
    out += [stack(ev_p, i) for i in range(5)] + [stack(od_p, i) for i in range(5)]
    out += [jnp.stack(mem_k_out), jnp.stack(mem_v_out)]
    out += [stack(ev_s, i) for i in range(5)] + [stack(od_s, i) for i in range(5)]
    return tuple(out)
```

```python
import functools
import math

import numpy as np
import jax
import jax.numpy as jnp
from jax import lax
from jax.experimental import pallas as pl
from jax.experimental.pallas import tpu as pltpu

F32 = jnp.float32
BF16 = jnp.bfloat16
NEG_INF = float("-inf")

HEAD_DIM = 128
DH_A = HEAD_DIM // 2
N_IDX_HEADS = 16
IDX_DIM = 64
DSA_TOPK = 256
MOBA_BLOCK = 256
MOBA_TOPK = 3
N_BUCKETS = 32
T5_MAX_EXACT = N_BUCKETS // 2
T5_MAX_DIST = 128
RMS_EPS = 1e-6
NG_FFN1_PRE, NG_FFN1_POST, NG_MIX_PRE, NG_MIX_POST = 0, 1, 2, 3
NG_X_PRE, NG_X_POST, NG_FFN2_PRE, NG_FFN2_POST, NG_MEM = 4, 5, 6, 7, 8

LANES = 128
ROW_TILE = 512
COL_TILE = 512
VMEM_LIMIT = 56 * 1024 * 1024

NT_DIMS = (((1,), (1,)), ((), ()))


def _params(*sem):
    return pltpu.CompilerParams(dimension_semantics=sem, vmem_limit_bytes=VMEM_LIMIT)


def _rms(x, g):
    return x * lax.rsqrt(jnp.mean(x * x, axis=-1, keepdims=True) + RMS_EPS) * g


def _round_up(n, m):
    return (n + m - 1) // m * m


def _bucket_np(dist):
    n = np.maximum(dist, 0)
    n_f = np.maximum(n, 1).astype(np.float32)
    large = T5_MAX_EXACT + (np.log(n_f / np.float32(T5_MAX_EXACT)) / np.float32(math.log(T5_MAX_DIST / T5_MAX_EXACT))
                            * np.float32(N_BUCKETS - T5_MAX_EXACT)).astype(np.int32)
    return np.where(n < T5_MAX_EXACT, n, np.minimum(large, N_BUCKETS - 1)).astype(np.int32)


def _rel_bias(tab, dists):
    onehot = (_bucket_np(dists)[:, None] == np.arange(N_BUCKETS)[None, :]).astype(np.float32)
    rel = jnp.sum(jnp.asarray(onehot)[:, :, None] * tab[None].astype(F32), axis=1)
    return jnp.where(jnp.asarray(dists >= 0)[:, None], rel, NEG_INF).T


def _toeplitz(u, n_rows, n_cols):
    h = u.shape[0]
    period = n_rows + n_cols
    w = jnp.concatenate([u[:, :n_cols][:, ::-1], jnp.zeros((h, 1), u.dtype), u[:, n_cols:][:, ::-1]], axis=1)
    flat = jnp.tile(w, (1, n_rows))[:, :n_rows * (period - 1)]
    return flat.reshape(h, n_rows, period - 1)[:, :, :n_cols]


def _dist_tile(tab, n_rows, n_cols, offset):
    dists = np.arange(n_rows + n_cols - 1) - (n_cols - 1) + offset
    return _toeplitz(_rel_bias(tab, dists), n_rows, n_cols)


def _far_tile(tab, n_rows, n_cols):
    return jnp.broadcast_to(tab[N_BUCKETS - 1].astype(F32)[:, None, None], (tab.shape[1], n_rows, n_cols))


def _sample_bias(tab, n_new, page):
    assert page + 1 >= T5_MAX_DIST
    return jnp.stack([_far_tile(tab, n_new, page), _dist_tile(tab, n_new, page, page), _dist_tile(tab, n_new, page, 0)])


def _expand_heads(tile):
    h, r, c = tile.shape
    same = jnp.asarray(np.eye(h, dtype=bool))[:, None, None, :]
    return jnp.where(same, tile[:, :, :, None], NEG_INF).reshape(h * r, c * h)


def _rms_matmul_body(x_ref, g_ref, w_ref, *rest, groups, emit16):
    outs, h_scr = rest[:-1], rest[-1]
    j = pl.program_id(1)

    @pl.when(j == 0)
    def _():
        h_scr[...] = _rms(x_ref[...], g_ref[...]).astype(BF16)

    y = jnp.dot(h_scr[...], w_ref[...], preferred_element_type=F32)
    for (first, count), o_ref in zip(groups, outs):
        @pl.when((j >= first) & (j < first + count))
        def _():
            o_ref[...] = y
    if emit16:
        outs[len(groups)][...] = y.astype(BF16)


def _stacked(lead, *block):
    return (None,) * len(lead) + tuple(block), tuple(lead)


def _rms_matmul(x, g, w16, lead=(), *, f32_groups=None, emit16=False):
    m, d = x.shape
    n = w16.shape[-1]
    tm = min(m, 2 * ROW_TILE)
    tn = min(n, COL_TILE)
    assert m % tm == 0 and n % tn == 0
    groups = tuple(f32_groups) if f32_groups is not None else ((0, n // tn),)
    out_shape, out_specs = [], []
    for first, count in groups:
        out_shape.append(jax.ShapeDtypeStruct((m, count * tn), F32))
        out_specs.append(pl.BlockSpec((tm, tn), lambda i, j, first=first, count=count:
                                      (i, jnp.clip(j - first, 0, count - 1))))
    if emit16:
        out_shape.append(jax.ShapeDtypeStruct((m, n), BF16))
        out_specs.append(pl.BlockSpec((tm, tn), lambda i, j: (i, j)))
    w_block, w_lead = _stacked(lead, d, tn)
    res = pl.pallas_call(
        functools.partial(_rms_matmul_body, groups=groups, emit16=emit16),
        grid=(m // tm, n // tn),
        in_specs=[pl.BlockSpec((tm, d), lambda i, j: (i, 0)),
                  pl.BlockSpec((1, d), lambda i, j: (0, 0)),
                  pl.BlockSpec(w_block, lambda i, j: w_lead + (0, j))],
        out_specs=out_specs,
        out_shape=out_shape,
        scratch_shapes=[pltpu.VMEM((tm, d), BF16)],
        compiler_params=_params("parallel", "arbitrary"),
        name="rms_matmul",
    )(x, g.reshape(1, d), w16)
    return res if len(res) > 1 else res[0]


def _ffn_body(x_ref, gpre_ref, gpost_ref, wg_ref, wu_ref, wd_ref, o_ref, h_scr, acc_scr):
    j = pl.program_id(1)

    @pl.when(j == 0)
    def _():
        h_scr[...] = _rms(x_ref[...], gpre_ref[...]).astype(BF16)
        acc_scr[...] = jnp.zeros_like(acc_scr)

    h = h_scr[...]
    gate = jnp.dot(h, wg_ref[...], preferred_element_type=F32)
    up = jnp.dot(h, wu_ref[...], preferred_element_type=F32)
    act = (gate * jax.nn.sigmoid(gate) * up).astype(BF16)
    acc_scr[...] += jnp.dot(act, wd_ref[...], preferred_element_type=F32)

    @pl.when(j == pl.num_programs(1) - 1)
    def _():
        o_ref[...] = x_ref[...] + 0.5 * _rms(acc_scr[...], gpost_ref[...])


def _ffn(x, g_pre, g_post, wg16, wu16, wd16, lead):
    m, d = x.shape
    ff = wg16.shape[-1]
    tm = min(m, ROW_TILE)
    tf = min(ff, COL_TILE)
    assert m % tm == 0 and ff % tf == 0
    up_block, w_lead = _stacked(lead, d, tf)
    down_block, _ = _stacked(lead, tf, d)
    return pl.pallas_call(
        _ffn_body,
        grid=(m // tm, ff // tf),
        in_specs=[pl.BlockSpec((tm, d), lambda i, j: (i, 0)),
                  pl.BlockSpec((1, d), lambda i, j: (0, 0)),
                  pl.BlockSpec((1, d), lambda i, j: (0, 0)),
                  pl.BlockSpec(up_block, lambda i, j: w_lead + (0, j)),
                  pl.BlockSpec(up_block, lambda i, j: w_lead + (0, j)),
                  pl.BlockSpec(down_block, lambda i, j: w_lead + (j, 0))],
        out_specs=pl.BlockSpec((tm, d), lambda i, j: (i, 0)),
        out_shape=jax.ShapeDtypeStruct((m, d), F32),
        scratch_shapes=[pltpu.VMEM((tm, d), BF16), pltpu.VMEM((tm, d), F32)],
        compiler_params=_params("parallel", "arbitrary"),
        name="ffn",
    )(x, g_pre.reshape(1, d), g_post.reshape(1, d), wg16, wu16, wd16)


def _out_body(*refs, n_parts):
    x_ref, g_ref = refs[0], refs[1]
    o_refs = refs[2:2 + n_parts]
    w_refs = refs[2 + n_parts:2 + 2 * n_parts]
    out_ref = refs[-1]
    y = None
    for o_ref, w_ref in zip(o_refs, w_refs):
        t = jnp.dot(o_ref[...].astype(BF16), w_ref[...], preferred_element_type=F32)
        y = t if y is None else y + t
    out_ref[...] = x_ref[...] + _rms(y, g_ref[...])


def _out_proj(x, g, parts, w16, lead):
    m, d = x.shape
    tm = min(m, ROW_TILE)
    k = parts[0].shape[1]
    assert m % tm == 0 and all(p.shape[1] == k for p in parts) and len(parts) * k == w16.shape[-2]
    w_block, w_lead = _stacked(lead, k, d)
    in_specs = [pl.BlockSpec((tm, d), lambda i: (i, 0)), pl.BlockSpec((1, d), lambda i: (0, 0))]
    in_specs += [pl.BlockSpec((tm, k), lambda i: (i, 0)) for _ in parts]
    in_specs += [pl.BlockSpec(w_block, lambda i, n=n: w_lead + (n, 0)) for n in range(len(parts))]
    return pl.pallas_call(
        functools.partial(_out_body, n_parts=len(parts)),
        grid=(m // tm,),
        in_specs=in_specs,
        out_specs=pl.BlockSpec((tm, d), lambda i: (i, 0)),
        out_shape=jax.ShapeDtypeStruct((m, d), F32),
        compiler_params=_params("parallel"),
        name="out_proj",
    )(x, g.reshape(1, d), *parts, *([w16] * len(parts)))


def _cross_body(q_ref, k_ref, v_ref, o_ref, *, n_heads):
    q = q_ref[...].astype(BF16)
    k = k_ref[...].astype(BF16)
    v = v_ref[...].astype(BF16)
    for h in range(n_heads):
        sl = slice(h * HEAD_DIM, (h + 1) * HEAD_DIM)
        s = lax.dot_general(q[:, sl], k[:, sl], NT_DIMS, preferred_element_type=F32) * HEAD_DIM ** -0.5
        p = jnp.exp(s - jnp.max(s, axis=-1, keepdims=True))
        l = jnp.sum(p, axis=-1, keepdims=True)
        o = jnp.dot(p.astype(BF16), v[:, sl], preferred_element_type=F32) / l
        o_ref[:, sl] = o.astype(o_ref.dtype)


def _cross_attend(q, mem_k, mem_v):
    b, t, w = q.shape
    n_mem = mem_k.shape[1]
    tq = min(t, ROW_TILE)
    assert t % tq == 0
    return pl.pallas_call(
        functools.partial(_cross_body, n_heads=w // HEAD_DIM),
        grid=(b, t // tq),
        in_specs=[pl.BlockSpec((None, tq, w), lambda bi, i: (bi, i, 0)),
                  pl.BlockSpec((None, n_mem, w), lambda bi, i: (bi, 0, 0)),
                  pl.BlockSpec((None, n_mem, w), lambda bi, i: (bi, 0, 0))],
        out_specs=pl.BlockSpec((None, tq, w), lambda bi, i: (bi, i, 0)),
        out_shape=jax.ShapeDtypeStruct((b, t, w), F32),
        compiler_params=_params("parallel", "parallel"),
        name="cross_attend",
    )(q, mem_k, mem_v)


def _diff_lambda(lam_ref, lam_init):
    lv = lam_ref[...]
    return (jnp.exp(jnp.sum(lv[0:1] * lv[1:2], axis=-1, keepdims=True))
            - jnp.exp(jnp.sum(lv[2:3] * lv[3:4], axis=-1, keepdims=True)) + lam_init)


def _fold_lanes(x, op):
    parts = [x[:, c * LANES:(c + 1) * LANES] for c in range(x.shape[1] // LANES)]
    return functools.reduce(op, parts)


def _masked_softmax_pv(n_chunks, scores, values, s_scr):
    n_maps, _, rows, _ = s_scr.shape

    def first(j, mx):
        out = []
        for a, s in enumerate(scores(j)):
            s_scr[a, j] = s
            out.append(jnp.maximum(mx[a], _fold_lanes(s, jnp.maximum)))
        return tuple(out)

    mx = lax.fori_loop(0, n_chunks, first, tuple(jnp.full((rows, LANES), NEG_INF, F32) for _ in range(n_maps)))
    m = [jnp.max(x, axis=-1, keepdims=True) for x in mx]

    def second(j, carry):
        v = values(j)
        out = []
        for a in range(n_maps):
            l, acc = carry[a]
            p = jnp.exp(s_scr[a, j] - m[a])
            out.append((l + _fold_lanes(p, jnp.add), acc + jnp.dot(p.astype(BF16), v, preferred_element_type=F32)))
        return tuple(out)

    init = tuple((jnp.zeros((rows, LANES), F32), jnp.zeros((rows, HEAD_DIM), F32)) for _ in range(n_maps))
    res = lax.fori_loop(0, n_chunks, second, init)
    return [acc / jnp.sum(l, axis=-1, keepdims=True) for l, acc in res]


def _prompt_tiles(t):
    tq = min(t, 256)
    tk = min(t, 512)
    assert t % tk == 0 and tk % tq == 0
    return tq, tk


def _chunk_bias(tab, tq, tk):
    assert tq + 1 >= T5_MAX_DIST and tk % tq == 0
    n_far = tk // tq + 1
    return _dist_tile(tab, tq, tk + n_far * tq, n_far * tq)


def _bias_window(bias_ref, k, tq, tk):
    n_far = tk // tq + 1
    start = pl.multiple_of((n_far - jnp.minimum(k, n_far)) * tq, tq)
    return bias_ref[:, pl.ds(start, tk)]


def _diff_prompt_body(q_ref, k_ref, v_ref, bias_ref, lam_ref, g_ref, o_ref, s_scr, *, tq, tk, lam_init):
    i = pl.program_id(2)
    ratio = tk // tq
    q = q_ref[...]
    q0, q1 = q[:, :DH_A], q[:, DH_A:]
    scale = DH_A ** -0.5

    def scores(j):
        ks = k_ref[pl.ds(pl.multiple_of(j * tk, tk), tk), :]
        bt = _bias_window(bias_ref, i - ratio * j, tq, tk)
        return (lax.dot_general(q0, ks[:, :DH_A], NT_DIMS, preferred_element_type=F32) * scale + bt,
                lax.dot_general(q1, ks[:, DH_A:], NT_DIMS, preferred_element_type=F32) * scale + bt)

    def values(j):
        return v_ref[pl.ds(pl.multiple_of(j * tk, tk), tk), :]

    o0, o1 = _masked_softmax_pv(i // ratio + 1, scores, values, s_scr)
    o = o0 - _diff_lambda(lam_ref, lam_init) * o1
    o_ref[...] = (_rms(o, g_ref[...]) * (1.0 - lam_init)).astype(o_ref.dtype)


def _diff_prompt(p16, col_q, col_k, col_v, n_heads, tab, lam_vec, g_subln, lam_init):
    b, t, _ = p16.shape
    tq, tk = _prompt_tiles(t)
    bias = _chunk_bias(tab, tq, tk)
    cq, ck, cv = col_q // HEAD_DIM, col_k // HEAD_DIM, col_v // HEAD_DIM
    return pl.pallas_call(
        functools.partial(_diff_prompt_body, tq=tq, tk=tk, lam_init=lam_init),
        grid=(b, n_heads, t // tq),
        in_specs=[pl.BlockSpec((None, tq, HEAD_DIM), lambda bi, h, i: (bi, i, cq + h)),
                  pl.BlockSpec((None, t, HEAD_DIM), lambda bi, h, i: (bi, 0, ck + h)),
                  pl.BlockSpec((None, t, HEAD_DIM), lambda bi, h, i: (bi, 0, cv + h)),
                  pl.BlockSpec((None,) + bias.shape[1:], lambda bi, h, i: (h, 0, 0)),
                  pl.BlockSpec(lam_vec.shape, lambda bi, h, i: (0, 0)),
                  pl.BlockSpec((1, HEAD_DIM), lambda bi, h, i: (0, 0))],
        out_specs=pl.BlockSpec((None, tq, HEAD_DIM), lambda bi, h, i: (bi, i, h)),
        out_shape=jax.ShapeDtypeStruct((b, t, n_heads * HEAD_DIM), BF16),
        scratch_shapes=[pltpu.VMEM((2, t // tk, tq, tk), F32)],
        compiler_params=_params("parallel", "parallel", "arbitrary"),
        name="diff_prompt",
    )(p16, p16, p16, bias, lam_vec, g_subln.reshape(1, HEAD_DIM))


def _fox_prompt_body(q_ref, k_ref, v_ref, cum_ref, cumt_ref, o_ref, s_scr, *, tq, tk):
    h = pl.program_id(1)
    i = pl.program_id(2)
    q = q_ref[...]
    cum = cum_ref[...]
    lane = lax.broadcasted_iota(jnp.int32, cum.shape, 1)
    cq = jnp.sum(jnp.where(lane == h, cum, 0.0), axis=-1, keepdims=True)
    ahead = lax.broadcasted_iota(jnp.int32, (tq, tk), 1) - lax.broadcasted_iota(jnp.int32, (tq, tk), 0)

    def scores(j):
        start = pl.multiple_of(j * tk, tk)
        ck = cumt_ref[:, pl.ds(start, tk)]
        s = lax.dot_general(q, k_ref[pl.ds(start, tk), :], NT_DIMS, preferred_element_type=F32)
        s = s * HEAD_DIM ** -0.5 + (cq - ck)
        return (jnp.where(ahead <= i * tq - j * tk, s, NEG_INF),)

    def values(j):
        return v_ref[pl.ds(pl.multiple_of(j * tk, tk), tk), :]

    o, = _masked_softmax_pv(i // (tk // tq) + 1, scores, values, s_scr)
    o_ref[...] = o.astype(o_ref.dtype)


def _fox_prompt(p16, col_q, col_k, col_v, n_heads, cum):
    b, t, _ = p16.shape
    tq, tk = _prompt_tiles(t)
    cq, ck, cv = col_q // HEAD_DIM, col_k // HEAD_DIM, col_v // HEAD_DIM
    cum_t = jnp.swapaxes(cum, 1, 2).reshape(b, n_heads, 1, t)
    return pl.pallas_call(
        functools.partial(_fox_prompt_body, tq=tq, tk=tk),
        grid=(b, n_heads, t // tq),
        in_specs=[pl.BlockSpec((None, tq, HEAD_DIM), lambda bi, h, i: (bi, i, cq + h)),
                  pl.BlockSpec((None, t, HEAD_DIM), lambda bi, h, i: (bi, 0, ck + h)),
                  pl.BlockSpec((None, t, HEAD_DIM), lambda bi, h, i: (bi, 0, cv + h)),
                  pl.BlockSpec((None, tq, n_heads), lambda bi, h, i: (bi, i, 0)),
                  pl.BlockSpec((None, None, 1, t), lambda bi, h, i: (bi, h, 0, 0))],
        out_specs=pl.BlockSpec((None, tq, HEAD_DIM), lambda bi, h, i: (bi, i, h)),
        out_shape=jax.ShapeDtypeStruct((b, t, n_heads * HEAD_DIM), BF16),
        scratch_shapes=[pltpu.VMEM((1, t // tk, tq, tk), F32)],
        compiler_params=_params("parallel", "parallel", "arbitrary"),
        name="fox_prompt",
    )(p16, p16, p16, cum, cum_t)


def _top_blocks_negmask(gate, n_top, limit):
    n_blk = gate.shape[1]
    lane = lax.broadcasted_iota(jnp.int32, gate.shape, 1).astype(F32)
    chosen = jnp.zeros(gate.shape, F32)
    g = gate
    for _ in range(n_top):
        mx = jnp.max(g, axis=-1, keepdims=True)
        idx = jnp.min(jnp.where(g == mx, lane, float(n_blk)), axis=-1, keepdims=True)
        pick = lane == idx
        chosen = jnp.where(pick & (idx < limit), 1.0, chosen)
        g = jnp.where(pick, NEG_INF, g)
    return jnp.where(chosen > 0.0, 0.0, NEG_INF)


def _moba_prompt_body(q_ref, k_ref, v_ref, q32_ref, k32_ref, bias_ref, o_ref, kmean_scr, s_scr, *, n_blk, tk):
    blk = MOBA_BLOCK
    ratio = tk // blk
    i = pl.program_id(2)

    @pl.when(i == 0)
    def _():
        kmean_scr[...] = jnp.mean(k32_ref[...].reshape(n_blk, blk, HEAD_DIM), axis=1)

    gate = lax.dot_general(q32_ref[...], kmean_scr[...], NT_DIMS, preferred_element_type=F32,
                           precision=lax.Precision.HIGHEST)
    lane = lax.broadcasted_iota(jnp.int32, gate.shape, 1)
    gate = jnp.where(lane < i, gate, NEG_INF)
    sel = _top_blocks_negmask(gate, min(MOBA_TOPK, n_blk), i.astype(F32))
    sel = jnp.where(lane == i, 0.0, sel)
    q = q_ref[...]

    def scores(j):
        ks = k_ref[pl.ds(pl.multiple_of(j * tk, tk), tk), :]
        s = lax.dot_general(q, ks, NT_DIMS, preferred_element_type=F32) * HEAD_DIM ** -0.5
        s = s + _bias_window(bias_ref, i - ratio * j, blk, tk)
        parts = []
        for c in range(ratio):
            keep = jnp.min(jnp.where(lane == j * ratio + c, sel, 0.0), axis=-1, keepdims=True)
            parts.append(s[:, c * blk:(c + 1) * blk] + keep)
        return (parts[0] if ratio == 1 else jnp.concatenate(parts, axis=1),)

    def values(j):
        return v_ref[pl.ds(pl.multiple_of(j * tk, tk), tk), :]

    o, = _masked_softmax_pv(i // ratio + 1, scores, values, s_scr)
    o_ref[...] = o.astype(o_ref.dtype)


def _moba_prompt(p16, q32, k32, col_q, col_k, col_v, n_heads, tab):
    b, t, _ = p16.shape
    blk = MOBA_BLOCK
    assert t % blk == 0
    tk = min(t, 2 * blk)
    assert t % tk == 0
    bias = _chunk_bias(tab, blk, tk)
    cq, ck, cv = col_q // HEAD_DIM, col_k // HEAD_DIM, col_v // HEAD_DIM
    return pl.pallas_call(
        functools.partial(_moba_prompt_body, n_blk=t // blk, tk=tk),
        grid=(b, n_heads, t // blk),
        in_specs=[pl.BlockSpec((None, blk, HEAD_DIM), lambda bi, h, i: (bi, i, cq + h)),
                  pl.BlockSpec((None, t, HEAD_DIM), lambda bi, h, i: (bi, 0, ck + h)),
                  pl.BlockSpec((None, t, HEAD_DIM), lambda bi, h, i: (bi, 0, cv + h)),
                  pl.BlockSpec((None, blk, HEAD_DIM), lambda bi, h, i: (bi, i, h)),
                  pl.BlockSpec((None, t, HEAD_DIM), lambda bi, h, i: (bi, 0, h)),
                  pl.BlockSpec((None,) + bias.shape[1:], lambda bi, h, i: (h, 0, 0))],
        out_specs=pl.BlockSpec((None, blk, HEAD_DIM), lambda bi, h, i: (bi, i, h)),
        out_shape=jax.ShapeDtypeStruct((b, t, n_heads * HEAD_DIM), BF16),
        scratch_shapes=[pltpu.VMEM((t // blk, HEAD_DIM), F32), pltpu.VMEM((1, t // tk, blk, tk), F32)],
        compiler_params=_params("parallel", "parallel", "arbitrary"),
        name="moba_prompt",
    )(p16, p16, p16, q32, k32, bias)


KEY_SIGN = -2 ** 31
KEY_OF_NEG_INF = -2139095041


def _order_key(score):
    bits = pltpu.bitcast(score, jnp.int32)
    key = jnp.where(bits < 0, bits ^ 0x7FFFFFFF, bits)
    return jnp.where(score == 0.0, 0, key)


def _kth_largest_key(count_ge, n_rows, k):
    def bit_body(b, ans):
        cand = ans | jnp.left_shift(jnp.int32(1), 31 - b)
        cnt, = count_ge([cand ^ KEY_SIGN])
        return jnp.where(cnt >= k, cand, ans)

    ans = lax.fori_loop(0, 32, bit_body, jnp.zeros((n_rows, 1), jnp.int32))
    return ans ^ KEY_SIGN


def _dsa_prompt_body(iq_ref, ikw_ref, kidx_ref, qb_ref, kb_ref, vb_ref, bias_ref, o_ref, key_scr, nm_scr, w_scr,
                     s_scr, *, tq, n_heads, n_sel):
    i = pl.program_id(1)
    n_chunks = i + 1
    iq = iq_ref[...].reshape(N_IDX_HEADS * tq, IDX_DIM)
    w = ikw_ref[:, IDX_DIM:IDX_DIM + N_IDX_HEADS] * N_IDX_HEADS ** -0.5 * IDX_DIM ** -0.5
    for n in range(N_IDX_HEADS):
        w_scr[n] = jnp.broadcast_to(w[:, n:n + 1], (tq, tq))
    row = lax.broadcasted_iota(jnp.int32, (tq, tq), 0)
    col = lax.broadcasted_iota(jnp.int32, (tq, tq), 1)

    def score_body(j, _):
        kc = kidx_ref[pl.ds(pl.multiple_of(j * tq, tq), tq), :][:, :IDX_DIM]
        rel = jnp.maximum(lax.dot_general(iq, kc, NT_DIMS, preferred_element_type=F32), 0.0)
        rel = rel.reshape(N_IDX_HEADS, tq, tq)
        sc = w_scr[0] * rel[0]
        for n in range(1, N_IDX_HEADS):
            sc = sc + w_scr[n] * rel[n]
        sc = jnp.where((j < i) | (col <= row), sc, NEG_INF)
        key_scr[j] = _order_key(sc)
        return 0

    lax.fori_loop(0, n_chunks, score_body, 0)

    def count_many(preds):
        def body(j, accs):
            k = key_scr[j]
            return tuple(acc + jnp.where(pred(k), 1.0, 0.0) for acc, pred in zip(accs, preds))
        accs = lax.fori_loop(0, n_chunks, body, tuple(jnp.zeros((tq, tq), F32) for _ in preds))
        return [jnp.sum(acc, axis=-1, keepdims=True) for acc in accs]

    def count(pred):
        return count_many([pred])[0]

    thr = _kth_largest_key(lambda ts: count_many([(lambda k, t=t: k >= t) for t in ts]), tq, n_sel)
    cnt_ge = count(lambda k: k >= thr)
    tie = jnp.max(jnp.where((cnt_ge > n_sel) & (thr > KEY_OF_NEG_INF), 1.0, 0.0)) > 0.0

    @pl.when(jnp.logical_not(tie))
    def _():
        def body(j, _):
            nm_scr[j] = jnp.where(key_scr[j] >= thr, 0.0, NEG_INF)
            return 0
        lax.fori_loop(0, n_chunks, body, 0)

    @pl.when(tie)
    def _():
        allow = n_sel - count(lambda k: k > thr)
        tri = jnp.where(row <= col, 1.0, 0.0).astype(BF16)

        def body(j, before):
            k = key_scr[j]
            eq = jnp.where(k == thr, 1.0, 0.0)
            rank = jnp.dot(eq.astype(BF16), tri, preferred_element_type=F32) + before
            keep = jnp.where(k > thr, 1.0, jnp.where(rank <= allow, eq, 0.0))
            nm_scr[j] = jnp.where(keep > 0.0, 0.0, NEG_INF)
            return before + jnp.sum(eq, axis=-1, keepdims=True)
        lax.fori_loop(0, n_chunks, body, jnp.zeros((tq, 1), F32))

    qb = qb_ref[...]
    qs = jnp.concatenate([qb[:, h * HEAD_DIM:(h + 1) * HEAD_DIM] for h in range(n_heads)], axis=0)

    def scores(j):
        rows = pl.ds(pl.multiple_of(j * tq, tq), tq)
        s = lax.dot_general(qs, kb_ref[rows, :], NT_DIMS, preferred_element_type=F32) * HEAD_DIM ** -0.5
        return (s + _bias_window(bias_ref, i - j, tq, tq) + jnp.tile(nm_scr[j], (n_heads, 1)),)

    def values(j):
        return vb_ref[pl.ds(pl.multiple_of(j * tq, tq), tq), :]

    o, = _masked_softmax_pv(n_chunks, scores, values, s_scr)
    for h in range(n_heads):
        o_ref[:, h * HEAD_DIM:(h + 1) * HEAD_DIM] = o[h * tq:(h + 1) * tq].astype(o_ref.dtype)


def _dsa_prompt(p16, small32, iq_t, col_qb, col_kb, col_vb, col_ik, col_ik32, n_heads, bias, n_sel, tq):
    b, t, _ = p16.shape
    qw = n_heads * HEAD_DIM
    assert col_qb % qw == 0 and col_ik % LANES == 0 and col_ik32 % LANES == 0
    return pl.pallas_call(
        functools.partial(_dsa_prompt_body, tq=tq, n_heads=n_heads, n_sel=n_sel),
        grid=(b, t // tq),
        in_specs=[pl.BlockSpec((None, N_IDX_HEADS, tq, IDX_DIM), lambda bi, i: (bi, 0, i, 0)),
                  pl.BlockSpec((None, tq, LANES), lambda bi, i: (bi, i, col_ik32 // LANES)),
                  pl.BlockSpec((None, t, LANES), lambda bi, i: (bi, 0, col_ik // LANES)),
                  pl.BlockSpec((None, tq, qw), lambda bi, i: (bi, i, col_qb // qw)),
                  pl.BlockSpec((None, t, HEAD_DIM), lambda bi, i: (bi, 0, col_kb // HEAD_DIM)),
                  pl.BlockSpec((None, t, HEAD_DIM), lambda bi, i: (bi, 0, col_vb // HEAD_DIM)),
                  pl.BlockSpec(bias.shape, lambda bi, i: (0, 0))],
        out_specs=pl.BlockSpec((None, tq, qw), lambda bi, i: (bi, i, 0)),
        out_shape=jax.ShapeDtypeStruct((b, t, qw), BF16),
        scratch_shapes=[pltpu.VMEM((t // tq, tq, tq), jnp.int32), pltpu.VMEM((t // tq, tq, tq), F32),
                        pltpu.VMEM((N_IDX_HEADS, tq, tq), F32), pltpu.VMEM((1, t // tq, n_heads * tq, tq), F32)],
        compiler_params=_params("parallel", "arbitrary"),
        name="dsa_prompt",
    )(iq_t, small32, p16, p16, p16, p16, bias)


def _block_diag_rows(q, dtype):
    b, tn, g, d = q.shape
    eye = jnp.eye(g, dtype=q.dtype)
    return jnp.einsum('btgd,gk->bgtkd', q, eye).reshape(b, g * tn, g * d).astype(dtype)


def _page_specs(layer, n_pages, group, rows, width):
    def spec(g):
        return pl.BlockSpec((None, None, rows, width),
                            lambda bi, p, pt: (layer, pt[bi, jnp.minimum(p * group + g, n_pages - 1)], 0, 0))
    return [spec(g) for g in range(group)]


def _past_bias_index(page_idx, n_pages):
    return jnp.clip(page_idx - (n_pages - 2), 0, 1)


def _paged_call(body, pt, n_steps, operands, in_specs, out_shape, out_spec, scratch, name):
    return pl.pallas_call(
        body,
        grid_spec=pltpu.PrefetchScalarGridSpec(
            num_scalar_prefetch=1, grid=(pt.shape[0], n_steps),
            in_specs=in_specs, out_specs=out_spec, scratch_shapes=scratch),
        out_shape=out_shape,
        compiler_params=_params("parallel", "arbitrary"),
        name=name,
    )(pt, *operands)


def _per_batch(rows, width):
    return pl.BlockSpec((None, rows, width), lambda bi, p, pt: (bi, 0, 0))


def _whole(shape):
    return pl.BlockSpec(shape, lambda bi, p, pt: (0,) * len(shape))


def _rows_page(x, page):
    b, tn, w = x.shape
    return jnp.pad(x, ((0, 0), (0, page - tn), (0, 0))).reshape(b, page * (w // HEAD_DIM), HEAD_DIM)


def _transposed_page(x, page):
    return jnp.pad(jnp.swapaxes(x, 1, 2), ((0, 0), (0, 0), (0, page - x.shape[1])))


def _flash_scratch(rows, width):
    return [pltpu.VMEM((rows, 1), F32), pltpu.VMEM((rows, 1), F32), pltpu.VMEM((rows, width), F32)]


def _flash_init(m_scr, l_scr, acc_scr):
    m_scr[...] = jnp.full(m_scr.shape, NEG_INF, F32)
    l_scr[...] = jnp.zeros(l_scr.shape, F32)
    acc_scr[...] = jnp.zeros(acc_scr.shape, F32)


def _flash_update(s, pv, m_scr, l_scr, acc_scr):
    m = m_scr[...]
    m_new = jnp.maximum(m, jnp.max(s, axis=-1, keepdims=True))
    m_safe = jnp.where(m_new == NEG_INF, 0.0, m_new)
    p = jnp.exp(s - m_safe)
    alpha = jnp.exp(m - m_safe)
    l_scr[...] = alpha * l_scr[...] + jnp.sum(p, axis=-1, keepdims=True)
    acc_scr[...] = alpha * acc_scr[...] + pv(p.astype(BF16))
    m_scr[...] = m_new


def _cat16(refs, axis):
    parts = [r[...].astype(BF16) for r in refs]
    return parts[0] if len(parts) == 1 else jnp.concatenate(parts, axis=axis)


SAMPLE_PAGE_GROUP = 8
SMALL_PAGE_GROUP = 8


def _heads_first(o, b, n_heads, tn):
    return jnp.swapaxes(o.reshape(b, n_heads, tn, HEAD_DIM), 1, 2).reshape(b, tn, n_heads * HEAD_DIM)


def _diff_sample_body(pt_ref, wq_ref, bias_ref, lam_ref, g_ref, kn_ref, vn_ref, *rest,
                      n_pages, group, page, n_new, n_heads, lam_init):
    k_refs, v_refs = rest[:group], rest[group:2 * group]
    o_ref, m_scr, l_scr, acc_scr = rest[2 * group:]
    p = pl.program_id(1)
    rows_h = 2 * n_new

    @pl.when(p == 0)
    def _():
        _flash_init(m_scr, l_scr, acc_scr)

    def step(ks, vs, bias):
        s = jnp.dot(wq_ref[...], _cat16(ks, 1), preferred_element_type=F32) * DH_A ** -0.5 + bias

        def pv(p16):
            outs = []
            for h in range(n_heads):
                vh = [v[pl.ds(h, page, stride=n_heads), :].astype(BF16) for v in vs]
                vh = vh[0] if len(vh) == 1 else jnp.concatenate(vh, axis=0)
                outs.append(jnp.dot(p16[h * rows_h:(h + 1) * rows_h], vh, preferred_element_type=F32))
            return jnp.concatenate(outs, axis=0)

        _flash_update(s, pv, m_scr, l_scr, acc_scr)

    @pl.when(p < n_pages // group)
    def _():
        tiles = [bias_ref[_past_bias_index(p * group + g, n_pages)] for g in range(group)]
        step(k_refs, v_refs, jnp.concatenate(tiles, axis=1))

    @pl.when(p == n_pages // group)
    def _():
        step([kn_ref], [vn_ref], bias_ref[2])
        lam = _diff_lambda(lam_ref, lam_init)
        on = acc_scr[...] / l_scr[...]
        for h in range(n_heads):
            r0 = h * rows_h
            o = on[r0:r0 + n_new] - lam * on[r0 + n_new:r0 + rows_h]
            o_ref[h * n_new:(h + 1) * n_new, :] = _rms(o, g_ref[...]) * (1.0 - lam_init)


def _diff_sample(pt, layer, q, cache_kt, cache_v, k_new, v_new, tab, lam_vec, g_subln, lam_init):
    b, tn, n_heads = q.shape[:3]
    n_pages = pt.shape[1]
    width, page = cache_kt.shape[2:]
    group = min(SAMPLE_PAGE_GROUP, n_pages)
    assert n_pages % group == 0
    wq = _block_diag_rows(q.reshape(b, tn, 2 * n_heads, DH_A), BF16)
    rows = 2 * n_heads * tn
    bias = _sample_bias(tab, tn, page)
    bias = jnp.broadcast_to(bias[:, :, None], (3, n_heads, 2, tn, page)).reshape(3, rows, page)
    out = _paged_call(
        functools.partial(_diff_sample_body, n_pages=n_pages, group=group, page=page, n_new=tn, n_heads=n_heads,
                          lam_init=lam_init),
        pt, n_pages // group + 1,
        (wq, bias, lam_vec, g_subln.reshape(1, HEAD_DIM), _transposed_page(k_new, page), _rows_page(v_new, page))
        + (cache_kt,) * group + (cache_v,) * group,
        [_per_batch(rows, width), _whole(bias.shape), _whole(lam_vec.shape), _whole((1, HEAD_DIM)),
         _per_batch(width, page), _per_batch(page * n_heads, HEAD_DIM)]
        + _page_specs(layer, n_pages, group, width, page) + _page_specs(layer, n_pages, group, page * n_heads, HEAD_DIM),
        jax.ShapeDtypeStruct((b, n_heads * tn, HEAD_DIM), F32), _per_batch(n_heads * tn, HEAD_DIM),
        _flash_scratch(rows, HEAD_DIM), "diff_sample")
    return _heads_first(out, b, n_heads, tn)


def _fox_sample_body(pt_ref, q_ref, cq_ref, ckp_ref, ckn_ref, hm_ref, nm_ref, kn_ref, vn_ref, *rest,
                     n_pages, group):
    k_refs, v_refs = rest[:group], rest[group:2 * group]
    o_ref, m_scr, l_scr, acc_scr = rest[2 * group:]
    p = pl.program_id(1)

    @pl.when(p == 0)
    def _():
        _flash_init(m_scr, l_scr, acc_scr)

    def step(ks, vs, ck, mask):
        s = lax.dot_general(q_ref[...], _cat16(ks, 0), NT_DIMS, preferred_element_type=F32)
        s = s * HEAD_DIM ** -0.5 + (cq_ref[...] - ck) + mask
        _flash_update(s, lambda p16: jnp.dot(p16, _cat16(vs, 0), preferred_element_type=F32), m_scr, l_scr, acc_scr)

    @pl.when(p < n_pages // group)
    def _():
        step(k_refs, v_refs, ckp_ref[...], hm_ref[...])

    @pl.when(p == n_pages // group)
    def _():
        step([kn_ref], [vn_ref], ckn_ref[...], nm_ref[...])
        o_ref[...] = acc_scr[...] / l_scr[...]


def _fox_sample(pt, layer, q, cache_k, cache_v, k_new, v_new, cum_q, cum_past, cum_new):
    b, tn, n_heads = q.shape[:3]
    n_pages = pt.shape[1]
    prow = cache_k.shape[2]
    page = prow // n_heads
    group = min(SAMPLE_PAGE_GROUP, n_pages)
    assert n_pages % group == 0
    rows = n_heads * tn
    q_rows = jnp.swapaxes(q, 1, 2).reshape(b, rows, HEAD_DIM).astype(BF16)
    cq = jnp.swapaxes(cum_q, 1, 2).reshape(b, rows, 1)
    ck_past = cum_past.reshape(b, n_pages // group, 1, group * prow)
    ck_new = jnp.pad(cum_new, ((0, 0), (0, page - tn), (0, 0))).reshape(b, 1, prow)
    head_mask = _expand_heads(jnp.zeros((n_heads, tn, page), F32))
    qi = np.arange(tn)[:, None]
    ci = np.arange(page)[None, :]
    causal = np.broadcast_to(np.where(ci <= qi, 0.0, NEG_INF).astype(np.float32), (n_heads, tn, page))
    new_mask = _expand_heads(jnp.asarray(causal))
    out = _paged_call(
        functools.partial(_fox_sample_body, n_pages=n_pages, group=group),
        pt, n_pages // group + 1,
        (q_rows, cq, ck_past, ck_new, jnp.tile(head_mask, (1, group)), new_mask,
         _rows_page(k_new, page), _rows_page(v_new, page)) + (cache_k,) * group + (cache_v,) * group,
        [_per_batch(rows, HEAD_DIM), _per_batch(rows, 1),
         pl.BlockSpec((None, None, 1, group * prow),
                      lambda bi, p, pt_: (bi, jnp.minimum(p, n_pages // group - 1), 0, 0)),
         _per_batch(1, prow), _whole((rows, group * prow)), _whole((rows, prow)),
         _per_batch(prow, HEAD_DIM), _per_batch(prow, HEAD_DIM)]
        + _page_specs(layer, n_pages, group, prow, HEAD_DIM) + _page_specs(layer, n_pages, group, prow, HEAD_DIM),
        jax.ShapeDtypeStruct((b, rows, HEAD_DIM), F32), _per_batch(rows, HEAD_DIM),
        _flash_scratch(rows, HEAD_DIM), "fox_sample")
    return _heads_first(out, b, n_heads, tn)


def _kmean_body(pt_ref, *refs, n_heads, pages_per_block):
    k_refs, o_ref = refs[:-1], refs[-1]
    for blk in range(len(k_refs) // pages_per_block):
        total = None
        for k_ref in k_refs[blk * pages_per_block:(blk + 1) * pages_per_block]:
            k = k_ref[...]
            s = jnp.sum(k.reshape(k.shape[0] // n_heads, n_heads, HEAD_DIM), axis=0)
            total = s if total is None else total + s
        o_ref[blk] = total * (1.0 / MOBA_BLOCK)


def _moba_gate_body(wq_ref, kmean_ref, o_ref, *, n_blk):
    gate = lax.dot_general(wq_ref[...], kmean_ref[...], NT_DIMS, preferred_element_type=F32,
                           precision=lax.Precision.HIGHEST)
    o_ref[...] = _top_blocks_negmask(gate, min(MOBA_TOPK, n_blk), float(n_blk))


def _moba_sample_body(pt_ref, q_ref, bias_ref, sel_ref, kn_ref, vn_ref, *rest, n_pages, group, pages_per_block):
    k_refs, v_refs = rest[:group], rest[group:2 * group]
    o_ref, m_scr, l_scr, acc_scr = rest[2 * group:]
    p = pl.program_id(1)

    @pl.when(p == 0)
    def _():
        _flash_init(m_scr, l_scr, acc_scr)

    def scores(k_ref, bias):
        s = lax.dot_general(q_ref[...], k_ref[...].astype(BF16), NT_DIMS, preferred_element_type=F32)
        return s * HEAD_DIM ** -0.5 + bias

    def update(s, vs):
        _flash_update(s, lambda p16: jnp.dot(p16, _cat16(vs, 0), preferred_element_type=F32), m_scr, l_scr, acc_scr)

    @pl.when(p < n_pages // group)
    def _():
        sel = sel_ref[...]
        lane = lax.broadcasted_iota(jnp.int32, sel.shape, 1)
        parts = []
        for g in range(group):
            page_idx = p * group + g
            row_mask = jnp.min(jnp.where(lane == page_idx // pages_per_block, sel, 0.0), axis=-1, keepdims=True)
            parts.append(scores(k_refs[g], bias_ref[_past_bias_index(page_idx, n_pages)]) + row_mask)
        update(parts[0] if group == 1 else jnp.concatenate(parts, axis=1), v_refs)

    @pl.when(p == n_pages // group)
    def _():
        update(scores(kn_ref, bias_ref[2]), [vn_ref])
        o_ref[...] = acc_scr[...] / l_scr[...]


def _moba_sample(pt, layer, q, cache_k, cache_v, k_new, v_new, tab):
    b, tn, n_heads = q.shape[:3]
    n_pages = pt.shape[1]
    prow = cache_k.shape[2]
    page = prow // n_heads
    width = n_heads * HEAD_DIM
    assert MOBA_BLOCK % page == 0 and (n_pages * page) % MOBA_BLOCK == 0 and tn < MOBA_BLOCK
    ppb = MOBA_BLOCK // page
    n_blk = n_pages // ppb
    group = min(SAMPLE_PAGE_GROUP, n_pages)
    assert n_pages % group == 0
    rows = n_heads * tn
    bps = group // ppb if group % ppb == 0 else 1
    assert n_blk % bps == 0
    kmean = _paged_call(
        functools.partial(_kmean_body, n_heads=n_heads, pages_per_block=ppb), pt, n_blk // bps,
        (cache_k,) * (bps * ppb), _page_specs(layer, n_pages, bps * ppb, prow, HEAD_DIM),
        jax.ShapeDtypeStruct((b, n_blk, n_heads, HEAD_DIM), F32),
        pl.BlockSpec((None, bps, n_heads, HEAD_DIM), lambda bi, p, pt_: (bi, p, 0, 0)), [], "moba_kmean")
    sel = pl.pallas_call(
        functools.partial(_moba_gate_body, n_blk=n_blk),
        grid=(b,),
        in_specs=[pl.BlockSpec((None, rows, width), lambda bi: (bi, 0, 0)),
                  pl.BlockSpec((None, n_blk, width), lambda bi: (bi, 0, 0))],
        out_specs=pl.BlockSpec((None, rows, n_blk), lambda bi: (bi, 0, 0)),
        out_shape=jax.ShapeDtypeStruct((b, rows, n_blk), F32),
        compiler_params=_params("parallel"),
        name="moba_gate",
    )(_block_diag_rows(q, F32), kmean.reshape(b, n_blk, width))
    bias = _sample_bias(tab, tn, page)
    bias = jnp.stack([_expand_heads(bias[i]) for i in range(3)])
    q_rows = jnp.swapaxes(q, 1, 2).reshape(b, rows, HEAD_DIM).astype(BF16)
    out = _paged_call(
        functools.partial(_moba_sample_body, n_pages=n_pages, group=group, pages_per_block=ppb),
        pt, n_pages // group + 1,
        (q_rows, bias, sel, _rows_page(k_new, page), _rows_page(v_new, page)) + (cache_k,) * group + (cache_v,) * group,
        [_per_batch(rows, HEAD_DIM), _whole(bias.shape), _per_batch(rows, n_blk),
         _per_batch(prow, HEAD_DIM), _per_batch(prow, HEAD_DIM)]
        + _page_specs(layer, n_pages, group, prow, HEAD_DIM) + _page_specs(layer, n_pages, group, prow, HEAD_DIM),
        jax.ShapeDtypeStruct((b, rows, HEAD_DIM), F32), _per_batch(rows, HEAD_DIM),
        _flash_scratch(rows, HEAD_DIM), "moba_sample")
    return _heads_first(out, b, n_heads, tn)


def _dsa_score_body(pt_ref, iq_ref, w_ref, mask_ref, kn_ref, *rest, n_pages, group, n_new):
    k_refs, o_ref = rest[:group], rest[group]
    p = pl.program_id(1)

    def score(kt16):
        rel = jnp.dot(iq_ref[...], kt16, preferred_element_type=F32)
        rel = jnp.maximum(rel * IDX_DIM ** -0.5, 0.0) * w_ref[...]
        return jnp.sum(rel.reshape(N_IDX_HEADS, n_new, rel.shape[1]), axis=0)

    @pl.when(p < n_pages // group)
    def _():
        o_ref[...] = score(_cat16(k_refs, 1))

    @pl.when(p == n_pages // group)
    def _():
        page = kn_ref.shape[1]
        o_ref[...] = jnp.full(o_ref.shape, NEG_INF, F32)
        o_ref[:, :page] = score(kn_ref[...].astype(BF16)) + mask_ref[...]


def _dsa_select_body(sc_ref, o_ref, key_scr, *, n_new, n_sel, chunk):
    width = sc_ref.shape[1]
    key_scr[...] = _order_key(sc_ref[...])

    def count(pred):
        return jnp.sum(jnp.where(pred(key_scr[...]), 1.0, 0.0), axis=-1, keepdims=True)

    thr = _kth_largest_key(lambda ts: [count(lambda k, t=t: k >= t) for t in ts], n_new, n_sel)
    cnt_ge = count(lambda k: k >= thr)
    tie = jnp.max(jnp.where((cnt_ge > n_sel) & (thr > KEY_OF_NEG_INF), 1.0, 0.0)) > 0.0

    @pl.when(jnp.logical_not(tie))
    def _():
        o_ref[...] = jnp.where(key_scr[...] >= thr, 0.0, NEG_INF)

    @pl.when(tie)
    def _():
        allow = n_sel - count(lambda k: k > thr)
        r = lax.broadcasted_iota(jnp.int32, (chunk, chunk), 0)
        c = lax.broadcasted_iota(jnp.int32, (chunk, chunk), 1)
        tri = jnp.where(r <= c, 1.0, 0.0).astype(BF16)

        def body(j, before):
            cols = pl.ds(pl.multiple_of(j * chunk, chunk), chunk)
            k = key_scr[:, cols]
            eq = jnp.where(k == thr, 1.0, 0.0)
            rank = jnp.dot(eq.astype(BF16), tri, preferred_element_type=F32) + before
            keep = jnp.where(k > thr, 1.0, jnp.where(rank <= allow, eq, 0.0))
            o_ref[:, cols] = jnp.where(keep > 0.0, 0.0, NEG_INF)
            return before + jnp.sum(eq, axis=-1, keepdims=True)
        lax.fori_loop(0, width // chunk, body, jnp.zeros((n_new, 1), F32))


def _dsa_sample_body(pt_ref, q_ref, bias_ref, nm_ref, kn_ref, vn_ref, *rest, n_pages, group, n_heads):
    k_refs, v_refs = rest[:group], rest[group:2 * group]
    o_ref, m_scr, l_scr, acc_scr = rest[2 * group:]
    p = pl.program_id(1)

    @pl.when(p == 0)
    def _():
        _flash_init(m_scr, l_scr, acc_scr)

    def step(ks, vs, bias, nm):
        s = lax.dot_general(q_ref[...], _cat16(ks, 0), NT_DIMS, preferred_element_type=F32)
        s = s * HEAD_DIM ** -0.5 + bias + jnp.tile(nm, (n_heads, 1))
        _flash_update(s, lambda p16: jnp.dot(p16, _cat16(vs, 0), preferred_element_type=F32), m_scr, l_scr, acc_scr)

    @pl.when(p < n_pages // group)
    def _():
        tiles = [bias_ref[_past_bias_index(p * group + g, n_pages)] for g in range(group)]
        step(k_refs, v_refs, jnp.concatenate(tiles, axis=1), nm_ref[...])

    @pl.when(p == n_pages // group)
    def _():
        page = kn_ref.shape[0]
        step([kn_ref], [vn_ref], bias_ref[2], nm_ref[:, :page])
        o_ref[...] = acc_scr[...] / l_scr[...]


def _dsa_sample(pt, layer, qb, iq, iw, cache_k, cache_v, cache_idx_t, k_new, v_new, ik_new, tab):
    b, tn, n_heads = qb.shape[:3]
    n_pages = pt.shape[1]
    page = cache_k.shape[2]
    n_sel = min(DSA_TOPK, (n_pages * page + tn) // 4)
    group = min(SMALL_PAGE_GROUP, n_pages)
    assert n_pages % group == 0
    n_steps = n_pages // group + 1
    width = n_steps * group * page
    pad = ((0, 0), (0, page - tn), (0, 0))
    qi = np.arange(tn)[:, None]
    ci = np.arange(page)[None, :]
    new_mask = jnp.asarray(np.where(ci <= qi, 0.0, NEG_INF).astype(np.float32))
    n_iq = N_IDX_HEADS * tn
    iq_rows = jnp.swapaxes(iq, 1, 2).reshape(b, n_iq, IDX_DIM).astype(BF16)
    w_rows = (jnp.swapaxes(iw, 1, 2).astype(F32) * N_IDX_HEADS ** -0.5).reshape(b, n_iq, 1)
    step_cols = pl.BlockSpec((None, tn, group * page), lambda bi, p, pt_: (bi, 0, p))
    scores = _paged_call(
        functools.partial(_dsa_score_body, n_pages=n_pages, group=group, n_new=tn),
        pt, n_steps,
        (iq_rows, w_rows, new_mask, _transposed_page(ik_new, page)) + (cache_idx_t,) * group,
        [_per_batch(n_iq, IDX_DIM), _per_batch(n_iq, 1), _whole(new_mask.shape), _per_batch(IDX_DIM, page)]
        + _page_specs(layer, n_pages, group, IDX_DIM, page),
        jax.ShapeDtypeStruct((b, tn, width), F32), step_cols, [], "dsa_score")
    negmask = pl.pallas_call(
        functools.partial(_dsa_select_body, n_new=tn, n_sel=n_sel, chunk=page),
        grid=(b,),
        in_specs=[pl.BlockSpec((None, tn, width), lambda bi: (bi, 0, 0))],
        out_specs=pl.BlockSpec((None, tn, width), lambda bi: (bi, 0, 0)),
        out_shape=jax.ShapeDtypeStruct((b, tn, width), F32),
        scratch_shapes=[pltpu.VMEM((tn, width), jnp.int32)],
        compiler_params=_params("parallel"),
        name="dsa_select",
    )(scores)
    rows = n_heads * tn
    q_rows = jnp.swapaxes(qb, 1, 2).reshape(b, rows, HEAD_DIM).astype(BF16)
    bias = _sample_bias(tab, tn, page).reshape(3, rows, page)
    out = _paged_call(
        functools.partial(_dsa_sample_body, n_pages=n_pages, group=group, n_heads=n_heads),
        pt, n_steps,
        (q_rows, bias, negmask, jnp.pad(k_new, pad), jnp.pad(v_new, pad)) + (cache_k,) * group + (cache_v,) * group,
        [_per_batch(rows, HEAD_DIM), _whole(bias.shape), step_cols, _per_batch(page, HEAD_DIM),
         _per_batch(page, HEAD_DIM)]
        + _page_specs(layer, n_pages, group, page, HEAD_DIM) + _page_specs(layer, n_pages, group, page, HEAD_DIM),
        jax.ShapeDtypeStruct((b, rows, HEAD_DIM), F32), _per_batch(rows, HEAD_DIM),
        _flash_scratch(rows, HEAD_DIM), "dsa_sample")
    return _heads_first(out, b, n_heads, tn)


def _pad_cols(w, n):
    return jnp.pad(w, ((0, 0),) * (w.ndim - 1) + ((0, n - w.shape[-1]),))


def _mixer_even(h_in, g, w_in16, past, pt, layer_e, lam_vec, g_subln, lam_init, tab, b, t):
    d = h_in.shape[1]
    n_a = n_b = (d // HEAD_DIM) // 2
    wa, wb = n_a * HEAD_DIM, n_b * HEAD_DIM
    wi = N_IDX_HEADS * IDX_DIM
    c_qa, c_ka, c_va, c_qb = 0, wa, 2 * wa, 3 * wa
    c_iq = c_qb + wb
    c_small = c_iq + wi
    c_kb, c_vb, c_ik = c_small, c_small + HEAD_DIM, c_small + 2 * HEAD_DIM
    tn = COL_TILE
    assert wa % tn == 0 and c_small % tn == 0 and w_in16.shape[-1] == c_small + tn
    ka, va, small, p16 = _rms_matmul(h_in, g, w_in16, (layer_e,), emit16=True,
                                     f32_groups=((c_ka // tn, wa // tn), (c_va // tn, wa // tn), (c_small // tn, 1)))
    p16 = p16.reshape(b, t, -1)
    ka, va, small = ka.reshape(b, t, wa), va.reshape(b, t, wa), small.reshape(b, t, tn)
    kb = small[..., :HEAD_DIM]
    vb = small[..., HEAD_DIM:2 * HEAD_DIM]
    ik = small[..., 2 * HEAD_DIM:2 * HEAD_DIM + IDX_DIM]
    rows = (ka.reshape(b, t, n_a, 2, DH_A), va.reshape(b, t, n_a, HEAD_DIM), kb, vb, ik)
    tab_a, tab_b = tab[:, :n_a], tab[:, n_a:]
    if past is None:
        o_a = _diff_prompt(p16, c_qa, c_ka, c_va, n_a, tab_a, lam_vec, g_subln, lam_init)
        tq = min(t, 128)
        bias_b = _chunk_bias(tab_b, tq, tq).reshape(n_b * tq, 3 * tq)
        iq_t = jnp.swapaxes(p16[..., c_iq:c_small].reshape(b, t, N_IDX_HEADS, IDX_DIM), 1, 2)
        o_b = _dsa_prompt(p16, small, iq_t, c_qb, c_kb, c_vb, c_ik, 2 * HEAD_DIM, n_b, bias_b,
                          min(DSA_TOPK, t // 4), tq)
    else:
        cache_a_k, cache_a_v, cache_b_k, cache_b_v, cache_b_idx = past
        qa = p16[..., c_qa:c_qa + wa].reshape(b, t, n_a, 2, DH_A)
        o_a = _diff_sample(pt, layer_e, qa, cache_a_k, cache_a_v, ka, va, tab_a, lam_vec, g_subln, lam_init)
        qb = p16[..., c_qb:c_qb + wb].reshape(b, t, n_b, HEAD_DIM)
        iq = p16[..., c_iq:c_small].reshape(b, t, N_IDX_HEADS, IDX_DIM)
        iw = small[..., 2 * HEAD_DIM + IDX_DIM:2 * HEAD_DIM + IDX_DIM + N_IDX_HEADS]
        o_b = _dsa_sample(pt, layer_e, qb, iq, iw, cache_b_k, cache_b_v, cache_b_idx, kb, vb, ik, tab_b)
    return (o_a.reshape(b * t, wa), o_b.reshape(b * t, wb)), rows


def _mixer_odd(h_in, g, w_in16, past, pt, layer_o, b_forget, tab, b, t):
    d = h_in.shape[1]
    n_c = n_d = (d // HEAD_DIM) // 2
    wc, wd = n_c * HEAD_DIM, n_d * HEAD_DIM
    c_qc, c_kc, c_vc, c_qd = 0, wc, 2 * wc, 3 * wc
    c_kd = c_qd + wd
    c_vd = c_kd + wd
    c_fc = c_vd + wd
    tn = COL_TILE
    assert wc % tn == 0 and wd % tn == 0 and w_in16.shape[-1] == c_fc + tn
    groups = tuple((c // tn, wc // tn) for c in (c_kc, c_vc, c_qd, c_kd, c_vd)) + ((c_fc // tn, 1),)
    kc, vc, qd32, kd, vd, small, p16 = _rms_matmul(h_in, g, w_in16, (layer_o,), emit16=True, f32_groups=groups)
    p16 = p16.reshape(b, t, -1)
    kc, vc, qd32, kd, vd = (a.reshape(b, t, wc) for a in (kc, vc, qd32, kd, vd))
    log_f = jax.nn.log_sigmoid(small.reshape(b, t, tn)[..., :n_c] + b_forget.astype(F32))
    rows = (kc.reshape(b, t, n_c, HEAD_DIM), vc.reshape(b, t, n_c, HEAD_DIM), log_f,
            kd.reshape(b, t, n_d, HEAD_DIM), vd.reshape(b, t, n_d, HEAD_DIM))
    tab_d = tab[:, n_c:]
    if past is None:
        o_c = _fox_prompt(p16, c_qc, c_kc, c_vc, n_c, jnp.cumsum(log_f, axis=1))
        o_d = _moba_prompt(p16, qd32, kd, c_qd, c_kd, c_vd, n_d, tab_d)
    else:
        cache_c_k, cache_c_v, cache_c_logf, cache_d_k, cache_d_v = past
        n_pages = pt.shape[1]
        page = cache_c_logf.shape[2]
        logf_past = cache_c_logf[layer_o][pt].reshape(b, n_pages * page, n_c)
        cum = jnp.cumsum(jnp.concatenate([logf_past, log_f], axis=1).astype(F32), axis=1)
        cum_q = cum[:, n_pages * page:]
        qc = p16[..., c_qc:c_qc + wc].reshape(b, t, n_c, HEAD_DIM)
        o_c = _fox_sample(pt, layer_o, qc, cache_c_k, cache_c_v, kc, vc, cum_q, cum[:, :n_pages * page], cum_q)
        o_d = _moba_sample(pt, layer_o, qd32.reshape(b, t, n_d, HEAD_DIM), cache_d_k, cache_d_v, kd, vd, tab_d)
    return (o_c.reshape(b * t, wc), o_d.reshape(b * t, wd)), rows


def _run_trunk(x, past_even, past_odd, pt, mem_kv, prm):
    b, t, d = x.shape
    x = x.reshape(b * t, d)
    depth = prm['norm_g'].shape[0]
    rows_even, rows_odd = [], []
    for layer in range(depth):
        g = prm['norm_g'][layer]
        wg, wu, wd = prm['w_ffn_gate'], prm['w_ffn_up'], prm['w_ffn_down']
        x = _ffn(x, g[NG_FFN1_PRE], g[NG_FFN1_POST], wg, wu, wd, (layer, 0))
        if layer % 2 == 0:
            e = layer // 2
            lam_init = 0.8 - 0.6 * math.exp(-0.3 * layer)
            parts, rows = _mixer_even(x, g[NG_MIX_PRE], prm['w_in_even'], past_even, pt, e,
                                      prm['diff_lambda'][e].astype(F32), prm['g_subln'][e].astype(F32), lam_init,
                                      prm['t5_table'], b, t)
            rows_even.append(rows)
            x = _out_proj(x, g[NG_MIX_POST], parts, prm['w_out_even'], (e,))
        else:
            o = layer // 2
            parts, rows = _mixer_odd(x, g[NG_MIX_PRE], prm['w_in_odd'], past_odd, pt, o,
                                     prm['b_forget'][o], prm['t5_table'], b, t)
            rows_odd.append(rows)
            x = _out_proj(x, g[NG_MIX_POST], parts, prm['w_out_odd'], (o,))
        mk, mv = mem_kv[layer]
        q = _rms_matmul(x, g[NG_X_PRE], prm['w_xq'], (layer,))
        o_x = _cross_attend(q.reshape(b, t, -1), mk, mv)
        x = _out_proj(x, g[NG_X_POST], [o_x.reshape(b * t, -1)], prm['w_xo'], (layer,))
        x = _ffn(x, g[NG_FFN2_PRE], g[NG_FFN2_POST], wg, wu, wd, (layer, 1))
    return x.reshape(b, t, d), rows_even, rows_odd


def kernel(x_prompt, x_sample, cache_a_k, cache_a_v, cache_b_k, cache_b_v, cache_b_idx, cache_c_k, cache_c_v, cache_c_logf, cache_d_k, cache_d_v, cache_mem_k, cache_mem_v, page_table, mem_prompt, t5_table, norm_g, w_ffn_gate, w_ffn_up, w_ffn_down, w_xq, w_xk, w_xv, w_xo, w_in_even, w_out_even, diff_lambda, g_subln, w_in_odd, w_out_odd, b_forget):
    depth = norm_g.shape[0]
    d_model = x_prompt.shape[-1]
    n_c = (d_model // HEAD_DIM) // 2
    half = n_c * HEAD_DIM
    wi = N_IDX_HEADS * IDX_DIM
    e_cut = 4 * half
    e_small = w_in_even[..., e_cut:e_cut + 2 * HEAD_DIM], w_in_even[..., e_cut + 2 * HEAD_DIM + wi:]
    w_in_even_r = jnp.concatenate([w_in_even[..., :e_cut], w_in_even[..., e_cut + 2 * HEAD_DIM:e_cut + 2 * HEAD_DIM + wi],
                                   *e_small], axis=-1)
    c_fc = 3 * half
    w_in_odd_r = jnp.concatenate([w_in_odd[..., :c_fc], w_in_odd[..., c_fc + n_c:], w_in_odd[..., c_fc:c_fc + n_c]],
                                 axis=-1)
    prm = {
        't5_table': t5_table.astype(F32), 'norm_g': norm_g.astype(F32),
        'w_ffn_gate': w_ffn_gate.astype(BF16), 'w_ffn_up': w_ffn_up.astype(BF16), 'w_ffn_down': w_ffn_down.astype(BF16),
        'w_xq': w_xq.astype(BF16), 'w_xo': w_xo.astype(BF16),
        'w_in_even': _pad_cols(w_in_even_r, e_cut + wi + COL_TILE).astype(BF16),
        'w_out_even': w_out_even.astype(BF16),
        'w_in_odd': _pad_cols(w_in_odd_r, 6 * half + COL_TILE).astype(BF16),
        'w_out_odd': w_out_odd.astype(BF16),
        'diff_lambda': diff_lambda, 'g_subln': g_subln, 'b_forget': b_forget,
    }
    b_p, n_mem, _ = mem_prompt.shape
    hx_w = w_xk.shape[-1]

    mem_kv_p, mem_k_out, mem_v_out = [], [], []
    for l in range(depth):
        w_kv = jnp.concatenate([w_xk[l], w_xv[l]], axis=-1).astype(BF16)
        kv = _rms_matmul(mem_prompt.reshape(b_p * n_mem, d_model), norm_g[l, NG_MEM].astype(F32), w_kv)
        mk = kv[:, :hx_w].reshape(b_p, n_mem, hx_w)
        mv = kv[:, hx_w:].reshape(b_p, n_mem, hx_w)
        mem_kv_p.append((mk, mv))
        mem_k_out.append(mk.reshape(b_p, n_mem, hx_w // HEAD_DIM, HEAD_DIM))
        mem_v_out.append(mv.reshape(b_p, n_mem, hx_w // HEAD_DIM, HEAD_DIM))
    y_prompt, ev_p, od_p = _run_trunk(x_prompt.astype(F32), None, None, None, mem_kv_p, prm)

    def rows_pages(c):
        return c.reshape(c.shape[:2] + (c.shape[2] * c.shape[3], c.shape[4]))

    def transposed_pages(c):
        c = c.reshape(c.shape[:3] + (-1,))
        return jnp.swapaxes(c, 2, 3)

    past_even = (transposed_pages(cache_a_k), rows_pages(cache_a_v), cache_b_k, cache_b_v,
                 transposed_pages(cache_b_idx))
    past_odd = (rows_pages(cache_c_k), rows_pages(cache_c_v), cache_c_logf, rows_pages(cache_d_k),
                rows_pages(cache_d_v))
    b_s = x_sample.shape[0]
    mem_kv_s = [(cache_mem_k[l].reshape(b_s, n_mem, hx_w), cache_mem_v[l].reshape(b_s, n_mem, hx_w))
                for l in range(depth)]
    y_sample, ev_s, od_s = _run_trunk(x_sample.astype(F32), past_even, past_odd, page_table.astype(jnp.int32),
                                      mem_kv_s, prm)

    def stack(rows, i):
        return jnp.stack([r[i] for r in rows])

    out = [y_prompt, y_sample]
    out += [stack(ev_p, i) for i in range(5)] + [stack(od_p, i) for i in range(5)]
    out += [jnp.stack(mem_k_out), jnp.stack(mem_v_out)]
    out += [stack(ev_s, i) for i in range(5)] + [stack(od_s, i) for i in range(5)]
    return tuple(out)
```

```python
import functools
import math

import numpy as np
import jax
import jax.numpy as jnp
from jax import lax
from jax.experimental import pallas as pl
from jax.experimental.pallas import tpu as pltpu

F32 = jnp.float32
BF16 = jnp.bfloat16
NEG_INF = float("-inf")

HEAD_DIM = 128
DH_A = HEAD_DIM // 2
N_IDX_HEADS = 16
IDX_DIM = 64
DSA_TOPK = 256
MOBA_BLOCK = 256
MOBA_TOPK = 3
N_BUCKETS = 32
T5_MAX_EXACT = N_BUCKETS // 2
T5_MAX_DIST = 128
RMS_EPS = 1e-6
NG_FFN1_PRE, NG_FFN1_POST, NG_MIX_PRE, NG_MIX_POST = 0, 1, 2, 3
NG_X_PRE, NG_X_POST, NG_FFN2_PRE, NG_FFN2_POST, NG_MEM = 4, 5, 6, 7, 8

LANES = 128
ROW_TILE = 512
COL_TILE = 512
VMEM_LIMIT = 56 * 1024 * 1024

NT_DIMS = (((1,), (1,)), ((), ()))


def _params(*sem):
    return pltpu.CompilerParams(dimension_semantics=sem, vmem_limit_bytes=VMEM_LIMIT)


def _rms(x, g):
    return x * lax.rsqrt(jnp.mean(x * x, axis=-1, keepdims=True) + RMS_EPS) * g


def _round_up(n, m):
    return (n + m - 1) // m * m


def _bucket_np(dist):
    n = np.maximum(dist, 0)
    n_f = np.maximum(n, 1).astype(np.float32)
    large = T5_MAX_EXACT + (np.log(n_f / np.float32(T5_MAX_EXACT)) / np.float32(math.log(T5_MAX_DIST / T5_MAX_EXACT))
                            * np.float32(N_BUCKETS - T5_MAX_EXACT)).astype(np.int32)
    return np.where(n < T5_MAX_EXACT, n, np.minimum(large, N_BUCKETS - 1)).astype(np.int32)


def _rel_bias(tab, dists):
    onehot = (_bucket_np(dists)[:, None] == np.arange(N_BUCKETS)[None, :]).astype(np.float32)
    rel = jnp.sum(jnp.asarray(onehot)[:, :, None] * tab[None].astype(F32), axis=1)
    return jnp.where(jnp.asarray(dists >= 0)[:, None], rel, NEG_INF).T


def _toeplitz(u, n_rows, n_cols):
    h = u.shape[0]
    period = n_rows + n_cols
    w = jnp.concatenate([u[:, :n_cols][:, ::-1], jnp.zeros((h, 1), u.dtype), u[:, n_cols:][:, ::-1]], axis=1)
    flat = jnp.tile(w, (1, n_rows))[:, :n_rows * (period - 1)]
    return flat.reshape(h, n_rows, period - 1)[:, :, :n_cols]


def _dist_tile(tab, n_rows, n_cols, offset):
    dists = np.arange(n_rows + n_cols - 1) - (n_cols - 1) + offset
    return _toeplitz(_rel_bias(tab, dists), n_rows, n_cols)


def _far_tile(tab, n_rows, n_cols):
    return jnp.broadcast_to(tab[N_BUCKETS - 1].astype(F32)[:, None, None], (tab.shape[1], n_rows, n_cols))


def _sample_bias(tab, n_new, page):
    assert page + 1 >= T5_MAX_DIST
    return jnp.stack([_far_tile(tab, n_new, page), _dist_tile(tab, n_new, page, page), _dist_tile(tab, n_new, page, 0)])


def _expand_heads(tile):
    h, r, c = tile.shape
    same = jnp.asarray(np.eye(h, dtype=bool))[:, None, None, :]
    return jnp.where(same, tile[:, :, :, None], NEG_INF).reshape(h * r, c * h)


def _rms_matmul_body(x_ref, g_ref, w_ref, *rest, groups, emit16, w_transposed):
    outs, h_scr = rest[:-1], rest[-1]
    j = pl.program_id(1)

    @pl.when(j == 0)
    def _():
        h_scr[...] = _rms(x_ref[...], g_ref[...]).astype(BF16)

    if w_transposed:
        y = lax.dot_general(h_scr[...], w_ref[...], NT_DIMS, preferred_element_type=F32)
    else:
        y = jnp.dot(h_scr[...], w_ref[...], preferred_element_type=F32)
    for (first, count, transposed), o_ref in zip(groups, outs):
        @pl.when((j >= first) & (j < first + count))
        def _():
            o_ref[...] = y.T if transposed else y
    if emit16:
        outs[len(groups)][...] = y.astype(BF16)


def _stacked(lead, *block):
    return (None,) * len(lead) + tuple(block), tuple(lead)


def _rms_matmul(x, g, w16, lead=(), *, f32_groups=None, emit16=False, w_transposed=False, rows_per_batch=None):
    m, d = x.shape
    n = w16.shape[-2] if w_transposed else w16.shape[-1]
    tm = min(m, 2 * ROW_TILE)
    tn = min(n, COL_TILE)
    assert m % tm == 0 and n % tn == 0
    groups = tuple(tuple(grp) + (False,) * (3 - len(grp)) for grp in (f32_groups or ((0, n // tn),)))
    out_shape, out_specs = [], []
    for first, count, transposed in groups:
        col = lambda j, first=first, count=count: jnp.clip(j - first, 0, count - 1)
        if transposed:
            assert rows_per_batch % tm == 0
            per_b = rows_per_batch // tm
            out_shape.append(jax.ShapeDtypeStruct((m // rows_per_batch, count * tn, rows_per_batch), F32))
            out_specs.append(pl.BlockSpec((None, tn, tm), lambda i, j, col=col: (i // per_b, col(j), i % per_b)))
        else:
            out_shape.append(jax.ShapeDtypeStruct((m, count * tn), F32))
            out_specs.append(pl.BlockSpec((tm, tn), lambda i, j, col=col: (i, col(j))))
    if emit16:
        out_shape.append(jax.ShapeDtypeStruct((m, n), BF16))
        out_specs.append(pl.BlockSpec((tm, tn), lambda i, j: (i, j)))
    w_block, w_lead = _stacked(lead, *((tn, d) if w_transposed else (d, tn)))
    w_index = (lambda i, j: w_lead + (j, 0)) if w_transposed else (lambda i, j: w_lead + (0, j))
    res = pl.pallas_call(
        functools.partial(_rms_matmul_body, groups=groups, emit16=emit16, w_transposed=w_transposed),
        grid=(m // tm, n // tn),
        in_specs=[pl.BlockSpec((tm, d), lambda i, j: (i, 0)),
                  pl.BlockSpec((1, d), lambda i, j: (0, 0)),
                  pl.BlockSpec(w_block, w_index)],
        out_specs=out_specs,
        out_shape=out_shape,
        scratch_shapes=[pltpu.VMEM((tm, d), BF16)],
        compiler_params=_params("parallel", "arbitrary"),
        name="rms_matmul",
    )(x, g.reshape(1, d), w16)
    return res if len(res) > 1 else res[0]


def _ffn_body(x_ref, gpre_ref, gpost_ref, wg_ref, wu_ref, wd_ref, o_ref, h_scr, acc_scr):
    j = pl.program_id(1)

    @pl.when(j == 0)
    def _():
        h_scr[...] = _rms(x_ref[...], gpre_ref[...]).astype(BF16)
        acc_scr[...] = jnp.zeros_like(acc_scr)

    h = h_scr[...]
    gate = jnp.dot(h, wg_ref[...], preferred_element_type=F32)
    up = jnp.dot(h, wu_ref[...], preferred_element_type=F32)
    act = (gate * jax.nn.sigmoid(gate) * up).astype(BF16)
    acc_scr[...] += jnp.dot(act, wd_ref[...], preferred_element_type=F32)

    @pl.when(j == pl.num_programs(1) - 1)
    def _():
        o_ref[...] = x_ref[...] + 0.5 * _rms(acc_scr[...], gpost_ref[...])


def _ffn(x, g_pre, g_post, wg16, wu16, wd16, lead):
    m, d = x.shape
    ff = wg16.shape[-1]
    tm = min(m, ROW_TILE)
    tf = min(ff, COL_TILE)
    assert m % tm == 0 and ff % tf == 0
    up_block, w_lead = _stacked(lead, d, tf)
    down_block, _ = _stacked(lead, tf, d)
    return pl.pallas_call(
        _ffn_body,
        grid=(m // tm, ff // tf),
        in_specs=[pl.BlockSpec((tm, d), lambda i, j: (i, 0)),
                  pl.BlockSpec((1, d), lambda i, j: (0, 0)),
                  pl.BlockSpec((1, d), lambda i, j: (0, 0)),
                  pl.BlockSpec(up_block, lambda i, j: w_lead + (0, j)),
                  pl.BlockSpec(up_block, lambda i, j: w_lead + (0, j)),
                  pl.BlockSpec(down_block, lambda i, j: w_lead + (j, 0))],
        out_specs=pl.BlockSpec((tm, d), lambda i, j: (i, 0)),
        out_shape=jax.ShapeDtypeStruct((m, d), F32),
        scratch_shapes=[pltpu.VMEM((tm, d), BF16), pltpu.VMEM((tm, d), F32)],
        compiler_params=_params("parallel", "arbitrary"),
        name="ffn",
    )(x, g_pre.reshape(1, d), g_post.reshape(1, d), wg16, wu16, wd16)


def _out_body(*refs, n_parts):
    x_ref, g_ref = refs[0], refs[1]
    o_refs = refs[2:2 + n_parts]
    w_refs = refs[2 + n_parts:2 + 2 * n_parts]
    out_ref = refs[-1]
    y = None
    for o_ref, w_ref in zip(o_refs, w_refs):
        t = jnp.dot(o_ref[...].astype(BF16), w_ref[...], preferred_element_type=F32)
        y = t if y is None else y + t
    out_ref[...] = x_ref[...] + _rms(y, g_ref[...])


def _out_proj(x, g, parts, w16, lead):
    m, d = x.shape
    tm = min(m, ROW_TILE)
    k = parts[0].shape[1]
    assert m % tm == 0 and all(p.shape[1] == k for p in parts) and len(parts) * k == w16.shape[-2]
    w_block, w_lead = _stacked(lead, k, d)
    in_specs = [pl.BlockSpec((tm, d), lambda i: (i, 0)), pl.BlockSpec((1, d), lambda i: (0, 0))]
    in_specs += [pl.BlockSpec((tm, k), lambda i: (i, 0)) for _ in parts]
    in_specs += [pl.BlockSpec(w_block, lambda i, n=n: w_lead + (n, 0)) for n in range(len(parts))]
    return pl.pallas_call(
        functools.partial(_out_body, n_parts=len(parts)),
        grid=(m // tm,),
        in_specs=in_specs,
        out_specs=pl.BlockSpec((tm, d), lambda i: (i, 0)),
        out_shape=jax.ShapeDtypeStruct((m, d), F32),
        compiler_params=_params("parallel"),
        name="out_proj",
    )(x, g.reshape(1, d), *parts, *([w16] * len(parts)))


def _cross_body(q_ref, k_ref, v_ref, o_ref, *, n_heads):
    q = q_ref[...].astype(BF16)
    k = k_ref[...].astype(BF16)
    v = v_ref[...].astype(BF16)
    for h in range(n_heads):
        sl = slice(h * HEAD_DIM, (h + 1) * HEAD_DIM)
        s = lax.dot_general(q[:, sl], k[:, sl], NT_DIMS, preferred_element_type=F32) * HEAD_DIM ** -0.5
        p = jnp.exp(s - jnp.max(s, axis=-1, keepdims=True))
        l = jnp.sum(p, axis=-1, keepdims=True)
        o = jnp.dot(p.astype(BF16), v[:, sl], preferred_element_type=F32) / l
        o_ref[:, sl] = o.astype(o_ref.dtype)


def _cross_attend(q, mem_k, mem_v):
    b, t, w = q.shape
    n_mem = mem_k.shape[1]
    tq = min(t, ROW_TILE)
    assert t % tq == 0
    return pl.pallas_call(
        functools.partial(_cross_body, n_heads=w // HEAD_DIM),
        grid=(b, t // tq),
        in_specs=[pl.BlockSpec((None, tq, w), lambda bi, i: (bi, i, 0)),
                  pl.BlockSpec((None, n_mem, w), lambda bi, i: (bi, 0, 0)),
                  pl.BlockSpec((None, n_mem, w), lambda bi, i: (bi, 0, 0))],
        out_specs=pl.BlockSpec((None, tq, w), lambda bi, i: (bi, i, 0)),
        out_shape=jax.ShapeDtypeStruct((b, t, w), F32),
        compiler_params=_params("parallel", "parallel"),
        name="cross_attend",
    )(q, mem_k, mem_v)


def _diff_lambda(lam_ref, lam_init):
    lv = lam_ref[...]
    return (jnp.exp(jnp.sum(lv[0:1] * lv[1:2], axis=-1, keepdims=True))
            - jnp.exp(jnp.sum(lv[2:3] * lv[3:4], axis=-1, keepdims=True)) + lam_init)


def _fold_lanes(x, op):
    parts = [x[:, c * LANES:(c + 1) * LANES] for c in range(x.shape[1] // LANES)]
    return functools.reduce(op, parts)


def _masked_softmax_pv(n_chunks, scores, values, s_scr):
    n_maps, _, rows, _ = s_scr.shape

    def first(j, mx):
        out = []
        for a, s in enumerate(scores(j)):
            s_scr[a, j] = s
            out.append(jnp.maximum(mx[a], _fold_lanes(s, jnp.maximum)))
        return tuple(out)

    mx = lax.fori_loop(0, n_chunks, first, tuple(jnp.full((rows, LANES), NEG_INF, F32) for _ in range(n_maps)))
    m = [jnp.max(x, axis=-1, keepdims=True) for x in mx]

    def second(j, carry):
        v = values(j)
        out = []
        for a in range(n_maps):
            l, acc = carry[a]
            p = jnp.exp(s_scr[a, j] - m[a])
            out.append((l + _fold_lanes(p, jnp.add), acc + jnp.dot(p.astype(BF16), v, preferred_element_type=F32)))
        return tuple(out)

    init = tuple((jnp.zeros((rows, LANES), F32), jnp.zeros((rows, HEAD_DIM), F32)) for _ in range(n_maps))
    res = lax.fori_loop(0, n_chunks, second, init)
    return [acc / jnp.sum(l, axis=-1, keepdims=True) for l, acc in res]


def _prompt_tiles(t):
    tq = min(t, 256)
    tk = min(t, 512)
    assert t % tk == 0 and tk % tq == 0
    return tq, tk


def _chunk_bias(tab, tq, tk):
    assert tq + 1 >= T5_MAX_DIST and tk % tq == 0
    n_far = tk // tq + 1
    return _dist_tile(tab, tq, tk + n_far * tq, n_far * tq)


def _bias_window(bias_ref, k, tq, tk):
    n_far = tk // tq + 1
    start = pl.multiple_of((n_far - jnp.minimum(k, n_far)) * tq, tq)
    return bias_ref[:, pl.ds(start, tk)]


def _diff_prompt_body(q_ref, k_ref, v_ref, bias_ref, lam_ref, g_ref, o_ref, s_scr, *, tq, tk, lam_init):
    i = pl.program_id(2)
    ratio = tk // tq
    q = q_ref[...]
    q0, q1 = q[:, :DH_A], q[:, DH_A:]
    scale = DH_A ** -0.5

    def scores(j):
        ks = k_ref[pl.ds(pl.multiple_of(j * tk, tk), tk), :]
        bt = _bias_window(bias_ref, i - ratio * j, tq, tk)
        return (lax.dot_general(q0, ks[:, :DH_A], NT_DIMS, preferred_element_type=F32) * scale + bt,
                lax.dot_general(q1, ks[:, DH_A:], NT_DIMS, preferred_element_type=F32) * scale + bt)

    def values(j):
        return v_ref[pl.ds(pl.multiple_of(j * tk, tk), tk), :]

    o0, o1 = _masked_softmax_pv(i // ratio + 1, scores, values, s_scr)
    o = o0 - _diff_lambda(lam_ref, lam_init) * o1
    o_ref[...] = (_rms(o, g_ref[...]) * (1.0 - lam_init)).astype(o_ref.dtype)


def _diff_prompt(p16, col_q, col_k, col_v, n_heads, tab, lam_vec, g_subln, lam_init):
    b, t, _ = p16.shape
    tq, tk = _prompt_tiles(t)
    bias = _chunk_bias(tab, tq, tk)
    cq, ck, cv = col_q // HEAD_DIM, col_k // HEAD_DIM, col_v // HEAD_DIM
    return pl.pallas_call(
        functools.partial(_diff_prompt_body, tq=tq, tk=tk, lam_init=lam_init),
        grid=(b, n_heads, t // tq),
        in_specs=[pl.BlockSpec((None, tq, HEAD_DIM), lambda bi, h, i: (bi, i, cq + h)),
                  pl.BlockSpec((None, t, HEAD_DIM), lambda bi, h, i: (bi, 0, ck + h)),
                  pl.BlockSpec((None, t, HEAD_DIM), lambda bi, h, i: (bi, 0, cv + h)),
                  pl.BlockSpec((None,) + bias.shape[1:], lambda bi, h, i: (h, 0, 0)),
                  pl.BlockSpec(lam_vec.shape, lambda bi, h, i: (0, 0)),
                  pl.BlockSpec((1, HEAD_DIM), lambda bi, h, i: (0, 0))],
        out_specs=pl.BlockSpec((None, tq, HEAD_DIM), lambda bi, h, i: (bi, i, h)),
        out_shape=jax.ShapeDtypeStruct((b, t, n_heads * HEAD_DIM), BF16),
        scratch_shapes=[pltpu.VMEM((2, t // tk, tq, tk), F32)],
        compiler_params=_params("parallel", "parallel", "arbitrary"),
        name="diff_prompt",
    )(p16, p16, p16, bias, lam_vec, g_subln.reshape(1, HEAD_DIM))


def _fox_prompt_body(q_ref, k_ref, v_ref, cum_ref, cumt_ref, o_ref, s_scr, *, tq, tk):
    h = pl.program_id(1)
    i = pl.program_id(2)
    q = q_ref[...]
    cum = cum_ref[...]
    lane = lax.broadcasted_iota(jnp.int32, cum.shape, 1)
    cq = jnp.sum(jnp.where(lane == h, cum, 0.0), axis=-1, keepdims=True)
    ahead = lax.broadcasted_iota(jnp.int32, (tq, tk), 1) - lax.broadcasted_iota(jnp.int32, (tq, tk), 0)

    def scores(j):
        start = pl.multiple_of(j * tk, tk)
        ck = cumt_ref[:, pl.ds(start, tk)]
        s = lax.dot_general(q, k_ref[pl.ds(start, tk), :], NT_DIMS, preferred_element_type=F32)
        s = s * HEAD_DIM ** -0.5 + (cq - ck)
        return (jnp.where(ahead <= i * tq - j * tk, s, NEG_INF),)

    def values(j):
        return v_ref[pl.ds(pl.multiple_of(j * tk, tk), tk), :]

    o, = _masked_softmax_pv(i // (tk // tq) + 1, scores, values, s_scr)
    o_ref[...] = o.astype(o_ref.dtype)


def _fox_prompt(p16, col_q, col_k, col_v, n_heads, cum):
    b, t, _ = p16.shape
    tq, tk = _prompt_tiles(t)
    cq, ck, cv = col_q // HEAD_DIM, col_k // HEAD_DIM, col_v // HEAD_DIM
    cum_t = jnp.swapaxes(cum, 1, 2).reshape(b, n_heads, 1, t)
    return pl.pallas_call(
        functools.partial(_fox_prompt_body, tq=tq, tk=tk),
        grid=(b, n_heads, t // tq),
        in_specs=[pl.BlockSpec((None, tq, HEAD_DIM), lambda bi, h, i: (bi, i, cq + h)),
                  pl.BlockSpec((None, t, HEAD_DIM), lambda bi, h, i: (bi, 0, ck + h)),
                  pl.BlockSpec((None, t, HEAD_DIM), lambda bi, h, i: (bi, 0, cv + h)),
                  pl.BlockSpec((None, tq, n_heads), lambda bi, h, i: (bi, i, 0)),
                  pl.BlockSpec((None, None, 1, t), lambda bi, h, i: (bi, h, 0, 0))],
        out_specs=pl.BlockSpec((None, tq, HEAD_DIM), lambda bi, h, i: (bi, i, h)),
        out_shape=jax.ShapeDtypeStruct((b, t, n_heads * HEAD_DIM), BF16),
        scratch_shapes=[pltpu.VMEM((1, t // tk, tq, tk), F32)],
        compiler_params=_params("parallel", "parallel", "arbitrary"),
        name="fox_prompt",
    )(p16, p16, p16, cum, cum_t)


def _top_blocks(gate, n_top, limit, axis):
    n_blk = gate.shape[axis]
    blk_id = lax.broadcasted_iota(jnp.int32, gate.shape, axis).astype(F32)
    chosen = jnp.zeros(gate.shape, F32)
    g = gate
    for _ in range(n_top):
        mx = jnp.max(g, axis=axis, keepdims=True)
        idx = jnp.min(jnp.where(g == mx, blk_id, float(n_blk)), axis=axis, keepdims=True)
        pick = blk_id == idx
        chosen = jnp.where(pick & (idx < limit), 1.0, chosen)
        g = jnp.where(pick, NEG_INF, g)
    return chosen


def _top_blocks_negmask(gate, n_top, limit):
    return jnp.where(_top_blocks(gate, n_top, limit, 1) > 0.0, 0.0, NEG_INF)


def _moba_prompt_body(q_ref, k_ref, v_ref, q32_ref, k32_ref, bias_ref, o_ref, keep_scr, s_scr, *, n_blk, tk):
    blk = MOBA_BLOCK
    ratio = tk // blk
    i = pl.program_id(2)

    @pl.when(i == 0)
    def _():
        kmean = jnp.mean(k32_ref[...].reshape(n_blk, blk, HEAD_DIM), axis=1)
        gate = lax.dot_general(kmean, q32_ref[...], NT_DIMS, preferred_element_type=F32,
                               precision=lax.Precision.HIGHEST)
        blk_id = lax.broadcasted_iota(jnp.int32, gate.shape, 0)
        own = lax.broadcasted_iota(jnp.int32, gate.shape, 1) // blk
        gate = jnp.where(blk_id < own, gate, NEG_INF)
        chosen = _top_blocks(gate, min(MOBA_TOPK, n_blk), own[0:1].astype(F32), 0)
        keep_scr[...] = jnp.where(blk_id == own, 1.0, chosen)

    keep_t = keep_scr[:, pl.ds(pl.multiple_of(i * blk, blk), blk)].astype(BF16)
    eye = jnp.where(lax.broadcasted_iota(jnp.int32, (blk, blk), 0) == lax.broadcasted_iota(jnp.int32, (blk, blk), 1),
                    1.0, 0.0).astype(BF16)
    sel = jnp.where(lax.dot_general(eye, keep_t, NT_DIMS, preferred_element_type=F32) > 0.5, 0.0, NEG_INF)
    lane = lax.broadcasted_iota(jnp.int32, sel.shape, 1)
    q = q_ref[...]

    def scores(j):
        ks = k_ref[pl.ds(pl.multiple_of(j * tk, tk), tk), :]
        s = lax.dot_general(q, ks, NT_DIMS, preferred_element_type=F32) * HEAD_DIM ** -0.5
        s = s + _bias_window(bias_ref, i - ratio * j, blk, tk)
        parts = []
        for c in range(ratio):
            keep = jnp.min(jnp.where(lane == j * ratio + c, sel, 0.0), axis=-1, keepdims=True)
            parts.append(s[:, c * blk:(c + 1) * blk] + keep)
        return (parts[0] if ratio == 1 else jnp.concatenate(parts, axis=1),)

    def values(j):
        return v_ref[pl.ds(pl.multiple_of(j * tk, tk), tk), :]

    o, = _masked_softmax_pv(i // ratio + 1, scores, values, s_scr)
    o_ref[...] = o.astype(o_ref.dtype)


def _moba_prompt(p16, q32, k32, col_q, col_k, col_v, n_heads, tab):
    b, t, _ = p16.shape
    blk = MOBA_BLOCK
    assert t % blk == 0
    tk = min(t, 2 * blk)
    assert t % tk == 0
    bias = _chunk_bias(tab, blk, tk)
    cq, ck, cv = col_q // HEAD_DIM, col_k // HEAD_DIM, col_v // HEAD_DIM
    return pl.pallas_call(
        functools.partial(_moba_prompt_body, n_blk=t // blk, tk=tk),
        grid=(b, n_heads, t // blk),
        in_specs=[pl.BlockSpec((None, blk, HEAD_DIM), lambda bi, h, i: (bi, i, cq + h)),
                  pl.BlockSpec((None, t, HEAD_DIM), lambda bi, h, i: (bi, 0, ck + h)),
                  pl.BlockSpec((None, t, HEAD_DIM), lambda bi, h, i: (bi, 0, cv + h)),
                  pl.BlockSpec((None, t, HEAD_DIM), lambda bi, h, i: (bi, 0, h)),
                  pl.BlockSpec((None, t, HEAD_DIM), lambda bi, h, i: (bi, 0, h)),
                  pl.BlockSpec((None,) + bias.shape[1:], lambda bi, h, i: (h, 0, 0))],
        out_specs=pl.BlockSpec((None, blk, HEAD_DIM), lambda bi, h, i: (bi, i, h)),
        out_shape=jax.ShapeDtypeStruct((b, t, n_heads * HEAD_DIM), BF16),
        scratch_shapes=[pltpu.VMEM((t // blk, t), F32), pltpu.VMEM((1, t // tk, blk, tk), F32)],
        compiler_params=_params("parallel", "parallel", "arbitrary"),
        name="moba_prompt",
    )(p16, p16, p16, q32, k32, bias)


KEY_SIGN = -2 ** 31
KEY_OF_NEG_INF = -2139095041


def _order_key(score):
    bits = pltpu.bitcast(score, jnp.int32)
    key = jnp.where(bits < 0, bits ^ 0x7FFFFFFF, bits)
    return jnp.where(score == 0.0, 0, key)


def _kth_largest_key(count_ge, n_rows, k):
    def bit_body(b, ans):
        cand = ans | jnp.left_shift(jnp.int32(1), 31 - b)
        cnt, = count_ge([cand ^ KEY_SIGN])
        return jnp.where(cnt >= k, cand, ans)

    ans = lax.fori_loop(0, 32, bit_body, jnp.zeros((n_rows, 1), jnp.int32))
    return ans ^ KEY_SIGN


def _dsa_prompt_body(iq_ref, ikw_ref, kidx_ref, qb_ref, kb_ref, vb_ref, bias_ref, o_ref, key_scr, nm_scr, w_scr,
                     s_scr, *, tq, n_heads, n_sel):
    i = pl.program_id(1)
    n_chunks = i + 1
    iq = iq_ref[...].reshape(N_IDX_HEADS * tq, IDX_DIM)
    w = ikw_ref[:, IDX_DIM:IDX_DIM + N_IDX_HEADS] * N_IDX_HEADS ** -0.5 * IDX_DIM ** -0.5
    for n in range(N_IDX_HEADS):
        w_scr[n] = jnp.broadcast_to(w[:, n:n + 1], (tq, tq))
    row = lax.broadcasted_iota(jnp.int32, (tq, tq), 0)
    col = lax.broadcasted_iota(jnp.int32, (tq, tq), 1)

    def score_body(j, _):
        kc = kidx_ref[pl.ds(pl.multiple_of(j * tq, tq), tq), :][:, :IDX_DIM]
        rel = jnp.maximum(lax.dot_general(iq, kc, NT_DIMS, preferred_element_type=F32), 0.0)
        rel = rel.reshape(N_IDX_HEADS, tq, tq)
        sc = w_scr[0] * rel[0]
        for n in range(1, N_IDX_HEADS):
            sc = sc + w_scr[n] * rel[n]
        sc = jnp.where((j < i) | (col <= row), sc, NEG_INF)
        key_scr[j] = _order_key(sc)
        return 0

    lax.fori_loop(0, n_chunks, score_body, 0)

    def count_many(preds):
        def body(j, accs):
            k = key_scr[j]
            return tuple(acc + jnp.where(pred(k), 1.0, 0.0) for acc, pred in zip(accs, preds))
        accs = lax.fori_loop(0, n_chunks, body, tuple(jnp.zeros((tq, tq), F32) for _ in preds))
        return [jnp.sum(acc, axis=-1, keepdims=True) for acc in accs]

    def count(pred):
        return count_many([pred])[0]

    thr = _kth_largest_key(lambda ts: count_many([(lambda k, t=t: k >= t) for t in ts]), tq, n_sel)
    cnt_ge = count(lambda k: k >= thr)
    tie = jnp.max(jnp.where((cnt_ge > n_sel) & (thr > KEY_OF_NEG_INF), 1.0, 0.0)) > 0.0

    @pl.when(jnp.logical_not(tie))
    def _():
        def body(j, _):
            nm_scr[j] = jnp.where(key_scr[j] >= thr, 0.0, NEG_INF)
            return 0
        lax.fori_loop(0, n_chunks, body, 0)

    @pl.when(tie)
    def _():
        allow = n_sel - count(lambda k: k > thr)
        tri = jnp.where(row <= col, 1.0, 0.0).astype(BF16)

        def body(j, before):
            k = key_scr[j]
            eq = jnp.where(k == thr, 1.0, 0.0)
            rank = jnp.dot(eq.astype(BF16), tri, preferred_element_type=F32) + before
            keep = jnp.where(k > thr, 1.0, jnp.where(rank <= allow, eq, 0.0))
            nm_scr[j] = jnp.where(keep > 0.0, 0.0, NEG_INF)
            return before + jnp.sum(eq, axis=-1, keepdims=True)
        lax.fori_loop(0, n_chunks, body, jnp.zeros((tq, 1), F32))

    qb = qb_ref[...]
    qs = jnp.concatenate([qb[:, h * HEAD_DIM:(h + 1) * HEAD_DIM] for h in range(n_heads)], axis=0)

    def scores(j):
        rows = pl.ds(pl.multiple_of(j * tq, tq), tq)
        s = lax.dot_general(qs, kb_ref[rows, :], NT_DIMS, preferred_element_type=F32) * HEAD_DIM ** -0.5
        return (s + _bias_window(bias_ref, i - j, tq, tq) + jnp.tile(nm_scr[j], (n_heads, 1)),)

    def values(j):
        return vb_ref[pl.ds(pl.multiple_of(j * tq, tq), tq), :]

    o, = _masked_softmax_pv(n_chunks, scores, values, s_scr)
    for h in range(n_heads):
        o_ref[:, h * HEAD_DIM:(h + 1) * HEAD_DIM] = o[h * tq:(h + 1) * tq].astype(o_ref.dtype)


def _dsa_prompt(p16, small32, iq_t, col_qb, col_kb, col_vb, col_ik, col_ik32, n_heads, bias, n_sel, tq):
    b, t, _ = p16.shape
    qw = n_heads * HEAD_DIM
    assert col_qb % qw == 0 and col_ik % LANES == 0 and col_ik32 % LANES == 0
    return pl.pallas_call(
        functools.partial(_dsa_prompt_body, tq=tq, n_heads=n_heads, n_sel=n_sel),
        grid=(b, t // tq),
        in_specs=[pl.BlockSpec((None, N_IDX_HEADS, tq, IDX_DIM), lambda bi, i: (bi, 0, i, 0)),
                  pl.BlockSpec((None, tq, LANES), lambda bi, i: (bi, i, col_ik32 // LANES)),
                  pl.BlockSpec((None, t, LANES), lambda bi, i: (bi, 0, col_ik // LANES)),
                  pl.BlockSpec((None, tq, qw), lambda bi, i: (bi, i, col_qb // qw)),
                  pl.BlockSpec((None, t, HEAD_DIM), lambda bi, i: (bi, 0, col_kb // HEAD_DIM)),
                  pl.BlockSpec((None, t, HEAD_DIM), lambda bi, i: (bi, 0, col_vb // HEAD_DIM)),
                  pl.BlockSpec(bias.shape, lambda bi, i: (0, 0))],
        out_specs=pl.BlockSpec((None, tq, qw), lambda bi, i: (bi, i, 0)),
        out_shape=jax.ShapeDtypeStruct((b, t, qw), BF16),
        scratch_shapes=[pltpu.VMEM((t // tq, tq, tq), jnp.int32), pltpu.VMEM((t // tq, tq, tq), F32),
                        pltpu.VMEM((N_IDX_HEADS, tq, tq), F32), pltpu.VMEM((1, t // tq, n_heads * tq, tq), F32)],
        compiler_params=_params("parallel", "arbitrary"),
        name="dsa_prompt",
    )(iq_t, small32, p16, p16, p16, p16, bias)


def _block_diag_rows(q, dtype):
    b, tn, g, d = q.shape
    eye = jnp.eye(g, dtype=q.dtype)
    return jnp.einsum('btgd,gk->bgtkd', q, eye).reshape(b, g * tn, g * d).astype(dtype)


def _page_specs(layer, n_pages, group, rows, width):
    def spec(g):
        return pl.BlockSpec((None, None, rows, width),
                            lambda bi, p, pt: (layer, pt[bi, jnp.minimum(p * group + g, n_pages - 1)], 0, 0))
    return [spec(g) for g in range(group)]


def _past_bias_index(page_idx, n_pages):
    return jnp.clip(page_idx - (n_pages - 2), 0, 1)


def _paged_call(body, pt, n_steps, operands, in_specs, out_shape, out_spec, scratch, name):
    return pl.pallas_call(
        body,
        grid_spec=pltpu.PrefetchScalarGridSpec(
            num_scalar_prefetch=1, grid=(pt.shape[0], n_steps),
            in_specs=in_specs, out_specs=out_spec, scratch_shapes=scratch),
        out_shape=out_shape,
        compiler_params=_params("parallel", "arbitrary"),
        name=name,
    )(pt, *operands)


def _per_batch(rows, width):
    return pl.BlockSpec((None, rows, width), lambda bi, p, pt: (bi, 0, 0))


def _whole(shape):
    return pl.BlockSpec(shape, lambda bi, p, pt: (0,) * len(shape))


def _rows_page(x, page):
    b, tn, w = x.shape
    return jnp.pad(x, ((0, 0), (0, page - tn), (0, 0))).reshape(b, page * (w // HEAD_DIM), HEAD_DIM)


def _transposed_page(x, page):
    return jnp.pad(jnp.swapaxes(x, 1, 2), ((0, 0), (0, 0), (0, page - x.shape[1])))


def _flash_scratch(rows, width):
    return [pltpu.VMEM((rows, 1), F32), pltpu.VMEM((rows, 1), F32), pltpu.VMEM((rows, width), F32)]


def _flash_init(m_scr, l_scr, acc_scr):
    m_scr[...] = jnp.full(m_scr.shape, NEG_INF, F32)
    l_scr[...] = jnp.zeros(l_scr.shape, F32)
    acc_scr[...] = jnp.zeros(acc_scr.shape, F32)


def _flash_update(s, pv, m_scr, l_scr, acc_scr):
    m = m_scr[...]
    m_new = jnp.maximum(m, jnp.max(s, axis=-1, keepdims=True))
    m_safe = jnp.where(m_new == NEG_INF, 0.0, m_new)
    p = jnp.exp(s - m_safe)
    alpha = jnp.exp(m - m_safe)
    l_scr[...] = alpha * l_scr[...] + jnp.sum(p, axis=-1, keepdims=True)
    acc_scr[...] = alpha * acc_scr[...] + pv(p.astype(BF16))
    m_scr[...] = m_new


def _cat16(refs, axis):
    parts = [r[...].astype(BF16) for r in refs]
    return parts[0] if len(parts) == 1 else jnp.concatenate(parts, axis=axis)


SAMPLE_PAGE_GROUP = 8
SMALL_PAGE_GROUP = 8


def _heads_first(o, b, n_heads, tn):
    return jnp.swapaxes(o.reshape(b, n_heads, tn, HEAD_DIM), 1, 2).reshape(b, tn, n_heads * HEAD_DIM)


def _diff_sample_body(pt_ref, wq_ref, bias_ref, lam_ref, g_ref, kn_ref, vn_ref, *rest,
                      n_pages, group, page, n_new, n_heads, lam_init):
    k_refs, v_refs = rest[:group], rest[group:2 * group]
    o_ref, m_scr, l_scr, acc_scr = rest[2 * group:]
    p = pl.program_id(1)
    rows_h = 2 * n_new

    @pl.when(p == 0)
    def _():
        _flash_init(m_scr, l_scr, acc_scr)

    def step(ks, vs, bias):
        s = jnp.dot(wq_ref[...], _cat16(ks, 1), preferred_element_type=F32) * DH_A ** -0.5 + bias

        def pv(p16):
            outs = []
            for h in range(n_heads):
                vh = [v[pl.ds(h, page, stride=n_heads), :].astype(BF16) for v in vs]
                vh = vh[0] if len(vh) == 1 else jnp.concatenate(vh, axis=0)
                outs.append(jnp.dot(p16[h * rows_h:(h + 1) * rows_h], vh, preferred_element_type=F32))
            return jnp.concatenate(outs, axis=0)

        _flash_update(s, pv, m_scr, l_scr, acc_scr)

    @pl.when(p < n_pages // group)
    def _():
        tiles = [bias_ref[_past_bias_index(p * group + g, n_pages)] for g in range(group)]
        step(k_refs, v_refs, jnp.concatenate(tiles, axis=1))

    @pl.when(p == n_pages // group)
    def _():
        step([kn_ref], [vn_ref], bias_ref[2])
        lam = _diff_lambda(lam_ref, lam_init)
        on = acc_scr[...] / l_scr[...]
        for h in range(n_heads):
            r0 = h * rows_h
            o = on[r0:r0 + n_new] - lam * on[r0 + n_new:r0 + rows_h]
            o_ref[h * n_new:(h + 1) * n_new, :] = _rms(o, g_ref[...]) * (1.0 - lam_init)


def _diff_sample(pt, layer, q, cache_kt, cache_v, k_new, v_new, tab, lam_vec, g_subln, lam_init):
    b, tn, n_heads = q.shape[:3]
    n_pages = pt.shape[1]
    width, page = cache_kt.shape[2:]
    group = min(SAMPLE_PAGE_GROUP, n_pages)
    assert n_pages % group == 0
    wq = _block_diag_rows(q.reshape(b, tn, 2 * n_heads, DH_A), BF16)
    rows = 2 * n_heads * tn
    bias = _sample_bias(tab, tn, page)
    bias = jnp.broadcast_to(bias[:, :, None], (3, n_heads, 2, tn, page)).reshape(3, rows, page)
    out = _paged_call(
        functools.partial(_diff_sample_body, n_pages=n_pages, group=group, page=page, n_new=tn, n_heads=n_heads,
                          lam_init=lam_init),
        pt, n_pages // group + 1,
        (wq, bias, lam_vec, g_subln.reshape(1, HEAD_DIM), _transposed_page(k_new, page), _rows_page(v_new, page))
        + (cache_kt,) * group + (cache_v,) * group,
        [_per_batch(rows, width), _whole(bias.shape), _whole(lam_vec.shape), _whole((1, HEAD_DIM)),
         _per_batch(width, page), _per_batch(page * n_heads, HEAD_DIM)]
        + _page_specs(layer, n_pages, group, width, page) + _page_specs(layer, n_pages, group, page * n_heads, HEAD_DIM),
        jax.ShapeDtypeStruct((b, n_heads * tn, HEAD_DIM), F32), _per_batch(n_heads * tn, HEAD_DIM),
        _flash_scratch(rows, HEAD_DIM), "diff_sample")
    return _heads_first(out, b, n_heads, tn)


def _fox_sample_body(pt_ref, q_ref, cq_ref, ckp_ref, ckn_ref, hm_ref, nm_ref, kn_ref, vn_ref, *rest,
                     n_pages, group):
    k_refs, v_refs = rest[:group], rest[group:2 * group]
    o_ref, m_scr, l_scr, acc_scr = rest[2 * group:]
    p = pl.program_id(1)

    @pl.when(p == 0)
    def _():
        _flash_init(m_scr, l_scr, acc_scr)

    def step(ks, vs, ck, mask):
        s = lax.dot_general(q_ref[...], _cat16(ks, 0), NT_DIMS, preferred_element_type=F32)
        s = s * HEAD_DIM ** -0.5 + (cq_ref[...] - ck) + mask
        _flash_update(s, lambda p16: jnp.dot(p16, _cat16(vs, 0), preferred_element_type=F32), m_scr, l_scr, acc_scr)

    @pl.when(p < n_pages // group)
    def _():
        step(k_refs, v_refs, ckp_ref[...], hm_ref[...])

    @pl.when(p == n_pages // group)
    def _():
        step([kn_ref], [vn_ref], ckn_ref[...], nm_ref[...])
        o_ref[...] = acc_scr[...] / l_scr[...]


def _fox_sample(pt, layer, q, cache_k, cache_v, k_new, v_new, cum_q, cum_past, cum_new):
    b, tn, n_heads = q.shape[:3]
    n_pages = pt.shape[1]
    prow = cache_k.shape[2]
    page = prow // n_heads
    group = min(SAMPLE_PAGE_GROUP, n_pages)
    assert n_pages % group == 0
    rows = n_heads * tn
    q_rows = jnp.swapaxes(q, 1, 2).reshape(b, rows, HEAD_DIM).astype(BF16)
    cq = jnp.swapaxes(cum_q, 1, 2).reshape(b, rows, 1)
    ck_past = cum_past.reshape(b, n_pages // group, 1, group * prow)
    ck_new = jnp.pad(cum_new, ((0, 0), (0, page - tn), (0, 0))).reshape(b, 1, prow)
    head_mask = _expand_heads(jnp.zeros((n_heads, tn, page), F32))
    qi = np.arange(tn)[:, None]
    ci = np.arange(page)[None, :]
    causal = np.broadcast_to(np.where(ci <= qi, 0.0, NEG_INF).astype(np.float32), (n_heads, tn, page))
    new_mask = _expand_heads(jnp.asarray(causal))
    out = _paged_call(
        functools.partial(_fox_sample_body, n_pages=n_pages, group=group),
        pt, n_pages // group + 1,
        (q_rows, cq, ck_past, ck_new, jnp.tile(head_mask, (1, group)), new_mask,
         _rows_page(k_new, page), _rows_page(v_new, page)) + (cache_k,) * group + (cache_v,) * group,
        [_per_batch(rows, HEAD_DIM), _per_batch(rows, 1),
         pl.BlockSpec((None, None, 1, group * prow),
                      lambda bi, p, pt_: (bi, jnp.minimum(p, n_pages // group - 1), 0, 0)),
         _per_batch(1, prow), _whole((rows, group * prow)), _whole((rows, prow)),
         _per_batch(prow, HEAD_DIM), _per_batch(prow, HEAD_DIM)]
        + _page_specs(layer, n_pages, group, prow, HEAD_DIM) + _page_specs(layer, n_pages, group, prow, HEAD_DIM),
        jax.ShapeDtypeStruct((b, rows, HEAD_DIM), F32), _per_batch(rows, HEAD_DIM),
        _flash_scratch(rows, HEAD_DIM), "fox_sample")
    return _heads_first(out, b, n_heads, tn)


def _kmean_body(pt_ref, *refs, n_heads, pages_per_block):
    k_refs, o_ref = refs[:-1], refs[-1]
    for blk in range(len(k_refs) // pages_per_block):
        total = None
        for k_ref in k_refs[blk * pages_per_block:(blk + 1) * pages_per_block]:
            k = k_ref[...]
            s = jnp.sum(k.reshape(k.shape[0] // n_heads, n_heads, HEAD_DIM), axis=0)
            total = s if total is None else total + s
        o_ref[blk] = total * (1.0 / MOBA_BLOCK)


def _moba_gate_body(wq_ref, kmean_ref, o_ref, *, n_blk):
    gate = lax.dot_general(wq_ref[...], kmean_ref[...], NT_DIMS, preferred_element_type=F32,
                           precision=lax.Precision.HIGHEST)
    o_ref[...] = _top_blocks_negmask(gate, min(MOBA_TOPK, n_blk), float(n_blk))


def _moba_sample_body(pt_ref, q_ref, bias_ref, sel_ref, kn_ref, vn_ref, *rest, n_pages, group, pages_per_block):
    k_refs, v_refs = rest[:group], rest[group:2 * group]
    o_ref, m_scr, l_scr, acc_scr = rest[2 * group:]
    p = pl.program_id(1)

    @pl.when(p == 0)
    def _():
        _flash_init(m_scr, l_scr, acc_scr)

    def scores(k_ref, bias):
        s = lax.dot_general(q_ref[...], k_ref[...].astype(BF16), NT_DIMS, preferred_element_type=F32)
        return s * HEAD_DIM ** -0.5 + bias

    def update(s, vs):
        _flash_update(s, lambda p16: jnp.dot(p16, _cat16(vs, 0), preferred_element_type=F32), m_scr, l_scr, acc_scr)

    @pl.when(p < n_pages // group)
    def _():
        sel = sel_ref[...]
        lane = lax.broadcasted_iota(jnp.int32, sel.shape, 1)
        parts = []
        for g in range(group):
            page_idx = p * group + g
            row_mask = jnp.min(jnp.where(lane == page_idx // pages_per_block, sel, 0.0), axis=-1, keepdims=True)
            parts.append(scores(k_refs[g], bias_ref[_past_bias_index(page_idx, n_pages)]) + row_mask)
        update(parts[0] if group == 1 else jnp.concatenate(parts, axis=1), v_refs)

    @pl.when(p == n_pages // group)
    def _():
        update(scores(kn_ref, bias_ref[2]), [vn_ref])
        o_ref[...] = acc_scr[...] / l_scr[...]


def _moba_sample(pt, layer, q, cache_k, cache_v, k_new, v_new, tab):
    b, tn, n_heads = q.shape[:3]
    n_pages = pt.shape[1]
    prow = cache_k.shape[2]
    page = prow // n_heads
    width = n_heads * HEAD_DIM
    assert MOBA_BLOCK % page == 0 and (n_pages * page) % MOBA_BLOCK == 0 and tn < MOBA_BLOCK
    ppb = MOBA_BLOCK // page
    n_blk = n_pages // ppb
    group = min(SAMPLE_PAGE_GROUP, n_pages)
    assert n_pages % group == 0
    rows = n_heads * tn
    bps = group // ppb if group % ppb == 0 else 1
    assert n_blk % bps == 0
    kmean = _paged_call(
        functools.partial(_kmean_body, n_heads=n_heads, pages_per_block=ppb), pt, n_blk // bps,
        (cache_k,) * (bps * ppb), _page_specs(layer, n_pages, bps * ppb, prow, HEAD_DIM),
        jax.ShapeDtypeStruct((b, n_blk, n_heads, HEAD_DIM), F32),
        pl.BlockSpec((None, bps, n_heads, HEAD_DIM), lambda bi, p, pt_: (bi, p, 0, 0)), [], "moba_kmean")
    sel = pl.pallas_call(
        functools.partial(_moba_gate_body, n_blk=n_blk),
        grid=(b,),
        in_specs=[pl.BlockSpec((None, rows, width), lambda bi: (bi, 0, 0)),
                  pl.BlockSpec((None, n_blk, width), lambda bi: (bi, 0, 0))],
        out_specs=pl.BlockSpec((None, rows, n_blk), lambda bi: (bi, 0, 0)),
        out_shape=jax.ShapeDtypeStruct((b, rows, n_blk), F32),
        compiler_params=_params("parallel"),
        name="moba_gate",
    )(_block_diag_rows(q, F32), kmean.reshape(b, n_blk, width))
    bias = _sample_bias(tab, tn, page)
    bias = jnp.stack([_expand_heads(bias[i]) for i in range(3)])
    q_rows = jnp.swapaxes(q, 1, 2).reshape(b, rows, HEAD_DIM).astype(BF16)
    out = _paged_call(
        functools.partial(_moba_sample_body, n_pages=n_pages, group=group, pages_per_block=ppb),
        pt, n_pages // group + 1,
        (q_rows, bias, sel, _rows_page(k_new, page), _rows_page(v_new, page)) + (cache_k,) * group + (cache_v,) * group,
        [_per_batch(rows, HEAD_DIM), _whole(bias.shape), _per_batch(rows, n_blk),
         _per_batch(prow, HEAD_DIM), _per_batch(prow, HEAD_DIM)]
        + _page_specs(layer, n_pages, group, prow, HEAD_DIM) + _page_specs(layer, n_pages, group, prow, HEAD_DIM),
        jax.ShapeDtypeStruct((b, rows, HEAD_DIM), F32), _per_batch(rows, HEAD_DIM),
        _flash_scratch(rows, HEAD_DIM), "moba_sample")
    return _heads_first(out, b, n_heads, tn)


def _dsa_score_body(pt_ref, iq_ref, w_ref, mask_ref, kn_ref, *rest, n_pages, group, n_new):
    k_refs, o_ref = rest[:group], rest[group]
    p = pl.program_id(1)

    def score(kt16):
        rel = jnp.dot(iq_ref[...], kt16, preferred_element_type=F32)
        rel = jnp.maximum(rel * IDX_DIM ** -0.5, 0.0) * w_ref[...]
        return jnp.sum(rel.reshape(N_IDX_HEADS, n_new, rel.shape[1]), axis=0)

    @pl.when(p < n_pages // group)
    def _():
        o_ref[...] = score(_cat16(k_refs, 1))

    @pl.when(p == n_pages // group)
    def _():
        page = kn_ref.shape[1]
        o_ref[...] = jnp.full(o_ref.shape, NEG_INF, F32)
        o_ref[:, :page] = score(kn_ref[...].astype(BF16)) + mask_ref[...]


def _dsa_select_body(sc_ref, o_ref, key_scr, *, n_new, n_sel, chunk):
    width = sc_ref.shape[1]
    key_scr[...] = _order_key(sc_ref[...])

    def count(pred):
        return jnp.sum(jnp.where(pred(key_scr[...]), 1.0, 0.0), axis=-1, keepdims=True)

    thr = _kth_largest_key(lambda ts: [count(lambda k, t=t: k >= t) for t in ts], n_new, n_sel)
    cnt_ge = count(lambda k: k >= thr)
    tie = jnp.max(jnp.where((cnt_ge > n_sel) & (thr > KEY_OF_NEG_INF), 1.0, 0.0)) > 0.0

    @pl.when(jnp.logical_not(tie))
    def _():
        o_ref[...] = jnp.where(key_scr[...] >= thr, 0.0, NEG_INF)

    @pl.when(tie)
    def _():
        allow = n_sel - count(lambda k: k > thr)
        r = lax.broadcasted_iota(jnp.int32, (chunk, chunk), 0)
        c = lax.broadcasted_iota(jnp.int32, (chunk, chunk), 1)
        tri = jnp.where(r <= c, 1.0, 0.0).astype(BF16)

        def body(j, before):
            cols = pl.ds(pl.multiple_of(j * chunk, chunk), chunk)
            k = key_scr[:, cols]
            eq = jnp.where(k == thr, 1.0, 0.0)
            rank = jnp.dot(eq.astype(BF16), tri, preferred_element_type=F32) + before
            keep = jnp.where(k > thr, 1.0, jnp.where(rank <= allow, eq, 0.0))
            o_ref[:, cols] = jnp.where(keep > 0.0, 0.0, NEG_INF)
            return before + jnp.sum(eq, axis=-1, keepdims=True)
        lax.fori_loop(0, width // chunk, body, jnp.zeros((n_new, 1), F32))


def _dsa_sample_body(pt_ref, q_ref, bias_ref, nm_ref, kn_ref, vn_ref, *rest, n_pages, group, n_heads):
    k_refs, v_refs = rest[:group], rest[group:2 * group]
    o_ref, m_scr, l_scr, acc_scr = rest[2 * group:]
    p = pl.program_id(1)

    @pl.when(p == 0)
    def _():
        _flash_init(m_scr, l_scr, acc_scr)

    def step(ks, vs, bias, nm):
        s = lax.dot_general(q_ref[...], _cat16(ks, 0), NT_DIMS, preferred_element_type=F32)
        s = s * HEAD_DIM ** -0.5 + bias + jnp.tile(nm, (n_heads, 1))
        _flash_update(s, lambda p16: jnp.dot(p16, _cat16(vs, 0), preferred_element_type=F32), m_scr, l_scr, acc_scr)

    @pl.when(p < n_pages // group)
    def _():
        tiles = [bias_ref[_past_bias_index(p * group + g, n_pages)] for g in range(group)]
        step(k_refs, v_refs, jnp.concatenate(tiles, axis=1), nm_ref[...])

    @pl.when(p == n_pages // group)
    def _():
        page = kn_ref.shape[0]
        step([kn_ref], [vn_ref], bias_ref[2], nm_ref[:, :page])
        o_ref[...] = acc_scr[...] / l_scr[...]


def _dsa_sample(pt, layer, qb, iq, iw, cache_k, cache_v, cache_idx_t, k_new, v_new, ik_new, tab):
    b, tn, n_heads = qb.shape[:3]
    n_pages = pt.shape[1]
    page = cache_k.shape[2]
    n_sel = min(DSA_TOPK, (n_pages * page + tn) // 4)
    group = min(SMALL_PAGE_GROUP, n_pages)
    assert n_pages % group == 0
    n_steps = n_pages // group + 1
    width = n_steps * group * page
    pad = ((0, 0), (0, page - tn), (0, 0))
    qi = np.arange(tn)[:, None]
    ci = np.arange(page)[None, :]
    new_mask = jnp.asarray(np.where(ci <= qi, 0.0, NEG_INF).astype(np.float32))
    n_iq = N_IDX_HEADS * tn
    iq_rows = jnp.swapaxes(iq, 1, 2).reshape(b, n_iq, IDX_DIM).astype(BF16)
    w_rows = (jnp.swapaxes(iw, 1, 2).astype(F32) * N_IDX_HEADS ** -0.5).reshape(b, n_iq, 1)
    step_cols = pl.BlockSpec((None, tn, group * page), lambda bi, p, pt_: (bi, 0, p))
    scores = _paged_call(
        functools.partial(_dsa_score_body, n_pages=n_pages, group=group, n_new=tn),
        pt, n_steps,
        (iq_rows, w_rows, new_mask, _transposed_page(ik_new, page)) + (cache_idx_t,) * group,
        [_per_batch(n_iq, IDX_DIM), _per_batch(n_iq, 1), _whole(new_mask.shape), _per_batch(IDX_DIM, page)]
        + _page_specs(layer, n_pages, group, IDX_DIM, page),
        jax.ShapeDtypeStruct((b, tn, width), F32), step_cols, [], "dsa_score")
    negmask = pl.pallas_call(
        functools.partial(_dsa_select_body, n_new=b * tn, n_sel=n_sel, chunk=page),
        grid=(1,),
        in_specs=[pl.BlockSpec((b * tn, width), lambda i: (0, 0))],
        out_specs=pl.BlockSpec((b * tn, width), lambda i: (0, 0)),
        out_shape=jax.ShapeDtypeStruct((b * tn, width), F32),
        scratch_shapes=[pltpu.VMEM((b * tn, width), jnp.int32)],
        compiler_params=_params("arbitrary"),
        name="dsa_select",
    )(scores.reshape(b * tn, width)).reshape(b, tn, width)
    rows = n_heads * tn
    q_rows = jnp.swapaxes(qb, 1, 2).reshape(b, rows, HEAD_DIM).astype(BF16)
    bias = _sample_bias(tab, tn, page).reshape(3, rows, page)
    out = _paged_call(
        functools.partial(_dsa_sample_body, n_pages=n_pages, group=group, n_heads=n_heads),
        pt, n_steps,
        (q_rows, bias, negmask, jnp.pad(k_new, pad), jnp.pad(v_new, pad)) + (cache_k,) * group + (cache_v,) * group,
        [_per_batch(rows, HEAD_DIM), _whole(bias.shape), step_cols, _per_batch(page, HEAD_DIM),
         _per_batch(page, HEAD_DIM)]
        + _page_specs(layer, n_pages, group, page, HEAD_DIM) + _page_specs(layer, n_pages, group, page, HEAD_DIM),
        jax.ShapeDtypeStruct((b, rows, HEAD_DIM), F32), _per_batch(rows, HEAD_DIM),
        _flash_scratch(rows, HEAD_DIM), "dsa_sample")
    return _heads_first(out, b, n_heads, tn)


def _pad_rows(w, n):
    return jnp.pad(w, ((0, 0),) * (w.ndim - 2) + ((0, n - w.shape[-2]), (0, 0)))


def _mixer_even(h_in, g, w_in16, past, pt, layer_e, lam_vec, g_subln, lam_init, tab, b, t):
    d = h_in.shape[1]
    n_a = n_b = (d // HEAD_DIM) // 2
    wa, wb = n_a * HEAD_DIM, n_b * HEAD_DIM
    wi = N_IDX_HEADS * IDX_DIM
    c_qa, c_ka, c_va, c_qb = 0, wa, 2 * wa, 3 * wa
    c_iq = c_qb + wb
    c_small = c_iq + wi
    c_kb, c_vb, c_ik = c_small, c_small + HEAD_DIM, c_small + 2 * HEAD_DIM
    tn = COL_TILE
    assert wa % tn == 0 and c_small % tn == 0 and w_in16.shape[-2] == c_small + tn
    ka_t = past is None and t % min(b * t, 2 * ROW_TILE) == 0
    ka, va, small, p16 = _rms_matmul(
        h_in, g, w_in16, (layer_e,), emit16=True, w_transposed=True, rows_per_batch=t,
        f32_groups=((c_ka // tn, wa // tn, ka_t), (c_va // tn, wa // tn), (c_small // tn, 1)))
    p16 = p16.reshape(b, t, -1)
    va, small = va.reshape(b, t, wa), small.reshape(b, t, tn)
    if ka_t:
        a_k = jnp.transpose(ka.reshape(b, n_a, 2, DH_A, t), (0, 4, 1, 2, 3))
    else:
        ka = ka.reshape(b, t, wa)
        a_k = ka.reshape(b, t, n_a, 2, DH_A)
    kb = small[..., :HEAD_DIM]
    vb = small[..., HEAD_DIM:2 * HEAD_DIM]
    ik = small[..., 2 * HEAD_DIM:2 * HEAD_DIM + IDX_DIM]
    rows = (a_k, va.reshape(b, t, n_a, HEAD_DIM), kb, vb, ik)
    tab_a, tab_b = tab[:, :n_a], tab[:, n_a:]
    if past is None:
        o_a = _diff_prompt(p16, c_qa, c_ka, c_va, n_a, tab_a, lam_vec, g_subln, lam_init)
        tq = min(t, 128)
        bias_b = _chunk_bias(tab_b, tq, tq).reshape(n_b * tq, 3 * tq)
        iq_t = jnp.swapaxes(p16[..., c_iq:c_small].reshape(b, t, N_IDX_HEADS, IDX_DIM), 1, 2)
        o_b = _dsa_prompt(p16, small, iq_t, c_qb, c_kb, c_vb, c_ik, 2 * HEAD_DIM, n_b, bias_b,
                          min(DSA_TOPK, t // 4), tq)
    else:
        cache_a_k, cache_a_v, cache_b_k, cache_b_v, cache_b_idx = past
        qa = p16[..., c_qa:c_qa + wa].reshape(b, t, n_a, 2, DH_A)
        o_a = _diff_sample(pt, layer_e, qa, cache_a_k, cache_a_v, ka, va, tab_a, lam_vec, g_subln, lam_init)
        qb = p16[..., c_qb:c_qb + wb].reshape(b, t, n_b, HEAD_DIM)
        iq = p16[..., c_iq:c_small].reshape(b, t, N_IDX_HEADS, IDX_DIM)
        iw = small[..., 2 * HEAD_DIM + IDX_DIM:2 * HEAD_DIM + IDX_DIM + N_IDX_HEADS]
        o_b = _dsa_sample(pt, layer_e, qb, iq, iw, cache_b_k, cache_b_v, cache_b_idx, kb, vb, ik, tab_b)
    return (o_a.reshape(b * t, wa), o_b.reshape(b * t, wb)), rows


def _mixer_odd(h_in, g, w_in16, past, pt, layer_o, b_forget, tab, b, t):
    d = h_in.shape[1]
    n_c = n_d = (d // HEAD_DIM) // 2
    wc, wd = n_c * HEAD_DIM, n_d * HEAD_DIM
    c_qc, c_kc, c_vc, c_qd = 0, wc, 2 * wc, 3 * wc
    c_kd = c_qd + wd
    c_vd = c_kd + wd
    c_fc = c_vd + wd
    tn = COL_TILE
    assert wc % tn == 0 and wd % tn == 0 and w_in16.shape[-2] == c_fc + tn
    groups = tuple((c // tn, wc // tn) for c in (c_kc, c_vc, c_qd, c_kd, c_vd)) + ((c_fc // tn, 1),)
    kc, vc, qd32, kd, vd, small, p16 = _rms_matmul(h_in, g, w_in16, (layer_o,), emit16=True, w_transposed=True,
                                                   f32_groups=groups)
    p16 = p16.reshape(b, t, -1)
    kc, vc, qd32, kd, vd = (a.reshape(b, t, wc) for a in (kc, vc, qd32, kd, vd))
    log_f = jax.nn.log_sigmoid(small.reshape(b, t, tn)[..., :n_c] + b_forget.astype(F32))
    rows = (kc.reshape(b, t, n_c, HEAD_DIM), vc.reshape(b, t, n_c, HEAD_DIM), log_f,
            kd.reshape(b, t, n_d, HEAD_DIM), vd.reshape(b, t, n_d, HEAD_DIM))
    tab_d = tab[:, n_c:]
    if past is None:
        o_c = _fox_prompt(p16, c_qc, c_kc, c_vc, n_c, jnp.cumsum(log_f, axis=1))
        o_d = _moba_prompt(p16, qd32, kd, c_qd, c_kd, c_vd, n_d, tab_d)
    else:
        cache_c_k, cache_c_v, cache_c_logf, cache_d_k, cache_d_v = past
        n_pages = pt.shape[1]
        page = cache_c_logf.shape[2]
        logf_past = cache_c_logf[layer_o][pt].reshape(b, n_pages * page, n_c)
        cum = jnp.cumsum(jnp.concatenate([logf_past, log_f], axis=1).astype(F32), axis=1)
        cum_q = cum[:, n_pages * page:]
        qc = p16[..., c_qc:c_qc + wc].reshape(b, t, n_c, HEAD_DIM)
        o_c = _fox_sample(pt, layer_o, qc, cache_c_k, cache_c_v, kc, vc, cum_q, cum[:, :n_pages * page], cum_q)
        o_d = _moba_sample(pt, layer_o, qd32.reshape(b, t, n_d, HEAD_DIM), cache_d_k, cache_d_v, kd, vd, tab_d)
    return (o_c.reshape(b * t, wc), o_d.reshape(b * t, wd)), rows


def _run_trunk(x, past_even, past_odd, pt, mem_kv, prm):
    b, t, d = x.shape
    x = x.reshape(b * t, d)
    depth = prm['norm_g'].shape[0]
    rows_even, rows_odd = [], []
    for layer in range(depth):
        g = prm['norm_g'][layer]
        wg, wu, wd = prm['w_ffn_gate'], prm['w_ffn_up'], prm['w_ffn_down']
        x = _ffn(x, g[NG_FFN1_PRE], g[NG_FFN1_POST], wg, wu, wd, (layer, 0))
        if layer % 2 == 0:
            e = layer // 2
            lam_init = 0.8 - 0.6 * math.exp(-0.3 * layer)
            parts, rows = _mixer_even(x, g[NG_MIX_PRE], prm['w_in_even'], past_even, pt, e,
                                      prm['diff_lambda'][e].astype(F32), prm['g_subln'][e].astype(F32), lam_init,
                                      prm['t5_table'], b, t)
            rows_even.append(rows)
            x = _out_proj(x, g[NG_MIX_POST], parts, prm['w_out_even'], (e,))
        else:
            o = layer // 2
            parts, rows = _mixer_odd(x, g[NG_MIX_PRE], prm['w_in_odd'], past_odd, pt, o,
                                     prm['b_forget'][o], prm['t5_table'], b, t)
            rows_odd.append(rows)
            x = _out_proj(x, g[NG_MIX_POST], parts, prm['w_out_odd'], (o,))
        mk, mv = mem_kv[layer]
        q = _rms_matmul(x, g[NG_X_PRE], prm['w_xq'], (layer,))
        o_x = _cross_attend(q.reshape(b, t, -1), mk, mv)
        x = _out_proj(x, g[NG_X_POST], [o_x.reshape(b * t, -1)], prm['w_xo'], (layer,))
        x = _ffn(x, g[NG_FFN2_PRE], g[NG_FFN2_POST], wg, wu, wd, (layer, 1))
    return x.reshape(b, t, d), rows_even, rows_odd


def kernel(x_prompt, x_sample, cache_a_k, cache_a_v, cache_b_k, cache_b_v, cache_b_idx, cache_c_k, cache_c_v, cache_c_logf, cache_d_k, cache_d_v, cache_mem_k, cache_mem_v, page_table, mem_prompt, t5_table, norm_g, w_ffn_gate, w_ffn_up, w_ffn_down, w_xq, w_xk, w_xv, w_xo, w_in_even, w_out_even, diff_lambda, g_subln, w_in_odd, w_out_odd, b_forget):
    depth = norm_g.shape[0]
    d_model = x_prompt.shape[-1]
    n_c = (d_model // HEAD_DIM) // 2
    half = n_c * HEAD_DIM
    wi = N_IDX_HEADS * IDX_DIM
    e_cut = 4 * half
    w_e = jnp.swapaxes(w_in_even, 1, 2)
    w_in_even_r = jnp.concatenate([w_e[:, :e_cut], w_e[:, e_cut + 2 * HEAD_DIM:e_cut + 2 * HEAD_DIM + wi],
                                   w_e[:, e_cut:e_cut + 2 * HEAD_DIM], w_e[:, e_cut + 2 * HEAD_DIM + wi:]], axis=1)
    c_fc = 3 * half
    w_o = jnp.swapaxes(w_in_odd, 1, 2)
    w_in_odd_r = jnp.concatenate([w_o[:, :c_fc], w_o[:, c_fc + n_c:], w_o[:, c_fc:c_fc + n_c]], axis=1)
    prm = {
        't5_table': t5_table.astype(F32), 'norm_g': norm_g.astype(F32),
        'w_ffn_gate': w_ffn_gate.astype(BF16), 'w_ffn_up': w_ffn_up.astype(BF16), 'w_ffn_down': w_ffn_down.astype(BF16),
        'w_xq': w_xq.astype(BF16), 'w_xo': w_xo.astype(BF16),
        'w_in_even': _pad_rows(w_in_even_r, e_cut + wi + COL_TILE).astype(BF16),
        'w_out_even': w_out_even.astype(BF16),
        'w_in_odd': _pad_rows(w_in_odd_r, 6 * half + COL_TILE).astype(BF16),
        'w_out_odd': w_out_odd.astype(BF16),
        'diff_lambda': diff_lambda, 'g_subln': g_subln, 'b_forget': b_forget,
    }
    b_p, n_mem, _ = mem_prompt.shape
    hx_w = w_xk.shape[-1]

    mem_kv_p, mem_k_out, mem_v_out = [], [], []
    for l in range(depth):
        w_kv = jnp.concatenate([w_xk[l], w_xv[l]], axis=-1).astype(BF16)
        kv = _rms_matmul(mem_prompt.reshape(b_p * n_mem, d_model), norm_g[l, NG_MEM].astype(F32), w_kv)
        mk = kv[:, :hx_w].reshape(b_p, n_mem, hx_w)
        mv = kv[:, hx_w:].reshape(b_p, n_mem, hx_w)
        mem_kv_p.append((mk, mv))
        mem_k_out.append(mk.reshape(b_p, n_mem, hx_w // HEAD_DIM, HEAD_DIM))
        mem_v_out.append(mv.reshape(b_p, n_mem, hx_w // HEAD_DIM, HEAD_DIM))
    y_prompt, ev_p, od_p = _run_trunk(x_prompt.astype(F32), None, None, None, mem_kv_p, prm)

    def rows_pages(c):
        return c.reshape(c.shape[:2] + (c.shape[2] * c.shape[3], c.shape[4]))

    def transposed_pages(c):
        c = c.reshape(c.shape[:3] + (-1,))
        return jnp.swapaxes(c, 2, 3)

    past_even = (transposed_pages(cache_a_k), rows_pages(cache_a_v), cache_b_k, cache_b_v,
                 transposed_pages(cache_b_idx))
    past_odd = (rows_pages(cache_c_k), rows_pages(cache_c_v), cache_c_logf, rows_pages(cache_d_k),
                rows_pages(cache_d_v))
    b_s = x_sample.shape[0]
    mem_kv_s = [(cache_mem_k[l].reshape(b_s, n_mem, hx_w), cache_mem_v[l].reshape(b_s, n_mem, hx_w))
                for l in range(depth)]
    y_sample, ev_s, od_s = _run_trunk(x_sample.astype(F32), past_even, past_odd, page_table.astype(jnp.int32),
                                      mem_kv_s, prm)

    def stack(rows, i):
        return jnp.stack([r[i] for r in rows])

    out = [y_prompt, y_sample]
    out += [stack(ev_p, i) for i in range(5)] + [stack(od_p, i) for i in range(5)]
    out += [jnp.stack(mem_k_out), jnp.stack(mem_v_out)]
    out += [stack(ev_s, i) for i in range(5)] + [stack(od_s, i) for i in range(5)]
    return tuple(out)
```

```python
import functools
import math

import numpy as np
import jax
import jax.numpy as jnp
from jax import lax
from jax.experimental import pallas as pl
from jax.experimental.pallas import tpu as pltpu

F32 = jnp.float32
BF16 = jnp.bfloat16
NEG_INF = float("-inf")

HEAD_DIM = 128
DH_A = HEAD_DIM // 2
N_IDX_HEADS = 16
IDX_DIM = 64
DSA_TOPK = 256
MOBA_BLOCK = 256
MOBA_TOPK = 3
N_BUCKETS = 32
T5_MAX_EXACT = N_BUCKETS // 2
T5_MAX_DIST = 128
RMS_EPS = 1e-6
NG_FFN1_PRE, NG_FFN1_POST, NG_MIX_PRE, NG_MIX_POST = 0, 1, 2, 3
NG_X_PRE, NG_X_POST, NG_FFN2_PRE, NG_FFN2_POST, NG_MEM = 4, 5, 6, 7, 8

LANES = 128
ROW_TILE = 512
COL_TILE = 512
VMEM_LIMIT = 56 * 1024 * 1024

NT_DIMS = (((1,), (1,)), ((), ()))


def _params(*sem):
    return pltpu.CompilerParams(dimension_semantics=sem, vmem_limit_bytes=VMEM_LIMIT)


def _rms(x, g):
    return x * lax.rsqrt(jnp.mean(x * x, axis=-1, keepdims=True) + RMS_EPS) * g


def _round_up(n, m):
    return (n + m - 1) // m * m


def _bucket_np(dist):
    n = np.maximum(dist, 0)
    n_f = np.maximum(n, 1).astype(np.float32)
    large = T5_MAX_EXACT + (np.log(n_f / np.float32(T5_MAX_EXACT)) / np.float32(math.log(T5_MAX_DIST / T5_MAX_EXACT))
                            * np.float32(N_BUCKETS - T5_MAX_EXACT)).astype(np.int32)
    return np.where(n < T5_MAX_EXACT, n, np.minimum(large, N_BUCKETS - 1)).astype(np.int32)


def _rel_bias(tab, dists):
    onehot = (_bucket_np(dists)[:, None] == np.arange(N_BUCKETS)[None, :]).astype(np.float32)
    rel = jnp.sum(jnp.asarray(onehot)[:, :, None] * tab[None].astype(F32), axis=1)
    return jnp.where(jnp.asarray(dists >= 0)[:, None], rel, NEG_INF).T


def _toeplitz(u, n_rows, n_cols):
    h = u.shape[0]
    period = n_rows + n_cols
    w = jnp.concatenate([u[:, :n_cols][:, ::-1], jnp.zeros((h, 1), u.dtype), u[:, n_cols:][:, ::-1]], axis=1)
    flat = jnp.tile(w, (1, n_rows))[:, :n_rows * (period - 1)]
    return flat.reshape(h, n_rows, period - 1)[:, :, :n_cols]


def _dist_tile(tab, n_rows, n_cols, offset):
    dists = np.arange(n_rows + n_cols - 1) - (n_cols - 1) + offset
    return _toeplitz(_rel_bias(tab, dists), n_rows, n_cols)


def _far_tile(tab, n_rows, n_cols):
    return jnp.broadcast_to(tab[N_BUCKETS - 1].astype(F32)[:, None, None], (tab.shape[1], n_rows, n_cols))


def _sample_bias(tab, n_new, page):
    assert page + 1 >= T5_MAX_DIST
    return jnp.stack([_far_tile(tab, n_new, page), _dist_tile(tab, n_new, page, page), _dist_tile(tab, n_new, page, 0)])


def _expand_heads(tile):
    h, r, c = tile.shape
    same = jnp.asarray(np.eye(h, dtype=bool))[:, None, None, :]
    return jnp.where(same, tile[:, :, :, None], NEG_INF).reshape(h * r, c * h)


def _rms_matmul_body(x_ref, g_ref, w_ref, *rest, groups, emit16, w_transposed):
    outs, h_scr = rest[:-1], rest[-1]
    j = pl.program_id(1)

    @pl.when(j == 0)
    def _():
        h_scr[...] = _rms(x_ref[...], g_ref[...]).astype(BF16)

    if w_transposed:
        y = lax.dot_general(h_scr[...], w_ref[...], NT_DIMS, preferred_element_type=F32)
    else:
        y = jnp.dot(h_scr[...], w_ref[...], preferred_element_type=F32)
    for (first, count, transposed), o_ref in zip(groups, outs):
        @pl.when((j >= first) & (j < first + count))
        def _():
            o_ref[...] = y.T if transposed else y
    if emit16:
        outs[len(groups)][...] = y.astype(BF16)


def _stacked(lead, *block):
    return (None,) * len(lead) + tuple(block), tuple(lead)


def _rms_matmul(x, g, w16, lead=(), *, f32_groups=None, emit16=False, w_transposed=False, rows_per_batch=None):
    m, d = x.shape
    n = w16.shape[-2] if w_transposed else w16.shape[-1]
    tm = min(m, 2 * ROW_TILE)
    tn = min(n, COL_TILE)
    assert m % tm == 0 and n % tn == 0
    groups = tuple(tuple(grp) + (False,) * (3 - len(grp)) for grp in (f32_groups or ((0, n // tn),)))
    out_shape, out_specs = [], []
    for first, count, transposed in groups:
        col = lambda j, first=first, count=count: jnp.clip(j - first, 0, count - 1)
        if transposed:
            assert rows_per_batch % tm == 0
            per_b = rows_per_batch // tm
            out_shape.append(jax.ShapeDtypeStruct((m // rows_per_batch, count * tn, rows_per_batch), F32))
            out_specs.append(pl.BlockSpec((None, tn, tm), lambda i, j, col=col: (i // per_b, col(j), i % per_b)))
        else:
            out_shape.append(jax.ShapeDtypeStruct((m, count * tn), F32))
            out_specs.append(pl.BlockSpec((tm, tn), lambda i, j, col=col: (i, col(j))))
    if emit16:
        out_shape.append(jax.ShapeDtypeStruct((m, n), BF16))
        out_specs.append(pl.BlockSpec((tm, tn), lambda i, j: (i, j)))
    w_block, w_lead = _stacked(lead, *((tn, d) if w_transposed else (d, tn)))
    w_index = (lambda i, j: w_lead + (j, 0)) if w_transposed else (lambda i, j: w_lead + (0, j))
    res = pl.pallas_call(
        functools.partial(_rms_matmul_body, groups=groups, emit16=emit16, w_transposed=w_transposed),
        grid=(m // tm, n // tn),
        in_specs=[pl.BlockSpec((tm, d), lambda i, j: (i, 0)),
                  pl.BlockSpec((1, d), lambda i, j: (0, 0)),
                  pl.BlockSpec(w_block, w_index)],
        out_specs=out_specs,
        out_shape=out_shape,
        scratch_shapes=[pltpu.VMEM((tm, d), BF16)],
        compiler_params=_params("parallel", "arbitrary"),
        name="rms_matmul",
    )(x, g.reshape(1, d), w16)
    return res if len(res) > 1 else res[0]


def _ffn_body(x_ref, gpre_ref, gpost_ref, wg_ref, wu_ref, wd_ref, o_ref, h_scr, acc_scr):
    j = pl.program_id(1)

    @pl.when(j == 0)
    def _():
        h_scr[...] = _rms(x_ref[...], gpre_ref[...]).astype(BF16)
        acc_scr[...] = jnp.zeros_like(acc_scr)

    h = h_scr[...]
    gate = jnp.dot(h, wg_ref[...], preferred_element_type=F32)
    up = jnp.dot(h, wu_ref[...], preferred_element_type=F32)
    act = (gate * jax.nn.sigmoid(gate) * up).astype(BF16)
    acc_scr[...] += jnp.dot(act, wd_ref[...], preferred_element_type=F32)

    @pl.when(j == pl.num_programs(1) - 1)
    def _():
        o_ref[...] = x_ref[...] + 0.5 * _rms(acc_scr[...], gpost_ref[...])


def _ffn(x, g_pre, g_post, wg16, wu16, wd16, lead):
    m, d = x.shape
    ff = wg16.shape[-1]
    tm = min(m, ROW_TILE)
    tf = min(ff, COL_TILE)
    assert m % tm == 0 and ff % tf == 0
    up_block, w_lead = _stacked(lead, d, tf)
    down_block, _ = _stacked(lead, tf, d)
    return pl.pallas_call(
        _ffn_body,
        grid=(m // tm, ff // tf),
        in_specs=[pl.BlockSpec((tm, d), lambda i, j: (i, 0)),
                  pl.BlockSpec((1, d), lambda i, j: (0, 0)),
                  pl.BlockSpec((1, d), lambda i, j: (0, 0)),
                  pl.BlockSpec(up_block, lambda i, j: w_lead + (0, j)),
                  pl.BlockSpec(up_block, lambda i, j: w_lead + (0, j)),
                  pl.BlockSpec(down_block, lambda i, j: w_lead + (j, 0))],
        out_specs=pl.BlockSpec((tm, d), lambda i, j: (i, 0)),
        out_shape=jax.ShapeDtypeStruct((m, d), F32),
        scratch_shapes=[pltpu.VMEM((tm, d), BF16), pltpu.VMEM((tm, d), F32)],
        compiler_params=_params("parallel", "arbitrary"),
        name="ffn",
    )(x, g_pre.reshape(1, d), g_post.reshape(1, d), wg16, wu16, wd16)


def _out_body(*refs, n_parts):
    x_ref, g_ref = refs[0], refs[1]
    o_refs = refs[2:2 + n_parts]
    w_refs = refs[2 + n_parts:2 + 2 * n_parts]
    out_ref = refs[-1]
    y = None
    for o_ref, w_ref in zip(o_refs, w_refs):
        t = jnp.dot(o_ref[...].astype(BF16), w_ref[...], preferred_element_type=F32)
        y = t if y is None else y + t
    out_ref[...] = x_ref[...] + _rms(y, g_ref[...])


def _out_proj(x, g, parts, w16, lead):
    m, d = x.shape
    tm = min(m, ROW_TILE)
    k = parts[0].shape[1]
    assert m % tm == 0 and all(p.shape[1] == k for p in parts) and len(parts) * k == w16.shape[-2]
    w_block, w_lead = _stacked(lead, k, d)
    in_specs = [pl.BlockSpec((tm, d), lambda i: (i, 0)), pl.BlockSpec((1, d), lambda i: (0, 0))]
    in_specs += [pl.BlockSpec((tm, k), lambda i: (i, 0)) for _ in parts]
    in_specs += [pl.BlockSpec(w_block, lambda i, n=n: w_lead + (n, 0)) for n in range(len(parts))]
    return pl.pallas_call(
        functools.partial(_out_body, n_parts=len(parts)),
        grid=(m // tm,),
        in_specs=in_specs,
        out_specs=pl.BlockSpec((tm, d), lambda i: (i, 0)),
        out_shape=jax.ShapeDtypeStruct((m, d), F32),
        compiler_params=_params("parallel"),
        name="out_proj",
    )(x, g.reshape(1, d), *parts, *([w16] * len(parts)))


def _cross_body(q_ref, k_ref, v_ref, o_ref, *, n_heads):
    q = q_ref[...].astype(BF16)
    k = k_ref[...].astype(BF16)
    v = v_ref[...].astype(BF16)
    for h in range(n_heads):
        sl = slice(h * HEAD_DIM, (h + 1) * HEAD_DIM)
        s = lax.dot_general(q[:, sl], k[:, sl], NT_DIMS, preferred_element_type=F32) * HEAD_DIM ** -0.5
        p = jnp.exp(s - jnp.max(s, axis=-1, keepdims=True))
        l = jnp.sum(p, axis=-1, keepdims=True)
        o = jnp.dot(p.astype(BF16), v[:, sl], preferred_element_type=F32) / l
        o_ref[:, sl] = o.astype(o_ref.dtype)


def _cross_attend(q, mem_k, mem_v):
    b, t, w = q.shape
    n_mem = mem_k.shape[1]
    tq = min(t, ROW_TILE)
    assert t % tq == 0
    return pl.pallas_call(
        functools.partial(_cross_body, n_heads=w // HEAD_DIM),
        grid=(b, t // tq),
        in_specs=[pl.BlockSpec((None, tq, w), lambda bi, i: (bi, i, 0)),
                  pl.BlockSpec((None, n_mem, w), lambda bi, i: (bi, 0, 0)),
                  pl.BlockSpec((None, n_mem, w), lambda bi, i: (bi, 0, 0))],
        out_specs=pl.BlockSpec((None, tq, w), lambda bi, i: (bi, i, 0)),
        out_shape=jax.ShapeDtypeStruct((b, t, w), F32),
        compiler_params=_params("parallel", "parallel"),
        name="cross_attend",
    )(q, mem_k, mem_v)


def _diff_lambda(lam_ref, lam_init):
    lv = lam_ref[...]
    return (jnp.exp(jnp.sum(lv[0:1] * lv[1:2], axis=-1, keepdims=True))
            - jnp.exp(jnp.sum(lv[2:3] * lv[3:4], axis=-1, keepdims=True)) + lam_init)


def _fold_lanes(x, op):
    parts = [x[:, c * LANES:(c + 1) * LANES] for c in range(x.shape[1] // LANES)]
    return functools.reduce(op, parts)


def _masked_softmax_pv(n_chunks, scores, values, s_scr):
    n_maps, _, rows, _ = s_scr.shape

    def first(j, mx):
        out = []
        for a, s in enumerate(scores(j)):
            s_scr[a, j] = s
            out.append(jnp.maximum(mx[a], _fold_lanes(s, jnp.maximum)))
        return tuple(out)

    mx = lax.fori_loop(0, n_chunks, first, tuple(jnp.full((rows, LANES), NEG_INF, F32) for _ in range(n_maps)))
    m = [jnp.max(x, axis=-1, keepdims=True) for x in mx]

    def second(j, carry):
        v = values(j)
        out = []
        for a in range(n_maps):
            l, acc = carry[a]
            p = jnp.exp(s_scr[a, j] - m[a])
            out.append((l + _fold_lanes(p, jnp.add), acc + jnp.dot(p.astype(BF16), v, preferred_element_type=F32)))
        return tuple(out)

    init = tuple((jnp.zeros((rows, LANES), F32), jnp.zeros((rows, HEAD_DIM), F32)) for _ in range(n_maps))
    res = lax.fori_loop(0, n_chunks, second, init)
    return [acc / jnp.sum(l, axis=-1, keepdims=True) for l, acc in res]


def _prompt_tiles(t):
    tq = min(t, 256)
    tk = min(t, 512)
    assert t % tk == 0 and tk % tq == 0
    return tq, tk


def _chunk_bias(tab, tq, tk):
    assert tq + 1 >= T5_MAX_DIST and tk % tq == 0
    n_far = tk // tq + 1
    return _dist_tile(tab, tq, tk + n_far * tq, n_far * tq)


def _bias_window(bias_ref, k, tq, tk):
    n_far = tk // tq + 1
    start = pl.multiple_of((n_far - jnp.minimum(k, n_far)) * tq, tq)
    return bias_ref[:, pl.ds(start, tk)]


def _diff_prompt_body(q_ref, k_ref, v_ref, bias_ref, lam_ref, g_ref, o_ref, s_scr, *, tq, tk, lam_init):
    i = pl.program_id(2)
    ratio = tk // tq
    q = q_ref[...]
    q0, q1 = q[:, :DH_A], q[:, DH_A:]
    scale = DH_A ** -0.5

    def scores(j):
        ks = k_ref[pl.ds(pl.multiple_of(j * tk, tk), tk), :]
        bt = _bias_window(bias_ref, i - ratio * j, tq, tk)
        return (lax.dot_general(q0, ks[:, :DH_A], NT_DIMS, preferred_element_type=F32) * scale + bt,
                lax.dot_general(q1, ks[:, DH_A:], NT_DIMS, preferred_element_type=F32) * scale + bt)

    def values(j):
        return v_ref[pl.ds(pl.multiple_of(j * tk, tk), tk), :]

    o0, o1 = _masked_softmax_pv(i // ratio + 1, scores, values, s_scr)
    o = o0 - _diff_lambda(lam_ref, lam_init) * o1
    o_ref[...] = (_rms(o, g_ref[...]) * (1.0 - lam_init)).astype(o_ref.dtype)


def _diff_prompt(p16, col_q, col_k, col_v, n_heads, tab, lam_vec, g_subln, lam_init):
    b, t, _ = p16.shape
    tq, tk = _prompt_tiles(t)
    bias = _chunk_bias(tab, tq, tk)
    cq, ck, cv = col_q // HEAD_DIM, col_k // HEAD_DIM, col_v // HEAD_DIM
    return pl.pallas_call(
        functools.partial(_diff_prompt_body, tq=tq, tk=tk, lam_init=lam_init),
        grid=(b, n_heads, t // tq),
        in_specs=[pl.BlockSpec((None, tq, HEAD_DIM), lambda bi, h, i: (bi, i, cq + h)),
                  pl.BlockSpec((None, t, HEAD_DIM), lambda bi, h, i: (bi, 0, ck + h)),
                  pl.BlockSpec((None, t, HEAD_DIM), lambda bi, h, i: (bi, 0, cv + h)),
                  pl.BlockSpec((None,) + bias.shape[1:], lambda bi, h, i: (h, 0, 0)),
                  pl.BlockSpec(lam_vec.shape, lambda bi, h, i: (0, 0)),
                  pl.BlockSpec((1, HEAD_DIM), lambda bi, h, i: (0, 0))],
        out_specs=pl.BlockSpec((None, tq, HEAD_DIM), lambda bi, h, i: (bi, i, h)),
        out_shape=jax.ShapeDtypeStruct((b, t, n_heads * HEAD_DIM), BF16),
        scratch_shapes=[pltpu.VMEM((2, t // tk, tq, tk), F32)],
        compiler_params=_params("parallel", "parallel", "arbitrary"),
        name="diff_prompt",
    )(p16, p16, p16, bias, lam_vec, g_subln.reshape(1, HEAD_DIM))


def _fox_prompt_body(q_ref, k_ref, v_ref, cum_ref, cumt_ref, o_ref, s_scr, *, tq, tk):
    h = pl.program_id(1)
    i = pl.program_id(2)
    q = q_ref[...]
    cum = cum_ref[...]
    lane = lax.broadcasted_iota(jnp.int32, cum.shape, 1)
    cq = jnp.sum(jnp.where(lane == h, cum, 0.0), axis=-1, keepdims=True)
    ahead = lax.broadcasted_iota(jnp.int32, (tq, tk), 1) - lax.broadcasted_iota(jnp.int32, (tq, tk), 0)

    def scores(j):
        start = pl.multiple_of(j * tk, tk)
        ck = cumt_ref[:, pl.ds(start, tk)]
        s = lax.dot_general(q, k_ref[pl.ds(start, tk), :], NT_DIMS, preferred_element_type=F32)
        s = s * HEAD_DIM ** -0.5 + (cq - ck)
        return (jnp.where(ahead <= i * tq - j * tk, s, NEG_INF),)

    def values(j):
        return v_ref[pl.ds(pl.multiple_of(j * tk, tk), tk), :]

    o, = _masked_softmax_pv(i // (tk // tq) + 1, scores, values, s_scr)
    o_ref[...] = o.astype(o_ref.dtype)


def _fox_prompt(p16, col_q, col_k, col_v, n_heads, cum):
    b, t, _ = p16.shape
    tq, tk = _prompt_tiles(t)
    cq, ck, cv = col_q // HEAD_DIM, col_k // HEAD_DIM, col_v // HEAD_DIM
    cum_t = jnp.swapaxes(cum, 1, 2).reshape(b, n_heads, 1, t)
    return pl.pallas_call(
        functools.partial(_fox_prompt_body, tq=tq, tk=tk),
        grid=(b, n_heads, t // tq),
        in_specs=[pl.BlockSpec((None, tq, HEAD_DIM), lambda bi, h, i: (bi, i, cq + h)),
                  pl.BlockSpec((None, t, HEAD_DIM), lambda bi, h, i: (bi, 0, ck + h)),
                  pl.BlockSpec((None, t, HEAD_DIM), lambda bi, h, i: (bi, 0, cv + h)),
                  pl.BlockSpec((None, tq, n_heads), lambda bi, h, i: (bi, i, 0)),
                  pl.BlockSpec((None, None, 1, t), lambda bi, h, i: (bi, h, 0, 0))],
        out_specs=pl.BlockSpec((None, tq, HEAD_DIM), lambda bi, h, i: (bi, i, h)),
        out_shape=jax.ShapeDtypeStruct((b, t, n_heads * HEAD_DIM), BF16),
        scratch_shapes=[pltpu.VMEM((1, t // tk, tq, tk), F32)],
        compiler_params=_params("parallel", "parallel", "arbitrary"),
        name="fox_prompt",
    )(p16, p16, p16, cum, cum_t)


def _top_blocks(gate, n_top, limit, axis):
    n_blk = gate.shape[axis]
    blk_id = lax.broadcasted_iota(jnp.int32, gate.shape, axis).astype(F32)
    chosen = jnp.zeros(gate.shape, F32)
    g = gate
    for _ in range(n_top):
        mx = jnp.max(g, axis=axis, keepdims=True)
        idx = jnp.min(jnp.where(g == mx, blk_id, float(n_blk)), axis=axis, keepdims=True)
        pick = blk_id == idx
        chosen = jnp.where(pick & (idx < limit), 1.0, chosen)
        g = jnp.where(pick, NEG_INF, g)
    return chosen


def _top_blocks_negmask(gate, n_top, limit):
    return jnp.where(_top_blocks(gate, n_top, limit, 1) > 0.0, 0.0, NEG_INF)


def _moba_prompt_body(q_ref, k_ref, v_ref, q32_ref, k32_ref, bias_ref, o_ref, keep_scr, s_scr, *, n_blk, tk):
    blk = MOBA_BLOCK
    ratio = tk // blk
    i = pl.program_id(2)

    @pl.when(i == 0)
    def _():
        kmean = jnp.mean(k32_ref[...].reshape(n_blk, blk, HEAD_DIM), axis=1)
        gate = lax.dot_general(kmean, q32_ref[...], NT_DIMS, preferred_element_type=F32,
                               precision=lax.Precision.HIGHEST)
        blk_id = lax.broadcasted_iota(jnp.int32, gate.shape, 0)
        own = lax.broadcasted_iota(jnp.int32, gate.shape, 1) // blk
        gate = jnp.where(blk_id < own, gate, NEG_INF)
        chosen = _top_blocks(gate, min(MOBA_TOPK, n_blk), own[0:1].astype(F32), 0)
        keep_scr[...] = jnp.where(blk_id == own, 1.0, chosen)

    keep_t = keep_scr[:, pl.ds(pl.multiple_of(i * blk, blk), blk)].astype(BF16)
    eye = jnp.where(lax.broadcasted_iota(jnp.int32, (blk, blk), 0) == lax.broadcasted_iota(jnp.int32, (blk, blk), 1),
                    1.0, 0.0).astype(BF16)
    sel = jnp.where(lax.dot_general(eye, keep_t, NT_DIMS, preferred_element_type=F32) > 0.5, 0.0, NEG_INF)
    lane = lax.broadcasted_iota(jnp.int32, sel.shape, 1)
    q = q_ref[...]

    def scores(j):
        ks = k_ref[pl.ds(pl.multiple_of(j * tk, tk), tk), :]
        s = lax.dot_general(q, ks, NT_DIMS, preferred_element_type=F32) * HEAD_DIM ** -0.5
        s = s + _bias_window(bias_ref, i - ratio * j, blk, tk)
        parts = []
        for c in range(ratio):
            keep = jnp.min(jnp.where(lane == j * ratio + c, sel, 0.0), axis=-1, keepdims=True)
            parts.append(s[:, c * blk:(c + 1) * blk] + keep)
        return (parts[0] if ratio == 1 else jnp.concatenate(parts, axis=1),)

    def values(j):
        return v_ref[pl.ds(pl.multiple_of(j * tk, tk), tk), :]

    o, = _masked_softmax_pv(i // ratio + 1, scores, values, s_scr)
    o_ref[...] = o.astype(o_ref.dtype)


def _moba_prompt(p16, q32, k32, col_q, col_k, col_v, n_heads, tab):
    b, t, _ = p16.shape
    blk = MOBA_BLOCK
    assert t % blk == 0
    tk = min(t, 2 * blk)
    assert t % tk == 0
    bias = _chunk_bias(tab, blk, tk)
    cq, ck, cv = col_q // HEAD_DIM, col_k // HEAD_DIM, col_v // HEAD_DIM
    return pl.pallas_call(
        functools.partial(_moba_prompt_body, n_blk=t // blk, tk=tk),
        grid=(b, n_heads, t // blk),
        in_specs=[pl.BlockSpec((None, blk, HEAD_DIM), lambda bi, h, i: (bi, i, cq + h)),
                  pl.BlockSpec((None, t, HEAD_DIM), lambda bi, h, i: (bi, 0, ck + h)),
                  pl.BlockSpec((None, t, HEAD_DIM), lambda bi, h, i: (bi, 0, cv + h)),
                  pl.BlockSpec((None, t, HEAD_DIM), lambda bi, h, i: (bi, 0, h)),
                  pl.BlockSpec((None, t, HEAD_DIM), lambda bi, h, i: (bi, 0, h)),
                  pl.BlockSpec((None,) + bias.shape[1:], lambda bi, h, i: (h, 0, 0))],
        out_specs=pl.BlockSpec((None, blk, HEAD_DIM), lambda bi, h, i: (bi, i, h)),
        out_shape=jax.ShapeDtypeStruct((b, t, n_heads * HEAD_DIM), BF16),
        scratch_shapes=[pltpu.VMEM((t // blk, t), F32), pltpu.VMEM((1, t // tk, blk, tk), F32)],
        compiler_params=_params("parallel", "parallel", "arbitrary"),
        name="moba_prompt",
    )(p16, p16, p16, q32, k32, bias)


KEY_SIGN = -2 ** 31
KEY_OF_NEG_INF = -2139095041


def _order_key(score):
    bits = pltpu.bitcast(score, jnp.int32)
    key = jnp.where(bits < 0, bits ^ 0x7FFFFFFF, bits)
    return jnp.where(score == 0.0, 0, key)


def _kth_largest_key(count_ge, n_rows, k):
    def bit_body(b, ans):
        cand = ans | jnp.left_shift(jnp.int32(1), 31 - b)
        cnt, = count_ge([cand ^ KEY_SIGN])
        return jnp.where(cnt >= k, cand, ans)

    ans = lax.fori_loop(0, 32, bit_body, jnp.zeros((n_rows, 1), jnp.int32))
    return ans ^ KEY_SIGN


def _dsa_prompt_body(iq_ref, ikw_ref, kidx_ref, qb_ref, kb_ref, vb_ref, bias_ref, o_ref, key_scr, nm_scr, w_scr,
                     s_scr, *, tq, tk, n_heads, n_sel):
    i = pl.program_id(1)
    n_chunks = i + 1
    iq = iq_ref[...].reshape(N_IDX_HEADS * tq, IDX_DIM)
    w = ikw_ref[:, IDX_DIM:IDX_DIM + N_IDX_HEADS] * N_IDX_HEADS ** -0.5 * IDX_DIM ** -0.5
    for n in range(N_IDX_HEADS):
        w_scr[n] = jnp.broadcast_to(w[:, n:n + 1], (tq, tq))
    row = lax.broadcasted_iota(jnp.int32, (tq, tq), 0)
    col = lax.broadcasted_iota(jnp.int32, (tq, tq), 1)

    def score_body(j, _):
        kc = kidx_ref[pl.ds(pl.multiple_of(j * tq, tq), tq), :][:, :IDX_DIM]
        rel = jnp.maximum(lax.dot_general(iq, kc, NT_DIMS, preferred_element_type=F32), 0.0)
        rel = rel.reshape(N_IDX_HEADS, tq, tq)
        sc = w_scr[0] * rel[0]
        for n in range(1, N_IDX_HEADS):
            sc = sc + w_scr[n] * rel[n]
        sc = jnp.where((j < i) | (col <= row), sc, NEG_INF)
        key_scr[j] = _order_key(sc)
        return 0

    lax.fori_loop(0, n_chunks, score_body, 0)

    def count_many(preds):
        def body(j, accs):
            k = key_scr[j]
            return tuple(acc + jnp.where(pred(k), 1.0, 0.0) for acc, pred in zip(accs, preds))
        accs = lax.fori_loop(0, n_chunks, body, tuple(jnp.zeros((tq, tq), F32) for _ in preds))
        return [jnp.sum(acc, axis=-1, keepdims=True) for acc in accs]

    def count(pred):
        return count_many([pred])[0]

    thr = _kth_largest_key(lambda ts: count_many([(lambda k, t=t: k >= t) for t in ts]), tq, n_sel)
    cnt_ge = count(lambda k: k >= thr)
    tie = jnp.max(jnp.where((cnt_ge > n_sel) & (thr > KEY_OF_NEG_INF), 1.0, 0.0)) > 0.0

    @pl.when(jnp.logical_not(tie))
    def _():
        def body(j, _):
            nm_scr[j] = jnp.where(key_scr[j] >= thr, 0.0, NEG_INF)
            return 0
        lax.fori_loop(0, n_chunks, body, 0)

    @pl.when(tie)
    def _():
        allow = n_sel - count(lambda k: k > thr)
        tri = jnp.where(row <= col, 1.0, 0.0).astype(BF16)

        def body(j, before):
            k = key_scr[j]
            eq = jnp.where(k == thr, 1.0, 0.0)
            rank = jnp.dot(eq.astype(BF16), tri, preferred_element_type=F32) + before
            keep = jnp.where(k > thr, 1.0, jnp.where(rank <= allow, eq, 0.0))
            nm_scr[j] = jnp.where(keep > 0.0, 0.0, NEG_INF)
            return before + jnp.sum(eq, axis=-1, keepdims=True)
        lax.fori_loop(0, n_chunks, body, jnp.zeros((tq, 1), F32))

    qb = qb_ref[...]
    qs = jnp.concatenate([qb[:, h * HEAD_DIM:(h + 1) * HEAD_DIM] for h in range(n_heads)], axis=0)

    ratio = tk // tq
    n_wide = i // ratio + 1

    def clear(j, _):
        nm_scr[j] = jnp.zeros((tq, tq), F32)
        return 0

    lax.fori_loop(n_chunks, n_wide * ratio, clear, 0)

    def scores(j):
        rows = pl.ds(pl.multiple_of(j * tk, tk), tk)
        s = lax.dot_general(qs, kb_ref[rows, :], NT_DIMS, preferred_element_type=F32) * HEAD_DIM ** -0.5
        nm = jnp.concatenate([nm_scr[j * ratio + c] for c in range(ratio)], axis=1) if ratio > 1 else nm_scr[j]
        return (s + _bias_window(bias_ref, i - ratio * j, tq, tk) + jnp.tile(nm, (n_heads, 1)),)

    def values(j):
        return vb_ref[pl.ds(pl.multiple_of(j * tk, tk), tk), :]

    o, = _masked_softmax_pv(n_wide, scores, values, s_scr)
    for h in range(n_heads):
        o_ref[:, h * HEAD_DIM:(h + 1) * HEAD_DIM] = o[h * tq:(h + 1) * tq].astype(o_ref.dtype)


def _dsa_prompt(p16, small32, iq_t, col_qb, col_kb, col_vb, col_ik, col_ik32, n_heads, tab, n_sel):
    b, t, _ = p16.shape
    tq = min(t, 128)
    tk = min(t, 4 * tq)
    assert t % tk == 0 and tk % tq == 0
    bias = _chunk_bias(tab, tq, tk)
    bias = bias.reshape(n_heads * tq, bias.shape[-1])
    qw = n_heads * HEAD_DIM
    assert col_qb % qw == 0 and col_ik % LANES == 0 and col_ik32 % LANES == 0
    return pl.pallas_call(
        functools.partial(_dsa_prompt_body, tq=tq, tk=tk, n_heads=n_heads, n_sel=n_sel),
        grid=(b, t // tq),
        in_specs=[pl.BlockSpec((None, N_IDX_HEADS, tq, IDX_DIM), lambda bi, i: (bi, 0, i, 0)),
                  pl.BlockSpec((None, tq, LANES), lambda bi, i: (bi, i, col_ik32 // LANES)),
                  pl.BlockSpec((None, t, LANES), lambda bi, i: (bi, 0, col_ik // LANES)),
                  pl.BlockSpec((None, tq, qw), lambda bi, i: (bi, i, col_qb // qw)),
                  pl.BlockSpec((None, t, HEAD_DIM), lambda bi, i: (bi, 0, col_kb // HEAD_DIM)),
                  pl.BlockSpec((None, t, HEAD_DIM), lambda bi, i: (bi, 0, col_vb // HEAD_DIM)),
                  pl.BlockSpec(bias.shape, lambda bi, i: (0, 0))],
        out_specs=pl.BlockSpec((None, tq, qw), lambda bi, i: (bi, i, 0)),
        out_shape=jax.ShapeDtypeStruct((b, t, qw), BF16),
        scratch_shapes=[pltpu.VMEM((t // tq, tq, tq), jnp.int32), pltpu.VMEM((t // tq, tq, tq), F32),
                        pltpu.VMEM((N_IDX_HEADS, tq, tq), F32), pltpu.VMEM((1, t // tk, n_heads * tq, tk), F32)],
        compiler_params=_params("parallel", "arbitrary"),
        name="dsa_prompt",
    )(iq_t, small32, p16, p16, p16, p16, bias)


def _block_diag_rows(q, dtype):
    b, tn, g, d = q.shape
    eye = jnp.eye(g, dtype=q.dtype)
    return jnp.einsum('btgd,gk->bgtkd', q, eye).reshape(b, g * tn, g * d).astype(dtype)


def _page_specs(layer, n_pages, group, rows, width):
    def spec(g):
        return pl.BlockSpec((None, None, rows, width),
                            lambda bi, p, pt: (layer, pt[bi, jnp.minimum(p * group + g, n_pages - 1)], 0, 0))
    return [spec(g) for g in range(group)]


def _past_bias_index(page_idx, n_pages):
    return jnp.clip(page_idx - (n_pages - 2), 0, 1)


def _paged_call(body, pt, n_steps, operands, in_specs, out_shape, out_spec, scratch, name):
    return pl.pallas_call(
        body,
        grid_spec=pltpu.PrefetchScalarGridSpec(
            num_scalar_prefetch=1, grid=(pt.shape[0], n_steps),
            in_specs=in_specs, out_specs=out_spec, scratch_shapes=scratch),
        out_shape=out_shape,
        compiler_params=_params("parallel", "arbitrary"),
        name=name,
    )(pt, *operands)


def _per_batch(rows, width):
    return pl.BlockSpec((None, rows, width), lambda bi, p, pt: (bi, 0, 0))


def _whole(shape):
    return pl.BlockSpec(shape, lambda bi, p, pt: (0,) * len(shape))


def _rows_page(x, page):
    b, tn, w = x.shape
    return jnp.pad(x, ((0, 0), (0, page - tn), (0, 0))).reshape(b, page * (w // HEAD_DIM), HEAD_DIM)


def _transposed_page(x, page):
    return jnp.pad(jnp.swapaxes(x, 1, 2), ((0, 0), (0, 0), (0, page - x.shape[1])))


def _flash_scratch(rows, width):
    return [pltpu.VMEM((rows, 1), F32), pltpu.VMEM((rows, 1), F32), pltpu.VMEM((rows, width), F32)]


def _flash_init(m_scr, l_scr, acc_scr):
    m_scr[...] = jnp.full(m_scr.shape, NEG_INF, F32)
    l_scr[...] = jnp.zeros(l_scr.shape, F32)
    acc_scr[...] = jnp.zeros(acc_scr.shape, F32)


def _flash_update(s, pv, m_scr, l_scr, acc_scr):
    m = m_scr[...]
    m_new = jnp.maximum(m, jnp.max(s, axis=-1, keepdims=True))
    m_safe = jnp.where(m_new == NEG_INF, 0.0, m_new)
    p = jnp.exp(s - m_safe)
    alpha = jnp.exp(m - m_safe)
    l_scr[...] = alpha * l_scr[...] + jnp.sum(p, axis=-1, keepdims=True)
    acc_scr[...] = alpha * acc_scr[...] + pv(p.astype(BF16))
    m_scr[...] = m_new


def _cat16(refs, axis):
    parts = [r[...].astype(BF16) for r in refs]
    return parts[0] if len(parts) == 1 else jnp.concatenate(parts, axis=axis)


SAMPLE_PAGE_GROUP = 8
SMALL_PAGE_GROUP = 8

def _heads_first(o, b, n_heads, tn):
    return jnp.swapaxes(o.reshape(b, n_heads, tn, HEAD_DIM), 1, 2).reshape(b, tn, n_heads * HEAD_DIM)


def _diff_sample_body(pt_ref, wq_ref, bias_ref, lam_ref, g_ref, kn_ref, vn_ref, *rest,
                      n_pages, group, page, n_new, n_heads, lam_init):
    k_refs, v_refs = rest[:group], rest[group:2 * group]
    o_ref, m_scr, l_scr, acc_scr = rest[2 * group:]
    p = pl.program_id(1)
    rows_h = 2 * n_new

    @pl.when(p == 0)
    def _():
        _flash_init(m_scr, l_scr, acc_scr)

    def step(ks, vs, bias):
        s = jnp.dot(wq_ref[...], _cat16(ks, 1), preferred_element_type=F32) * DH_A ** -0.5 + bias

        def pv(p16):
            outs = []
            for h in range(n_heads):
                vh = [v[pl.ds(h, page, stride=n_heads), :].astype(BF16) for v in vs]
                vh = vh[0] if len(vh) == 1 else jnp.concatenate(vh, axis=0)
                outs.append(jnp.dot(p16[h * rows_h:(h + 1) * rows_h], vh, preferred_element_type=F32))
            return jnp.concatenate(outs, axis=0)

        _flash_update(s, pv, m_scr, l_scr, acc_scr)

    @pl.when(p < n_pages // group)
    def _():
        tiles = [bias_ref[_past_bias_index(p * group + g, n_pages)] for g in range(group)]
        step(k_refs, v_refs, jnp.concatenate(tiles, axis=1))

    @pl.when(p == n_pages // group)
    def _():
        step([kn_ref], [vn_ref], bias_ref[2])
        lam = _diff_lambda(lam_ref, lam_init)
        on = acc_scr[...] / l_scr[...]
        for h in range(n_heads):
            r0 = h * rows_h
            o = on[r0:r0 + n_new] - lam * on[r0 + n_new:r0 + rows_h]
            o_ref[h * n_new:(h + 1) * n_new, :] = _rms(o, g_ref[...]) * (1.0 - lam_init)


def _diff_sample(pt, layer, q, cache_kt, cache_v, k_new, v_new, tab, lam_vec, g_subln, lam_init):
    b, tn, n_heads = q.shape[:3]
    n_pages = pt.shape[1]
    width, page = cache_kt.shape[2:]
    group = min(SAMPLE_PAGE_GROUP, n_pages)
    assert n_pages % group == 0
    wq = _block_diag_rows(q.reshape(b, tn, 2 * n_heads, DH_A), BF16)
    rows = 2 * n_heads * tn
    bias = _sample_bias(tab, tn, page)
    bias = jnp.broadcast_to(bias[:, :, None], (3, n_heads, 2, tn, page)).reshape(3, rows, page)
    out = _paged_call(
        functools.partial(_diff_sample_body, n_pages=n_pages, group=group, page=page, n_new=tn, n_heads=n_heads,
                          lam_init=lam_init),
        pt, n_pages // group + 1,
        (wq, bias, lam_vec, g_subln.reshape(1, HEAD_DIM), _transposed_page(k_new, page), _rows_page(v_new, page))
        + (cache_kt,) * group + (cache_v,) * group,
        [_per_batch(rows, width), _whole(bias.shape), _whole(lam_vec.shape), _whole((1, HEAD_DIM)),
         _per_batch(width, page), _per_batch(page * n_heads, HEAD_DIM)]
        + _page_specs(layer, n_pages, group, width, page) + _page_specs(layer, n_pages, group, page * n_heads, HEAD_DIM),
        jax.ShapeDtypeStruct((b, n_heads * tn, HEAD_DIM), F32), _per_batch(n_heads * tn, HEAD_DIM),
        _flash_scratch(rows, HEAD_DIM), "diff_sample")
    return _heads_first(out, b, n_heads, tn)


def _fox_sample_body(pt_ref, q_ref, cq_ref, ckp_ref, ckn_ref, hm_ref, nm_ref, kn_ref, vn_ref, *rest,
                     n_pages, group):
    k_refs, v_refs = rest[:group], rest[group:2 * group]
    o_ref, m_scr, l_scr, acc_scr = rest[2 * group:]
    p = pl.program_id(1)

    @pl.when(p == 0)
    def _():
        _flash_init(m_scr, l_scr, acc_scr)

    def step(ks, vs, ck, mask):
        s = lax.dot_general(q_ref[...], _cat16(ks, 0), NT_DIMS, preferred_element_type=F32)
        s = s * HEAD_DIM ** -0.5 + (cq_ref[...] - ck) + mask
        _flash_update(s, lambda p16: jnp.dot(p16, _cat16(vs, 0), preferred_element_type=F32), m_scr, l_scr, acc_scr)

    @pl.when(p < n_pages // group)
    def _():
        step(k_refs, v_refs, ckp_ref[...], hm_ref[...])

    @pl.when(p == n_pages // group)
    def _():
        step([kn_ref], [vn_ref], ckn_ref[...], nm_ref[...])
        o_ref[...] = acc_scr[...] / l_scr[...]


def _fox_sample(pt, layer, q, cache_k, cache_v, k_new, v_new, cum_q, cum_past, cum_new):
    b, tn, n_heads = q.shape[:3]
    n_pages = pt.shape[1]
    prow = cache_k.shape[2]
    page = prow // n_heads
    group = min(SAMPLE_PAGE_GROUP, n_pages)
    assert n_pages % group == 0
    rows = n_heads * tn
    q_rows = jnp.swapaxes(q, 1, 2).reshape(b, rows, HEAD_DIM).astype(BF16)
    cq = jnp.swapaxes(cum_q, 1, 2).reshape(b, rows, 1)
    ck_past = cum_past.reshape(b, n_pages // group, 1, group * prow)
    ck_new = jnp.pad(cum_new, ((0, 0), (0, page - tn), (0, 0))).reshape(b, 1, prow)
    head_mask = _expand_heads(jnp.zeros((n_heads, tn, page), F32))
    qi = np.arange(tn)[:, None]
    ci = np.arange(page)[None, :]
    causal = np.broadcast_to(np.where(ci <= qi, 0.0, NEG_INF).astype(np.float32), (n_heads, tn, page))
    new_mask = _expand_heads(jnp.asarray(causal))
    out = _paged_call(
        functools.partial(_fox_sample_body, n_pages=n_pages, group=group),
        pt, n_pages // group + 1,
        (q_rows, cq, ck_past, ck_new, jnp.tile(head_mask, (1, group)), new_mask,
         _rows_page(k_new, page), _rows_page(v_new, page)) + (cache_k,) * group + (cache_v,) * group,
        [_per_batch(rows, HEAD_DIM), _per_batch(rows, 1),
         pl.BlockSpec((None, None, 1, group * prow),
                      lambda bi, p, pt_: (bi, jnp.minimum(p, n_pages // group - 1), 0, 0)),
         _per_batch(1, prow), _whole((rows, group * prow)), _whole((rows, prow)),
         _per_batch(prow, HEAD_DIM), _per_batch(prow, HEAD_DIM)]
        + _page_specs(layer, n_pages, group, prow, HEAD_DIM) + _page_specs(layer, n_pages, group, prow, HEAD_DIM),
        jax.ShapeDtypeStruct((b, rows, HEAD_DIM), F32), _per_batch(rows, HEAD_DIM),
        _flash_scratch(rows, HEAD_DIM), "fox_sample")
    return _heads_first(out, b, n_heads, tn)


def _kmean_body(pt_ref, *refs, n_heads, pages_per_block):
    k_refs, o_ref = refs[:-1], refs[-1]
    for blk in range(len(k_refs) // pages_per_block):
        total = None
        for k_ref in k_refs[blk * pages_per_block:(blk + 1) * pages_per_block]:
            k = k_ref[...]
            s = jnp.sum(k.reshape(k.shape[0] // n_heads, n_heads, HEAD_DIM), axis=0)
            total = s if total is None else total + s
        o_ref[blk] = total * (1.0 / MOBA_BLOCK)


def _moba_gate_body(wq_ref, kmean_ref, o_ref, *, n_blk):
    gate = lax.dot_general(wq_ref[...], kmean_ref[...], NT_DIMS, preferred_element_type=F32,
                           precision=lax.Precision.HIGHEST)
    o_ref[...] = _top_blocks_negmask(gate, min(MOBA_TOPK, n_blk), float(n_blk))


def _moba_sample_body(pt_ref, q_ref, bias_ref, sel_ref, kn_ref, vn_ref, *rest, n_pages, group, pages_per_block):
    k_refs, v_refs = rest[:group], rest[group:2 * group]
    o_ref, m_scr, l_scr, acc_scr = rest[2 * group:]
    p = pl.program_id(1)

    @pl.when(p == 0)
    def _():
        _flash_init(m_scr, l_scr, acc_scr)

    def scores(k_ref, bias):
        s = lax.dot_general(q_ref[...], k_ref[...].astype(BF16), NT_DIMS, preferred_element_type=F32)
        return s * HEAD_DIM ** -0.5 + bias

    def update(s, vs):
        _flash_update(s, lambda p16: jnp.dot(p16, _cat16(vs, 0), preferred_element_type=F32), m_scr, l_scr, acc_scr)

    @pl.when(p < n_pages // group)
    def _():
        sel = sel_ref[...]
        lane = lax.broadcasted_iota(jnp.int32, sel.shape, 1)
        parts = []
        for g in range(group):
            page_idx = p * group + g
            row_mask = jnp.min(jnp.where(lane == page_idx // pages_per_block, sel, 0.0), axis=-1, keepdims=True)
            parts.append(scores(k_refs[g], bias_ref[_past_bias_index(page_idx, n_pages)]) + row_mask)
        update(parts[0] if group == 1 else jnp.concatenate(parts, axis=1), v_refs)

    @pl.when(p == n_pages // group)
    def _():
        update(scores(kn_ref, bias_ref[2]), [vn_ref])
        o_ref[...] = acc_scr[...] / l_scr[...]


def _moba_sample(pt, layer, q, cache_k, cache_v, k_new, v_new, tab):
    b, tn, n_heads = q.shape[:3]
    n_pages = pt.shape[1]
    prow = cache_k.shape[2]
    page = prow // n_heads
    width = n_heads * HEAD_DIM
    assert MOBA_BLOCK % page == 0 and (n_pages * page) % MOBA_BLOCK == 0 and tn < MOBA_BLOCK
    ppb = MOBA_BLOCK // page
    n_blk = n_pages // ppb
    group = min(SAMPLE_PAGE_GROUP, n_pages)
    assert n_pages % group == 0
    rows = n_heads * tn
    bps = group // ppb if group % ppb == 0 else 1
    assert n_blk % bps == 0
    kmean = _paged_call(
        functools.partial(_kmean_body, n_heads=n_heads, pages_per_block=ppb), pt, n_blk // bps,
        (cache_k,) * (bps * ppb), _page_specs(layer, n_pages, bps * ppb, prow, HEAD_DIM),
        jax.ShapeDtypeStruct((b, n_blk, n_heads, HEAD_DIM), F32),
        pl.BlockSpec((None, bps, n_heads, HEAD_DIM), lambda bi, p, pt_: (bi, p, 0, 0)), [], "moba_kmean")
    sel = pl.pallas_call(
        functools.partial(_moba_gate_body, n_blk=n_blk),
        grid=(b,),
        in_specs=[pl.BlockSpec((None, rows, width), lambda bi: (bi, 0, 0)),
                  pl.BlockSpec((None, n_blk, width), lambda bi: (bi, 0, 0))],
        out_specs=pl.BlockSpec((None, rows, n_blk), lambda bi: (bi, 0, 0)),
        out_shape=jax.ShapeDtypeStruct((b, rows, n_blk), F32),
        compiler_params=_params("parallel"),
        name="moba_gate",
    )(_block_diag_rows(q, F32), kmean.reshape(b, n_blk, width))
    bias = _sample_bias(tab, tn, page)
    bias = jnp.stack([_expand_heads(bias[i]) for i in range(3)])
    q_rows = jnp.swapaxes(q, 1, 2).reshape(b, rows, HEAD_DIM).astype(BF16)
    out = _paged_call(
        functools.partial(_moba_sample_body, n_pages=n_pages, group=group, pages_per_block=ppb),
        pt, n_pages // group + 1,
        (q_rows, bias, sel, _rows_page(k_new, page), _rows_page(v_new, page)) + (cache_k,) * group + (cache_v,) * group,
        [_per_batch(rows, HEAD_DIM), _whole(bias.shape), _per_batch(rows, n_blk),
         _per_batch(prow, HEAD_DIM), _per_batch(prow, HEAD_DIM)]
        + _page_specs(layer, n_pages, group, prow, HEAD_DIM) + _page_specs(layer, n_pages, group, prow, HEAD_DIM),
        jax.ShapeDtypeStruct((b, rows, HEAD_DIM), F32), _per_batch(rows, HEAD_DIM),
        _flash_scratch(rows, HEAD_DIM), "moba_sample")
    return _heads_first(out, b, n_heads, tn)


def _dsa_score_body(pt_ref, iq_ref, w_ref, mask_ref, kn_ref, *rest, n_pages, group, n_new):
    k_refs, o_ref = rest[:group], rest[group]
    p = pl.program_id(1)

    def score(kt16):
        rel = jnp.dot(iq_ref[...], kt16, preferred_element_type=F32)
        rel = jnp.maximum(rel * IDX_DIM ** -0.5, 0.0) * w_ref[...]
        return jnp.sum(rel.reshape(N_IDX_HEADS, n_new, rel.shape[1]), axis=0)

    @pl.when(p < n_pages // group)
    def _():
        o_ref[...] = score(_cat16(k_refs, 1))

    @pl.when(p == n_pages // group)
    def _():
        page = kn_ref.shape[1]
        o_ref[...] = jnp.full(o_ref.shape, NEG_INF, F32)
        o_ref[:, :page] = score(kn_ref[...].astype(BF16)) + mask_ref[...]


def _dsa_select_body(sc_ref, o_ref, key_scr, *, n_new, n_sel, chunk):
    width = sc_ref.shape[1]
    key_scr[...] = _order_key(sc_ref[...])

    def count(pred):
        return jnp.sum(jnp.where(pred(key_scr[...]), 1.0, 0.0), axis=-1, keepdims=True)

    thr = _kth_largest_key(lambda ts: [count(lambda k, t=t: k >= t) for t in ts], n_new, n_sel)
    cnt_ge = count(lambda k: k >= thr)
    tie = jnp.max(jnp.where((cnt_ge > n_sel) & (thr > KEY_OF_NEG_INF), 1.0, 0.0)) > 0.0

    @pl.when(jnp.logical_not(tie))
    def _():
        o_ref[...] = jnp.where(key_scr[...] >= thr, 0.0, NEG_INF)

    @pl.when(tie)
    def _():
        allow = n_sel - count(lambda k: k > thr)
        r = lax.broadcasted_iota(jnp.int32, (chunk, chunk), 0)
        c = lax.broadcasted_iota(jnp.int32, (chunk, chunk), 1)
        tri = jnp.where(r <= c, 1.0, 0.0).astype(BF16)

        def body(j, before):
            cols = pl.ds(pl.multiple_of(j * chunk, chunk), chunk)
            k = key_scr[:, cols]
            eq = jnp.where(k == thr, 1.0, 0.0)
            rank = jnp.dot(eq.astype(BF16), tri, preferred_element_type=F32) + before
            keep = jnp.where(k > thr, 1.0, jnp.where(rank <= allow, eq, 0.0))
            o_ref[:, cols] = jnp.where(keep > 0.0, 0.0, NEG_INF)
            return before + jnp.sum(eq, axis=-1, keepdims=True)
        lax.fori_loop(0, width // chunk, body, jnp.zeros((n_new, 1), F32))


def _dsa_sample_body(pt_ref, q_ref, bias_ref, nm_ref, kn_ref, vn_ref, *rest, n_pages, group, n_heads):
    k_refs, v_refs = rest[:group], rest[group:2 * group]
    o_ref, m_scr, l_scr, acc_scr = rest[2 * group:]
    p = pl.program_id(1)

    @pl.when(p == 0)
    def _():
        _flash_init(m_scr, l_scr, acc_scr)

    def step(ks, vs, bias, nm):
        s = lax.dot_general(q_ref[...], _cat16(ks, 0), NT_DIMS, preferred_element_type=F32)
        s = s * HEAD_DIM ** -0.5 + bias + jnp.tile(nm, (n_heads, 1))
        _flash_update(s, lambda p16: jnp.dot(p16, _cat16(vs, 0), preferred_element_type=F32), m_scr, l_scr, acc_scr)

    @pl.when(p < n_pages // group)
    def _():
        tiles = [bias_ref[_past_bias_index(p * group + g, n_pages)] for g in range(group)]
        step(k_refs, v_refs, jnp.concatenate(tiles, axis=1), nm_ref[...])

    @pl.when(p == n_pages // group)
    def _():
        page = kn_ref.shape[0]
        step([kn_ref], [vn_ref], bias_ref[2], nm_ref[:, :page])
        o_ref[...] = acc_scr[...] / l_scr[...]


def _dsa_sample(pt, layer, qb, iq, iw, cache_k, cache_v, cache_idx_t, k_new, v_new, ik_new, tab):
    b, tn, n_heads = qb.shape[:3]
    n_pages = pt.shape[1]
    page = cache_k.shape[2]
    n_sel = min(DSA_TOPK, (n_pages * page + tn) // 4)
    group = min(SMALL_PAGE_GROUP, n_pages)
    assert n_pages % group == 0
    n_steps = n_pages // group + 1
    width = n_steps * group * page
    pad = ((0, 0), (0, page - tn), (0, 0))
    qi = np.arange(tn)[:, None]
    ci = np.arange(page)[None, :]
    new_mask = jnp.asarray(np.where(ci <= qi, 0.0, NEG_INF).astype(np.float32))
    n_iq = N_IDX_HEADS * tn
    iq_rows = jnp.swapaxes(iq, 1, 2).reshape(b, n_iq, IDX_DIM).astype(BF16)
    w_rows = (jnp.swapaxes(iw, 1, 2).astype(F32) * N_IDX_HEADS ** -0.5).reshape(b, n_iq, 1)
    step_cols = pl.BlockSpec((None, tn, group * page), lambda bi, p, pt_: (bi, 0, p))
    scores = _paged_call(
        functools.partial(_dsa_score_body, n_pages=n_pages, group=group, n_new=tn),
        pt, n_steps,
        (iq_rows, w_rows, new_mask, _transposed_page(ik_new, page)) + (cache_idx_t,) * group,
        [_per_batch(n_iq, IDX_DIM), _per_batch(n_iq, 1), _whole(new_mask.shape), _per_batch(IDX_DIM, page)]
        + _page_specs(layer, n_pages, group, IDX_DIM, page),
        jax.ShapeDtypeStruct((b, tn, width), F32), step_cols, [], "dsa_score")
    negmask = pl.pallas_call(
        functools.partial(_dsa_select_body, n_new=b * tn, n_sel=n_sel, chunk=page),
        grid=(1,),
        in_specs=[pl.BlockSpec((b * tn, width), lambda i: (0, 0))],
        out_specs=pl.BlockSpec((b * tn, width), lambda i: (0, 0)),
        out_shape=jax.ShapeDtypeStruct((b * tn, width), F32),
        scratch_shapes=[pltpu.VMEM((b * tn, width), jnp.int32)],
        compiler_params=_params("arbitrary"),
        name="dsa_select",
    )(scores.reshape(b * tn, width)).reshape(b, tn, width)
    rows = n_heads * tn
    q_rows = jnp.swapaxes(qb, 1, 2).reshape(b, rows, HEAD_DIM).astype(BF16)
    bias = _sample_bias(tab, tn, page).reshape(3, rows, page)
    out = _paged_call(
        functools.partial(_dsa_sample_body, n_pages=n_pages, group=group, n_heads=n_heads),
        pt, n_steps,
        (q_rows, bias, negmask, jnp.pad(k_new, pad), jnp.pad(v_new, pad)) + (cache_k,) * group + (cache_v,) * group,
        [_per_batch(rows, HEAD_DIM), _whole(bias.shape), step_cols, _per_batch(page, HEAD_DIM),
         _per_batch(page, HEAD_DIM)]
        + _page_specs(layer, n_pages, group, page, HEAD_DIM) + _page_specs(layer, n_pages, group, page, HEAD_DIM),
        jax.ShapeDtypeStruct((b, rows, HEAD_DIM), F32), _per_batch(rows, HEAD_DIM),
        _flash_scratch(rows, HEAD_DIM), "dsa_sample")
    return _heads_first(out, b, n_heads, tn)


def _pad_rows(w, n):
    return jnp.pad(w, ((0, 0),) * (w.ndim - 2) + ((0, n - w.shape[-2]), (0, 0)))


def _mixer_even(h_in, g, w_in16, past, pt, layer_e, lam_vec, g_subln, lam_init, tab, b, t):
    d = h_in.shape[1]
    n_a = n_b = (d // HEAD_DIM) // 2
    wa, wb = n_a * HEAD_DIM, n_b * HEAD_DIM
    wi = N_IDX_HEADS * IDX_DIM
    c_qa, c_ka, c_va, c_qb = 0, wa, 2 * wa, 3 * wa
    c_iq = c_qb + wb
    c_small = c_iq + wi
    c_kb, c_vb, c_ik = c_small, c_small + HEAD_DIM, c_small + 2 * HEAD_DIM
    tn = COL_TILE
    assert wa % tn == 0 and c_small % tn == 0 and w_in16.shape[-2] == c_small + tn
    ka_t = past is None and t % min(b * t, 2 * ROW_TILE) == 0
    ka, va, small, p16 = _rms_matmul(
        h_in, g, w_in16, (layer_e,), emit16=True, w_transposed=True, rows_per_batch=t,
        f32_groups=((c_ka // tn, wa // tn, ka_t), (c_va // tn, wa // tn), (c_small // tn, 1)))
    p16 = p16.reshape(b, t, -1)
    va, small = va.reshape(b, t, wa), small.reshape(b, t, tn)
    if ka_t:
        a_k = jnp.transpose(ka.reshape(b, n_a, 2, DH_A, t), (0, 4, 1, 2, 3))
    else:
        ka = ka.reshape(b, t, wa)
        a_k = ka.reshape(b, t, n_a, 2, DH_A)
    kb = small[..., :HEAD_DIM]
    vb = small[..., HEAD_DIM:2 * HEAD_DIM]
    ik = small[..., 2 * HEAD_DIM:2 * HEAD_DIM + IDX_DIM]
    rows = (a_k, va.reshape(b, t, n_a, HEAD_DIM), kb, vb, ik)
    tab_a, tab_b = tab[:, :n_a], tab[:, n_a:]
    if past is None:
        o_a = _diff_prompt(p16, c_qa, c_ka, c_va, n_a, tab_a, lam_vec, g_subln, lam_init)
        iq_t = jnp.swapaxes(p16[..., c_iq:c_small].reshape(b, t, N_IDX_HEADS, IDX_DIM), 1, 2)
        o_b = _dsa_prompt(p16, small, iq_t, c_qb, c_kb, c_vb, c_ik, 2 * HEAD_DIM, n_b, tab_b,
                          min(DSA_TOPK, t // 4))
    else:
        cache_a_k, cache_a_v, cache_b_k, cache_b_v, cache_b_idx = past
        qa = p16[..., c_qa:c_qa + wa].reshape(b, t, n_a, 2, DH_A)
        o_a = _diff_sample(pt, layer_e, qa, cache_a_k, cache_a_v, ka, va, tab_a, lam_vec, g_subln, lam_init)
        qb = p16[..., c_qb:c_qb + wb].reshape(b, t, n_b, HEAD_DIM)
        iq = p16[..., c_iq:c_small].reshape(b, t, N_IDX_HEADS, IDX_DIM)
        iw = small[..., 2 * HEAD_DIM + IDX_DIM:2 * HEAD_DIM + IDX_DIM + N_IDX_HEADS]
        o_b = _dsa_sample(pt, layer_e, qb, iq, iw, cache_b_k, cache_b_v, cache_b_idx, kb, vb, ik, tab_b)
    return (o_a.reshape(b * t, wa), o_b.reshape(b * t, wb)), rows


def _mixer_odd(h_in, g, w_in16, past, pt, layer_o, b_forget, tab, b, t):
    d = h_in.shape[1]
    n_c = n_d = (d // HEAD_DIM) // 2
    wc, wd = n_c * HEAD_DIM, n_d * HEAD_DIM
    c_qc, c_kc, c_vc, c_qd = 0, wc, 2 * wc, 3 * wc
    c_kd = c_qd + wd
    c_vd = c_kd + wd
    c_fc = c_vd + wd
    tn = COL_TILE
    assert wc % tn == 0 and wd % tn == 0 and w_in16.shape[-2] == c_fc + tn
    groups = tuple((c // tn, wc // tn) for c in (c_kc, c_vc, c_qd, c_kd, c_vd)) + ((c_fc // tn, 1),)
    kc, vc, qd32, kd, vd, small, p16 = _rms_matmul(h_in, g, w_in16, (layer_o,), emit16=True, w_transposed=True,
                                                   f32_groups=groups)
    p16 = p16.reshape(b, t, -1)
    kc, vc, qd32, kd, vd = (a.reshape(b, t, wc) for a in (kc, vc, qd32, kd, vd))
    log_f = jax.nn.log_sigmoid(small.reshape(b, t, tn)[..., :n_c] + b_forget.astype(F32))
    rows = (kc.reshape(b, t, n_c, HEAD_DIM), vc.reshape(b, t, n_c, HEAD_DIM), log_f,
            kd.reshape(b, t, n_d, HEAD_DIM), vd.reshape(b, t, n_d, HEAD_DIM))
    tab_d = tab[:, n_c:]
    if past is None:
        o_c = _fox_prompt(p16, c_qc, c_kc, c_vc, n_c, jnp.cumsum(log_f, axis=1))
        o_d = _moba_prompt(p16, qd32, kd, c_qd, c_kd, c_vd, n_d, tab_d)
    else:
        cache_c_k, cache_c_v, cache_c_logf, cache_d_k, cache_d_v = past
        n_pages = pt.shape[1]
        page = cache_c_logf.shape[2]
        logf_past = cache_c_logf[layer_o][pt].reshape(b, n_pages * page, n_c)
        cum = jnp.cumsum(jnp.concatenate([logf_past, log_f], axis=1).astype(F32), axis=1)
        cum_q = cum[:, n_pages * page:]
        qc = p16[..., c_qc:c_qc + wc].reshape(b, t, n_c, HEAD_DIM)
        o_c = _fox_sample(pt, layer_o, qc, cache_c_k, cache_c_v, kc, vc, cum_q, cum[:, :n_pages * page], cum_q)
        o_d = _moba_sample(pt, layer_o, qd32.reshape(b, t, n_d, HEAD_DIM), cache_d_k, cache_d_v, kd, vd, tab_d)
    return (o_c.reshape(b * t, wc), o_d.reshape(b * t, wd)), rows


def _run_trunk(x, past_even, past_odd, pt, mem_kv, prm):
    b, t, d = x.shape
    x = x.reshape(b * t, d)
    depth = prm['norm_g'].shape[0]
    rows_even, rows_odd = [], []
    for layer in range(depth):
        g = prm['norm_g'][layer]
        wg, wu, wd = prm['w_ffn_gate'], prm['w_ffn_up'], prm['w_ffn_down']
        x = _ffn(x, g[NG_FFN1_PRE], g[NG_FFN1_POST], wg, wu, wd, (layer, 0))
        if layer % 2 == 0:
            e = layer // 2
            lam_init = 0.8 - 0.6 * math.exp(-0.3 * layer)
            parts, rows = _mixer_even(x, g[NG_MIX_PRE], prm['w_in_even'], past_even, pt, e,
                                      prm['diff_lambda'][e].astype(F32), prm['g_subln'][e].astype(F32), lam_init,
                                      prm['t5_table'], b, t)
            rows_even.append(rows)
            x = _out_proj(x, g[NG_MIX_POST], parts, prm['w_out_even'], (e,))
        else:
            o = layer // 2
            parts, rows = _mixer_odd(x, g[NG_MIX_PRE], prm['w_in_odd'], past_odd, pt, o,
                                     prm['b_forget'][o], prm['t5_table'], b, t)
            rows_odd.append(rows)
            x = _out_proj(x, g[NG_MIX_POST], parts, prm['w_out_odd'], (o,))
        mk, mv = mem_kv[layer]
        q = _rms_matmul(x, g[NG_X_PRE], prm['w_xq'], (layer,))
        o_x = _cross_attend(q.reshape(b, t, -1), mk, mv)
        x = _out_proj(x, g[NG_X_POST], [o_x.reshape(b * t, -1)], prm['w_xo'], (layer,))
        x = _ffn(x, g[NG_FFN2_PRE], g[NG_FFN2_POST], wg, wu, wd, (layer, 1))
    return x.reshape(b, t, d), rows_even, rows_odd


def kernel(x_prompt, x_sample, cache_a_k, cache_a_v, cache_b_k, cache_b_v, cache_b_idx, cache_c_k, cache_c_v, cache_c_logf, cache_d_k, cache_d_v, cache_mem_k, cache_mem_v, page_table, mem_prompt, t5_table, norm_g, w_ffn_gate, w_ffn_up, w_ffn_down, w_xq, w_xk, w_xv, w_xo, w_in_even, w_out_even, diff_lambda, g_subln, w_in_odd, w_out_odd, b_forget):
    depth = norm_g.shape[0]
    d_model = x_prompt.shape[-1]
    n_c = (d_model // HEAD_DIM) // 2
    half = n_c * HEAD_DIM
    wi = N_IDX_HEADS * IDX_DIM
    e_cut = 4 * half
    w_e = jnp.swapaxes(w_in_even, 1, 2)
    w_in_even_r = jnp.concatenate([w_e[:, :e_cut], w_e[:, e_cut + 2 * HEAD_DIM:e_cut + 2 * HEAD_DIM + wi],
                                   w_e[:, e_cut:e_cut + 2 * HEAD_DIM], w_e[:, e_cut + 2 * HEAD_DIM + wi:]], axis=1)
    c_fc = 3 * half
    w_o = jnp.swapaxes(w_in_odd, 1, 2)
    w_in_odd_r = jnp.concatenate([w_o[:, :c_fc], w_o[:, c_fc + n_c:], w_o[:, c_fc:c_fc + n_c]], axis=1)
    prm = {
        't5_table': t5_table.astype(F32), 'norm_g': norm_g.astype(F32),
        'w_ffn_gate': w_ffn_gate.astype(BF16), 'w_ffn_up': w_ffn_up.astype(BF16), 'w_ffn_down': w_ffn_down.astype(BF16),
        'w_xq': w_xq.astype(BF16), 'w_xo': w_xo.astype(BF16),
        'w_in_even': _pad_rows(w_in_even_r, e_cut + wi + COL_TILE).astype(BF16),
        'w_out_even': w_out_even.astype(BF16),
        'w_in_odd': _pad_rows(w_in_odd_r, 6 * half + COL_TILE).astype(BF16),
        'w_out_odd': w_out_odd.astype(BF16),
        'diff_lambda': diff_lambda, 'g_subln': g_subln, 'b_forget': b_forget,
    }
    b_p, n_mem, _ = mem_prompt.shape
    hx_w = w_xk.shape[-1]

    mem_kv_p, mem_k_out, mem_v_out = [], [], []
    for l in range(depth):
        w_kv = jnp.concatenate([w_xk[l], w_xv[l]], axis=-1).astype(BF16)
        kv = _rms_matmul(mem_prompt.reshape(b_p * n_mem, d_model), norm_g[l, NG_MEM].astype(F32), w_kv)
        mk = kv[:, :hx_w].reshape(b_p, n_mem, hx_w)
        mv = kv[:, hx_w:].reshape(b_p, n_mem, hx_w)
        mem_kv_p.append((mk, mv))
        mem_k_out.append(mk.reshape(b_p, n_mem, hx_w // HEAD_DIM, HEAD_DIM))
        mem_v_out.append(mv.reshape(b_p, n_mem, hx_w // HEAD_DIM, HEAD_DIM))
    y_prompt, ev_p, od_p = _run_trunk(x_prompt.astype(F32), None, None, None, mem_kv_p, prm)

    def rows_pages(c):
        return c.reshape(c.shape[:2] + (c.shape[2] * c.shape[3], c.shape[4]))

    def transposed_pages(c):
        c = c.reshape(c.shape[:3] + (-1,))
        return jnp.swapaxes(c, 2, 3)

    past_even = (transposed_pages(cache_a_k), rows_pages(cache_a_v), cache_b_k, cache_b_v,
                 transposed_pages(cache_b_idx))
    past_odd = (rows_pages(cache_c_k), rows_pages(cache_c_v), cache_c_logf, rows_pages(cache_d_k),
                rows_pages(cache_d_v))
    b_s = x_sample.shape[0]
    mem_kv_s = [(cache_mem_k[l].reshape(b_s, n_mem, hx_w), cache_mem_v[l].reshape(b_s, n_mem, hx_w))
                for l in range(depth)]
    y_sample, ev_s, od_s = _run_trunk(x_sample.astype(F32), past_even, past_odd, page_table.astype(jnp.int32),
                                      mem_kv_s, prm)

    def stack(rows, i):
        return jnp.stack([r[i] for r in rows])

    out = [y_prompt, y_sample]
    out += [stack(ev_p, i) for i in range(5)] + [stack(od_p, i) for i in range(5)]
    out += [jnp.stack(mem_k_out), jnp.stack(mem_v_out)]
    out += [stack(ev_s, i) for i in range(5)] + [stack(od_s, i) for i in range(5)]
    return tuple(out)
```

```python
import functools
import math

import numpy as np
import jax
import jax.numpy as jnp
from jax import lax
from jax.experimental import pallas as pl
from jax.experimental.pallas import tpu as pltpu

F32 = jnp.float32
BF16 = jnp.bfloat16
NEG_INF = float("-inf")

HEAD_DIM = 128
DH_A = HEAD_DIM // 2
N_IDX_HEADS = 16
IDX_DIM = 64
DSA_TOPK = 256
MOBA_BLOCK = 256
MOBA_TOPK = 3
N_BUCKETS = 32
T5_MAX_EXACT = N_BUCKETS // 2
T5_MAX_DIST = 128
RMS_EPS = 1e-6
NG_FFN1_PRE, NG_FFN1_POST, NG_MIX_PRE, NG_MIX_POST = 0, 1, 2, 3
NG_X_PRE, NG_X_POST, NG_FFN2_PRE, NG_FFN2_POST, NG_MEM = 4, 5, 6, 7, 8

LANES = 128
ROW_TILE = 512
COL_TILE = 512
VMEM_LIMIT = 56 * 1024 * 1024

NT_DIMS = (((1,), (1,)), ((), ()))


def _params(*sem):
    return pltpu.CompilerParams(dimension_semantics=sem, vmem_limit_bytes=VMEM_LIMIT)


def _rms(x, g):
    return x * lax.rsqrt(jnp.mean(x * x, axis=-1, keepdims=True) + RMS_EPS) * g


def _round_up(n, m):
    return (n + m - 1) // m * m


def _bucket_np(dist):
    n = np.maximum(dist, 0)
    n_f = np.maximum(n, 1).astype(np.float32)
    large = T5_MAX_EXACT + (np.log(n_f / np.float32(T5_MAX_EXACT)) / np.float32(math.log(T5_MAX_DIST / T5_MAX_EXACT))
                            * np.float32(N_BUCKETS - T5_MAX_EXACT)).astype(np.int32)
    return np.where(n < T5_MAX_EXACT, n, np.minimum(large, N_BUCKETS - 1)).astype(np.int32)


def _rel_bias(tab, dists):
    onehot = (_bucket_np(dists)[:, None] == np.arange(N_BUCKETS)[None, :]).astype(np.float32)
    rel = jnp.sum(jnp.asarray(onehot)[:, :, None] * tab[None].astype(F32), axis=1)
    return jnp.where(jnp.asarray(dists >= 0)[:, None], rel, NEG_INF).T


def _toeplitz(u, n_rows, n_cols):
    h = u.shape[0]
    period = n_rows + n_cols
    w = jnp.concatenate([u[:, :n_cols][:, ::-1], jnp.zeros((h, 1), u.dtype), u[:, n_cols:][:, ::-1]], axis=1)
    flat = jnp.tile(w, (1, n_rows))[:, :n_rows * (period - 1)]
    return flat.reshape(h, n_rows, period - 1)[:, :, :n_cols]


def _dist_tile(tab, n_rows, n_cols, offset):
    dists = np.arange(n_rows + n_cols - 1) - (n_cols - 1) + offset
    return _toeplitz(_rel_bias(tab, dists), n_rows, n_cols)


def _far_tile(tab, n_rows, n_cols):
    return jnp.broadcast_to(tab[N_BUCKETS - 1].astype(F32)[:, None, None], (tab.shape[1], n_rows, n_cols))


def _sample_bias(tab, n_new, page):
    assert page + 1 >= T5_MAX_DIST
    return jnp.stack([_far_tile(tab, n_new, page), _dist_tile(tab, n_new, page, page), _dist_tile(tab, n_new, page, 0)])


def _expand_heads(tile):
    h, r, c = tile.shape
    same = jnp.asarray(np.eye(h, dtype=bool))[:, None, None, :]
    return jnp.where(same, tile[:, :, :, None], NEG_INF).reshape(h * r, c * h)


def _rms_matmul_body(x_ref, g_ref, w_ref, *rest, groups, emit16, w_transposed):
    outs, h_scr = rest[:-1], rest[-1]
    j = pl.program_id(1)

    @pl.when(j == 0)
    def _():
        h_scr[...] = _rms(x_ref[...], g_ref[...]).astype(BF16)

    if w_transposed:
        y = lax.dot_general(h_scr[...], w_ref[...], NT_DIMS, preferred_element_type=F32)
    else:
        y = jnp.dot(h_scr[...], w_ref[...], preferred_element_type=F32)
    for (first, count, transposed), o_ref in zip(groups, outs):
        @pl.when((j >= first) & (j < first + count))
        def _():
            o_ref[...] = y.T if transposed else y
    if emit16:
        outs[len(groups)][...] = y.astype(BF16)


def _stacked(lead, *block):
    return (None,) * len(lead) + tuple(block), tuple(lead)


def _rms_matmul(x, g, w16, lead=(), *, f32_groups=None, emit16=False, w_transposed=False, rows_per_batch=None):
    m, d = x.shape
    n = w16.shape[-2] if w_transposed else w16.shape[-1]
    tm = min(m, 2 * ROW_TILE)
    tn = min(n, COL_TILE)
    assert m % tm == 0 and n % tn == 0
    groups = tuple(tuple(grp) + (False,) * (3 - len(grp)) for grp in (f32_groups or ((0, n // tn),)))
    out_shape, out_specs = [], []
    for first, count, transposed in groups:
        col = lambda j, first=first, count=count: jnp.clip(j - first, 0, count - 1)
        if transposed:
            assert rows_per_batch % tm == 0
            per_b = rows_per_batch // tm
            out_shape.append(jax.ShapeDtypeStruct((m // rows_per_batch, count * tn, rows_per_batch), F32))
            out_specs.append(pl.BlockSpec((None, tn, tm), lambda i, j, col=col: (i // per_b, col(j), i % per_b)))
        else:
            out_shape.append(jax.ShapeDtypeStruct((m, count * tn), F32))
            out_specs.append(pl.BlockSpec((tm, tn), lambda i, j, col=col: (i, col(j))))
    if emit16:
        out_shape.append(jax.ShapeDtypeStruct((m, n), BF16))
        out_specs.append(pl.BlockSpec((tm, tn), lambda i, j: (i, j)))
    w_block, w_lead = _stacked(lead, *((tn, d) if w_transposed else (d, tn)))
    w_index = (lambda i, j: w_lead + (j, 0)) if w_transposed else (lambda i, j: w_lead + (0, j))
    res = pl.pallas_call(
        functools.partial(_rms_matmul_body, groups=groups, emit16=emit16, w_transposed=w_transposed),
        grid=(m // tm, n // tn),
        in_specs=[pl.BlockSpec((tm, d), lambda i, j: (i, 0)),
                  pl.BlockSpec((1, d), lambda i, j: (0, 0)),
                  pl.BlockSpec(w_block, w_index)],
        out_specs=out_specs,
        out_shape=out_shape,
        scratch_shapes=[pltpu.VMEM((tm, d), BF16)],
        compiler_params=_params("parallel", "arbitrary"),
        name="rms_matmul",
    )(x, g.reshape(1, d), w16)
    return res if len(res) > 1 else res[0]


def _ffn_body(x_ref, gpre_ref, gpost_ref, wg_ref, wu_ref, wd_ref, o_ref, h_scr, acc_scr):
    j = pl.program_id(1)

    @pl.when(j == 0)
    def _():
        h_scr[...] = _rms(x_ref[...], gpre_ref[...]).astype(BF16)
        acc_scr[...] = jnp.zeros_like(acc_scr)

    h = h_scr[...]
    gate = jnp.dot(h, wg_ref[...], preferred_element_type=F32)
    up = jnp.dot(h, wu_ref[...], preferred_element_type=F32)
    act = (gate * jax.nn.sigmoid(gate) * up).astype(BF16)
    acc_scr[...] += jnp.dot(act, wd_ref[...], preferred_element_type=F32)

    @pl.when(j == pl.num_programs(1) - 1)
    def _():
        o_ref[...] = x_ref[...] + 0.5 * _rms(acc_scr[...], gpost_ref[...])


def _ffn(x, g_pre, g_post, wg16, wu16, wd16, lead):
    m, d = x.shape
    ff = wg16.shape[-1]
    tm = min(m, ROW_TILE)
    tf = min(ff, COL_TILE)
    assert m % tm == 0 and ff % tf == 0
    up_block, w_lead = _stacked(lead, d, tf)
    down_block, _ = _stacked(lead, tf, d)
    return pl.pallas_call(
        _ffn_body,
        grid=(m // tm, ff // tf),
        in_specs=[pl.BlockSpec((tm, d), lambda i, j: (i, 0)),
                  pl.BlockSpec((1, d), lambda i, j: (0, 0)),
                  pl.BlockSpec((1, d), lambda i, j: (0, 0)),
                  pl.BlockSpec(up_block, lambda i, j: w_lead + (0, j)),
                  pl.BlockSpec(up_block, lambda i, j: w_lead + (0, j)),
                  pl.BlockSpec(down_block, lambda i, j: w_lead + (j, 0))],
        out_specs=pl.BlockSpec((tm, d), lambda i, j: (i, 0)),
        out_shape=jax.ShapeDtypeStruct((m, d), F32),
        scratch_shapes=[pltpu.VMEM((tm, d), BF16), pltpu.VMEM((tm, d), F32)],
        compiler_params=_params("parallel", "arbitrary"),
        name="ffn",
    )(x, g_pre.reshape(1, d), g_post.reshape(1, d), wg16, wu16, wd16)


def _out_body(*refs, n_parts):
    x_ref, g_ref = refs[0], refs[1]
    o_refs = refs[2:2 + n_parts]
    w_refs = refs[2 + n_parts:2 + 2 * n_parts]
    out_ref = refs[-1]
    y = None
    for o_ref, w_ref in zip(o_refs, w_refs):
        t = jnp.dot(o_ref[...].astype(BF16), w_ref[...], preferred_element_type=F32)
        y = t if y is None else y + t
    out_ref[...] = x_ref[...] + _rms(y, g_ref[...])


def _out_proj(x, g, parts, w16, lead):
    m, d = x.shape
    tm = min(m, ROW_TILE)
    k = parts[0].shape[1]
    assert m % tm == 0 and all(p.shape[1] == k for p in parts) and len(parts) * k == w16.shape[-2]
    w_block, w_lead = _stacked(lead, k, d)
    in_specs = [pl.BlockSpec((tm, d), lambda i: (i, 0)), pl.BlockSpec((1, d), lambda i: (0, 0))]
    in_specs += [pl.BlockSpec((tm, k), lambda i: (i, 0)) for _ in parts]
    in_specs += [pl.BlockSpec(w_block, lambda i, n=n: w_lead + (n, 0)) for n in range(len(parts))]
    return pl.pallas_call(
        functools.partial(_out_body, n_parts=len(parts)),
        grid=(m // tm,),
        in_specs=in_specs,
        out_specs=pl.BlockSpec((tm, d), lambda i: (i, 0)),
        out_shape=jax.ShapeDtypeStruct((m, d), F32),
        compiler_params=_params("parallel"),
        name="out_proj",
    )(x, g.reshape(1, d), *parts, *([w16] * len(parts)))


def _cross_body(q_ref, k_ref, v_ref, o_ref, *, n_heads):
    q = q_ref[...].astype(BF16)
    k = k_ref[...].astype(BF16)
    v = v_ref[...].astype(BF16)
    for h in range(n_heads):
        sl = slice(h * HEAD_DIM, (h + 1) * HEAD_DIM)
        s = lax.dot_general(q[:, sl], k[:, sl], NT_DIMS, preferred_element_type=F32) * HEAD_DIM ** -0.5
        p = jnp.exp(s - jnp.max(s, axis=-1, keepdims=True))
        l = jnp.sum(p, axis=-1, keepdims=True)
        o = jnp.dot(p.astype(BF16), v[:, sl], preferred_element_type=F32) / l
        o_ref[:, sl] = o.astype(o_ref.dtype)


def _cross_attend(q, mem_k, mem_v):
    b, t, w = q.shape
    n_mem = mem_k.shape[1]
    tq = min(t, ROW_TILE)
    assert t % tq == 0
    return pl.pallas_call(
        functools.partial(_cross_body, n_heads=w // HEAD_DIM),
        grid=(b, t // tq),
        in_specs=[pl.BlockSpec((None, tq, w), lambda bi, i: (bi, i, 0)),
                  pl.BlockSpec((None, n_mem, w), lambda bi, i: (bi, 0, 0)),
                  pl.BlockSpec((None, n_mem, w), lambda bi, i: (bi, 0, 0))],
        out_specs=pl.BlockSpec((None, tq, w), lambda bi, i: (bi, i, 0)),
        out_shape=jax.ShapeDtypeStruct((b, t, w), F32),
        compiler_params=_params("parallel", "parallel"),
        name="cross_attend",
    )(q, mem_k, mem_v)


def _diff_lambda(lam_ref, lam_init):
    lv = lam_ref[...]
    return (jnp.exp(jnp.sum(lv[0:1] * lv[1:2], axis=-1, keepdims=True))
            - jnp.exp(jnp.sum(lv[2:3] * lv[3:4], axis=-1, keepdims=True)) + lam_init)


def _fold_lanes(x, op):
    parts = [x[:, c * LANES:(c + 1) * LANES] for c in range(x.shape[1] // LANES)]
    return functools.reduce(op, parts)


def _masked_softmax_pv(n_chunks, scores, values, s_scr):
    n_maps, _, rows, _ = s_scr.shape

    def first(j, mx):
        out = []
        for a, s in enumerate(scores(j)):
            s_scr[a, j] = s
            out.append(jnp.maximum(mx[a], _fold_lanes(s, jnp.maximum)))
        return tuple(out)

    mx = lax.fori_loop(0, n_chunks, first, tuple(jnp.full((rows, LANES), NEG_INF, F32) for _ in range(n_maps)))
    m = [jnp.max(x, axis=-1, keepdims=True) for x in mx]

    def second(j, carry):
        v = values(j)
        out = []
        for a in range(n_maps):
            l, acc = carry[a]
            p = jnp.exp(s_scr[a, j] - m[a])
            out.append((l + _fold_lanes(p, jnp.add), acc + jnp.dot(p.astype(BF16), v, preferred_element_type=F32)))
        return tuple(out)

    init = tuple((jnp.zeros((rows, LANES), F32), jnp.zeros((rows, HEAD_DIM), F32)) for _ in range(n_maps))
    res = lax.fori_loop(0, n_chunks, second, init)
    return [acc / jnp.sum(l, axis=-1, keepdims=True) for l, acc in res]


def _prompt_tiles(t):
    tq = min(t, 256)
    tk = min(t, 512)
    assert t % tk == 0 and tk % tq == 0
    return tq, tk


def _chunk_bias(tab, tq, tk):
    assert tq + 1 >= T5_MAX_DIST and tk % tq == 0
    n_far = tk // tq + 1
    return _dist_tile(tab, tq, tk + n_far * tq, n_far * tq)


def _bias_window(bias_ref, k, tq, tk):
    n_far = tk // tq + 1
    start = pl.multiple_of((n_far - jnp.minimum(k, n_far)) * tq, tq)
    return bias_ref[:, pl.ds(start, tk)]


def _diff_prompt_body(q_ref, k_ref, v_ref, bias_ref, lam_ref, g_ref, o_ref, s_scr, *, tq, tk, lam_init):
    i = pl.program_id(2)
    ratio = tk // tq
    q = q_ref[...]
    q0, q1 = q[:, :DH_A], q[:, DH_A:]
    scale = DH_A ** -0.5

    def scores(j):
        ks = k_ref[pl.ds(pl.multiple_of(j * tk, tk), tk), :]
        bt = _bias_window(bias_ref, i - ratio * j, tq, tk)
        return (lax.dot_general(q0, ks[:, :DH_A], NT_DIMS, preferred_element_type=F32) * scale + bt,
                lax.dot_general(q1, ks[:, DH_A:], NT_DIMS, preferred_element_type=F32) * scale + bt)

    def values(j):
        return v_ref[pl.ds(pl.multiple_of(j * tk, tk), tk), :]

    o0, o1 = _masked_softmax_pv(i // ratio + 1, scores, values, s_scr)
    o = o0 - _diff_lambda(lam_ref, lam_init) * o1
    o_ref[...] = (_rms(o, g_ref[...]) * (1.0 - lam_init)).astype(o_ref.dtype)


def _diff_prompt(p16, col_q, col_k, col_v, n_heads, tab, lam_vec, g_subln, lam_init):
    b, t, _ = p16.shape
    tq, tk = _prompt_tiles(t)
    bias = _chunk_bias(tab, tq, tk)
    cq, ck, cv = col_q // HEAD_DIM, col_k // HEAD_DIM, col_v // HEAD_DIM
    return pl.pallas_call(
        functools.partial(_diff_prompt_body, tq=tq, tk=tk, lam_init=lam_init),
        grid=(b, n_heads, t // tq),
        in_specs=[pl.BlockSpec((None, tq, HEAD_DIM), lambda bi, h, i: (bi, i, cq + h)),
                  pl.BlockSpec((None, t, HEAD_DIM), lambda bi, h, i: (bi, 0, ck + h)),
                  pl.BlockSpec((None, t, HEAD_DIM), lambda bi, h, i: (bi, 0, cv + h)),
                  pl.BlockSpec((None,) + bias.shape[1:], lambda bi, h, i: (h, 0, 0)),
                  pl.BlockSpec(lam_vec.shape, lambda bi, h, i: (0, 0)),
                  pl.BlockSpec((1, HEAD_DIM), lambda bi, h, i: (0, 0))],
        out_specs=pl.BlockSpec((None, tq, HEAD_DIM), lambda bi, h, i: (bi, i, h)),
        out_shape=jax.ShapeDtypeStruct((b, t, n_heads * HEAD_DIM), BF16),
        scratch_shapes=[pltpu.VMEM((2, t // tk, tq, tk), F32)],
        compiler_params=_params("parallel", "parallel", "arbitrary"),
        name="diff_prompt",
    )(p16, p16, p16, bias, lam_vec, g_subln.reshape(1, HEAD_DIM))


def _fox_prompt_body(q_ref, k_ref, v_ref, cum_ref, cumt_ref, o_ref, s_scr, *, tq, tk):
    h = pl.program_id(1)
    i = pl.program_id(2)
    q = q_ref[...]
    cum = cum_ref[...]
    lane = lax.broadcasted_iota(jnp.int32, cum.shape, 1)
    cq = jnp.sum(jnp.where(lane == h, cum, 0.0), axis=-1, keepdims=True)
    ahead = lax.broadcasted_iota(jnp.int32, (tq, tk), 1) - lax.broadcasted_iota(jnp.int32, (tq, tk), 0)

    def scores(j):
        start = pl.multiple_of(j * tk, tk)
        ck = cumt_ref[:, pl.ds(start, tk)]
        s = lax.dot_general(q, k_ref[pl.ds(start, tk), :], NT_DIMS, preferred_element_type=F32)
        s = s * HEAD_DIM ** -0.5 + (cq - ck)
        return (jnp.where(ahead <= i * tq - j * tk, s, NEG_INF),)

    def values(j):
        return v_ref[pl.ds(pl.multiple_of(j * tk, tk), tk), :]

    o, = _masked_softmax_pv(i // (tk // tq) + 1, scores, values, s_scr)
    o_ref[...] = o.astype(o_ref.dtype)


def _fox_prompt(p16, col_q, col_k, col_v, n_heads, cum):
    b, t, _ = p16.shape
    tq, tk = _prompt_tiles(t)
    cq, ck, cv = col_q // HEAD_DIM, col_k // HEAD_DIM, col_v // HEAD_DIM
    cum_t = jnp.swapaxes(cum, 1, 2).reshape(b, n_heads, 1, t)
    return pl.pallas_call(
        functools.partial(_fox_prompt_body, tq=tq, tk=tk),
        grid=(b, n_heads, t // tq),
        in_specs=[pl.BlockSpec((None, tq, HEAD_DIM), lambda bi, h, i: (bi, i, cq + h)),
                  pl.BlockSpec((None, t, HEAD_DIM), lambda bi, h, i: (bi, 0, ck + h)),
                  pl.BlockSpec((None, t, HEAD_DIM), lambda bi, h, i: (bi, 0, cv + h)),
                  pl.BlockSpec((None, tq, n_heads), lambda bi, h, i: (bi, i, 0)),
                  pl.BlockSpec((None, None, 1, t), lambda bi, h, i: (bi, h, 0, 0))],
        out_specs=pl.BlockSpec((None, tq, HEAD_DIM), lambda bi, h, i: (bi, i, h)),
        out_shape=jax.ShapeDtypeStruct((b, t, n_heads * HEAD_DIM), BF16),
        scratch_shapes=[pltpu.VMEM((1, t // tk, tq, tk), F32)],
        compiler_params=_params("parallel", "parallel", "arbitrary"),
        name="fox_prompt",
    )(p16, p16, p16, cum, cum_t)


def _top_blocks(gate, n_top, limit, axis):
    n_blk = gate.shape[axis]
    blk_id = lax.broadcasted_iota(jnp.int32, gate.shape, axis).astype(F32)
    chosen = jnp.zeros(gate.shape, F32)
    g = gate
    for _ in range(n_top):
        mx = jnp.max(g, axis=axis, keepdims=True)
        idx = jnp.min(jnp.where(g == mx, blk_id, float(n_blk)), axis=axis, keepdims=True)
        pick = blk_id == idx
        chosen = jnp.where(pick & (idx < limit), 1.0, chosen)
        g = jnp.where(pick, NEG_INF, g)
    return chosen


def _top_blocks_negmask(gate, n_top, limit):
    return jnp.where(_top_blocks(gate, n_top, limit, 1) > 0.0, 0.0, NEG_INF)


def _moba_prompt_body(q_ref, k_ref, v_ref, q32_ref, k32_ref, bias_ref, o_ref, keep_scr, s_scr, *, n_blk, tk):
    blk = MOBA_BLOCK
    ratio = tk // blk
    i = pl.program_id(2)

    @pl.when(i == 0)
    def _():
        kmean = jnp.mean(k32_ref[...].reshape(n_blk, blk, HEAD_DIM), axis=1)
        gate = lax.dot_general(kmean, q32_ref[...], NT_DIMS, preferred_element_type=F32,
                               precision=lax.Precision.HIGHEST)
        blk_id = lax.broadcasted_iota(jnp.int32, gate.shape, 0)
        own = lax.broadcasted_iota(jnp.int32, gate.shape, 1) // blk
        gate = jnp.where(blk_id < own, gate, NEG_INF)
        chosen = _top_blocks(gate, min(MOBA_TOPK, n_blk), own[0:1].astype(F32), 0)
        keep_scr[...] = jnp.where(blk_id == own, 1.0, chosen)

    keep_t = keep_scr[:, pl.ds(pl.multiple_of(i * blk, blk), blk)].astype(BF16)
    eye = jnp.where(lax.broadcasted_iota(jnp.int32, (blk, blk), 0) == lax.broadcasted_iota(jnp.int32, (blk, blk), 1),
                    1.0, 0.0).astype(BF16)
    sel = jnp.where(lax.dot_general(eye, keep_t, NT_DIMS, preferred_element_type=F32) > 0.5, 0.0, NEG_INF)
    lane = lax.broadcasted_iota(jnp.int32, sel.shape, 1)
    q = q_ref[...]

    def scores(j):
        ks = k_ref[pl.ds(pl.multiple_of(j * tk, tk), tk), :]
        s = lax.dot_general(q, ks, NT_DIMS, preferred_element_type=F32) * HEAD_DIM ** -0.5
        s = s + _bias_window(bias_ref, i - ratio * j, blk, tk)
        parts = []
        for c in range(ratio):
            keep = jnp.min(jnp.where(lane == j * ratio + c, sel, 0.0), axis=-1, keepdims=True)
            parts.append(s[:, c * blk:(c + 1) * blk] + keep)
        return (parts[0] if ratio == 1 else jnp.concatenate(parts, axis=1),)

    def values(j):
        return v_ref[pl.ds(pl.multiple_of(j * tk, tk), tk), :]

    o, = _masked_softmax_pv(i // ratio + 1, scores, values, s_scr)
    o_ref[...] = o.astype(o_ref.dtype)


def _moba_prompt(p16, q32, k32, col_q, col_k, col_v, n_heads, tab):
    b, t, _ = p16.shape
    blk = MOBA_BLOCK
    assert t % blk == 0
    tk = min(t, 2 * blk)
    assert t % tk == 0
    bias = _chunk_bias(tab, blk, tk)
    cq, ck, cv = col_q // HEAD_DIM, col_k // HEAD_DIM, col_v // HEAD_DIM
    return pl.pallas_call(
        functools.partial(_moba_prompt_body, n_blk=t // blk, tk=tk),
        grid=(b, n_heads, t // blk),
        in_specs=[pl.BlockSpec((None, blk, HEAD_DIM), lambda bi, h, i: (bi, i, cq + h)),
                  pl.BlockSpec((None, t, HEAD_DIM), lambda bi, h, i: (bi, 0, ck + h)),
                  pl.BlockSpec((None, t, HEAD_DIM), lambda bi, h, i: (bi, 0, cv + h)),
                  pl.BlockSpec((None, t, HEAD_DIM), lambda bi, h, i: (bi, 0, h)),
                  pl.BlockSpec((None, t, HEAD_DIM), lambda bi, h, i: (bi, 0, h)),
                  pl.BlockSpec((None,) + bias.shape[1:], lambda bi, h, i: (h, 0, 0))],
        out_specs=pl.BlockSpec((None, blk, HEAD_DIM), lambda bi, h, i: (bi, i, h)),
        out_shape=jax.ShapeDtypeStruct((b, t, n_heads * HEAD_DIM), BF16),
        scratch_shapes=[pltpu.VMEM((t // blk, t), F32), pltpu.VMEM((1, t // tk, blk, tk), F32)],
        compiler_params=_params("parallel", "parallel", "arbitrary"),
        name="moba_prompt",
    )(p16, p16, p16, q32, k32, bias)


KEY_SIGN = -2 ** 31
KEY_OF_NEG_INF = -2139095041


def _order_key(score):
    bits = pltpu.bitcast(score, jnp.int32)
    key = jnp.where(bits < 0, bits ^ 0x7FFFFFFF, bits)
    return jnp.where(score == 0.0, 0, key)


def _kth_largest_key(count_ge, n_rows, k):
    def bit_body(b, ans):
        cand = ans | jnp.left_shift(jnp.int32(1), 31 - b)
        cnt, = count_ge([cand ^ KEY_SIGN])
        return jnp.where(cnt >= k, cand, ans)

    ans = lax.fori_loop(0, 32, bit_body, jnp.zeros((n_rows, 1), jnp.int32))
    return ans ^ KEY_SIGN


def _dsa_prompt_body(iq_ref, ikw_ref, kidx_ref, qb_ref, kb_ref, vb_ref, bias_ref, o_ref, key_scr, nm_scr, w_scr,
                     s_scr, *, tq, tk, n_heads, n_sel):
    i = pl.program_id(1)
    n_chunks = i + 1
    iq = iq_ref[...].reshape(N_IDX_HEADS * tq, IDX_DIM)
    w = ikw_ref[:, IDX_DIM:IDX_DIM + N_IDX_HEADS] * N_IDX_HEADS ** -0.5 * IDX_DIM ** -0.5
    for n in range(N_IDX_HEADS):
        w_scr[n] = jnp.broadcast_to(w[:, n:n + 1], (tq, tq))
    row = lax.broadcasted_iota(jnp.int32, (tq, tq), 0)
    col = lax.broadcasted_iota(jnp.int32, (tq, tq), 1)

    def score_body(j, _):
        kc = kidx_ref[pl.ds(pl.multiple_of(j * tq, tq), tq), :][:, :IDX_DIM]
        rel = jnp.maximum(lax.dot_general(iq, kc, NT_DIMS, preferred_element_type=F32), 0.0)
        rel = rel.reshape(N_IDX_HEADS, tq, tq)
        sc = w_scr[0] * rel[0]
        for n in range(1, N_IDX_HEADS):
            sc = sc + w_scr[n] * rel[n]
        sc = jnp.where((j < i) | (col <= row), sc, NEG_INF)
        key_scr[j] = _order_key(sc)
        return 0

    lax.fori_loop(0, n_chunks, score_body, 0)

    def count_many(preds):
        def body(j, accs):
            k = key_scr[j]
            return tuple(acc + jnp.where(pred(k), 1.0, 0.0) for acc, pred in zip(accs, preds))
        accs = lax.fori_loop(0, n_chunks, body, tuple(jnp.zeros((tq, tq), F32) for _ in preds))
        return [jnp.sum(acc, axis=-1, keepdims=True) for acc in accs]

    def count(pred):
        return count_many([pred])[0]

    thr = _kth_largest_key(lambda ts: count_many([(lambda k, t=t: k >= t) for t in ts]), tq, n_sel)
    cnt_ge = count(lambda k: k >= thr)
    tie = jnp.max(jnp.where((cnt_ge > n_sel) & (thr > KEY_OF_NEG_INF), 1.0, 0.0)) > 0.0

    @pl.when(jnp.logical_not(tie))
    def _():
        def body(j, _):
            nm_scr[j] = jnp.where(key_scr[j] >= thr, 0.0, NEG_INF)
            return 0
        lax.fori_loop(0, n_chunks, body, 0)

    @pl.when(tie)
    def _():
        allow = n_sel - count(lambda k: k > thr)
        tri = jnp.where(row <= col, 1.0, 0.0).astype(BF16)

        def body(j, before):
            k = key_scr[j]
            eq = jnp.where(k == thr, 1.0, 0.0)
            rank = jnp.dot(eq.astype(BF16), tri, preferred_element_type=F32) + before
            keep = jnp.where(k > thr, 1.0, jnp.where(rank <= allow, eq, 0.0))
            nm_scr[j] = jnp.where(keep > 0.0, 0.0, NEG_INF)
            return before + jnp.sum(eq, axis=-1, keepdims=True)
        lax.fori_loop(0, n_chunks, body, jnp.zeros((tq, 1), F32))

    qb = qb_ref[...]
    qs = jnp.concatenate([qb[:, h * HEAD_DIM:(h + 1) * HEAD_DIM] for h in range(n_heads)], axis=0)

    ratio = tk // tq
    n_wide = i // ratio + 1

    def clear(j, _):
        nm_scr[j] = jnp.zeros((tq, tq), F32)
        return 0

    lax.fori_loop(n_chunks, n_wide * ratio, clear, 0)

    def scores(j):
        rows = pl.ds(pl.multiple_of(j * tk, tk), tk)
        s = lax.dot_general(qs, kb_ref[rows, :], NT_DIMS, preferred_element_type=F32) * HEAD_DIM ** -0.5
        nm = jnp.concatenate([nm_scr[j * ratio + c] for c in range(ratio)], axis=1) if ratio > 1 else nm_scr[j]
        return (s + _bias_window(bias_ref, i - ratio * j, tq, tk) + jnp.tile(nm, (n_heads, 1)),)

    def values(j):
        return vb_ref[pl.ds(pl.multiple_of(j * tk, tk), tk), :]

    o, = _masked_softmax_pv(n_wide, scores, values, s_scr)
    for h in range(n_heads):
        o_ref[:, h * HEAD_DIM:(h + 1) * HEAD_DIM] = o[h * tq:(h + 1) * tq].astype(o_ref.dtype)


def _dsa_prompt(p16, small32, iq_t, col_qb, col_kb, col_vb, col_ik, col_ik32, n_heads, tab, n_sel):
    b, t, _ = p16.shape
    tq = min(t, 128)
    tk = min(t, 4 * tq)
    assert t % tk == 0 and tk % tq == 0
    bias = _chunk_bias(tab, tq, tk)
    bias = bias.reshape(n_heads * tq, bias.shape[-1])
    qw = n_heads * HEAD_DIM
    assert col_qb % qw == 0 and col_ik % LANES == 0 and col_ik32 % LANES == 0
    return pl.pallas_call(
        functools.partial(_dsa_prompt_body, tq=tq, tk=tk, n_heads=n_heads, n_sel=n_sel),
        grid=(b, t // tq),
        in_specs=[pl.BlockSpec((None, N_IDX_HEADS, tq, IDX_DIM), lambda bi, i: (bi, 0, i, 0)),
                  pl.BlockSpec((None, tq, LANES), lambda bi, i: (bi, i, col_ik32 // LANES)),
                  pl.BlockSpec((None, t, LANES), lambda bi, i: (bi, 0, col_ik // LANES)),
                  pl.BlockSpec((None, tq, qw), lambda bi, i: (bi, i, col_qb // qw)),
                  pl.BlockSpec((None, t, HEAD_DIM), lambda bi, i: (bi, 0, col_kb // HEAD_DIM)),
                  pl.BlockSpec((None, t, HEAD_DIM), lambda bi, i: (bi, 0, col_vb // HEAD_DIM)),
                  pl.BlockSpec(bias.shape, lambda bi, i: (0, 0))],
        out_specs=pl.BlockSpec((None, tq, qw), lambda bi, i: (bi, i, 0)),
        out_shape=jax.ShapeDtypeStruct((b, t, qw), BF16),
        scratch_shapes=[pltpu.VMEM((t // tq, tq, tq), jnp.int32), pltpu.VMEM((t // tq, tq, tq), F32),
                        pltpu.VMEM((N_IDX_HEADS, tq, tq), F32), pltpu.VMEM((1, t // tk, n_heads * tq, tk), F32)],
        compiler_params=_params("parallel", "arbitrary"),
        name="dsa_prompt",
    )(iq_t, small32, p16, p16, p16, p16, bias)


def _block_diag_rows(q, dtype):
    b, tn, g, d = q.shape
    eye = jnp.eye(g, dtype=q.dtype)
    return jnp.einsum('btgd,gk->bgtkd', q, eye).reshape(b, g * tn, g * d).astype(dtype)


def _page_specs(layer, n_pages, group, rows, width):
    def spec(g):
        return pl.BlockSpec((None, None, rows, width),
                            lambda bi, p, pt: (layer, pt[bi, jnp.minimum(p * group + g, n_pages - 1)], 0, 0))
    return [spec(g) for g in range(group)]


def _past_bias_index(page_idx, n_pages):
    return jnp.clip(page_idx - (n_pages - 2), 0, 1)


def _paged_call(body, pt, n_steps, operands, in_specs, out_shape, out_spec, scratch, name):
    return pl.pallas_call(
        body,
        grid_spec=pltpu.PrefetchScalarGridSpec(
            num_scalar_prefetch=1, grid=(pt.shape[0], n_steps),
            in_specs=in_specs, out_specs=out_spec, scratch_shapes=scratch),
        out_shape=out_shape,
        compiler_params=_params("parallel", "arbitrary"),
        name=name,
    )(pt, *operands)


def _per_batch(rows, width):
    return pl.BlockSpec((None, rows, width), lambda bi, p, pt: (bi, 0, 0))


def _whole(shape):
    return pl.BlockSpec(shape, lambda bi, p, pt: (0,) * len(shape))


def _rows_page(x, page):
    b, tn, w = x.shape
    return jnp.pad(x, ((0, 0), (0, page - tn), (0, 0))).reshape(b, page * (w // HEAD_DIM), HEAD_DIM)


def _transposed_page(x, page):
    return jnp.pad(jnp.swapaxes(x, 1, 2), ((0, 0), (0, 0), (0, page - x.shape[1])))


def _flash_scratch(rows, width):
    return [pltpu.VMEM((rows, 1), F32), pltpu.VMEM((rows, 1), F32), pltpu.VMEM((rows, width), F32)]


def _flash_init(m_scr, l_scr, acc_scr):
    m_scr[...] = jnp.full(m_scr.shape, NEG_INF, F32)
    l_scr[...] = jnp.zeros(l_scr.shape, F32)
    acc_scr[...] = jnp.zeros(acc_scr.shape, F32)


def _flash_update(s, pv, m_scr, l_scr, acc_scr):
    m = m_scr[...]
    m_new = jnp.maximum(m, jnp.max(s, axis=-1, keepdims=True))
    m_safe = jnp.where(m_new == NEG_INF, 0.0, m_new)
    p = jnp.exp(s - m_safe)
    alpha = jnp.exp(m - m_safe)
    l_scr[...] = alpha * l_scr[...] + jnp.sum(p, axis=-1, keepdims=True)
    acc_scr[...] = alpha * acc_scr[...] + pv(p.astype(BF16))
    m_scr[...] = m_new


def _cat16(refs, axis):
    parts = [r[...].astype(BF16) for r in refs]
    return parts[0] if len(parts) == 1 else jnp.concatenate(parts, axis=axis)


SAMPLE_PAGE_GROUP = 8
SMALL_PAGE_GROUP = 8

def _heads_first(o, b, n_heads, tn):
    return jnp.swapaxes(o.reshape(b, n_heads, tn, HEAD_DIM), 1, 2).reshape(b, tn, n_heads * HEAD_DIM)


def _diff_sample_body(pt_ref, wq_ref, bias_ref, lam_ref, g_ref, kn_ref, vn_ref, *rest,
                      n_pages, group, page, n_new, n_heads, lam_init):
    k_refs, v_refs = rest[:group], rest[group:2 * group]
    o_ref, m_scr, l_scr, acc_scr = rest[2 * group:]
    p = pl.program_id(1)
    rows_h = 2 * n_new

    @pl.when(p == 0)
    def _():
        _flash_init(m_scr, l_scr, acc_scr)

    def step(ks, vs, bias):
        s = jnp.dot(wq_ref[...], _cat16(ks, 1), preferred_element_type=F32) * DH_A ** -0.5 + bias

        def pv(p16):
            outs = []
            for h in range(n_heads):
                vh = [v[pl.ds(h, page, stride=n_heads), :].astype(BF16) for v in vs]
                vh = vh[0] if len(vh) == 1 else jnp.concatenate(vh, axis=0)
                outs.append(jnp.dot(p16[h * rows_h:(h + 1) * rows_h], vh, preferred_element_type=F32))
            return jnp.concatenate(outs, axis=0)

        _flash_update(s, pv, m_scr, l_scr, acc_scr)

    @pl.when(p < n_pages // group)
    def _():
        tiles = [bias_ref[_past_bias_index(p * group + g, n_pages)] for g in range(group)]
        step(k_refs, v_refs, jnp.concatenate(tiles, axis=1))

    @pl.when(p == n_pages // group)
    def _():
        step([kn_ref], [vn_ref], bias_ref[2])
        lam = _diff_lambda(lam_ref, lam_init)
        on = acc_scr[...] / l_scr[...]
        for h in range(n_heads):
            r0 = h * rows_h
            o = on[r0:r0 + n_new] - lam * on[r0 + n_new:r0 + rows_h]
            o_ref[h * n_new:(h + 1) * n_new, :] = _rms(o, g_ref[...]) * (1.0 - lam_init)


def _diff_sample(pt, layer, q, cache_kt, cache_v, k_new, v_new, tab, lam_vec, g_subln, lam_init):
    b, tn, n_heads = q.shape[:3]
    n_pages = pt.shape[1]
    width, page = cache_kt.shape[2:]
    group = min(SAMPLE_PAGE_GROUP, n_pages)
    assert n_pages % group == 0
    wq = _block_diag_rows(q.reshape(b, tn, 2 * n_heads, DH_A), BF16)
    rows = 2 * n_heads * tn
    bias = _sample_bias(tab, tn, page)
    bias = jnp.broadcast_to(bias[:, :, None], (3, n_heads, 2, tn, page)).reshape(3, rows, page)
    out = _paged_call(
        functools.partial(_diff_sample_body, n_pages=n_pages, group=group, page=page, n_new=tn, n_heads=n_heads,
                          lam_init=lam_init),
        pt, n_pages // group + 1,
        (wq, bias, lam_vec, g_subln.reshape(1, HEAD_DIM), _transposed_page(k_new, page), _rows_page(v_new, page))
        + (cache_kt,) * group + (cache_v,) * group,
        [_per_batch(rows, width), _whole(bias.shape), _whole(lam_vec.shape), _whole((1, HEAD_DIM)),
         _per_batch(width, page), _per_batch(page * n_heads, HEAD_DIM)]
        + _page_specs(layer, n_pages, group, width, page) + _page_specs(layer, n_pages, group, page * n_heads, HEAD_DIM),
        jax.ShapeDtypeStruct((b, n_heads * tn, HEAD_DIM), F32), _per_batch(n_heads * tn, HEAD_DIM),
        _flash_scratch(rows, HEAD_DIM), "diff_sample")
    return _heads_first(out, b, n_heads, tn)


def _fox_sample_body(pt_ref, q_ref, cq_ref, ckp_ref, ckn_ref, hm_ref, nm_ref, kn_ref, vn_ref, *rest,
                     n_pages, group):
    k_refs, v_refs = rest[:group], rest[group:2 * group]
    o_ref, m_scr, l_scr, acc_scr = rest[2 * group:]
    p = pl.program_id(1)

    @pl.when(p == 0)
    def _():
        _flash_init(m_scr, l_scr, acc_scr)

    def step(ks, vs, ck, mask):
        s = lax.dot_general(q_ref[...], _cat16(ks, 0), NT_DIMS, preferred_element_type=F32)
        s = s * HEAD_DIM ** -0.5 + (cq_ref[...] - ck) + mask
        _flash_update(s, lambda p16: jnp.dot(p16, _cat16(vs, 0), preferred_element_type=F32), m_scr, l_scr, acc_scr)

    @pl.when(p < n_pages // group)
    def _():
        step(k_refs, v_refs, ckp_ref[...], hm_ref[...])

    @pl.when(p == n_pages // group)
    def _():
        step([kn_ref], [vn_ref], ckn_ref[...], nm_ref[...])
        o_ref[...] = acc_scr[...] / l_scr[...]


def _fox_sample(pt, layer, q, cache_k, cache_v, k_new, v_new, cum_q, cum_past, cum_new):
    b, tn, n_heads = q.shape[:3]
    n_pages = pt.shape[1]
    prow = cache_k.shape[2]
    page = prow // n_heads
    group = min(SAMPLE_PAGE_GROUP, n_pages)
    assert n_pages % group == 0
    rows = n_heads * tn
    q_rows = jnp.swapaxes(q, 1, 2).reshape(b, rows, HEAD_DIM).astype(BF16)
    cq = jnp.swapaxes(cum_q, 1, 2).reshape(b, rows, 1)
    ck_past = cum_past.reshape(b, n_pages // group, 1, group * prow)
    ck_new = jnp.pad(cum_new, ((0, 0), (0, page - tn), (0, 0))).reshape(b, 1, prow)
    head_mask = _expand_heads(jnp.zeros((n_heads, tn, page), F32))
    qi = np.arange(tn)[:, None]
    ci = np.arange(page)[None, :]
    causal = np.broadcast_to(np.where(ci <= qi, 0.0, NEG_INF).astype(np.float32), (n_heads, tn, page))
    new_mask = _expand_heads(jnp.asarray(causal))
    out = _paged_call(
        functools.partial(_fox_sample_body, n_pages=n_pages, group=group),
        pt, n_pages // group + 1,
        (q_rows, cq, ck_past, ck_new, jnp.tile(head_mask, (1, group)), new_mask,
         _rows_page(k_new, page), _rows_page(v_new, page)) + (cache_k,) * group + (cache_v,) * group,
        [_per_batch(rows, HEAD_DIM), _per_batch(rows, 1),
         pl.BlockSpec((None, None, 1, group * prow),
                      lambda bi, p, pt_: (bi, jnp.minimum(p, n_pages // group - 1), 0, 0)),
         _per_batch(1, prow), _whole((rows, group * prow)), _whole((rows, prow)),
         _per_batch(prow, HEAD_DIM), _per_batch(prow, HEAD_DIM)]
        + _page_specs(layer, n_pages, group, prow, HEAD_DIM) + _page_specs(layer, n_pages, group, prow, HEAD_DIM),
        jax.ShapeDtypeStruct((b, rows, HEAD_DIM), F32), _per_batch(rows, HEAD_DIM),
        _flash_scratch(rows, HEAD_DIM), "fox_sample")
    return _heads_first(out, b, n_heads, tn)


def _moba_blocks_body(pt_ref, q_ref, bias_ref, *rest, n_pages, group, pages_per_block, n_heads):
    k_refs, v_refs = rest[:group], rest[group:2 * group]
    acc_ref, stat_ref, kmean_ref = rest[2 * group:]
    p = pl.program_id(1)
    lane = lax.broadcasted_iota(jnp.int32, stat_ref.shape[1:], 1)
    for blk in range(group // pages_per_block):
        pages = range(blk * pages_per_block, (blk + 1) * pages_per_block)
        parts, total = [], None
        for g in pages:
            k = k_refs[g][...]
            s = lax.dot_general(q_ref[...], k.astype(BF16), NT_DIMS, preferred_element_type=F32) * HEAD_DIM ** -0.5
            parts.append(s + bias_ref[_past_bias_index(p * group + g, n_pages)])
            ksum = jnp.sum(k.reshape(k.shape[0] // n_heads, n_heads, HEAD_DIM), axis=0)
            total = ksum if total is None else total + ksum
        s = parts[0] if len(parts) == 1 else jnp.concatenate(parts, axis=1)
        m = jnp.max(s, axis=-1, keepdims=True)
        prob = jnp.exp(s - m)
        acc_ref[blk] = jnp.dot(prob.astype(BF16), _cat16([v_refs[g] for g in pages], 0), preferred_element_type=F32)
        stat_ref[blk] = jnp.where(lane < LANES // 2, m, jnp.sum(prob, axis=-1, keepdims=True))
        kmean_ref[blk] = total * (1.0 / MOBA_BLOCK)


def _moba_merge_body(wq_ref, kmean_ref, acc_ref, stat_ref, q_ref, kn_ref, vn_ref, bias_ref, o_ref, *, n_blk):
    gate = lax.dot_general(wq_ref[...], kmean_ref[...], NT_DIMS, preferred_element_type=F32,
                           precision=lax.Precision.HIGHEST)
    sel = _top_blocks_negmask(gate, min(MOBA_TOPK, n_blk), float(n_blk))
    lane = lax.broadcasted_iota(jnp.int32, sel.shape, 1)
    s = lax.dot_general(q_ref[...], kn_ref[...].astype(BF16), NT_DIMS, preferred_element_type=F32)
    s = s * HEAD_DIM ** -0.5 + bias_ref[...]
    m_own = jnp.max(s, axis=-1, keepdims=True)
    prob = jnp.exp(s - m_own)

    def block_max(blk):
        return stat_ref[blk][:, 0:1] + jnp.min(jnp.where(lane == blk, sel, 0.0), axis=-1, keepdims=True)

    unroll = math.gcd(n_blk, 8)

    def max_step(g, m):
        return functools.reduce(jnp.maximum, [block_max(g * unroll + u) for u in range(unroll)], m)

    m_all = lax.fori_loop(0, n_blk // unroll, max_step, m_own)

    def merge(g, carry):
        num, den = carry
        for u in range(unroll):
            blk = g * unroll + u
            w = jnp.exp(block_max(blk) - m_all)
            num = num + w * acc_ref[blk]
            den = den + w * stat_ref[blk][:, LANES // 2:LANES // 2 + 1]
        return num, den

    w_own = jnp.exp(m_own - m_all)
    num, den = lax.fori_loop(
        0, n_blk // unroll, merge,
        (w_own * jnp.dot(prob.astype(BF16), vn_ref[...].astype(BF16), preferred_element_type=F32),
         w_own * jnp.sum(prob, axis=-1, keepdims=True)))
    o_ref[...] = num / den


def _moba_sample(pt, layer, q, cache_k, cache_v, k_new, v_new, tab):
    b, tn, n_heads = q.shape[:3]
    n_pages = pt.shape[1]
    prow = cache_k.shape[2]
    page = prow // n_heads
    width = n_heads * HEAD_DIM
    assert MOBA_BLOCK % page == 0 and (n_pages * page) % MOBA_BLOCK == 0 and tn < MOBA_BLOCK
    ppb = MOBA_BLOCK // page
    n_blk = n_pages // ppb
    group = min(SAMPLE_PAGE_GROUP, n_pages)
    assert n_pages % group == 0 and group % ppb == 0
    bps = group // ppb
    rows = n_heads * tn
    bias = _sample_bias(tab, tn, page)
    bias = jnp.stack([_expand_heads(bias[i]) for i in range(3)])
    q_rows = jnp.swapaxes(q, 1, 2).reshape(b, rows, HEAD_DIM).astype(BF16)
    per_block = lambda r: pl.BlockSpec((None, bps, r, HEAD_DIM), lambda bi, p, pt_: (bi, p, 0, 0))
    acc, stat, kmean = _paged_call(
        functools.partial(_moba_blocks_body, n_pages=n_pages, group=group, pages_per_block=ppb, n_heads=n_heads),
        pt, n_pages // group,
        (q_rows, bias[:2]) + (cache_k,) * group + (cache_v,) * group,
        [_per_batch(rows, HEAD_DIM), _whole((2,) + bias.shape[1:])]
        + _page_specs(layer, n_pages, group, prow, HEAD_DIM) + _page_specs(layer, n_pages, group, prow, HEAD_DIM),
        [jax.ShapeDtypeStruct((b, n_blk, rows, HEAD_DIM), F32), jax.ShapeDtypeStruct((b, n_blk, rows, LANES), F32),
         jax.ShapeDtypeStruct((b, n_blk, n_heads, HEAD_DIM), F32)],
        [per_block(rows), per_block(rows), per_block(n_heads)], [], "moba_blocks")
    whole_b = lambda *shape: pl.BlockSpec((None,) + shape, lambda bi: (bi,) + (0,) * len(shape))
    out = pl.pallas_call(
        functools.partial(_moba_merge_body, n_blk=n_blk),
        grid=(b,),
        in_specs=[whole_b(rows, width), whole_b(n_blk, width), whole_b(n_blk, rows, HEAD_DIM),
                  whole_b(n_blk, rows, LANES), whole_b(rows, HEAD_DIM), whole_b(prow, HEAD_DIM),
                  whole_b(prow, HEAD_DIM), pl.BlockSpec((rows, prow), lambda bi: (0, 0))],
        out_specs=whole_b(rows, HEAD_DIM),
        out_shape=jax.ShapeDtypeStruct((b, rows, HEAD_DIM), F32),
        compiler_params=_params("parallel"),
        name="moba_merge",
    )(_block_diag_rows(q, F32), kmean.reshape(b, n_blk, width), acc, stat, q_rows,
      _rows_page(k_new, page), _rows_page(v_new, page), bias[2])
    return _heads_first(out, b, n_heads, tn)


def _dsa_score_body(pt_ref, iq_ref, w_ref, mask_ref, kn_ref, *rest, n_pages, group, n_new):
    k_refs, o_ref = rest[:group], rest[group]
    p = pl.program_id(1)

    def score(kt16):
        rel = jnp.dot(iq_ref[...], kt16, preferred_element_type=F32)
        rel = jnp.maximum(rel * IDX_DIM ** -0.5, 0.0) * w_ref[...]
        return jnp.sum(rel.reshape(N_IDX_HEADS, n_new, rel.shape[1]), axis=0)

    @pl.when(p < n_pages // group)
    def _():
        o_ref[...] = score(_cat16(k_refs, 1))

    @pl.when(p == n_pages // group)
    def _():
        page = kn_ref.shape[1]
        o_ref[...] = jnp.full(o_ref.shape, NEG_INF, F32)
        o_ref[:, :page] = score(kn_ref[...].astype(BF16)) + mask_ref[...]


def _dsa_select_body(sc_ref, o_ref, key_scr, *, n_new, n_sel, chunk):
    width = sc_ref.shape[1]
    key_scr[...] = _order_key(sc_ref[...])

    def count(pred):
        return jnp.sum(jnp.where(pred(key_scr[...]), 1.0, 0.0), axis=-1, keepdims=True)

    thr = _kth_largest_key(lambda ts: [count(lambda k, t=t: k >= t) for t in ts], n_new, n_sel)
    cnt_ge = count(lambda k: k >= thr)
    tie = jnp.max(jnp.where((cnt_ge > n_sel) & (thr > KEY_OF_NEG_INF), 1.0, 0.0)) > 0.0

    @pl.when(jnp.logical_not(tie))
    def _():
        o_ref[...] = jnp.where(key_scr[...] >= thr, 0.0, NEG_INF)

    @pl.when(tie)
    def _():
        allow = n_sel - count(lambda k: k > thr)
        r = lax.broadcasted_iota(jnp.int32, (chunk, chunk), 0)
        c = lax.broadcasted_iota(jnp.int32, (chunk, chunk), 1)
        tri = jnp.where(r <= c, 1.0, 0.0).astype(BF16)

        def body(j, before):
            cols = pl.ds(pl.multiple_of(j * chunk, chunk), chunk)
            k = key_scr[:, cols]
            eq = jnp.where(k == thr, 1.0, 0.0)
            rank = jnp.dot(eq.astype(BF16), tri, preferred_element_type=F32) + before
            keep = jnp.where(k > thr, 1.0, jnp.where(rank <= allow, eq, 0.0))
            o_ref[:, cols] = jnp.where(keep > 0.0, 0.0, NEG_INF)
            return before + jnp.sum(eq, axis=-1, keepdims=True)
        lax.fori_loop(0, width // chunk, body, jnp.zeros((n_new, 1), F32))


def _dsa_sample_body(pt_ref, q_ref, bias_ref, nm_ref, kn_ref, vn_ref, *rest, n_pages, group, n_heads):
    k_refs, v_refs = rest[:group], rest[group:2 * group]
    o_ref, m_scr, l_scr, acc_scr = rest[2 * group:]
    p = pl.program_id(1)

    @pl.when(p == 0)
    def _():
        _flash_init(m_scr, l_scr, acc_scr)

    def step(ks, vs, bias, nm):
        s = lax.dot_general(q_ref[...], _cat16(ks, 0), NT_DIMS, preferred_element_type=F32)
        s = s * HEAD_DIM ** -0.5 + bias + jnp.tile(nm, (n_heads, 1))
        _flash_update(s, lambda p16: jnp.dot(p16, _cat16(vs, 0), preferred_element_type=F32), m_scr, l_scr, acc_scr)

    @pl.when(p < n_pages // group)
    def _():
        tiles = [bias_ref[_past_bias_index(p * group + g, n_pages)] for g in range(group)]
        step(k_refs, v_refs, jnp.concatenate(tiles, axis=1), nm_ref[...])

    @pl.when(p == n_pages // group)
    def _():
        page = kn_ref.shape[0]
        step([kn_ref], [vn_ref], bias_ref[2], nm_ref[:, :page])
        o_ref[...] = acc_scr[...] / l_scr[...]


def _dsa_sample(pt, layer, qb, iq, iw, cache_k, cache_v, cache_idx_t, k_new, v_new, ik_new, tab):
    b, tn, n_heads = qb.shape[:3]
    n_pages = pt.shape[1]
    page = cache_k.shape[2]
    n_sel = min(DSA_TOPK, (n_pages * page + tn) // 4)
    group = min(SMALL_PAGE_GROUP, n_pages)
    assert n_pages % group == 0
    n_steps = n_pages // group + 1
    width = n_steps * group * page
    pad = ((0, 0), (0, page - tn), (0, 0))
    qi = np.arange(tn)[:, None]
    ci = np.arange(page)[None, :]
    new_mask = jnp.asarray(np.where(ci <= qi, 0.0, NEG_INF).astype(np.float32))
    n_iq = N_IDX_HEADS * tn
    iq_rows = jnp.swapaxes(iq, 1, 2).reshape(b, n_iq, IDX_DIM).astype(BF16)
    w_rows = (jnp.swapaxes(iw, 1, 2).astype(F32) * N_IDX_HEADS ** -0.5).reshape(b, n_iq, 1)
    step_cols = pl.BlockSpec((None, tn, group * page), lambda bi, p, pt_: (bi, 0, p))
    scores = _paged_call(
        functools.partial(_dsa_score_body, n_pages=n_pages, group=group, n_new=tn),
        pt, n_steps,
        (iq_rows, w_rows, new_mask, _transposed_page(ik_new, page)) + (cache_idx_t,) * group,
        [_per_batch(n_iq, IDX_DIM), _per_batch(n_iq, 1), _whole(new_mask.shape), _per_batch(IDX_DIM, page)]
        + _page_specs(layer, n_pages, group, IDX_DIM, page),
        jax.ShapeDtypeStruct((b, tn, width), F32), step_cols, [], "dsa_score")
    negmask = pl.pallas_call(
        functools.partial(_dsa_select_body, n_new=b * tn, n_sel=n_sel, chunk=page),
        grid=(1,),
        in_specs=[pl.BlockSpec((b * tn, width), lambda i: (0, 0))],
        out_specs=pl.BlockSpec((b * tn, width), lambda i: (0, 0)),
        out_shape=jax.ShapeDtypeStruct((b * tn, width), F32),
        scratch_shapes=[pltpu.VMEM((b * tn, width), jnp.int32)],
        compiler_params=_params("arbitrary"),
        name="dsa_select",
    )(scores.reshape(b * tn, width)).reshape(b, tn, width)
    rows = n_heads * tn
    q_rows = jnp.swapaxes(qb, 1, 2).reshape(b, rows, HEAD_DIM).astype(BF16)
    bias = _sample_bias(tab, tn, page).reshape(3, rows, page)
    out = _paged_call(
        functools.partial(_dsa_sample_body, n_pages=n_pages, group=group, n_heads=n_heads),
        pt, n_steps,
        (q_rows, bias, negmask, jnp.pad(k_new, pad), jnp.pad(v_new, pad)) + (cache_k,) * group + (cache_v,) * group,
        [_per_batch(rows, HEAD_DIM), _whole(bias.shape), step_cols, _per_batch(page, HEAD_DIM),
         _per_batch(page, HEAD_DIM)]
        + _page_specs(layer, n_pages, group, page, HEAD_DIM) + _page_specs(layer, n_pages, group, page, HEAD_DIM),
        jax.ShapeDtypeStruct((b, rows, HEAD_DIM), F32), _per_batch(rows, HEAD_DIM),
        _flash_scratch(rows, HEAD_DIM), "dsa_sample")
    return _heads_first(out, b, n_heads, tn)


def _pad_rows(w, n):
    return jnp.pad(w, ((0, 0),) * (w.ndim - 2) + ((0, n - w.shape[-2]), (0, 0)))


def _mixer_even(h_in, g, w_in16, past, pt, layer_e, lam_vec, g_subln, lam_init, tab, b, t):
    d = h_in.shape[1]
    n_a = n_b = (d // HEAD_DIM) // 2
    wa, wb = n_a * HEAD_DIM, n_b * HEAD_DIM
    wi = N_IDX_HEADS * IDX_DIM
    c_qa, c_ka, c_va, c_qb = 0, wa, 2 * wa, 3 * wa
    c_iq = c_qb + wb
    c_small = c_iq + wi
    c_kb, c_vb, c_ik = c_small, c_small + HEAD_DIM, c_small + 2 * HEAD_DIM
    tn = COL_TILE
    assert wa % tn == 0 and c_small % tn == 0 and w_in16.shape[-2] == c_small + tn
    ka_t = past is None and t % min(b * t, 2 * ROW_TILE) == 0
    ka, va, small, p16 = _rms_matmul(
        h_in, g, w_in16, (layer_e,), emit16=True, w_transposed=True, rows_per_batch=t,
        f32_groups=((c_ka // tn, wa // tn, ka_t), (c_va // tn, wa // tn), (c_small // tn, 1)))
    p16 = p16.reshape(b, t, -1)
    va, small = va.reshape(b, t, wa), small.reshape(b, t, tn)
    if ka_t:
        a_k = jnp.transpose(ka.reshape(b, n_a, 2, DH_A, t), (0, 4, 1, 2, 3))
    else:
        ka = ka.reshape(b, t, wa)
        a_k = ka.reshape(b, t, n_a, 2, DH_A)
    kb = small[..., :HEAD_DIM]
    vb = small[..., HEAD_DIM:2 * HEAD_DIM]
    ik = small[..., 2 * HEAD_DIM:2 * HEAD_DIM + IDX_DIM]
    rows = (a_k, va.reshape(b, t, n_a, HEAD_DIM), kb, vb, ik)
    tab_a, tab_b = tab[:, :n_a], tab[:, n_a:]
    if past is None:
        o_a = _diff_prompt(p16, c_qa, c_ka, c_va, n_a, tab_a, lam_vec, g_subln, lam_init)
        iq_t = jnp.swapaxes(p16[..., c_iq:c_small].reshape(b, t, N_IDX_HEADS, IDX_DIM), 1, 2)
        o_b = _dsa_prompt(p16, small, iq_t, c_qb, c_kb, c_vb, c_ik, 2 * HEAD_DIM, n_b, tab_b,
                          min(DSA_TOPK, t // 4))
    else:
        cache_a_k, cache_a_v, cache_b_k, cache_b_v, cache_b_idx = past
        qa = p16[..., c_qa:c_qa + wa].reshape(b, t, n_a, 2, DH_A)
        o_a = _diff_sample(pt, layer_e, qa, cache_a_k, cache_a_v, ka, va, tab_a, lam_vec, g_subln, lam_init)
        qb = p16[..., c_qb:c_qb + wb].reshape(b, t, n_b, HEAD_DIM)
        iq = p16[..., c_iq:c_small].reshape(b, t, N_IDX_HEADS, IDX_DIM)
        iw = small[..., 2 * HEAD_DIM + IDX_DIM:2 * HEAD_DIM + IDX_DIM + N_IDX_HEADS]
        o_b = _dsa_sample(pt, layer_e, qb, iq, iw, cache_b_k, cache_b_v, cache_b_idx, kb, vb, ik, tab_b)
    return (o_a.reshape(b * t, wa), o_b.reshape(b * t, wb)), rows


def _mixer_odd(h_in, g, w_in16, past, pt, layer_o, b_forget, tab, b, t):
    d = h_in.shape[1]
    n_c = n_d = (d // HEAD_DIM) // 2
    wc, wd = n_c * HEAD_DIM, n_d * HEAD_DIM
    c_qc, c_kc, c_vc, c_qd = 0, wc, 2 * wc, 3 * wc
    c_kd = c_qd + wd
    c_vd = c_kd + wd
    c_fc = c_vd + wd
    tn = COL_TILE
    assert wc % tn == 0 and wd % tn == 0 and w_in16.shape[-2] == c_fc + tn
    groups = tuple((c // tn, wc // tn) for c in (c_kc, c_vc, c_qd, c_kd, c_vd)) + ((c_fc // tn, 1),)
    kc, vc, qd32, kd, vd, small, p16 = _rms_matmul(h_in, g, w_in16, (layer_o,), emit16=True, w_transposed=True,
                                                   f32_groups=groups)
    p16 = p16.reshape(b, t, -1)
    kc, vc, qd32, kd, vd = (a.reshape(b, t, wc) for a in (kc, vc, qd32, kd, vd))
    log_f = jax.nn.log_sigmoid(small.reshape(b, t, tn)[..., :n_c] + b_forget.astype(F32))
    rows = (kc.reshape(b, t, n_c, HEAD_DIM), vc.reshape(b, t, n_c, HEAD_DIM), log_f,
            kd.reshape(b, t, n_d, HEAD_DIM), vd.reshape(b, t, n_d, HEAD_DIM))
    tab_d = tab[:, n_c:]
    if past is None:
        o_c = _fox_prompt(p16, c_qc, c_kc, c_vc, n_c, jnp.cumsum(log_f, axis=1))
        o_d = _moba_prompt(p16, qd32, kd, c_qd, c_kd, c_vd, n_d, tab_d)
    else:
        cache_c_k, cache_c_v, cache_c_logf, cache_d_k, cache_d_v = past
        n_pages = pt.shape[1]
        page = cache_c_logf.shape[2]
        logf_past = cache_c_logf[layer_o][pt].reshape(b, n_pages * page, n_c)
        cum = jnp.cumsum(jnp.concatenate([logf_past, log_f], axis=1).astype(F32), axis=1)
        cum_q = cum[:, n_pages * page:]
        qc = p16[..., c_qc:c_qc + wc].reshape(b, t, n_c, HEAD_DIM)
        o_c = _fox_sample(pt, layer_o, qc, cache_c_k, cache_c_v, kc, vc, cum_q, cum[:, :n_pages * page], cum_q)
        o_d = _moba_sample(pt, layer_o, qd32.reshape(b, t, n_d, HEAD_DIM), cache_d_k, cache_d_v, kd, vd, tab_d)
    return (o_c.reshape(b * t, wc), o_d.reshape(b * t, wd)), rows


def _run_trunk(x, past_even, past_odd, pt, mem_kv, prm):
    b, t, d = x.shape
    x = x.reshape(b * t, d)
    depth = prm['norm_g'].shape[0]
    rows_even, rows_odd = [], []
    for layer in range(depth):
        g = prm['norm_g'][layer]
        wg, wu, wd = prm['w_ffn_gate'], prm['w_ffn_up'], prm['w_ffn_down']
        x = _ffn(x, g[NG_FFN1_PRE], g[NG_FFN1_POST], wg, wu, wd, (layer, 0))
        if layer % 2 == 0:
            e = layer // 2
            lam_init = 0.8 - 0.6 * math.exp(-0.3 * layer)
            parts, rows = _mixer_even(x, g[NG_MIX_PRE], prm['w_in_even'], past_even, pt, e,
                                      prm['diff_lambda'][e].astype(F32), prm['g_subln'][e].astype(F32), lam_init,
                                      prm['t5_table'], b, t)
            rows_even.append(rows)
            x = _out_proj(x, g[NG_MIX_POST], parts, prm['w_out_even'], (e,))
        else:
            o = layer // 2
            parts, rows = _mixer_odd(x, g[NG_MIX_PRE], prm['w_in_odd'], past_odd, pt, o,
                                     prm['b_forget'][o], prm['t5_table'], b, t)
            rows_odd.append(rows)
            x = _out_proj(x, g[NG_MIX_POST], parts, prm['w_out_odd'], (o,))
        mk, mv = mem_kv[layer]
        q = _rms_matmul(x, g[NG_X_PRE], prm['w_xq'], (layer,))
        o_x = _cross_attend(q.reshape(b, t, -1), mk, mv)
        x = _out_proj(x, g[NG_X_POST], [o_x.reshape(b * t, -1)], prm['w_xo'], (layer,))
        x = _ffn(x, g[NG_FFN2_PRE], g[NG_FFN2_POST], wg, wu, wd, (layer, 1))
    return x.reshape(b, t, d), rows_even, rows_odd


def kernel(x_prompt, x_sample, cache_a_k, cache_a_v, cache_b_k, cache_b_v, cache_b_idx, cache_c_k, cache_c_v, cache_c_logf, cache_d_k, cache_d_v, cache_mem_k, cache_mem_v, page_table, mem_prompt, t5_table, norm_g, w_ffn_gate, w_ffn_up, w_ffn_down, w_xq, w_xk, w_xv, w_xo, w_in_even, w_out_even, diff_lambda, g_subln, w_in_odd, w_out_odd, b_forget):
    depth = norm_g.shape[0]
    d_model = x_prompt.shape[-1]
    n_c = (d_model // HEAD_DIM) // 2
    half = n_c * HEAD_DIM
    wi = N_IDX_HEADS * IDX_DIM
    e_cut = 4 * half
    w_e = jnp.swapaxes(w_in_even, 1, 2)
    w_in_even_r = jnp.concatenate([w_e[:, :e_cut], w_e[:, e_cut + 2 * HEAD_DIM:e_cut + 2 * HEAD_DIM + wi],
                                   w_e[:, e_cut:e_cut + 2 * HEAD_DIM], w_e[:, e_cut + 2 * HEAD_DIM + wi:]], axis=1)
    c_fc = 3 * half
    w_o = jnp.swapaxes(w_in_odd, 1, 2)
    w_in_odd_r = jnp.concatenate([w_o[:, :c_fc], w_o[:, c_fc + n_c:], w_o[:, c_fc:c_fc + n_c]], axis=1)
    prm = {
        't5_table': t5_table.astype(F32), 'norm_g': norm_g.astype(F32),
        'w_ffn_gate': w_ffn_gate.astype(BF16), 'w_ffn_up': w_ffn_up.astype(BF16), 'w_ffn_down': w_ffn_down.astype(BF16),
        'w_xq': w_xq.astype(BF16), 'w_xo': w_xo.astype(BF16),
        'w_in_even': _pad_rows(w_in_even_r, e_cut + wi + COL_TILE).astype(BF16),
        'w_out_even': w_out_even.astype(BF16),
        'w_in_odd': _pad_rows(w_in_odd_r, 6 * half + COL_TILE).astype(BF16),
        'w_out_odd': w_out_odd.astype(BF16),
        'diff_lambda': diff_lambda, 'g_subln': g_subln, 'b_forget': b_forget,
    }
    b_p, n_mem, _ = mem_prompt.shape
    hx_w = w_xk.shape[-1]

    mem_kv_p, mem_k_out, mem_v_out = [], [], []
    for l in range(depth):
        w_kv = jnp.concatenate([w_xk[l], w_xv[l]], axis=-1).astype(BF16)
        kv = _rms_matmul(mem_prompt.reshape(b_p * n_mem, d_model), norm_g[l, NG_MEM].astype(F32), w_kv)
        mk = kv[:, :hx_w].reshape(b_p, n_mem, hx_w)
        mv = kv[:, hx_w:].reshape(b_p, n_mem, hx_w)
        mem_kv_p.append((mk, mv))
        mem_k_out.append(mk.reshape(b_p, n_mem, hx_w // HEAD_DIM, HEAD_DIM))
        mem_v_out.append(mv.reshape(b_p, n_mem, hx_w // HEAD_DIM, HEAD_DIM))
    y_prompt, ev_p, od_p = _run_trunk(x_prompt.astype(F32), None, None, None, mem_kv_p, prm)

    def rows_pages(c):
        return c.reshape(c.shape[:2] + (c.shape[2] * c.shape[3], c.shape[4]))

    def transposed_pages(c):
        c = c.reshape(c.shape[:3] + (-1,))
        return jnp.swapaxes(c, 2, 3)

    past_even = (transposed_pages(cache_a_k), rows_pages(cache_a_v), cache_b_k, cache_b_v,
                 transposed_pages(cache_b_idx))
    past_odd = (rows_pages(cache_c_k), rows_pages(cache_c_v), cache_c_logf, rows_pages(cache_d_k),
                rows_pages(cache_d_v))
    b_s = x_sample.shape[0]
    mem_kv_s = [(cache_mem_k[l].reshape(b_s, n_mem, hx_w), cache_mem_v[l].reshape(b_s, n_mem, hx_w))
                for l in range(depth)]
    y_sample, ev_s, od_s = _run_trunk(x_sample.astype(F32), past_even, past_odd, page_table.astype(jnp.int32),
                                      mem_kv_s, prm)

    def stack(rows, i):
        return jnp.stack([r[i] for r in rows])

    out = [y_prompt, y_sample]
    out += [stack(ev_p, i) for i in range(5)] + [stack(od_p, i) for i in range(5)]
    out += [jnp.stack(mem_k_out), jnp.stack(mem_v_out)]
    out += [stack(ev_s, i) for i in range(5)] + [stack(od_s, i) for i in range(5)]
    return tuple(out)
```

```python
import functools
import math

import numpy as np
import jax
import jax.numpy as jnp
from jax import lax
from jax.experimental import pallas as pl
from jax.experimental.pallas import tpu as pltpu

F32 = jnp.float32
BF16 = jnp.bfloat16
NEG_INF = float("-inf")

HEAD_DIM = 128
DH_A = HEAD_DIM // 2
N_IDX_HEADS = 16
IDX_DIM = 64
DSA_TOPK = 256
MOBA_BLOCK = 256
MOBA_TOPK = 3
N_BUCKETS = 32
T5_MAX_EXACT = N_BUCKETS // 2
T5_MAX_DIST = 128
RMS_EPS = 1e-6
NG_FFN1_PRE, NG_FFN1_POST, NG_MIX_PRE, NG_MIX_POST = 0, 1, 2, 3
NG_X_PRE, NG_X_POST, NG_FFN2_PRE, NG_FFN2_POST, NG_MEM = 4, 5, 6, 7, 8

LANES = 128
ROW_TILE = 512
COL_TILE = 512
VMEM_LIMIT = 56 * 1024 * 1024

NT_DIMS = (((1,), (1,)), ((), ()))


def _params(*sem):
    return pltpu.CompilerParams(dimension_semantics=sem, vmem_limit_bytes=VMEM_LIMIT)


def _rms(x, g):
    return x * lax.rsqrt(jnp.mean(x * x, axis=-1, keepdims=True) + RMS_EPS) * g


def _bucket_np(dist):
    n = np.maximum(dist, 0)
    n_f = np.maximum(n, 1).astype(np.float32)
    large = T5_MAX_EXACT + (np.log(n_f / np.float32(T5_MAX_EXACT)) / np.float32(math.log(T5_MAX_DIST / T5_MAX_EXACT))
                            * np.float32(N_BUCKETS - T5_MAX_EXACT)).astype(np.int32)
    return np.where(n < T5_MAX_EXACT, n, np.minimum(large, N_BUCKETS - 1)).astype(np.int32)


def _rel_bias(tab, dists):
    onehot = (_bucket_np(dists)[:, None] == np.arange(N_BUCKETS)[None, :]).astype(np.float32)
    rel = jnp.sum(jnp.asarray(onehot)[:, :, None] * tab[None].astype(F32), axis=1)
    return jnp.where(jnp.asarray(dists >= 0)[:, None], rel, NEG_INF).T


def _toeplitz(u, n_rows, n_cols):
    h = u.shape[0]
    period = n_rows + n_cols
    w = jnp.concatenate([u[:, :n_cols][:, ::-1], jnp.zeros((h, 1), u.dtype), u[:, n_cols:][:, ::-1]], axis=1)
    flat = jnp.tile(w, (1, n_rows))[:, :n_rows * (period - 1)]
    return flat.reshape(h, n_rows, period - 1)[:, :, :n_cols]


def _dist_tile(tab, n_rows, n_cols, offset):
    dists = np.arange(n_rows + n_cols - 1) - (n_cols - 1) + offset
    return _toeplitz(_rel_bias(tab, dists), n_rows, n_cols)


def _far_tile(tab, n_rows, n_cols):
    return jnp.broadcast_to(tab[N_BUCKETS - 1].astype(F32)[:, None, None], (tab.shape[1], n_rows, n_cols))


def _sample_bias(tab, n_new, page):
    assert page + 1 >= T5_MAX_DIST
    return jnp.stack([_far_tile(tab, n_new, page), _dist_tile(tab, n_new, page, page), _dist_tile(tab, n_new, page, 0)])


def _expand_heads(tile):
    h, r, c = tile.shape
    same = jnp.asarray(np.eye(h, dtype=bool))[:, None, None, :]
    return jnp.where(same, tile[:, :, :, None], NEG_INF).reshape(h * r, c * h)


def _rms_matmul_body(x_ref, g_ref, w_ref, *rest, groups, emit16, w_transposed):
    outs, h_scr = rest[:-1], rest[-1]
    j = pl.program_id(1)

    @pl.when(j == 0)
    def _():
        h_scr[...] = _rms(x_ref[...], g_ref[...]).astype(BF16)

    if w_transposed:
        y = lax.dot_general(h_scr[...], w_ref[...], NT_DIMS, preferred_element_type=F32)
    else:
        y = jnp.dot(h_scr[...], w_ref[...], preferred_element_type=F32)
    for (first, count, transposed), o_ref in zip(groups, outs):
        @pl.when((j >= first) & (j < first + count))
        def _():
            o_ref[...] = y.T if transposed else y
    if emit16:
        outs[len(groups)][...] = y.astype(BF16)


def _stacked(lead, *block):
    return (None,) * len(lead) + tuple(block), tuple(lead)


def _rms_matmul(x, g, w16, lead=(), *, f32_groups=None, emit16=False, w_transposed=False, rows_per_batch=None):
    m, d = x.shape
    n = w16.shape[-2] if w_transposed else w16.shape[-1]
    tm = min(m, 2 * ROW_TILE)
    tn = min(n, COL_TILE)
    assert m % tm == 0 and n % tn == 0
    groups = tuple(tuple(grp) + (False,) * (3 - len(grp)) for grp in (f32_groups or ((0, n // tn),)))
    out_shape, out_specs = [], []
    for first, count, transposed in groups:
        col = lambda j, first=first, count=count: jnp.clip(j - first, 0, count - 1)
        if transposed:
            assert rows_per_batch % tm == 0
            per_b = rows_per_batch // tm
            out_shape.append(jax.ShapeDtypeStruct((m // rows_per_batch, count * tn, rows_per_batch), F32))
            out_specs.append(pl.BlockSpec((None, tn, tm), lambda i, j, col=col: (i // per_b, col(j), i % per_b)))
        else:
            out_shape.append(jax.ShapeDtypeStruct((m, count * tn), F32))
            out_specs.append(pl.BlockSpec((tm, tn), lambda i, j, col=col: (i, col(j))))
    if emit16:
        out_shape.append(jax.ShapeDtypeStruct((m, n), BF16))
        out_specs.append(pl.BlockSpec((tm, tn), lambda i, j: (i, j)))
    w_block, w_lead = _stacked(lead, *((tn, d) if w_transposed else (d, tn)))
    w_index = (lambda i, j: w_lead + (j, 0)) if w_transposed else (lambda i, j: w_lead + (0, j))
    res = pl.pallas_call(
        functools.partial(_rms_matmul_body, groups=groups, emit16=emit16, w_transposed=w_transposed),
        grid=(m // tm, n // tn),
        in_specs=[pl.BlockSpec((tm, d), lambda i, j: (i, 0)),
                  pl.BlockSpec((1, d), lambda i, j: (0, 0)),
                  pl.BlockSpec(w_block, w_index)],
        out_specs=out_specs,
        out_shape=out_shape,
        scratch_shapes=[pltpu.VMEM((tm, d), BF16)],
        compiler_params=_params("parallel", "arbitrary"),
        name="rms_matmul",
    )(x, g.reshape(1, d), w16)
    return res if len(res) > 1 else res[0]


def _ffn_body(x_ref, gpre_ref, gpost_ref, wg_ref, wu_ref, wd_ref, o_ref, h_scr, acc_scr):
    j = pl.program_id(1)

    @pl.when(j == 0)
    def _():
        h_scr[...] = _rms(x_ref[...], gpre_ref[...]).astype(BF16)
        acc_scr[...] = jnp.zeros_like(acc_scr)

    h = h_scr[...]
    gate = jnp.dot(h, wg_ref[...], preferred_element_type=F32)
    up = jnp.dot(h, wu_ref[...], preferred_element_type=F32)
    act = (gate * jax.nn.sigmoid(gate) * up).astype(BF16)
    acc_scr[...] += jnp.dot(act, wd_ref[...], preferred_element_type=F32)

    @pl.when(j == pl.num_programs(1) - 1)
    def _():
        o_ref[...] = x_ref[...] + 0.5 * _rms(acc_scr[...], gpost_ref[...])


def _ffn(x, g_pre, g_post, wg16, wu16, wd16, lead):
    m, d = x.shape
    ff = wg16.shape[-1]
    tm = min(m, ROW_TILE)
    tf = min(ff, COL_TILE)
    assert m % tm == 0 and ff % tf == 0
    up_block, w_lead = _stacked(lead, d, tf)
    down_block, _ = _stacked(lead, tf, d)
    return pl.pallas_call(
        _ffn_body,
        grid=(m // tm, ff // tf),
        in_specs=[pl.BlockSpec((tm, d), lambda i, j: (i, 0)),
                  pl.BlockSpec((1, d), lambda i, j: (0, 0)),
                  pl.BlockSpec((1, d), lambda i, j: (0, 0)),
                  pl.BlockSpec(up_block, lambda i, j: w_lead + (0, j)),
                  pl.BlockSpec(up_block, lambda i, j: w_lead + (0, j)),
                  pl.BlockSpec(down_block, lambda i, j: w_lead + (j, 0))],
        out_specs=pl.BlockSpec((tm, d), lambda i, j: (i, 0)),
        out_shape=jax.ShapeDtypeStruct((m, d), F32),
        scratch_shapes=[pltpu.VMEM((tm, d), BF16), pltpu.VMEM((tm, d), F32)],
        compiler_params=_params("parallel", "arbitrary"),
        name="ffn",
    )(x, g_pre.reshape(1, d), g_post.reshape(1, d), wg16, wu16, wd16)


def _out_body(*refs, n_parts):
    x_ref, g_ref = refs[0], refs[1]
    o_refs = refs[2:2 + n_parts]
    w_refs = refs[2 + n_parts:2 + 2 * n_parts]
    out_ref = refs[-1]
    y = None
    for o_ref, w_ref in zip(o_refs, w_refs):
        t = jnp.dot(o_ref[...].astype(BF16), w_ref[...], preferred_element_type=F32)
        y = t if y is None else y + t
    out_ref[...] = x_ref[...] + _rms(y, g_ref[...])


def _out_proj(x, g, parts, w16, lead):
    m, d = x.shape
    tm = min(m, ROW_TILE)
    k = parts[0].shape[1]
    assert m % tm == 0 and all(p.shape[1] == k for p in parts) and len(parts) * k == w16.shape[-2]
    w_block, w_lead = _stacked(lead, k, d)
    in_specs = [pl.BlockSpec((tm, d), lambda i: (i, 0)), pl.BlockSpec((1, d), lambda i: (0, 0))]
    in_specs += [pl.BlockSpec((tm, k), lambda i: (i, 0)) for _ in parts]
    in_specs += [pl.BlockSpec(w_block, lambda i, n=n: w_lead + (n, 0)) for n in range(len(parts))]
    return pl.pallas_call(
        functools.partial(_out_body, n_parts=len(parts)),
        grid=(m // tm,),
        in_specs=in_specs,
        out_specs=pl.BlockSpec((tm, d), lambda i: (i, 0)),
        out_shape=jax.ShapeDtypeStruct((m, d), F32),
        compiler_params=_params("parallel"),
        name="out_proj",
    )(x, g.reshape(1, d), *parts, *([w16] * len(parts)))


def _cross_body(q_ref, k_ref, v_ref, o_ref, *, n_heads):
    q = q_ref[...].astype(BF16)
    k = k_ref[...].astype(BF16)
    v = v_ref[...].astype(BF16)
    for h in range(n_heads):
        sl = slice(h * HEAD_DIM, (h + 1) * HEAD_DIM)
        s = lax.dot_general(q[:, sl], k[:, sl], NT_DIMS, preferred_element_type=F32) * HEAD_DIM ** -0.5
        p = jnp.exp(s - jnp.max(s, axis=-1, keepdims=True))
        l = jnp.sum(p, axis=-1, keepdims=True)
        o = jnp.dot(p.astype(BF16), v[:, sl], preferred_element_type=F32) / l
        o_ref[:, sl] = o.astype(o_ref.dtype)


def _cross_attend(q, mem_k, mem_v):
    b, t, w = q.shape
    n_mem = mem_k.shape[1]
    tq = min(t, ROW_TILE)
    assert t % tq == 0
    return pl.pallas_call(
        functools.partial(_cross_body, n_heads=w // HEAD_DIM),
        grid=(b, t // tq),
        in_specs=[pl.BlockSpec((None, tq, w), lambda bi, i: (bi, i, 0)),
                  pl.BlockSpec((None, n_mem, w), lambda bi, i: (bi, 0, 0)),
                  pl.BlockSpec((None, n_mem, w), lambda bi, i: (bi, 0, 0))],
        out_specs=pl.BlockSpec((None, tq, w), lambda bi, i: (bi, i, 0)),
        out_shape=jax.ShapeDtypeStruct((b, t, w), F32),
        compiler_params=_params("parallel", "parallel"),
        name="cross_attend",
    )(q, mem_k, mem_v)


def _diff_lambda(lam_ref, lam_init):
    lv = lam_ref[...]
    return (jnp.exp(jnp.sum(lv[0:1] * lv[1:2], axis=-1, keepdims=True))
            - jnp.exp(jnp.sum(lv[2:3] * lv[3:4], axis=-1, keepdims=True)) + lam_init)


def _fold_lanes(x, op):
    parts = [x[:, c * LANES:(c + 1) * LANES] for c in range(x.shape[1] // LANES)]
    return functools.reduce(op, parts)


def _masked_softmax_pv(n_chunks, scores, values, s_scr):
    n_maps, _, rows, _ = s_scr.shape

    def first(j, mx):
        out = []
        for a, s in enumerate(scores(j)):
            s_scr[a, j] = s
            out.append(jnp.maximum(mx[a], _fold_lanes(s, jnp.maximum)))
        return tuple(out)

    mx = lax.fori_loop(0, n_chunks, first, tuple(jnp.full((rows, LANES), NEG_INF, F32) for _ in range(n_maps)))
    m = [jnp.max(x, axis=-1, keepdims=True) for x in mx]

    def second(j, carry):
        v = values(j)
        out = []
        for a in range(n_maps):
            l, acc = carry[a]
            p = jnp.exp(s_scr[a, j] - m[a])
            out.append((l + _fold_lanes(p, jnp.add), acc + jnp.dot(p.astype(BF16), v, preferred_element_type=F32)))
        return tuple(out)

    init = tuple((jnp.zeros((rows, LANES), F32), jnp.zeros((rows, HEAD_DIM), F32)) for _ in range(n_maps))
    res = lax.fori_loop(0, n_chunks, second, init)
    return [acc / jnp.sum(l, axis=-1, keepdims=True) for l, acc in res]


def _prompt_tiles(t):
    tq = min(t, 256)
    tk = min(t, 512)
    assert t % tk == 0 and tk % tq == 0
    return tq, tk


def _chunk_bias(tab, tq, tk):
    assert tq + 1 >= T5_MAX_DIST and tk % tq == 0
    n_far = tk // tq + 1
    return _dist_tile(tab, tq, tk + n_far * tq, n_far * tq)


def _bias_window(bias_ref, k, tq, tk):
    n_far = tk // tq + 1
    start = pl.multiple_of((n_far - jnp.minimum(k, n_far)) * tq, tq)
    return bias_ref[:, pl.ds(start, tk)]


def _diff_prompt_body(q_ref, k_ref, v_ref, bias_ref, lam_ref, g_ref, o_ref, s_scr, *, tq, tk, lam_init):
    i = pl.program_id(2)
    ratio = tk // tq
    q = q_ref[...]
    q0, q1 = q[:, :DH_A], q[:, DH_A:]
    scale = DH_A ** -0.5

    def scores(j):
        ks = k_ref[pl.ds(pl.multiple_of(j * tk, tk), tk), :]
        bt = _bias_window(bias_ref, i - ratio * j, tq, tk)
        return (lax.dot_general(q0, ks[:, :DH_A], NT_DIMS, preferred_element_type=F32) * scale + bt,
                lax.dot_general(q1, ks[:, DH_A:], NT_DIMS, preferred_element_type=F32) * scale + bt)

    def values(j):
        return v_ref[pl.ds(pl.multiple_of(j * tk, tk), tk), :]

    o0, o1 = _masked_softmax_pv(i // ratio + 1, scores, values, s_scr)
    o = o0 - _diff_lambda(lam_ref, lam_init) * o1
    o_ref[...] = (_rms(o, g_ref[...]) * (1.0 - lam_init)).astype(o_ref.dtype)


def _diff_prompt(p16, col_q, col_k, col_v, n_heads, tab, lam_vec, g_subln, lam_init):
    b, t, _ = p16.shape
    tq, tk = _prompt_tiles(t)
    bias = _chunk_bias(tab, tq, tk)
    cq, ck, cv = col_q // HEAD_DIM, col_k // HEAD_DIM, col_v // HEAD_DIM
    return pl.pallas_call(
        functools.partial(_diff_prompt_body, tq=tq, tk=tk, lam_init=lam_init),
        grid=(b, n_heads, t // tq),
        in_specs=[pl.BlockSpec((None, tq, HEAD_DIM), lambda bi, h, i: (bi, i, cq + h)),
                  pl.BlockSpec((None, t, HEAD_DIM), lambda bi, h, i: (bi, 0, ck + h)),
                  pl.BlockSpec((None, t, HEAD_DIM), lambda bi, h, i: (bi, 0, cv + h)),
                  pl.BlockSpec((None,) + bias.shape[1:], lambda bi, h, i: (h, 0, 0)),
                  pl.BlockSpec(lam_vec.shape, lambda bi, h, i: (0, 0)),
                  pl.BlockSpec((1, HEAD_DIM), lambda bi, h, i: (0, 0))],
        out_specs=pl.BlockSpec((None, tq, HEAD_DIM), lambda bi, h, i: (bi, i, h)),
        out_shape=jax.ShapeDtypeStruct((b, t, n_heads * HEAD_DIM), BF16),
        scratch_shapes=[pltpu.VMEM((2, t // tk, tq, tk), F32)],
        compiler_params=_params("parallel", "parallel", "arbitrary"),
        name="diff_prompt",
    )(p16, p16, p16, bias, lam_vec, g_subln.reshape(1, HEAD_DIM))


def _fox_prompt_body(q_ref, k_ref, v_ref, cum_ref, cumt_ref, o_ref, s_scr, *, tq, tk):
    h = pl.program_id(1)
    i = pl.program_id(2)
    q = q_ref[...]
    cum = cum_ref[...]
    lane = lax.broadcasted_iota(jnp.int32, cum.shape, 1)
    cq = jnp.sum(jnp.where(lane == h, cum, 0.0), axis=-1, keepdims=True)
    ahead = lax.broadcasted_iota(jnp.int32, (tq, tk), 1) - lax.broadcasted_iota(jnp.int32, (tq, tk), 0)

    def scores(j):
        start = pl.multiple_of(j * tk, tk)
        ck = cumt_ref[:, pl.ds(start, tk)]
        s = lax.dot_general(q, k_ref[pl.ds(start, tk), :], NT_DIMS, preferred_element_type=F32)
        s = s * HEAD_DIM ** -0.5 + (cq - ck)
        return (jnp.where(ahead <= i * tq - j * tk, s, NEG_INF),)

    def values(j):
        return v_ref[pl.ds(pl.multiple_of(j * tk, tk), tk), :]

    o, = _masked_softmax_pv(i // (tk // tq) + 1, scores, values, s_scr)
    o_ref[...] = o.astype(o_ref.dtype)


def _fox_prompt(p16, col_q, col_k, col_v, n_heads, cum):
    b, t, _ = p16.shape
    tq, tk = _prompt_tiles(t)
    cq, ck, cv = col_q // HEAD_DIM, col_k // HEAD_DIM, col_v // HEAD_DIM
    cum_t = jnp.swapaxes(cum, 1, 2).reshape(b, n_heads, 1, t)
    return pl.pallas_call(
        functools.partial(_fox_prompt_body, tq=tq, tk=tk),
        grid=(b, n_heads, t // tq),
        in_specs=[pl.BlockSpec((None, tq, HEAD_DIM), lambda bi, h, i: (bi, i, cq + h)),
                  pl.BlockSpec((None, t, HEAD_DIM), lambda bi, h, i: (bi, 0, ck + h)),
                  pl.BlockSpec((None, t, HEAD_DIM), lambda bi, h, i: (bi, 0, cv + h)),
                  pl.BlockSpec((None, tq, n_heads), lambda bi, h, i: (bi, i, 0)),
                  pl.BlockSpec((None, None, 1, t), lambda bi, h, i: (bi, h, 0, 0))],
        out_specs=pl.BlockSpec((None, tq, HEAD_DIM), lambda bi, h, i: (bi, i, h)),
        out_shape=jax.ShapeDtypeStruct((b, t, n_heads * HEAD_DIM), BF16),
        scratch_shapes=[pltpu.VMEM((1, t // tk, tq, tk), F32)],
        compiler_params=_params("parallel", "parallel", "arbitrary"),
        name="fox_prompt",
    )(p16, p16, p16, cum, cum_t)


def _top_blocks(gate, n_top, limit, axis):
    n_blk = gate.shape[axis]
    blk_id = lax.broadcasted_iota(jnp.int32, gate.shape, axis).astype(F32)
    chosen = jnp.zeros(gate.shape, F32)
    g = gate
    for _ in range(n_top):
        mx = jnp.max(g, axis=axis, keepdims=True)
        idx = jnp.min(jnp.where(g == mx, blk_id, float(n_blk)), axis=axis, keepdims=True)
        pick = blk_id == idx
        chosen = jnp.where(pick & (idx < limit), 1.0, chosen)
        g = jnp.where(pick, NEG_INF, g)
    return chosen


def _top_blocks_negmask(gate, n_top, limit):
    return jnp.where(_top_blocks(gate, n_top, limit, 1) > 0.0, 0.0, NEG_INF)


def _moba_prompt_body(q_ref, k_ref, v_ref, q32_ref, k32_ref, bias_ref, o_ref, keep_scr, s_scr, *, n_blk, tk):
    blk = MOBA_BLOCK
    ratio = tk // blk
    i = pl.program_id(2)

    @pl.when(i == 0)
    def _():
        kmean = jnp.mean(k32_ref[...].reshape(n_blk, blk, HEAD_DIM), axis=1)
        gate = lax.dot_general(kmean, q32_ref[...], NT_DIMS, preferred_element_type=F32,
                               precision=lax.Precision.HIGHEST)
        blk_id = lax.broadcasted_iota(jnp.int32, gate.shape, 0)
        own = lax.broadcasted_iota(jnp.int32, gate.shape, 1) // blk
        gate = jnp.where(blk_id < own, gate, NEG_INF)
        chosen = _top_blocks(gate, min(MOBA_TOPK, n_blk), own[0:1].astype(F32), 0)
        keep_scr[...] = jnp.where(blk_id == own, 1.0, chosen)

    keep_t = keep_scr[:, pl.ds(pl.multiple_of(i * blk, blk), blk)].astype(BF16)
    eye = jnp.where(lax.broadcasted_iota(jnp.int32, (blk, blk), 0) == lax.broadcasted_iota(jnp.int32, (blk, blk), 1),
                    1.0, 0.0).astype(BF16)
    sel = jnp.where(lax.dot_general(eye, keep_t, NT_DIMS, preferred_element_type=F32) > 0.5, 0.0, NEG_INF)
    lane = lax.broadcasted_iota(jnp.int32, sel.shape, 1)
    q = q_ref[...]

    def scores(j):
        ks = k_ref[pl.ds(pl.multiple_of(j * tk, tk), tk), :]
        s = lax.dot_general(q, ks, NT_DIMS, preferred_element_type=F32) * HEAD_DIM ** -0.5
        s = s + _bias_window(bias_ref, i - ratio * j, blk, tk)
        parts = []
        for c in range(ratio):
            keep = jnp.min(jnp.where(lane == j * ratio + c, sel, 0.0), axis=-1, keepdims=True)
            parts.append(s[:, c * blk:(c + 1) * blk] + keep)
        return (parts[0] if ratio == 1 else jnp.concatenate(parts, axis=1),)

    def values(j):
        return v_ref[pl.ds(pl.multiple_of(j * tk, tk), tk), :]

    o, = _masked_softmax_pv(i // ratio + 1, scores, values, s_scr)
    o_ref[...] = o.astype(o_ref.dtype)


def _moba_prompt(p16, q32, k32, col_q, col_k, col_v, n_heads, tab):
    b, t, _ = p16.shape
    blk = MOBA_BLOCK
    assert t % blk == 0
    tk = min(t, 2 * blk)
    assert t % tk == 0
    bias = _chunk_bias(tab, blk, tk)
    cq, ck, cv = col_q // HEAD_DIM, col_k // HEAD_DIM, col_v // HEAD_DIM
    return pl.pallas_call(
        functools.partial(_moba_prompt_body, n_blk=t // blk, tk=tk),
        grid=(b, n_heads, t // blk),
        in_specs=[pl.BlockSpec((None, blk, HEAD_DIM), lambda bi, h, i: (bi, i, cq + h)),
                  pl.BlockSpec((None, t, HEAD_DIM), lambda bi, h, i: (bi, 0, ck + h)),
                  pl.BlockSpec((None, t, HEAD_DIM), lambda bi, h, i: (bi, 0, cv + h)),
                  pl.BlockSpec((None, t, HEAD_DIM), lambda bi, h, i: (bi, 0, h)),
                  pl.BlockSpec((None, t, HEAD_DIM), lambda bi, h, i: (bi, 0, h)),
                  pl.BlockSpec((None,) + bias.shape[1:], lambda bi, h, i: (h, 0, 0))],
        out_specs=pl.BlockSpec((None, blk, HEAD_DIM), lambda bi, h, i: (bi, i, h)),
        out_shape=jax.ShapeDtypeStruct((b, t, n_heads * HEAD_DIM), BF16),
        scratch_shapes=[pltpu.VMEM((t // blk, t), F32), pltpu.VMEM((1, t // tk, blk, tk), F32)],
        compiler_params=_params("parallel", "parallel", "arbitrary"),
        name="moba_prompt",
    )(p16, p16, p16, q32, k32, bias)


KEY_SIGN = -2 ** 31
KEY_OF_NEG_INF = -2139095041


def _order_key(score):
    bits = pltpu.bitcast(score, jnp.int32)
    key = jnp.where(bits < 0, bits ^ 0x7FFFFFFF, bits)
    return jnp.where(score == 0.0, 0, key)


def _kth_largest_key(count_ge, n_rows, k):
    def bit_body(b, ans):
        cand = ans | jnp.left_shift(jnp.int32(1), 31 - b)
        cnt, = count_ge([cand ^ KEY_SIGN])
        return jnp.where(cnt >= k, cand, ans)

    ans = lax.fori_loop(0, 32, bit_body, jnp.zeros((n_rows, 1), jnp.int32))
    return ans ^ KEY_SIGN


def _dsa_prompt_body(iq_ref, ikw_ref, kidx_ref, qb_ref, kb_ref, vb_ref, bias_ref, o_ref, key_scr, nm_scr, w_scr,
                     s_scr, *, tq, tk, n_heads, n_sel):
    i = pl.program_id(1)
    n_chunks = i + 1
    iq = iq_ref[...].reshape(N_IDX_HEADS * tq, IDX_DIM)
    w = ikw_ref[:, IDX_DIM:IDX_DIM + N_IDX_HEADS] * N_IDX_HEADS ** -0.5 * IDX_DIM ** -0.5
    for n in range(N_IDX_HEADS):
        w_scr[n] = jnp.broadcast_to(w[:, n:n + 1], (tq, tq))
    row = lax.broadcasted_iota(jnp.int32, (tq, tq), 0)
    col = lax.broadcasted_iota(jnp.int32, (tq, tq), 1)

    def score_body(j, _):
        kc = kidx_ref[pl.ds(pl.multiple_of(j * tq, tq), tq), :][:, :IDX_DIM]
        rel = jnp.maximum(lax.dot_general(iq, kc, NT_DIMS, preferred_element_type=F32), 0.0)
        rel = rel.reshape(N_IDX_HEADS, tq, tq)
        sc = w_scr[0] * rel[0]
        for n in range(1, N_IDX_HEADS):
            sc = sc + w_scr[n] * rel[n]
        sc = jnp.where((j < i) | (col <= row), sc, NEG_INF)
        key_scr[j] = _order_key(sc)
        return 0

    lax.fori_loop(0, n_chunks, score_body, 0)

    def count_many(preds):
        def body(j, accs):
            k = key_scr[j]
            return tuple(acc + jnp.where(pred(k), 1.0, 0.0) for acc, pred in zip(accs, preds))
        accs = lax.fori_loop(0, n_chunks, body, tuple(jnp.zeros((tq, tq), F32) for _ in preds))
        return [jnp.sum(acc, axis=-1, keepdims=True) for acc in accs]

    def count(pred):
        return count_many([pred])[0]

    thr = _kth_largest_key(lambda ts: count_many([(lambda k, t=t: k >= t) for t in ts]), tq, n_sel)
    cnt_ge = count(lambda k: k >= thr)
    tie = jnp.max(jnp.where((cnt_ge > n_sel) & (thr > KEY_OF_NEG_INF), 1.0, 0.0)) > 0.0

    @pl.when(jnp.logical_not(tie))
    def _():
        def body(j, _):
            nm_scr[j] = jnp.where(key_scr[j] >= thr, 0.0, NEG_INF)
            return 0
        lax.fori_loop(0, n_chunks, body, 0)

    @pl.when(tie)
    def _():
        allow = n_sel - count(lambda k: k > thr)
        tri = jnp.where(row <= col, 1.0, 0.0).astype(BF16)

        def body(j, before):
            k = key_scr[j]
            eq = jnp.where(k == thr, 1.0, 0.0)
            rank = jnp.dot(eq.astype(BF16), tri, preferred_element_type=F32) + before
            keep = jnp.where(k > thr, 1.0, jnp.where(rank <= allow, eq, 0.0))
            nm_scr[j] = jnp.where(keep > 0.0, 0.0, NEG_INF)
            return before + jnp.sum(eq, axis=-1, keepdims=True)
        lax.fori_loop(0, n_chunks, body, jnp.zeros((tq, 1), F32))

    qb = qb_ref[...]
    qs = jnp.concatenate([qb[:, h * HEAD_DIM:(h + 1) * HEAD_DIM] for h in range(n_heads)], axis=0)

    ratio = tk // tq
    n_wide = i // ratio + 1

    def clear(j, _):
        nm_scr[j] = jnp.zeros((tq, tq), F32)
        return 0

    lax.fori_loop(n_chunks, n_wide * ratio, clear, 0)

    def scores(j):
        rows = pl.ds(pl.multiple_of(j * tk, tk), tk)
        s = lax.dot_general(qs, kb_ref[rows, :], NT_DIMS, preferred_element_type=F32) * HEAD_DIM ** -0.5
        nm = jnp.concatenate([nm_scr[j * ratio + c] for c in range(ratio)], axis=1) if ratio > 1 else nm_scr[j]
        return (s + _bias_window(bias_ref, i - ratio * j, tq, tk) + jnp.tile(nm, (n_heads, 1)),)

    def values(j):
        return vb_ref[pl.ds(pl.multiple_of(j * tk, tk), tk), :]

    o, = _masked_softmax_pv(n_wide, scores, values, s_scr)
    for h in range(n_heads):
        o_ref[:, h * HEAD_DIM:(h + 1) * HEAD_DIM] = o[h * tq:(h + 1) * tq].astype(o_ref.dtype)


def _dsa_prompt(p16, small32, iq_t, col_qb, col_kb, col_vb, col_ik, col_ik32, n_heads, tab, n_sel):
    b, t, _ = p16.shape
    tq = min(t, 128)
    tk = min(t, 4 * tq)
    assert t % tk == 0 and tk % tq == 0
    bias = _chunk_bias(tab, tq, tk)
    bias = bias.reshape(n_heads * tq, bias.shape[-1])
    qw = n_heads * HEAD_DIM
    assert col_qb % qw == 0 and col_ik % LANES == 0 and col_ik32 % LANES == 0
    return pl.pallas_call(
        functools.partial(_dsa_prompt_body, tq=tq, tk=tk, n_heads=n_heads, n_sel=n_sel),
        grid=(b, t // tq),
        in_specs=[pl.BlockSpec((None, N_IDX_HEADS, tq, IDX_DIM), lambda bi, i: (bi, 0, i, 0)),
                  pl.BlockSpec((None, tq, LANES), lambda bi, i: (bi, i, col_ik32 // LANES)),
                  pl.BlockSpec((None, t, LANES), lambda bi, i: (bi, 0, col_ik // LANES)),
                  pl.BlockSpec((None, tq, qw), lambda bi, i: (bi, i, col_qb // qw)),
                  pl.BlockSpec((None, t, HEAD_DIM), lambda bi, i: (bi, 0, col_kb // HEAD_DIM)),
                  pl.BlockSpec((None, t, HEAD_DIM), lambda bi, i: (bi, 0, col_vb // HEAD_DIM)),
                  pl.BlockSpec(bias.shape, lambda bi, i: (0, 0))],
        out_specs=pl.BlockSpec((None, tq, qw), lambda bi, i: (bi, i, 0)),
        out_shape=jax.ShapeDtypeStruct((b, t, qw), BF16),
        scratch_shapes=[pltpu.VMEM((t // tq, tq, tq), jnp.int32), pltpu.VMEM((t // tq, tq, tq), F32),
                        pltpu.VMEM((N_IDX_HEADS, tq, tq), F32), pltpu.VMEM((1, t // tk, n_heads * tq, tk), F32)],
        compiler_params=_params("parallel", "arbitrary"),
        name="dsa_prompt",
    )(iq_t, small32, p16, p16, p16, p16, bias)


def _block_diag_rows(q, dtype):
    b, tn, g, d = q.shape
    eye = jnp.eye(g, dtype=q.dtype)
    return jnp.einsum('btgd,gk->bgtkd', q, eye).reshape(b, g * tn, g * d).astype(dtype)


def _page_specs(layer, n_pages, group, rows, width):
    def spec(g):
        return pl.BlockSpec((None, None, rows, width),
                            lambda bi, p, pt: (layer, pt[bi, jnp.minimum(p * group + g, n_pages - 1)], 0, 0))
    return [spec(g) for g in range(group)]


def _past_bias_index(page_idx, n_pages):
    return jnp.clip(page_idx - (n_pages - 2), 0, 1)


def _paged_call(body, pt, n_steps, operands, in_specs, out_shape, out_spec, scratch, name):
    return pl.pallas_call(
        body,
        grid_spec=pltpu.PrefetchScalarGridSpec(
            num_scalar_prefetch=1, grid=(pt.shape[0], n_steps),
            in_specs=in_specs, out_specs=out_spec, scratch_shapes=scratch),
        out_shape=out_shape,
        compiler_params=_params("parallel", "arbitrary"),
        name=name,
    )(pt, *operands)


def _per_batch(rows, width):
    return pl.BlockSpec((None, rows, width), lambda bi, p, pt: (bi, 0, 0))


def _whole(shape):
    return pl.BlockSpec(shape, lambda bi, p, pt: (0,) * len(shape))


def _rows_page(x, page):
    b, tn, w = x.shape
    return jnp.pad(x, ((0, 0), (0, page - tn), (0, 0))).reshape(b, page * (w // HEAD_DIM), HEAD_DIM)


def _transposed_page(x, page):
    return jnp.pad(jnp.swapaxes(x, 1, 2), ((0, 0), (0, 0), (0, page - x.shape[1])))


def _flash_scratch(rows, width):
    return [pltpu.VMEM((rows, 1), F32), pltpu.VMEM((rows, 1), F32), pltpu.VMEM((rows, width), F32)]


def _flash_init(m_scr, l_scr, acc_scr):
    m_scr[...] = jnp.full(m_scr.shape, NEG_INF, F32)
    l_scr[...] = jnp.zeros(l_scr.shape, F32)
    acc_scr[...] = jnp.zeros(acc_scr.shape, F32)


def _flash_update(s, pv, m_scr, l_scr, acc_scr):
    m = m_scr[...]
    m_new = jnp.maximum(m, jnp.max(s, axis=-1, keepdims=True))
    m_safe = jnp.where(m_new == NEG_INF, 0.0, m_new)
    p = jnp.exp(s - m_safe)
    alpha = jnp.exp(m - m_safe)
    l_scr[...] = alpha * l_scr[...] + jnp.sum(p, axis=-1, keepdims=True)
    acc_scr[...] = alpha * acc_scr[...] + pv(p.astype(BF16))
    m_scr[...] = m_new


def _cat16(refs, axis):
    parts = [r[...].astype(BF16) for r in refs]
    return parts[0] if len(parts) == 1 else jnp.concatenate(parts, axis=axis)


SAMPLE_PAGE_GROUP = 16
SMALL_PAGE_GROUP = 16
MERGE_UNROLL = 8

def _heads_first(o, b, n_heads, tn):
    return jnp.swapaxes(o.reshape(b, n_heads, tn, HEAD_DIM), 1, 2).reshape(b, tn, n_heads * HEAD_DIM)


def _diff_sample_body(pt_ref, wq_ref, bias_ref, lam_ref, g_ref, kn_ref, vn_ref, *rest,
                      n_pages, group, page, n_new, n_heads, lam_init):
    k_refs, v_refs = rest[:group], rest[group:2 * group]
    o_ref, m_scr, l_scr, acc_scr = rest[2 * group:]
    p = pl.program_id(1)
    rows_h = 2 * n_new

    @pl.when(p == 0)
    def _():
        _flash_init(m_scr, l_scr, acc_scr)

    def step(ks, vs, bias):
        s = jnp.dot(wq_ref[...], _cat16(ks, 1), preferred_element_type=F32) * DH_A ** -0.5 + bias

        def pv(p16):
            outs = []
            for h in range(n_heads):
                vh = [v[pl.ds(h, page, stride=n_heads), :].astype(BF16) for v in vs]
                vh = vh[0] if len(vh) == 1 else jnp.concatenate(vh, axis=0)
                outs.append(jnp.dot(p16[h * rows_h:(h + 1) * rows_h], vh, preferred_element_type=F32))
            return jnp.concatenate(outs, axis=0)

        _flash_update(s, pv, m_scr, l_scr, acc_scr)

    @pl.when(p < n_pages // group)
    def _():
        tiles = [bias_ref[_past_bias_index(p * group + g, n_pages)] for g in range(group)]
        step(k_refs, v_refs, jnp.concatenate(tiles, axis=1))

    @pl.when(p == n_pages // group)
    def _():
        step([kn_ref], [vn_ref], bias_ref[2])
        lam = _diff_lambda(lam_ref, lam_init)
        on = acc_scr[...] / l_scr[...]
        for h in range(n_heads):
            r0 = h * rows_h
            o = on[r0:r0 + n_new] - lam * on[r0 + n_new:r0 + rows_h]
            o_ref[h * n_new:(h + 1) * n_new, :] = _rms(o, g_ref[...]) * (1.0 - lam_init)


def _diff_sample(pt, layer, q, cache_kt, cache_v, k_new, v_new, tab, lam_vec, g_subln, lam_init):
    b, tn, n_heads = q.shape[:3]
    n_pages = pt.shape[1]
    width, page = cache_kt.shape[2:]
    group = min(SAMPLE_PAGE_GROUP, n_pages)
    assert n_pages % group == 0
    wq = _block_diag_rows(q.reshape(b, tn, 2 * n_heads, DH_A), BF16)
    rows = 2 * n_heads * tn
    bias = _sample_bias(tab, tn, page)
    bias = jnp.broadcast_to(bias[:, :, None], (3, n_heads, 2, tn, page)).reshape(3, rows, page)
    out = _paged_call(
        functools.partial(_diff_sample_body, n_pages=n_pages, group=group, page=page, n_new=tn, n_heads=n_heads,
                          lam_init=lam_init),
        pt, n_pages // group + 1,
        (wq, bias, lam_vec, g_subln.reshape(1, HEAD_DIM), _transposed_page(k_new, page), _rows_page(v_new, page))
        + (cache_kt,) * group + (cache_v,) * group,
        [_per_batch(rows, width), _whole(bias.shape), _whole(lam_vec.shape), _whole((1, HEAD_DIM)),
         _per_batch(width, page), _per_batch(page * n_heads, HEAD_DIM)]
        + _page_specs(layer, n_pages, group, width, page) + _page_specs(layer, n_pages, group, page * n_heads, HEAD_DIM),
        jax.ShapeDtypeStruct((b, n_heads * tn, HEAD_DIM), F32), _per_batch(n_heads * tn, HEAD_DIM),
        _flash_scratch(rows, HEAD_DIM), "diff_sample")
    return _heads_first(out, b, n_heads, tn)


def _fox_sample_body(pt_ref, q_ref, cq_ref, ckp_ref, ckn_ref, hm_ref, nm_ref, kn_ref, vn_ref, *rest,
                     n_pages, group):
    k_refs, v_refs = rest[:group], rest[group:2 * group]
    o_ref, m_scr, l_scr, acc_scr = rest[2 * group:]
    p = pl.program_id(1)

    @pl.when(p == 0)
    def _():
        _flash_init(m_scr, l_scr, acc_scr)

    def step(ks, vs, ck, mask):
        s = lax.dot_general(q_ref[...], _cat16(ks, 0), NT_DIMS, preferred_element_type=F32)
        s = s * HEAD_DIM ** -0.5 + (cq_ref[...] - ck) + mask
        _flash_update(s, lambda p16: jnp.dot(p16, _cat16(vs, 0), preferred_element_type=F32), m_scr, l_scr, acc_scr)

    @pl.when(p < n_pages // group)
    def _():
        step(k_refs, v_refs, ckp_ref[...], hm_ref[...])

    @pl.when(p == n_pages // group)
    def _():
        step([kn_ref], [vn_ref], ckn_ref[...], nm_ref[...])
        o_ref[...] = acc_scr[...] / l_scr[...]


def _fox_sample(pt, layer, q, cache_k, cache_v, k_new, v_new, cum_q, cum_past, cum_new):
    b, tn, n_heads = q.shape[:3]
    n_pages = pt.shape[1]
    prow = cache_k.shape[2]
    page = prow // n_heads
    group = min(SAMPLE_PAGE_GROUP, n_pages)
    assert n_pages % group == 0
    rows = n_heads * tn
    q_rows = jnp.swapaxes(q, 1, 2).reshape(b, rows, HEAD_DIM).astype(BF16)
    cq = jnp.swapaxes(cum_q, 1, 2).reshape(b, rows, 1)
    ck_past = cum_past.reshape(b, n_pages // group, 1, group * prow)
    ck_new = jnp.pad(cum_new, ((0, 0), (0, page - tn), (0, 0))).reshape(b, 1, prow)
    head_mask = _expand_heads(jnp.zeros((n_heads, tn, page), F32))
    qi = np.arange(tn)[:, None]
    ci = np.arange(page)[None, :]
    causal = np.broadcast_to(np.where(ci <= qi, 0.0, NEG_INF).astype(np.float32), (n_heads, tn, page))
    new_mask = _expand_heads(jnp.asarray(causal))
    out = _paged_call(
        functools.partial(_fox_sample_body, n_pages=n_pages, group=group),
        pt, n_pages // group + 1,
        (q_rows, cq, ck_past, ck_new, jnp.tile(head_mask, (1, group)), new_mask,
         _rows_page(k_new, page), _rows_page(v_new, page)) + (cache_k,) * group + (cache_v,) * group,
        [_per_batch(rows, HEAD_DIM), _per_batch(rows, 1),
         pl.BlockSpec((None, None, 1, group * prow),
                      lambda bi, p, pt_: (bi, jnp.minimum(p, n_pages // group - 1), 0, 0)),
         _per_batch(1, prow), _whole((rows, group * prow)), _whole((rows, prow)),
         _per_batch(prow, HEAD_DIM), _per_batch(prow, HEAD_DIM)]
        + _page_specs(layer, n_pages, group, prow, HEAD_DIM) + _page_specs(layer, n_pages, group, prow, HEAD_DIM),
        jax.ShapeDtypeStruct((b, rows, HEAD_DIM), F32), _per_batch(rows, HEAD_DIM),
        _flash_scratch(rows, HEAD_DIM), "fox_sample")
    return _heads_first(out, b, n_heads, tn)


def _moba_blocks_body(pt_ref, q_ref, bias_ref, *rest, n_pages, group, pages_per_block, n_heads):
    k_refs, v_refs = rest[:group], rest[group:2 * group]
    acc_ref, stat_ref, kmean_ref = rest[2 * group:]
    p = pl.program_id(1)
    lane = lax.broadcasted_iota(jnp.int32, stat_ref.shape[1:], 1)
    for blk in range(group // pages_per_block):
        pages = range(blk * pages_per_block, (blk + 1) * pages_per_block)
        parts, total = [], None
        for g in pages:
            k = k_refs[g][...]
            s = lax.dot_general(q_ref[...], k.astype(BF16), NT_DIMS, preferred_element_type=F32) * HEAD_DIM ** -0.5
            parts.append(s + bias_ref[_past_bias_index(p * group + g, n_pages)])
            ksum = jnp.sum(k.reshape(k.shape[0] // n_heads, n_heads, HEAD_DIM), axis=0)
            total = ksum if total is None else total + ksum
        s = parts[0] if len(parts) == 1 else jnp.concatenate(parts, axis=1)
        m = jnp.max(s, axis=-1, keepdims=True)
        prob = jnp.exp(s - m)
        acc_ref[blk] = jnp.dot(prob.astype(BF16), _cat16([v_refs[g] for g in pages], 0), preferred_element_type=F32)
        stat_ref[blk] = jnp.where(lane < LANES // 2, m, jnp.sum(prob, axis=-1, keepdims=True))
        kmean_ref[blk] = total * (1.0 / MOBA_BLOCK)


def _moba_merge_body(wq_ref, kmean_ref, acc_ref, stat_ref, q_ref, kn_ref, vn_ref, bias_ref, o_ref, *, n_blk):
    gate = lax.dot_general(wq_ref[...], kmean_ref[...], NT_DIMS, preferred_element_type=F32,
                           precision=lax.Precision.HIGHEST)
    sel = _top_blocks_negmask(gate, min(MOBA_TOPK, n_blk), float(n_blk))
    lane = lax.broadcasted_iota(jnp.int32, sel.shape, 1)
    s = lax.dot_general(q_ref[...], kn_ref[...].astype(BF16), NT_DIMS, preferred_element_type=F32)
    s = s * HEAD_DIM ** -0.5 + bias_ref[...]
    m_own = jnp.max(s, axis=-1, keepdims=True)
    prob = jnp.exp(s - m_own)

    def block_max(blk):
        return stat_ref[blk][:, 0:1] + jnp.min(jnp.where(lane == blk, sel, 0.0), axis=-1, keepdims=True)

    unroll = math.gcd(n_blk, MERGE_UNROLL)

    def max_step(g, m):
        return functools.reduce(jnp.maximum, [block_max(g * unroll + u) for u in range(unroll)], m)

    m_all = lax.fori_loop(0, n_blk // unroll, max_step, m_own)

    def merge(g, carry):
        num, den = carry
        for u in range(unroll):
            blk = g * unroll + u
            w = jnp.exp(block_max(blk) - m_all)
            num = num + w * acc_ref[blk]
            den = den + w * stat_ref[blk][:, LANES // 2:LANES // 2 + 1]
        return num, den

    w_own = jnp.exp(m_own - m_all)
    num, den = lax.fori_loop(
        0, n_blk // unroll, merge,
        (w_own * jnp.dot(prob.astype(BF16), vn_ref[...].astype(BF16), preferred_element_type=F32),
         w_own * jnp.sum(prob, axis=-1, keepdims=True)))
    o_ref[...] = num / den


def _moba_sample(pt, layer, q, cache_k, cache_v, k_new, v_new, tab):
    b, tn, n_heads = q.shape[:3]
    n_pages = pt.shape[1]
    prow = cache_k.shape[2]
    page = prow // n_heads
    width = n_heads * HEAD_DIM
    assert MOBA_BLOCK % page == 0 and (n_pages * page) % MOBA_BLOCK == 0 and tn < MOBA_BLOCK
    ppb = MOBA_BLOCK // page
    n_blk = n_pages // ppb
    group = min(SAMPLE_PAGE_GROUP, n_pages)
    assert n_pages % group == 0 and group % ppb == 0
    bps = group // ppb
    rows = n_heads * tn
    bias = _sample_bias(tab, tn, page)
    bias = jnp.stack([_expand_heads(bias[i]) for i in range(3)])
    q_rows = jnp.swapaxes(q, 1, 2).reshape(b, rows, HEAD_DIM).astype(BF16)
    per_block = lambda r: pl.BlockSpec((None, bps, r, HEAD_DIM), lambda bi, p, pt_: (bi, p, 0, 0))
    acc, stat, kmean = _paged_call(
        functools.partial(_moba_blocks_body, n_pages=n_pages, group=group, pages_per_block=ppb, n_heads=n_heads),
        pt, n_pages // group,
        (q_rows, bias[:2]) + (cache_k,) * group + (cache_v,) * group,
        [_per_batch(rows, HEAD_DIM), _whole((2,) + bias.shape[1:])]
        + _page_specs(layer, n_pages, group, prow, HEAD_DIM) + _page_specs(layer, n_pages, group, prow, HEAD_DIM),
        [jax.ShapeDtypeStruct((b, n_blk, rows, HEAD_DIM), F32), jax.ShapeDtypeStruct((b, n_blk, rows, LANES), F32),
         jax.ShapeDtypeStruct((b, n_blk, n_heads, HEAD_DIM), F32)],
        [per_block(rows), per_block(rows), per_block(n_heads)], [], "moba_blocks")
    whole_b = lambda *shape: pl.BlockSpec((None,) + shape, lambda bi: (bi,) + (0,) * len(shape))
    out = pl.pallas_call(
        functools.partial(_moba_merge_body, n_blk=n_blk),
        grid=(b,),
        in_specs=[whole_b(rows, width), whole_b(n_blk, width), whole_b(n_blk, rows, HEAD_DIM),
                  whole_b(n_blk, rows, LANES), whole_b(rows, HEAD_DIM), whole_b(prow, HEAD_DIM),
                  whole_b(prow, HEAD_DIM), pl.BlockSpec((rows, prow), lambda bi: (0, 0))],
        out_specs=whole_b(rows, HEAD_DIM),
        out_shape=jax.ShapeDtypeStruct((b, rows, HEAD_DIM), F32),
        compiler_params=_params("parallel"),
        name="moba_merge",
    )(_block_diag_rows(q, F32), kmean.reshape(b, n_blk, width), acc, stat, q_rows,
      _rows_page(k_new, page), _rows_page(v_new, page), bias[2])
    return _heads_first(out, b, n_heads, tn)


def _dsa_score_body(pt_ref, iq_ref, w_ref, mask_ref, kn_ref, *rest, n_pages, group, n_new):
    k_refs, o_ref = rest[:group], rest[group]
    p = pl.program_id(1)

    def score(kt16):
        rel = jnp.dot(iq_ref[...], kt16, preferred_element_type=F32)
        rel = jnp.maximum(rel * IDX_DIM ** -0.5, 0.0) * w_ref[...]
        return jnp.sum(rel.reshape(N_IDX_HEADS, n_new, rel.shape[1]), axis=0)

    @pl.when(p < n_pages // group)
    def _():
        o_ref[...] = score(_cat16(k_refs, 1))

    @pl.when(p == n_pages // group)
    def _():
        page = kn_ref.shape[1]
        o_ref[...] = jnp.full(o_ref.shape, NEG_INF, F32)
        o_ref[:, :page] = score(kn_ref[...].astype(BF16)) + mask_ref[...]


def _dsa_select_body(sc_ref, o_ref, key_scr, *, n_new, n_sel, chunk):
    width = sc_ref.shape[1]
    key_scr[...] = _order_key(sc_ref[...])

    def count(pred):
        return jnp.sum(jnp.where(pred(key_scr[...]), 1.0, 0.0), axis=-1, keepdims=True)

    thr = _kth_largest_key(lambda ts: [count(lambda k, t=t: k >= t) for t in ts], n_new, n_sel)
    cnt_ge = count(lambda k: k >= thr)
    tie = jnp.max(jnp.where((cnt_ge > n_sel) & (thr > KEY_OF_NEG_INF), 1.0, 0.0)) > 0.0

    @pl.when(jnp.logical_not(tie))
    def _():
        o_ref[...] = jnp.where(key_scr[...] >= thr, 0.0, NEG_INF)

    @pl.when(tie)
    def _():
        allow = n_sel - count(lambda k: k > thr)
        r = lax.broadcasted_iota(jnp.int32, (chunk, chunk), 0)
        c = lax.broadcasted_iota(jnp.int32, (chunk, chunk), 1)
        tri = jnp.where(r <= c, 1.0, 0.0).astype(BF16)

        def body(j, before):
            cols = pl.ds(pl.multiple_of(j * chunk, chunk), chunk)
            k = key_scr[:, cols]
            eq = jnp.where(k == thr, 1.0, 0.0)
            rank = jnp.dot(eq.astype(BF16), tri, preferred_element_type=F32) + before
            keep = jnp.where(k > thr, 1.0, jnp.where(rank <= allow, eq, 0.0))
            o_ref[:, cols] = jnp.where(keep > 0.0, 0.0, NEG_INF)
            return before + jnp.sum(eq, axis=-1, keepdims=True)
        lax.fori_loop(0, width // chunk, body, jnp.zeros((n_new, 1), F32))


def _dsa_sample_body(pt_ref, q_ref, bias_ref, nm_ref, kn_ref, vn_ref, *rest, n_pages, group, n_heads):
    k_refs, v_refs = rest[:group], rest[group:2 * group]
    o_ref, m_scr, l_scr, acc_scr = rest[2 * group:]
    p = pl.program_id(1)

    @pl.when(p == 0)
    def _():
        _flash_init(m_scr, l_scr, acc_scr)

    def step(ks, vs, bias, nm):
        s = lax.dot_general(q_ref[...], _cat16(ks, 0), NT_DIMS, preferred_element_type=F32)
        s = s * HEAD_DIM ** -0.5 + bias + jnp.tile(nm, (n_heads, 1))
        _flash_update(s, lambda p16: jnp.dot(p16, _cat16(vs, 0), preferred_element_type=F32), m_scr, l_scr, acc_scr)

    @pl.when(p < n_pages // group)
    def _():
        tiles = [bias_ref[_past_bias_index(p * group + g, n_pages)] for g in range(group)]
        step(k_refs, v_refs, jnp.concatenate(tiles, axis=1), nm_ref[...])

    @pl.when(p == n_pages // group)
    def _():
        page = kn_ref.shape[0]
        step([kn_ref], [vn_ref], bias_ref[2], nm_ref[:, :page])
        o_ref[...] = acc_scr[...] / l_scr[...]


def _dsa_sample(pt, layer, qb, iq, iw, cache_k, cache_v, cache_idx_t, k_new, v_new, ik_new, tab):
    b, tn, n_heads = qb.shape[:3]
    n_pages = pt.shape[1]
    page = cache_k.shape[2]
    n_sel = min(DSA_TOPK, (n_pages * page + tn) // 4)
    group = min(SMALL_PAGE_GROUP, n_pages)
    assert n_pages % group == 0
    n_steps = n_pages // group + 1
    width = n_steps * group * page
    pad = ((0, 0), (0, page - tn), (0, 0))
    qi = np.arange(tn)[:, None]
    ci = np.arange(page)[None, :]
    new_mask = jnp.asarray(np.where(ci <= qi, 0.0, NEG_INF).astype(np.float32))
    n_iq = N_IDX_HEADS * tn
    iq_rows = jnp.swapaxes(iq, 1, 2).reshape(b, n_iq, IDX_DIM).astype(BF16)
    w_rows = (jnp.swapaxes(iw, 1, 2).astype(F32) * N_IDX_HEADS ** -0.5).reshape(b, n_iq, 1)
    step_cols = pl.BlockSpec((None, tn, group * page), lambda bi, p, pt_: (bi, 0, p))
    scores = _paged_call(
        functools.partial(_dsa_score_body, n_pages=n_pages, group=group, n_new=tn),
        pt, n_steps,
        (iq_rows, w_rows, new_mask, _transposed_page(ik_new, page)) + (cache_idx_t,) * group,
        [_per_batch(n_iq, IDX_DIM), _per_batch(n_iq, 1), _whole(new_mask.shape), _per_batch(IDX_DIM, page)]
        + _page_specs(layer, n_pages, group, IDX_DIM, page),
        jax.ShapeDtypeStruct((b, tn, width), F32), step_cols, [], "dsa_score")
    negmask = pl.pallas_call(
        functools.partial(_dsa_select_body, n_new=b * tn, n_sel=n_sel, chunk=page),
        grid=(1,),
        in_specs=[pl.BlockSpec((b * tn, width), lambda i: (0, 0))],
        out_specs=pl.BlockSpec((b * tn, width), lambda i: (0, 0)),
        out_shape=jax.ShapeDtypeStruct((b * tn, width), F32),
        scratch_shapes=[pltpu.VMEM((b * tn, width), jnp.int32)],
        compiler_params=_params("arbitrary"),
        name="dsa_select",
    )(scores.reshape(b * tn, width)).reshape(b, tn, width)
    rows = n_heads * tn
    q_rows = jnp.swapaxes(qb, 1, 2).reshape(b, rows, HEAD_DIM).astype(BF16)
    bias = _sample_bias(tab, tn, page).reshape(3, rows, page)
    out = _paged_call(
        functools.partial(_dsa_sample_body, n_pages=n_pages, group=group, n_heads=n_heads),
        pt, n_steps,
        (q_rows, bias, negmask, jnp.pad(k_new, pad), jnp.pad(v_new, pad)) + (cache_k,) * group + (cache_v,) * group,
        [_per_batch(rows, HEAD_DIM), _whole(bias.shape), step_cols, _per_batch(page, HEAD_DIM),
         _per_batch(page, HEAD_DIM)]
        + _page_specs(layer, n_pages, group, page, HEAD_DIM) + _page_specs(layer, n_pages, group, page, HEAD_DIM),
        jax.ShapeDtypeStruct((b, rows, HEAD_DIM), F32), _per_batch(rows, HEAD_DIM),
        _flash_scratch(rows, HEAD_DIM), "dsa_sample")
    return _heads_first(out, b, n_heads, tn)


def _pad_rows(w, n):
    return jnp.pad(w, ((0, 0),) * (w.ndim - 2) + ((0, n - w.shape[-2]), (0, 0)))


def _mixer_even(h_in, g, w_in16, past, pt, layer_e, lam_vec, g_subln, lam_init, tab, b, t):
    d = h_in.shape[1]
    n_a = n_b = (d // HEAD_DIM) // 2
    wa, wb = n_a * HEAD_DIM, n_b * HEAD_DIM
    wi = N_IDX_HEADS * IDX_DIM
    c_qa, c_ka, c_va, c_qb = 0, wa, 2 * wa, 3 * wa
    c_iq = c_qb + wb
    c_small = c_iq + wi
    c_kb, c_vb, c_ik = c_small, c_small + HEAD_DIM, c_small + 2 * HEAD_DIM
    tn = COL_TILE
    assert wa % tn == 0 and c_small % tn == 0 and w_in16.shape[-2] == c_small + tn
    ka_t = past is None and t % min(b * t, 2 * ROW_TILE) == 0
    ka, va, small, p16 = _rms_matmul(
        h_in, g, w_in16, (layer_e,), emit16=True, w_transposed=True, rows_per_batch=t,
        f32_groups=((c_ka // tn, wa // tn, ka_t), (c_va // tn, wa // tn), (c_small // tn, 1)))
    p16 = p16.reshape(b, t, -1)
    va, small = va.reshape(b, t, wa), small.reshape(b, t, tn)
    if ka_t:
        a_k = jnp.transpose(ka.reshape(b, n_a, 2, DH_A, t), (0, 4, 1, 2, 3))
    else:
        ka = ka.reshape(b, t, wa)
        a_k = ka.reshape(b, t, n_a, 2, DH_A)
    kb = small[..., :HEAD_DIM]
    vb = small[..., HEAD_DIM:2 * HEAD_DIM]
    ik = small[..., 2 * HEAD_DIM:2 * HEAD_DIM + IDX_DIM]
    rows = (a_k, va.reshape(b, t, n_a, HEAD_DIM), kb, vb, ik)
    tab_a, tab_b = tab[:, :n_a], tab[:, n_a:]
    if past is None:
        o_a = _diff_prompt(p16, c_qa, c_ka, c_va, n_a, tab_a, lam_vec, g_subln, lam_init)
        iq_t = jnp.swapaxes(p16[..., c_iq:c_small].reshape(b, t, N_IDX_HEADS, IDX_DIM), 1, 2)
        o_b = _dsa_prompt(p16, small, iq_t, c_qb, c_kb, c_vb, c_ik, 2 * HEAD_DIM, n_b, tab_b,
                          min(DSA_TOPK, t // 4))
    else:
        cache_a_k, cache_a_v, cache_b_k, cache_b_v, cache_b_idx = past
        qa = p16[..., c_qa:c_qa + wa].reshape(b, t, n_a, 2, DH_A)
        o_a = _diff_sample(pt, layer_e, qa, cache_a_k, cache_a_v, ka, va, tab_a, lam_vec, g_subln, lam_init)
        qb = p16[..., c_qb:c_qb + wb].reshape(b, t, n_b, HEAD_DIM)
        iq = p16[..., c_iq:c_small].reshape(b, t, N_IDX_HEADS, IDX_DIM)
        iw = small[..., 2 * HEAD_DIM + IDX_DIM:2 * HEAD_DIM + IDX_DIM + N_IDX_HEADS]
        o_b = _dsa_sample(pt, layer_e, qb, iq, iw, cache_b_k, cache_b_v, cache_b_idx, kb, vb, ik, tab_b)
    return (o_a.reshape(b * t, wa), o_b.reshape(b * t, wb)), rows


def _mixer_odd(h_in, g, w_in16, past, pt, layer_o, b_forget, tab, b, t):
    d = h_in.shape[1]
    n_c = n_d = (d // HEAD_DIM) // 2
    wc, wd = n_c * HEAD_DIM, n_d * HEAD_DIM
    c_qc, c_kc, c_vc, c_qd = 0, wc, 2 * wc, 3 * wc
    c_kd = c_qd + wd
    c_vd = c_kd + wd
    c_fc = c_vd + wd
    tn = COL_TILE
    assert wc % tn == 0 and wd % tn == 0 and w_in16.shape[-2] == c_fc + tn
    groups = tuple((c // tn, wc // tn) for c in (c_kc, c_vc, c_qd, c_kd, c_vd)) + ((c_fc // tn, 1),)
    kc, vc, qd32, kd, vd, small, p16 = _rms_matmul(h_in, g, w_in16, (layer_o,), emit16=True, w_transposed=True,
                                                   f32_groups=groups)
    p16 = p16.reshape(b, t, -1)
    kc, vc, qd32, kd, vd = (a.reshape(b, t, wc) for a in (kc, vc, qd32, kd, vd))
    log_f = jax.nn.log_sigmoid(small.reshape(b, t, tn)[..., :n_c] + b_forget.astype(F32))
    rows = (kc.reshape(b, t, n_c, HEAD_DIM), vc.reshape(b, t, n_c, HEAD_DIM), log_f,
            kd.reshape(b, t, n_d, HEAD_DIM), vd.reshape(b, t, n_d, HEAD_DIM))
    tab_d = tab[:, n_c:]
    if past is None:
        o_c = _fox_prompt(p16, c_qc, c_kc, c_vc, n_c, jnp.cumsum(log_f, axis=1))
        o_d = _moba_prompt(p16, qd32, kd, c_qd, c_kd, c_vd, n_d, tab_d)
    else:
        cache_c_k, cache_c_v, cache_c_logf, cache_d_k, cache_d_v = past
        n_pages = pt.shape[1]
        page = cache_c_logf.shape[2]
        logf_past = cache_c_logf[layer_o][pt].reshape(b, n_pages * page, n_c)
        cum = jnp.cumsum(jnp.concatenate([logf_past, log_f], axis=1).astype(F32), axis=1)
        cum_q = cum[:, n_pages * page:]
        qc = p16[..., c_qc:c_qc + wc].reshape(b, t, n_c, HEAD_DIM)
        o_c = _fox_sample(pt, layer_o, qc, cache_c_k, cache_c_v, kc, vc, cum_q, cum[:, :n_pages * page], cum_q)
        o_d = _moba_sample(pt, layer_o, qd32.reshape(b, t, n_d, HEAD_DIM), cache_d_k, cache_d_v, kd, vd, tab_d)
    return (o_c.reshape(b * t, wc), o_d.reshape(b * t, wd)), rows


def _run_trunk(x, past_even, past_odd, pt, mem_kv, prm):
    b, t, d = x.shape
    x = x.reshape(b * t, d)
    depth = prm['norm_g'].shape[0]
    rows_even, rows_odd = [], []
    for layer in range(depth):
        g = prm['norm_g'][layer]
        wg, wu, wd = prm['w_ffn_gate'], prm['w_ffn_up'], prm['w_ffn_down']
        x = _ffn(x, g[NG_FFN1_PRE], g[NG_FFN1_POST], wg, wu, wd, (layer, 0))
        if layer % 2 == 0:
            e = layer // 2
            lam_init = 0.8 - 0.6 * math.exp(-0.3 * layer)
            parts, rows = _mixer_even(x, g[NG_MIX_PRE], prm['w_in_even'], past_even, pt, e,
                                      prm['diff_lambda'][e].astype(F32), prm['g_subln'][e].astype(F32), lam_init,
                                      prm['t5_table'], b, t)
            rows_even.append(rows)
            x = _out_proj(x, g[NG_MIX_POST], parts, prm['w_out_even'], (e,))
        else:
            o = layer // 2
            parts, rows = _mixer_odd(x, g[NG_MIX_PRE], prm['w_in_odd'], past_odd, pt, o,
                                     prm['b_forget'][o], prm['t5_table'], b, t)
            rows_odd.append(rows)
            x = _out_proj(x, g[NG_MIX_POST], parts, prm['w_out_odd'], (o,))
        mk, mv = mem_kv[layer]
        q = _rms_matmul(x, g[NG_X_PRE], prm['w_xq'], (layer,))
        o_x = _cross_attend(q.reshape(b, t, -1), mk, mv)
        x = _out_proj(x, g[NG_X_POST], [o_x.reshape(b * t, -1)], prm['w_xo'], (layer,))
        x = _ffn(x, g[NG_FFN2_PRE], g[NG_FFN2_POST], wg, wu, wd, (layer, 1))
    return x.reshape(b, t, d), rows_even, rows_odd


def kernel(x_prompt, x_sample, cache_a_k, cache_a_v, cache_b_k, cache_b_v, cache_b_idx, cache_c_k, cache_c_v, cache_c_logf, cache_d_k, cache_d_v, cache_mem_k, cache_mem_v, page_table, mem_prompt, t5_table, norm_g, w_ffn_gate, w_ffn_up, w_ffn_down, w_xq, w_xk, w_xv, w_xo, w_in_even, w_out_even, diff_lambda, g_subln, w_in_odd, w_out_odd, b_forget):
    depth = norm_g.shape[0]
    d_model = x_prompt.shape[-1]
    n_c = (d_model // HEAD_DIM) // 2
    half = n_c * HEAD_DIM
    wi = N_IDX_HEADS * IDX_DIM
    e_cut = 4 * half
    w_e = jnp.swapaxes(w_in_even, 1, 2)
    w_in_even_r = jnp.concatenate([w_e[:, :e_cut], w_e[:, e_cut + 2 * HEAD_DIM:e_cut + 2 * HEAD_DIM + wi],
                                   w_e[:, e_cut:e_cut + 2 * HEAD_DIM], w_e[:, e_cut + 2 * HEAD_DIM + wi:]], axis=1)
    c_fc = 3 * half
    w_o = jnp.swapaxes(w_in_odd, 1, 2)
    w_in_odd_r = jnp.concatenate([w_o[:, :c_fc], w_o[:, c_fc + n_c:], w_o[:, c_fc:c_fc + n_c]], axis=1)
    prm = {
        't5_table': t5_table.astype(F32), 'norm_g': norm_g.astype(F32),
        'w_ffn_gate': w_ffn_gate.astype(BF16), 'w_ffn_up': w_ffn_up.astype(BF16), 'w_ffn_down': w_ffn_down.astype(BF16),
        'w_xq': w_xq.astype(BF16), 'w_xo': w_xo.astype(BF16),
        'w_in_even': _pad_rows(w_in_even_r, e_cut + wi + COL_TILE).astype(BF16),
        'w_out_even': w_out_even.astype(BF16),
        'w_in_odd': _pad_rows(w_in_odd_r, 6 * half + COL_TILE).astype(BF16),
        'w_out_odd': w_out_odd.astype(BF16),
        'diff_lambda': diff_lambda, 'g_subln': g_subln, 'b_forget': b_forget,
    }
    b_p, n_mem, _ = mem_prompt.shape
    hx_w = w_xk.shape[-1]

    mem_kv_p, mem_k_out, mem_v_out = [], [], []
    for l in range(depth):
        w_kv = jnp.concatenate([w_xk[l], w_xv[l]], axis=-1).astype(BF16)
        kv = _rms_matmul(mem_prompt.reshape(b_p * n_mem, d_model), norm_g[l, NG_MEM].astype(F32), w_kv)
        mk = kv[:, :hx_w].reshape(b_p, n_mem, hx_w)
        mv = kv[:, hx_w:].reshape(b_p, n_mem, hx_w)
        mem_kv_p.append((mk, mv))
        mem_k_out.append(mk.reshape(b_p, n_mem, hx_w // HEAD_DIM, HEAD_DIM))
        mem_v_out.append(mv.reshape(b_p, n_mem, hx_w // HEAD_DIM, HEAD_DIM))
    y_prompt, ev_p, od_p = _run_trunk(x_prompt.astype(F32), None, None, None, mem_kv_p, prm)

    def rows_pages(c):
        return c.reshape(c.shape[:2] + (c.shape[2] * c.shape[3], c.shape[4]))

    def transposed_pages(c):
        c = c.reshape(c.shape[:3] + (-1,))
        return jnp.swapaxes(c, 2, 3)

    past_even = (transposed_pages(cache_a_k), rows_pages(cache_a_v), cache_b_k, cache_b_v,
                 transposed_pages(cache_b_idx))
    past_odd = (rows_pages(cache_c_k), rows_pages(cache_c_v), cache_c_logf, rows_pages(cache_d_k),
                rows_pages(cache_d_v))
    b_s = x_sample.shape[0]
    mem_kv_s = [(cache_mem_k[l].reshape(b_s, n_mem, hx_w), cache_mem_v[l].reshape(b_s, n_mem, hx_w))
                for l in range(depth)]
    y_sample, ev_s, od_s = _run_trunk(x_sample.astype(F32), past_even, past_odd, page_table.astype(jnp.int32),
                                      mem_kv_s, prm)

    def stack(rows, i):
        return jnp.stack([r[i] for r in rows])

    out = [y_prompt, y_sample]
    out += [stack(ev_p, i) for i in range(5)] + [stack(od_p, i) for i in range(5)]
    out += [jnp.stack(mem_k_out), jnp.stack(mem_v_out)]
    out += [stack(ev_s, i) for i in range(5)] + [stack(od_s, i) for i in range(5)]
    return tuple(out)
```

```python
import functools
import math

import numpy as np
import jax
import jax.numpy as jnp
from jax import lax
from jax.experimental import pallas as pl
from jax.experimental.pallas import tpu as pltpu

F32 = jnp.float32
BF16 = jnp.bfloat16
NEG_INF = float("-inf")

HEAD_DIM = 128
DH_A = HEAD_DIM // 2
N_IDX_HEADS = 16
IDX_DIM = 64
DSA_TOPK = 256
MOBA_BLOCK = 256
MOBA_TOPK = 3
N_BUCKETS = 32
T5_MAX_EXACT = N_BUCKETS // 2
T5_MAX_DIST = 128
RMS_EPS = 1e-6
NG_FFN1_PRE, NG_FFN1_POST, NG_MIX_PRE, NG_MIX_POST = 0, 1, 2, 3
NG_X_PRE, NG_X_POST, NG_FFN2_PRE, NG_FFN2_POST, NG_MEM = 4, 5, 6, 7, 8

LANES = 128
ROW_TILE = 512
COL_TILE = 512
VMEM_LIMIT = 56 * 1024 * 1024

NT_DIMS = (((1,), (1,)), ((), ()))


def _params(*sem):
    return pltpu.CompilerParams(dimension_semantics=sem, vmem_limit_bytes=VMEM_LIMIT)


def _rms(x, g):
    return x * lax.rsqrt(jnp.mean(x * x, axis=-1, keepdims=True) + RMS_EPS) * g


def _bucket_np(dist):
    n = np.maximum(dist, 0)
    n_f = np.maximum(n, 1).astype(np.float32)
    large = T5_MAX_EXACT + (np.log(n_f / np.float32(T5_MAX_EXACT)) / np.float32(math.log(T5_MAX_DIST / T5_MAX_EXACT))
                            * np.float32(N_BUCKETS - T5_MAX_EXACT)).astype(np.int32)
    return np.where(n < T5_MAX_EXACT, n, np.minimum(large, N_BUCKETS - 1)).astype(np.int32)


def _rel_bias(tab, dists):
    onehot = (_bucket_np(dists)[:, None] == np.arange(N_BUCKETS)[None, :]).astype(np.float32)
    rel = jnp.sum(jnp.asarray(onehot)[:, :, None] * tab[None].astype(F32), axis=1)
    return jnp.where(jnp.asarray(dists >= 0)[:, None], rel, NEG_INF).T


def _toeplitz(u, n_rows, n_cols):
    h = u.shape[0]
    period = n_rows + n_cols
    w = jnp.concatenate([u[:, :n_cols][:, ::-1], jnp.zeros((h, 1), u.dtype), u[:, n_cols:][:, ::-1]], axis=1)
    flat = jnp.tile(w, (1, n_rows))[:, :n_rows * (period - 1)]
    return flat.reshape(h, n_rows, period - 1)[:, :, :n_cols]


def _dist_tile(tab, n_rows, n_cols, offset):
    dists = np.arange(n_rows + n_cols - 1) - (n_cols - 1) + offset
    return _toeplitz(_rel_bias(tab, dists), n_rows, n_cols)


def _far_tile(tab, n_rows, n_cols):
    return jnp.broadcast_to(tab[N_BUCKETS - 1].astype(F32)[:, None, None], (tab.shape[1], n_rows, n_cols))


def _sample_bias(tab, n_new, page):
    assert page + 1 >= T5_MAX_DIST
    return jnp.stack([_far_tile(tab, n_new, page), _dist_tile(tab, n_new, page, page), _dist_tile(tab, n_new, page, 0)])


def _expand_heads(tile):
    h, r, c = tile.shape
    same = jnp.asarray(np.eye(h, dtype=bool))[:, None, None, :]
    return jnp.where(same, tile[:, :, :, None], NEG_INF).reshape(h * r, c * h)


def _rms_matmul_body(x_ref, g_ref, w_ref, *rest, groups, emit16, w_transposed):
    outs, h_scr = rest[:-1], rest[-1]
    j = pl.program_id(1)

    @pl.when(j == 0)
    def _():
        h_scr[...] = _rms(x_ref[...], g_ref[...]).astype(BF16)

    if w_transposed:
        y = lax.dot_general(h_scr[...], w_ref[...], NT_DIMS, preferred_element_type=F32)
    else:
        y = jnp.dot(h_scr[...], w_ref[...], preferred_element_type=F32)
    for (first, count, transposed), o_ref in zip(groups, outs):
        @pl.when((j >= first) & (j < first + count))
        def _():
            o_ref[...] = y.T if transposed else y
    if emit16:
        outs[len(groups)][...] = y.astype(BF16)


def _stacked(lead, *block):
    return (None,) * len(lead) + tuple(block), tuple(lead)


def _rms_matmul(x, g, w16, lead=(), *, f32_groups=None, emit16=False, w_transposed=False, rows_per_batch=None):
    m, d = x.shape
    n = w16.shape[-2] if w_transposed else w16.shape[-1]
    tm = min(m, 2 * ROW_TILE)
    tn = min(n, COL_TILE)
    assert m % tm == 0 and n % tn == 0
    groups = tuple(tuple(grp) + (False,) * (3 - len(grp)) for grp in (f32_groups or ((0, n // tn),)))
    out_shape, out_specs = [], []
    for first, count, transposed in groups:
        col = lambda j, first=first, count=count: jnp.clip(j - first, 0, count - 1)
        if transposed:
            assert rows_per_batch % tm == 0
            per_b = rows_per_batch // tm
            out_shape.append(jax.ShapeDtypeStruct((m // rows_per_batch, count * tn, rows_per_batch), F32))
            out_specs.append(pl.BlockSpec((None, tn, tm), lambda i, j, col=col: (i // per_b, col(j), i % per_b)))
        else:
            out_shape.append(jax.ShapeDtypeStruct((m, count * tn), F32))
            out_specs.append(pl.BlockSpec((tm, tn), lambda i, j, col=col: (i, col(j))))
    if emit16:
        out_shape.append(jax.ShapeDtypeStruct((m, n), BF16))
        out_specs.append(pl.BlockSpec((tm, tn), lambda i, j: (i, j)))
    w_block, w_lead = _stacked(lead, *((tn, d) if w_transposed else (d, tn)))
    w_index = (lambda i, j: w_lead + (j, 0)) if w_transposed else (lambda i, j: w_lead + (0, j))
    res = pl.pallas_call(
        functools.partial(_rms_matmul_body, groups=groups, emit16=emit16, w_transposed=w_transposed),
        grid=(m // tm, n // tn),
        in_specs=[pl.BlockSpec((tm, d), lambda i, j: (i, 0)),
                  pl.BlockSpec((1, d), lambda i, j: (0, 0)),
                  pl.BlockSpec(w_block, w_index)],
        out_specs=out_specs,
        out_shape=out_shape,
        scratch_shapes=[pltpu.VMEM((tm, d), BF16)],
        compiler_params=_params("parallel", "arbitrary"),
        name="rms_matmul",
    )(x, g.reshape(1, d), w16)
    return res if len(res) > 1 else res[0]


def _ffn_body(x_ref, gpre_ref, gpost_ref, wg_ref, wu_ref, wd_ref, o_ref, h_scr, acc_scr):
    j = pl.program_id(1)

    @pl.when(j == 0)
    def _():
        h_scr[...] = _rms(x_ref[...], gpre_ref[...]).astype(BF16)
        acc_scr[...] = jnp.zeros_like(acc_scr)

    h = h_scr[...]
    gate = jnp.dot(h, wg_ref[...], preferred_element_type=F32)
    up = jnp.dot(h, wu_ref[...], preferred_element_type=F32)
    act = (gate * jax.nn.sigmoid(gate) * up).astype(BF16)
    acc_scr[...] += jnp.dot(act, wd_ref[...], preferred_element_type=F32)

    @pl.when(j == pl.num_programs(1) - 1)
    def _():
        o_ref[...] = x_ref[...] + 0.5 * _rms(acc_scr[...], gpost_ref[...])


def _ffn(x, g_pre, g_post, wg16, wu16, wd16, lead):
    m, d = x.shape
    ff = wg16.shape[-1]
    tm = min(m, ROW_TILE)
    tf = min(ff, COL_TILE)
    assert m % tm == 0 and ff % tf == 0
    up_block, w_lead = _stacked(lead, d, tf)
    down_block, _ = _stacked(lead, tf, d)
    return pl.pallas_call(
        _ffn_body,
        grid=(m // tm, ff // tf),
        in_specs=[pl.BlockSpec((tm, d), lambda i, j: (i, 0)),
                  pl.BlockSpec((1, d), lambda i, j: (0, 0)),
                  pl.BlockSpec((1, d), lambda i, j: (0, 0)),
                  pl.BlockSpec(up_block, lambda i, j: w_lead + (0, j)),
                  pl.BlockSpec(up_block, lambda i, j: w_lead + (0, j)),
                  pl.BlockSpec(down_block, lambda i, j: w_lead + (j, 0))],
        out_specs=pl.BlockSpec((tm, d), lambda i, j: (i, 0)),
        out_shape=jax.ShapeDtypeStruct((m, d), F32),
        scratch_shapes=[pltpu.VMEM((tm, d), BF16), pltpu.VMEM((tm, d), F32)],
        compiler_params=_params("parallel", "arbitrary"),
        name="ffn",
    )(x, g_pre.reshape(1, d), g_post.reshape(1, d), wg16, wu16, wd16)


def _out_body(*refs, n_parts):
    x_ref, g_ref = refs[0], refs[1]
    o_refs = refs[2:2 + n_parts]
    w_refs = refs[2 + n_parts:2 + 2 * n_parts]
    out_ref = refs[-1]
    y = None
    for o_ref, w_ref in zip(o_refs, w_refs):
        t = jnp.dot(o_ref[...].astype(BF16), w_ref[...], preferred_element_type=F32)
        y = t if y is None else y + t
    out_ref[...] = x_ref[...] + _rms(y, g_ref[...])


def _out_proj(x, g, parts, w16, lead):
    m, d = x.shape
    tm = min(m, ROW_TILE)
    k = parts[0].shape[1]
    assert m % tm == 0 and all(p.shape[1] == k for p in parts) and len(parts) * k == w16.shape[-2]
    w_block, w_lead = _stacked(lead, k, d)
    in_specs = [pl.BlockSpec((tm, d), lambda i: (i, 0)), pl.BlockSpec((1, d), lambda i: (0, 0))]
    in_specs += [pl.BlockSpec((tm, k), lambda i: (i, 0)) for _ in parts]
    in_specs += [pl.BlockSpec(w_block, lambda i, n=n: w_lead + (n, 0)) for n in range(len(parts))]
    return pl.pallas_call(
        functools.partial(_out_body, n_parts=len(parts)),
        grid=(m // tm,),
        in_specs=in_specs,
        out_specs=pl.BlockSpec((tm, d), lambda i: (i, 0)),
        out_shape=jax.ShapeDtypeStruct((m, d), F32),
        compiler_params=_params("parallel"),
        name="out_proj",
    )(x, g.reshape(1, d), *parts, *([w16] * len(parts)))


def _cross_body(q_ref, k_ref, v_ref, o_ref, *, n_heads):
    q = q_ref[...].astype(BF16)
    k = k_ref[...].astype(BF16)
    v = v_ref[...].astype(BF16)
    for h in range(n_heads):
        sl = slice(h * HEAD_DIM, (h + 1) * HEAD_DIM)
        s = lax.dot_general(q[:, sl], k[:, sl], NT_DIMS, preferred_element_type=F32) * HEAD_DIM ** -0.5
        p = jnp.exp(s - jnp.max(s, axis=-1, keepdims=True))
        l = jnp.sum(p, axis=-1, keepdims=True)
        o = jnp.dot(p.astype(BF16), v[:, sl], preferred_element_type=F32) / l
        o_ref[:, sl] = o.astype(o_ref.dtype)


def _cross_attend(q, mem_k, mem_v):
    b, t, w = q.shape
    n_mem = mem_k.shape[1]
    tq = min(t, ROW_TILE)
    assert t % tq == 0
    return pl.pallas_call(
        functools.partial(_cross_body, n_heads=w // HEAD_DIM),
        grid=(b, t // tq),
        in_specs=[pl.BlockSpec((None, tq, w), lambda bi, i: (bi, i, 0)),
                  pl.BlockSpec((None, n_mem, w), lambda bi, i: (bi, 0, 0)),
                  pl.BlockSpec((None, n_mem, w), lambda bi, i: (bi, 0, 0))],
        out_specs=pl.BlockSpec((None, tq, w), lambda bi, i: (bi, i, 0)),
        out_shape=jax.ShapeDtypeStruct((b, t, w), F32),
        compiler_params=_params("parallel", "parallel"),
        name="cross_attend",
    )(q, mem_k, mem_v)


def _diff_lambda(lam_ref, lam_init):
    lv = lam_ref[...]
    return (jnp.exp(jnp.sum(lv[0:1] * lv[1:2], axis=-1, keepdims=True))
            - jnp.exp(jnp.sum(lv[2:3] * lv[3:4], axis=-1, keepdims=True)) + lam_init)


def _fold_lanes(x, op):
    parts = [x[:, c * LANES:(c + 1) * LANES] for c in range(x.shape[1] // LANES)]
    return functools.reduce(op, parts)


def _masked_softmax_pv(n_chunks, scores, values, s_scr):
    n_maps, _, rows, _ = s_scr.shape

    def first(j, mx):
        out = []
        for a, s in enumerate(scores(j)):
            s_scr[a, j] = s
            out.append(jnp.maximum(mx[a], _fold_lanes(s, jnp.maximum)))
        return tuple(out)

    mx = lax.fori_loop(0, n_chunks, first, tuple(jnp.full((rows, LANES), NEG_INF, F32) for _ in range(n_maps)))
    m = [jnp.max(x, axis=-1, keepdims=True) for x in mx]

    def second(j, carry):
        v = values(j)
        out = []
        for a in range(n_maps):
            l, acc = carry[a]
            p = jnp.exp(s_scr[a, j] - m[a])
            out.append((l + _fold_lanes(p, jnp.add), acc + jnp.dot(p.astype(BF16), v, preferred_element_type=F32)))
        return tuple(out)

    init = tuple((jnp.zeros((rows, LANES), F32), jnp.zeros((rows, HEAD_DIM), F32)) for _ in range(n_maps))
    res = lax.fori_loop(0, n_chunks, second, init)
    return [acc / jnp.sum(l, axis=-1, keepdims=True) for l, acc in res]


def _prompt_tiles(t):
    tq = min(t, 256)
    tk = min(t, 1024)
    assert t % tk == 0 and tk % tq == 0
    return tq, tk


def _chunk_bias(tab, tq, tk):
    assert tq + 1 >= T5_MAX_DIST and tk % tq == 0
    n_far = tk // tq + 1
    return _dist_tile(tab, tq, tk + n_far * tq, n_far * tq)


def _bias_window(bias_ref, k, tq, tk):
    n_far = tk // tq + 1
    start = pl.multiple_of((n_far - jnp.minimum(k, n_far)) * tq, tq)
    return bias_ref[:, pl.ds(start, tk)]


def _diff_prompt_body(q_ref, k_ref, v_ref, bias_ref, lam_ref, g_ref, o_ref, s_scr, *, tq, tk, lam_init):
    i = pl.program_id(2)
    ratio = tk // tq
    q = q_ref[...]
    q0, q1 = q[:, :DH_A], q[:, DH_A:]
    scale = DH_A ** -0.5

    def scores(j):
        ks = k_ref[pl.ds(pl.multiple_of(j * tk, tk), tk), :]
        bt = _bias_window(bias_ref, i - ratio * j, tq, tk)
        return (lax.dot_general(q0, ks[:, :DH_A], NT_DIMS, preferred_element_type=F32) * scale + bt,
                lax.dot_general(q1, ks[:, DH_A:], NT_DIMS, preferred_element_type=F32) * scale + bt)

    def values(j):
        return v_ref[pl.ds(pl.multiple_of(j * tk, tk), tk), :]

    o0, o1 = _masked_softmax_pv(i // ratio + 1, scores, values, s_scr)
    o = o0 - _diff_lambda(lam_ref, lam_init) * o1
    o_ref[...] = (_rms(o, g_ref[...]) * (1.0 - lam_init)).astype(o_ref.dtype)


def _diff_prompt(p16, col_q, col_k, col_v, n_heads, tab, lam_vec, g_subln, lam_init):
    b, t, _ = p16.shape
    tq, tk = _prompt_tiles(t)
    bias = _chunk_bias(tab, tq, tk)
    cq, ck, cv = col_q // HEAD_DIM, col_k // HEAD_DIM, col_v // HEAD_DIM
    return pl.pallas_call(
        functools.partial(_diff_prompt_body, tq=tq, tk=tk, lam_init=lam_init),
        grid=(b, n_heads, t // tq),
        in_specs=[pl.BlockSpec((None, tq, HEAD_DIM), lambda bi, h, i: (bi, i, cq + h)),
                  pl.BlockSpec((None, t, HEAD_DIM), lambda bi, h, i: (bi, 0, ck + h)),
                  pl.BlockSpec((None, t, HEAD_DIM), lambda bi, h, i: (bi, 0, cv + h)),
                  pl.BlockSpec((None,) + bias.shape[1:], lambda bi, h, i: (h, 0, 0)),
                  pl.BlockSpec(lam_vec.shape, lambda bi, h, i: (0, 0)),
                  pl.BlockSpec((1, HEAD_DIM), lambda bi, h, i: (0, 0))],
        out_specs=pl.BlockSpec((None, tq, HEAD_DIM), lambda bi, h, i: (bi, i, h)),
        out_shape=jax.ShapeDtypeStruct((b, t, n_heads * HEAD_DIM), BF16),
        scratch_shapes=[pltpu.VMEM((2, t // tk, tq, tk), F32)],
        compiler_params=_params("parallel", "parallel", "arbitrary"),
        name="diff_prompt",
    )(p16, p16, p16, bias, lam_vec, g_subln.reshape(1, HEAD_DIM))


def _fox_prompt_body(q_ref, k_ref, v_ref, cum_ref, cumt_ref, o_ref, s_scr, *, tq, tk):
    h = pl.program_id(1)
    i = pl.program_id(2)
    q = q_ref[...]
    cum = cum_ref[...]
    lane = lax.broadcasted_iota(jnp.int32, cum.shape, 1)
    cq = jnp.sum(jnp.where(lane == h, cum, 0.0), axis=-1, keepdims=True)
    ahead = lax.broadcasted_iota(jnp.int32, (tq, tk), 1) - lax.broadcasted_iota(jnp.int32, (tq, tk), 0)

    def scores(j):
        start = pl.multiple_of(j * tk, tk)
        ck = cumt_ref[:, pl.ds(start, tk)]
        s = lax.dot_general(q, k_ref[pl.ds(start, tk), :], NT_DIMS, preferred_element_type=F32)
        s = s * HEAD_DIM ** -0.5 + (cq - ck)
        return (jnp.where(ahead <= i * tq - j * tk, s, NEG_INF),)

    def values(j):
        return v_ref[pl.ds(pl.multiple_of(j * tk, tk), tk), :]

    o, = _masked_softmax_pv(i // (tk // tq) + 1, scores, values, s_scr)
    o_ref[...] = o.astype(o_ref.dtype)


def _fox_prompt(p16, col_q, col_k, col_v, n_heads, cum):
    b, t, _ = p16.shape
    tq, tk = _prompt_tiles(t)
    cq, ck, cv = col_q // HEAD_DIM, col_k // HEAD_DIM, col_v // HEAD_DIM
    cum_t = jnp.swapaxes(cum, 1, 2).reshape(b, n_heads, 1, t)
    return pl.pallas_call(
        functools.partial(_fox_prompt_body, tq=tq, tk=tk),
        grid=(b, n_heads, t // tq),
        in_specs=[pl.BlockSpec((None, tq, HEAD_DIM), lambda bi, h, i: (bi, i, cq + h)),
                  pl.BlockSpec((None, t, HEAD_DIM), lambda bi, h, i: (bi, 0, ck + h)),
                  pl.BlockSpec((None, t, HEAD_DIM), lambda bi, h, i: (bi, 0, cv + h)),
                  pl.BlockSpec((None, tq, n_heads), lambda bi, h, i: (bi, i, 0)),
                  pl.BlockSpec((None, None, 1, t), lambda bi, h, i: (bi, h, 0, 0))],
        out_specs=pl.BlockSpec((None, tq, HEAD_DIM), lambda bi, h, i: (bi, i, h)),
        out_shape=jax.ShapeDtypeStruct((b, t, n_heads * HEAD_DIM), BF16),
        scratch_shapes=[pltpu.VMEM((1, t // tk, tq, tk), F32)],
        compiler_params=_params("parallel", "parallel", "arbitrary"),
        name="fox_prompt",
    )(p16, p16, p16, cum, cum_t)


def _top_blocks(gate, n_top, limit, axis):
    n_blk = gate.shape[axis]
    blk_id = lax.broadcasted_iota(jnp.int32, gate.shape, axis).astype(F32)
    chosen = jnp.zeros(gate.shape, F32)
    g = gate
    for _ in range(n_top):
        mx = jnp.max(g, axis=axis, keepdims=True)
        idx = jnp.min(jnp.where(g == mx, blk_id, float(n_blk)), axis=axis, keepdims=True)
        pick = blk_id == idx
        chosen = jnp.where(pick & (idx < limit), 1.0, chosen)
        g = jnp.where(pick, NEG_INF, g)
    return chosen


def _top_blocks_negmask(gate, n_top, limit):
    return jnp.where(_top_blocks(gate, n_top, limit, 1) > 0.0, 0.0, NEG_INF)


def _moba_prompt_body(q_ref, k_ref, v_ref, q32_ref, k32_ref, bias_ref, o_ref, keep_scr, s_scr, *, n_blk, tk):
    blk = MOBA_BLOCK
    ratio = tk // blk
    i = pl.program_id(2)

    @pl.when(i == 0)
    def _():
        kmean = jnp.mean(k32_ref[...].reshape(n_blk, blk, HEAD_DIM), axis=1)
        gate = lax.dot_general(kmean, q32_ref[...], NT_DIMS, preferred_element_type=F32,
                               precision=lax.Precision.HIGHEST)
        blk_id = lax.broadcasted_iota(jnp.int32, gate.shape, 0)
        own = lax.broadcasted_iota(jnp.int32, gate.shape, 1) // blk
        gate = jnp.where(blk_id < own, gate, NEG_INF)
        chosen = _top_blocks(gate, min(MOBA_TOPK, n_blk), own[0:1].astype(F32), 0)
        keep_scr[...] = jnp.where(blk_id == own, 1.0, chosen)

    keep_t = keep_scr[:, pl.ds(pl.multiple_of(i * blk, blk), blk)].astype(BF16)
    eye = jnp.where(lax.broadcasted_iota(jnp.int32, (blk, blk), 0) == lax.broadcasted_iota(jnp.int32, (blk, blk), 1),
                    1.0, 0.0).astype(BF16)
    sel = jnp.where(lax.dot_general(eye, keep_t, NT_DIMS, preferred_element_type=F32) > 0.5, 0.0, NEG_INF)
    lane = lax.broadcasted_iota(jnp.int32, sel.shape, 1)
    q = q_ref[...]

    def scores(j):
        ks = k_ref[pl.ds(pl.multiple_of(j * tk, tk), tk), :]
        s = lax.dot_general(q, ks, NT_DIMS, preferred_element_type=F32) * HEAD_DIM ** -0.5
        s = s + _bias_window(bias_ref, i - ratio * j, blk, tk)
        parts = []
        for c in range(ratio):
            keep = jnp.min(jnp.where(lane == j * ratio + c, sel, 0.0), axis=-1, keepdims=True)
            parts.append(s[:, c * blk:(c + 1) * blk] + keep)
        return (parts[0] if ratio == 1 else jnp.concatenate(parts, axis=1),)

    def values(j):
        return v_ref[pl.ds(pl.multiple_of(j * tk, tk), tk), :]

    o, = _masked_softmax_pv(i // ratio + 1, scores, values, s_scr)
    o_ref[...] = o.astype(o_ref.dtype)


def _moba_prompt(p16, q32, k32, col_q, col_k, col_v, n_heads, tab):
    b, t, _ = p16.shape
    blk = MOBA_BLOCK
    assert t % blk == 0
    tk = min(t, 4 * blk)
    assert t % tk == 0
    bias = _chunk_bias(tab, blk, tk)
    cq, ck, cv = col_q // HEAD_DIM, col_k // HEAD_DIM, col_v // HEAD_DIM
    return pl.pallas_call(
        functools.partial(_moba_prompt_body, n_blk=t // blk, tk=tk),
        grid=(b, n_heads, t // blk),
        in_specs=[pl.BlockSpec((None, blk, HEAD_DIM), lambda bi, h, i: (bi, i, cq + h)),
                  pl.BlockSpec((None, t, HEAD_DIM), lambda bi, h, i: (bi, 0, ck + h)),
                  pl.BlockSpec((None, t, HEAD_DIM), lambda bi, h, i: (bi, 0, cv + h)),
                  pl.BlockSpec((None, t, HEAD_DIM), lambda bi, h, i: (bi, 0, h)),
                  pl.BlockSpec((None, t, HEAD_DIM), lambda bi, h, i: (bi, 0, h)),
                  pl.BlockSpec((None,) + bias.shape[1:], lambda bi, h, i: (h, 0, 0))],
        out_specs=pl.BlockSpec((None, blk, HEAD_DIM), lambda bi, h, i: (bi, i, h)),
        out_shape=jax.ShapeDtypeStruct((b, t, n_heads * HEAD_DIM), BF16),
        scratch_shapes=[pltpu.VMEM((t // blk, t), F32), pltpu.VMEM((1, t // tk, blk, tk), F32)],
        compiler_params=_params("parallel", "parallel", "arbitrary"),
        name="moba_prompt",
    )(p16, p16, p16, q32, k32, bias)


KEY_SIGN = -2 ** 31
KEY_OF_NEG_INF = -2139095041


def _order_key(score):
    bits = pltpu.bitcast(score, jnp.int32)
    key = jnp.where(bits < 0, bits ^ 0x7FFFFFFF, bits)
    return jnp.where(score == 0.0, 0, key)


def _kth_largest_key(count_ge, n_rows, k):
    def bit_body(b, ans):
        cand = ans | jnp.left_shift(jnp.int32(1), 31 - b)
        cnt, = count_ge([cand ^ KEY_SIGN])
        return jnp.where(cnt >= k, cand, ans)

    ans = lax.fori_loop(0, 32, bit_body, jnp.zeros((n_rows, 1), jnp.int32))
    return ans ^ KEY_SIGN


def _dsa_prompt_body(iq_ref, ikw_ref, kidx_ref, qb_ref, kb_ref, vb_ref, bias_ref, o_ref, key_scr, nm_scr, w_scr,
                     s_scr, *, tq, tk, n_heads, n_sel):
    i = pl.program_id(1)
    n_chunks = i + 1
    iq = iq_ref[...].reshape(N_IDX_HEADS * tq, IDX_DIM)
    w = ikw_ref[:, IDX_DIM:IDX_DIM + N_IDX_HEADS] * N_IDX_HEADS ** -0.5 * IDX_DIM ** -0.5
    for n in range(N_IDX_HEADS):
        w_scr[n] = jnp.broadcast_to(w[:, n:n + 1], (tq, tq))
    row = lax.broadcasted_iota(jnp.int32, (tq, tq), 0)
    col = lax.broadcasted_iota(jnp.int32, (tq, tq), 1)

    def score_body(j, _):
        kc = kidx_ref[pl.ds(pl.multiple_of(j * tq, tq), tq), :][:, :IDX_DIM]
        rel = jnp.maximum(lax.dot_general(iq, kc, NT_DIMS, preferred_element_type=F32), 0.0)
        rel = rel.reshape(N_IDX_HEADS, tq, tq)
        sc = w_scr[0] * rel[0]
        for n in range(1, N_IDX_HEADS):
            sc = sc + w_scr[n] * rel[n]
        sc = jnp.where((j < i) | (col <= row), sc, NEG_INF)
        key_scr[j] = _order_key(sc)
        return 0

    lax.fori_loop(0, n_chunks, score_body, 0)

    def count_many(preds):
        def body(j, accs):
            k = key_scr[j]
            return tuple(acc + jnp.where(pred(k), 1.0, 0.0) for acc, pred in zip(accs, preds))
        accs = lax.fori_loop(0, n_chunks, body, tuple(jnp.zeros((tq, tq), F32) for _ in preds))
        return [jnp.sum(acc, axis=-1, keepdims=True) for acc in accs]

    def count(pred):
        return count_many([pred])[0]

    thr = _kth_largest_key(lambda ts: count_many([(lambda k, t=t: k >= t) for t in ts]), tq, n_sel)
    cnt_ge = count(lambda k: k >= thr)
    tie = jnp.max(jnp.where((cnt_ge > n_sel) & (thr > KEY_OF_NEG_INF), 1.0, 0.0)) > 0.0

    @pl.when(jnp.logical_not(tie))
    def _():
        def body(j, _):
            nm_scr[j] = jnp.where(key_scr[j] >= thr, 0.0, NEG_INF)
            return 0
        lax.fori_loop(0, n_chunks, body, 0)

    @pl.when(tie)
    def _():
        allow = n_sel - count(lambda k: k > thr)
        tri = jnp.where(row <= col, 1.0, 0.0).astype(BF16)

        def body(j, before):
            k = key_scr[j]
            eq = jnp.where(k == thr, 1.0, 0.0)
            rank = jnp.dot(eq.astype(BF16), tri, preferred_element_type=F32) + before
            keep = jnp.where(k > thr, 1.0, jnp.where(rank <= allow, eq, 0.0))
            nm_scr[j] = jnp.where(keep > 0.0, 0.0, NEG_INF)
            return before + jnp.sum(eq, axis=-1, keepdims=True)
        lax.fori_loop(0, n_chunks, body, jnp.zeros((tq, 1), F32))

    qb = qb_ref[...]
    qs = jnp.concatenate([qb[:, h * HEAD_DIM:(h + 1) * HEAD_DIM] for h in range(n_heads)], axis=0)

    ratio = tk // tq
    n_wide = i // ratio + 1

    def clear(j, _):
        nm_scr[j] = jnp.zeros((tq, tq), F32)
        return 0

    lax.fori_loop(n_chunks, n_wide * ratio, clear, 0)

    def scores(j):
        rows = pl.ds(pl.multiple_of(j * tk, tk), tk)
        s = lax.dot_general(qs, kb_ref[rows, :], NT_DIMS, preferred_element_type=F32) * HEAD_DIM ** -0.5
        nm = jnp.concatenate([nm_scr[j * ratio + c] for c in range(ratio)], axis=1) if ratio > 1 else nm_scr[j]
        return (s + _bias_window(bias_ref, i - ratio * j, tq, tk) + jnp.tile(nm, (n_heads, 1)),)

    def values(j):
        return vb_ref[pl.ds(pl.multiple_of(j * tk, tk), tk), :]

    o, = _masked_softmax_pv(n_wide, scores, values, s_scr)
    for h in range(n_heads):
        o_ref[:, h * HEAD_DIM:(h + 1) * HEAD_DIM] = o[h * tq:(h + 1) * tq].astype(o_ref.dtype)


def _dsa_prompt(p16, small32, iq_t, col_qb, col_kb, col_vb, col_ik, col_ik32, n_heads, tab, n_sel):
    b, t, _ = p16.shape
    tq = min(t, 128)
    tk = min(t, 4 * tq)
    assert t % tk == 0 and tk % tq == 0
    bias = _chunk_bias(tab, tq, tk)
    bias = bias.reshape(n_heads * tq, bias.shape[-1])
    qw = n_heads * HEAD_DIM
    assert col_qb % qw == 0 and col_ik % LANES == 0 and col_ik32 % LANES == 0
    return pl.pallas_call(
        functools.partial(_dsa_prompt_body, tq=tq, tk=tk, n_heads=n_heads, n_sel=n_sel),
        grid=(b, t // tq),
        in_specs=[pl.BlockSpec((None, N_IDX_HEADS, tq, IDX_DIM), lambda bi, i: (bi, 0, i, 0)),
                  pl.BlockSpec((None, tq, LANES), lambda bi, i: (bi, i, col_ik32 // LANES)),
                  pl.BlockSpec((None, t, LANES), lambda bi, i: (bi, 0, col_ik // LANES)),
                  pl.BlockSpec((None, tq, qw), lambda bi, i: (bi, i, col_qb // qw)),
                  pl.BlockSpec((None, t, HEAD_DIM), lambda bi, i: (bi, 0, col_kb // HEAD_DIM)),
                  pl.BlockSpec((None, t, HEAD_DIM), lambda bi, i: (bi, 0, col_vb // HEAD_DIM)),
                  pl.BlockSpec(bias.shape, lambda bi, i: (0, 0))],
        out_specs=pl.BlockSpec((None, tq, qw), lambda bi, i: (bi, i, 0)),
        out_shape=jax.ShapeDtypeStruct((b, t, qw), BF16),
        scratch_shapes=[pltpu.VMEM((t // tq, tq, tq), jnp.int32), pltpu.VMEM((t // tq, tq, tq), F32),
                        pltpu.VMEM((N_IDX_HEADS, tq, tq), F32), pltpu.VMEM((1, t // tk, n_heads * tq, tk), F32)],
        compiler_params=_params("parallel", "arbitrary"),
        name="dsa_prompt",
    )(iq_t, small32, p16, p16, p16, p16, bias)


def _block_diag_rows(q, dtype):
    b, tn, g, d = q.shape
    eye = jnp.eye(g, dtype=q.dtype)
    return jnp.einsum('btgd,gk->bgtkd', q, eye).reshape(b, g * tn, g * d).astype(dtype)


def _page_specs(layer, n_pages, group, rows, width):
    def spec(g):
        return pl.BlockSpec((None, None, rows, width),
                            lambda bi, p, pt: (layer, pt[bi, jnp.minimum(p * group + g, n_pages - 1)], 0, 0))
    return [spec(g) for g in range(group)]


def _past_bias_index(page_idx, n_pages):
    return jnp.clip(page_idx - (n_pages - 2), 0, 1)


def _paged_call(body, pt, n_steps, operands, in_specs, out_shape, out_spec, scratch, name):
    return pl.pallas_call(
        body,
        grid_spec=pltpu.PrefetchScalarGridSpec(
            num_scalar_prefetch=1, grid=(pt.shape[0], n_steps),
            in_specs=in_specs, out_specs=out_spec, scratch_shapes=scratch),
        out_shape=out_shape,
        compiler_params=_params("parallel", "arbitrary"),
        name=name,
    )(pt, *operands)


def _per_batch(rows, width):
    return pl.BlockSpec((None, rows, width), lambda bi, p, pt: (bi, 0, 0))


def _whole(shape):
    return pl.BlockSpec(shape, lambda bi, p, pt: (0,) * len(shape))


def _rows_page(x, page):
    b, tn, w = x.shape
    return jnp.pad(x, ((0, 0), (0, page - tn), (0, 0))).reshape(b, page * (w // HEAD_DIM), HEAD_DIM)


def _transposed_page(x, page):
    return jnp.pad(jnp.swapaxes(x, 1, 2), ((0, 0), (0, 0), (0, page - x.shape[1])))


def _flash_scratch(rows, width):
    return [pltpu.VMEM((rows, 1), F32), pltpu.VMEM((rows, 1), F32), pltpu.VMEM((rows, width), F32)]


def _flash_init(m_scr, l_scr, acc_scr):
    m_scr[...] = jnp.full(m_scr.shape, NEG_INF, F32)
    l_scr[...] = jnp.zeros(l_scr.shape, F32)
    acc_scr[...] = jnp.zeros(acc_scr.shape, F32)


def _flash_update(s, pv, m_scr, l_scr, acc_scr):
    m = m_scr[...]
    m_new = jnp.maximum(m, jnp.max(s, axis=-1, keepdims=True))
    m_safe = jnp.where(m_new == NEG_INF, 0.0, m_new)
    p = jnp.exp(s - m_safe)
    alpha = jnp.exp(m - m_safe)
    l_scr[...] = alpha * l_scr[...] + jnp.sum(p, axis=-1, keepdims=True)
    acc_scr[...] = alpha * acc_scr[...] + pv(p.astype(BF16))
    m_scr[...] = m_new


def _cat16(refs, axis):
    parts = [r[...].astype(BF16) for r in refs]
    return parts[0] if len(parts) == 1 else jnp.concatenate(parts, axis=axis)


SAMPLE_PAGE_GROUP = 16
SMALL_PAGE_GROUP = 16
MERGE_UNROLL = 8

def _heads_first(o, b, n_heads, tn):
    return jnp.swapaxes(o.reshape(b, n_heads, tn, HEAD_DIM), 1, 2).reshape(b, tn, n_heads * HEAD_DIM)


def _diff_sample_body(pt_ref, wq_ref, bias_ref, lam_ref, g_ref, kn_ref, vn_ref, *rest,
                      n_pages, group, page, n_new, n_heads, lam_init):
    k_refs, v_refs = rest[:group], rest[group:2 * group]
    o_ref, m_scr, l_scr, acc_scr = rest[2 * group:]
    p = pl.program_id(1)
    rows_h = 2 * n_new

    @pl.when(p == 0)
    def _():
        _flash_init(m_scr, l_scr, acc_scr)

    def step(ks, vs, bias):
        s = jnp.dot(wq_ref[...], _cat16(ks, 1), preferred_element_type=F32) * DH_A ** -0.5 + bias

        def pv(p16):
            outs = []
            for h in range(n_heads):
                vh = [v[pl.ds(h, page, stride=n_heads), :].astype(BF16) for v in vs]
                vh = vh[0] if len(vh) == 1 else jnp.concatenate(vh, axis=0)
                outs.append(jnp.dot(p16[h * rows_h:(h + 1) * rows_h], vh, preferred_element_type=F32))
            return jnp.concatenate(outs, axis=0)

        _flash_update(s, pv, m_scr, l_scr, acc_scr)

    @pl.when(p < n_pages // group)
    def _():
        tiles = [bias_ref[_past_bias_index(p * group + g, n_pages)] for g in range(group)]
        step(k_refs, v_refs, jnp.concatenate(tiles, axis=1))

    @pl.when(p == n_pages // group)
    def _():
        step([kn_ref], [vn_ref], bias_ref[2])
        lam = _diff_lambda(lam_ref, lam_init)
        on = acc_scr[...] / l_scr[...]
        for h in range(n_heads):
            r0 = h * rows_h
            o = on[r0:r0 + n_new] - lam * on[r0 + n_new:r0 + rows_h]
            o_ref[h * n_new:(h + 1) * n_new, :] = _rms(o, g_ref[...]) * (1.0 - lam_init)


def _diff_sample(pt, layer, q, cache_kt, cache_v, k_new, v_new, tab, lam_vec, g_subln, lam_init):
    b, tn, n_heads = q.shape[:3]
    n_pages = pt.shape[1]
    width, page = cache_kt.shape[2:]
    group = min(SAMPLE_PAGE_GROUP, n_pages)
    assert n_pages % group == 0
    wq = _block_diag_rows(q.reshape(b, tn, 2 * n_heads, DH_A), BF16)
    rows = 2 * n_heads * tn
    bias = _sample_bias(tab, tn, page)
    bias = jnp.broadcast_to(bias[:, :, None], (3, n_heads, 2, tn, page)).reshape(3, rows, page)
    out = _paged_call(
        functools.partial(_diff_sample_body, n_pages=n_pages, group=group, page=page, n_new=tn, n_heads=n_heads,
                          lam_init=lam_init),
        pt, n_pages // group + 1,
        (wq, bias, lam_vec, g_subln.reshape(1, HEAD_DIM), _transposed_page(k_new, page), _rows_page(v_new, page))
        + (cache_kt,) * group + (cache_v,) * group,
        [_per_batch(rows, width), _whole(bias.shape), _whole(lam_vec.shape), _whole((1, HEAD_DIM)),
         _per_batch(width, page), _per_batch(page * n_heads, HEAD_DIM)]
        + _page_specs(layer, n_pages, group, width, page) + _page_specs(layer, n_pages, group, page * n_heads, HEAD_DIM),
        jax.ShapeDtypeStruct((b, n_heads * tn, HEAD_DIM), F32), _per_batch(n_heads * tn, HEAD_DIM),
        _flash_scratch(rows, HEAD_DIM), "diff_sample")
    return _heads_first(out, b, n_heads, tn)


def _fox_sample_body(pt_ref, q_ref, cq_ref, ckp_ref, ckn_ref, hm_ref, nm_ref, kn_ref, vn_ref, *rest,
                     n_pages, group):
    k_refs, v_refs = rest[:group], rest[group:2 * group]
    o_ref, m_scr, l_scr, acc_scr = rest[2 * group:]
    p = pl.program_id(1)

    @pl.when(p == 0)
    def _():
        _flash_init(m_scr, l_scr, acc_scr)

    def step(ks, vs, ck, mask):
        s = lax.dot_general(q_ref[...], _cat16(ks, 0), NT_DIMS, preferred_element_type=F32)
        s = s * HEAD_DIM ** -0.5 + (cq_ref[...] - ck) + mask
        _flash_update(s, lambda p16: jnp.dot(p16, _cat16(vs, 0), preferred_element_type=F32), m_scr, l_scr, acc_scr)

    @pl.when(p < n_pages // group)
    def _():
        step(k_refs, v_refs, ckp_ref[...], hm_ref[...])

    @pl.when(p == n_pages // group)
    def _():
        step([kn_ref], [vn_ref], ckn_ref[...], nm_ref[...])
        o_ref[...] = acc_scr[...] / l_scr[...]


def _fox_sample(pt, layer, q, cache_k, cache_v, k_new, v_new, cum_q, cum_past, cum_new):
    b, tn, n_heads = q.shape[:3]
    n_pages = pt.shape[1]
    prow = cache_k.shape[2]
    page = prow // n_heads
    group = min(SAMPLE_PAGE_GROUP, n_pages)
    assert n_pages % group == 0
    rows = n_heads * tn
    q_rows = jnp.swapaxes(q, 1, 2).reshape(b, rows, HEAD_DIM).astype(BF16)
    cq = jnp.swapaxes(cum_q, 1, 2).reshape(b, rows, 1)
    ck_past = cum_past.reshape(b, n_pages // group, 1, group * prow)
    ck_new = jnp.pad(cum_new, ((0, 0), (0, page - tn), (0, 0))).reshape(b, 1, prow)
    head_mask = _expand_heads(jnp.zeros((n_heads, tn, page), F32))
    qi = np.arange(tn)[:, None]
    ci = np.arange(page)[None, :]
    causal = np.broadcast_to(np.where(ci <= qi, 0.0, NEG_INF).astype(np.float32), (n_heads, tn, page))
    new_mask = _expand_heads(jnp.asarray(causal))
    out = _paged_call(
        functools.partial(_fox_sample_body, n_pages=n_pages, group=group),
        pt, n_pages // group + 1,
        (q_rows, cq, ck_past, ck_new, jnp.tile(head_mask, (1, group)), new_mask,
         _rows_page(k_new, page), _rows_page(v_new, page)) + (cache_k,) * group + (cache_v,) * group,
        [_per_batch(rows, HEAD_DIM), _per_batch(rows, 1),
         pl.BlockSpec((None, None, 1, group * prow),
                      lambda bi, p, pt_: (bi, jnp.minimum(p, n_pages // group - 1), 0, 0)),
         _per_batch(1, prow), _whole((rows, group * prow)), _whole((rows, prow)),
         _per_batch(prow, HEAD_DIM), _per_batch(prow, HEAD_DIM)]
        + _page_specs(layer, n_pages, group, prow, HEAD_DIM) + _page_specs(layer, n_pages, group, prow, HEAD_DIM),
        jax.ShapeDtypeStruct((b, rows, HEAD_DIM), F32), _per_batch(rows, HEAD_DIM),
        _flash_scratch(rows, HEAD_DIM), "fox_sample")
    return _heads_first(out, b, n_heads, tn)


def _moba_blocks_body(pt_ref, q_ref, bias_ref, *rest, n_pages, group, pages_per_block, n_heads):
    k_refs, v_refs = rest[:group], rest[group:2 * group]
    acc_ref, stat_ref, kmean_ref = rest[2 * group:]
    p = pl.program_id(1)
    lane = lax.broadcasted_iota(jnp.int32, stat_ref.shape[1:], 1)
    for blk in range(group // pages_per_block):
        pages = range(blk * pages_per_block, (blk + 1) * pages_per_block)
        parts, total = [], None
        for g in pages:
            k = k_refs[g][...]
            s = lax.dot_general(q_ref[...], k.astype(BF16), NT_DIMS, preferred_element_type=F32) * HEAD_DIM ** -0.5
            parts.append(s + bias_ref[_past_bias_index(p * group + g, n_pages)])
            ksum = jnp.sum(k.reshape(k.shape[0] // n_heads, n_heads, HEAD_DIM), axis=0)
            total = ksum if total is None else total + ksum
        s = parts[0] if len(parts) == 1 else jnp.concatenate(parts, axis=1)
        m = jnp.max(s, axis=-1, keepdims=True)
        prob = jnp.exp(s - m)
        acc_ref[blk] = jnp.dot(prob.astype(BF16), _cat16([v_refs[g] for g in pages], 0), preferred_element_type=F32)
        stat_ref[blk] = jnp.where(lane < LANES // 2, m, jnp.sum(prob, axis=-1, keepdims=True))
        kmean_ref[blk] = total * (1.0 / MOBA_BLOCK)


def _moba_merge_body(wq_ref, kmean_ref, acc_ref, stat_ref, q_ref, kn_ref, vn_ref, bias_ref, o_ref, *, n_blk):
    gate = lax.dot_general(wq_ref[...], kmean_ref[...], NT_DIMS, preferred_element_type=F32,
                           precision=lax.Precision.HIGHEST)
    sel = _top_blocks_negmask(gate, min(MOBA_TOPK, n_blk), float(n_blk))
    lane = lax.broadcasted_iota(jnp.int32, sel.shape, 1)
    s = lax.dot_general(q_ref[...], kn_ref[...].astype(BF16), NT_DIMS, preferred_element_type=F32)
    s = s * HEAD_DIM ** -0.5 + bias_ref[...]
    m_own = jnp.max(s, axis=-1, keepdims=True)
    prob = jnp.exp(s - m_own)

    def block_max(blk):
        return stat_ref[blk][:, 0:1] + jnp.min(jnp.where(lane == blk, sel, 0.0), axis=-1, keepdims=True)

    unroll = math.gcd(n_blk, MERGE_UNROLL)

    def max_step(g, m):
        return functools.reduce(jnp.maximum, [block_max(g * unroll + u) for u in range(unroll)], m)

    m_all = lax.fori_loop(0, n_blk // unroll, max_step, m_own)

    def merge(g, carry):
        num, den = carry
        for u in range(unroll):
            blk = g * unroll + u
            w = jnp.exp(block_max(blk) - m_all)
            num = num + w * acc_ref[blk]
            den = den + w * stat_ref[blk][:, LANES // 2:LANES // 2 + 1]
        return num, den

    w_own = jnp.exp(m_own - m_all)
    num, den = lax.fori_loop(
        0, n_blk // unroll, merge,
        (w_own * jnp.dot(prob.astype(BF16), vn_ref[...].astype(BF16), preferred_element_type=F32),
         w_own * jnp.sum(prob, axis=-1, keepdims=True)))
    o_ref[...] = num / den


def _moba_sample(pt, layer, q, cache_k, cache_v, k_new, v_new, tab):
    b, tn, n_heads = q.shape[:3]
    n_pages = pt.shape[1]
    prow = cache_k.shape[2]
    page = prow // n_heads
    width = n_heads * HEAD_DIM
    assert MOBA_BLOCK % page == 0 and (n_pages * page) % MOBA_BLOCK == 0 and tn < MOBA_BLOCK
    ppb = MOBA_BLOCK // page
    n_blk = n_pages // ppb
    group = min(SAMPLE_PAGE_GROUP, n_pages)
    assert n_pages % group == 0 and group % ppb == 0
    bps = group // ppb
    rows = n_heads * tn
    bias = _sample_bias(tab, tn, page)
    bias = jnp.stack([_expand_heads(bias[i]) for i in range(3)])
    q_rows = jnp.swapaxes(q, 1, 2).reshape(b, rows, HEAD_DIM).astype(BF16)
    per_block = lambda r: pl.BlockSpec((None, bps, r, HEAD_DIM), lambda bi, p, pt_: (bi, p, 0, 0))
    acc, stat, kmean = _paged_call(
        functools.partial(_moba_blocks_body, n_pages=n_pages, group=group, pages_per_block=ppb, n_heads=n_heads),
        pt, n_pages // group,
        (q_rows, bias[:2]) + (cache_k,) * group + (cache_v,) * group,
        [_per_batch(rows, HEAD_DIM), _whole((2,) + bias.shape[1:])]
        + _page_specs(layer, n_pages, group, prow, HEAD_DIM) + _page_specs(layer, n_pages, group, prow, HEAD_DIM),
        [jax.ShapeDtypeStruct((b, n_blk, rows, HEAD_DIM), F32), jax.ShapeDtypeStruct((b, n_blk, rows, LANES), F32),
         jax.ShapeDtypeStruct((b, n_blk, n_heads, HEAD_DIM), F32)],
        [per_block(rows), per_block(rows), per_block(n_heads)], [], "moba_blocks")
    whole_b = lambda *shape: pl.BlockSpec((None,) + shape, lambda bi: (bi,) + (0,) * len(shape))
    out = pl.pallas_call(
        functools.partial(_moba_merge_body, n_blk=n_blk),
        grid=(b,),
        in_specs=[whole_b(rows, width), whole_b(n_blk, width), whole_b(n_blk, rows, HEAD_DIM),
                  whole_b(n_blk, rows, LANES), whole_b(rows, HEAD_DIM), whole_b(prow, HEAD_DIM),
                  whole_b(prow, HEAD_DIM), pl.BlockSpec((rows, prow), lambda bi: (0, 0))],
        out_specs=whole_b(rows, HEAD_DIM),
        out_shape=jax.ShapeDtypeStruct((b, rows, HEAD_DIM), F32),
        compiler_params=_params("parallel"),
        name="moba_merge",
    )(_block_diag_rows(q, F32), kmean.reshape(b, n_blk, width), acc, stat, q_rows,
      _rows_page(k_new, page), _rows_page(v_new, page), bias[2])
    return _heads_first(out, b, n_heads, tn)


def _dsa_score_body(pt_ref, iq_ref, w_ref, mask_ref, kn_ref, *rest, n_pages, group, n_new):
    k_refs, o_ref = rest[:group], rest[group]
    p = pl.program_id(1)

    def score(kt16):
        rel = jnp.dot(iq_ref[...], kt16, preferred_element_type=F32)
        rel = jnp.maximum(rel * IDX_DIM ** -0.5, 0.0) * w_ref[...]
        return jnp.sum(rel.reshape(N_IDX_HEADS, n_new, rel.shape[1]), axis=0)

    @pl.when(p < n_pages // group)
    def _():
        o_ref[...] = score(_cat16(k_refs, 1))

    @pl.when(p == n_pages // group)
    def _():
        page = kn_ref.shape[1]
        o_ref[...] = jnp.full(o_ref.shape, NEG_INF, F32)
        o_ref[:, :page] = score(kn_ref[...].astype(BF16)) + mask_ref[...]


def _dsa_select_body(sc_ref, o_ref, key_scr, *, n_new, n_sel, chunk):
    width = sc_ref.shape[1]
    key_scr[...] = _order_key(sc_ref[...])

    def count(pred):
        return jnp.sum(jnp.where(pred(key_scr[...]), 1.0, 0.0), axis=-1, keepdims=True)

    thr = _kth_largest_key(lambda ts: [count(lambda k, t=t: k >= t) for t in ts], n_new, n_sel)
    cnt_ge = count(lambda k: k >= thr)
    tie = jnp.max(jnp.where((cnt_ge > n_sel) & (thr > KEY_OF_NEG_INF), 1.0, 0.0)) > 0.0

    @pl.when(jnp.logical_not(tie))
    def _():
        o_ref[...] = jnp.where(key_scr[...] >= thr, 0.0, NEG_INF)

    @pl.when(tie)
    def _():
        allow = n_sel - count(lambda k: k > thr)
        r = lax.broadcasted_iota(jnp.int32, (chunk, chunk), 0)
        c = lax.broadcasted_iota(jnp.int32, (chunk, chunk), 1)
        tri = jnp.where(r <= c, 1.0, 0.0).astype(BF16)

        def body(j, before):
            cols = pl.ds(pl.multiple_of(j * chunk, chunk), chunk)
            k = key_scr[:, cols]
            eq = jnp.where(k == thr, 1.0, 0.0)
            rank = jnp.dot(eq.astype(BF16), tri, preferred_element_type=F32) + before
            keep = jnp.where(k > thr, 1.0, jnp.where(rank <= allow, eq, 0.0))
            o_ref[:, cols] = jnp.where(keep > 0.0, 0.0, NEG_INF)
            return before + jnp.sum(eq, axis=-1, keepdims=True)
        lax.fori_loop(0, width // chunk, body, jnp.zeros((n_new, 1), F32))


def _dsa_sample_body(pt_ref, q_ref, bias_ref, nm_ref, kn_ref, vn_ref, *rest, n_pages, group, n_heads):
    k_refs, v_refs = rest[:group], rest[group:2 * group]
    o_ref, m_scr, l_scr, acc_scr = rest[2 * group:]
    p = pl.program_id(1)

    @pl.when(p == 0)
    def _():
        _flash_init(m_scr, l_scr, acc_scr)

    def step(ks, vs, bias, nm):
        s = lax.dot_general(q_ref[...], _cat16(ks, 0), NT_DIMS, preferred_element_type=F32)
        s = s * HEAD_DIM ** -0.5 + bias + jnp.tile(nm, (n_heads, 1))
        _flash_update(s, lambda p16: jnp.dot(p16, _cat16(vs, 0), preferred_element_type=F32), m_scr, l_scr, acc_scr)

    @pl.when(p < n_pages // group)
    def _():
        tiles = [bias_ref[_past_bias_index(p * group + g, n_pages)] for g in range(group)]
        step(k_refs, v_refs, jnp.concatenate(tiles, axis=1), nm_ref[...])

    @pl.when(p == n_pages // group)
    def _():
        page = kn_ref.shape[0]
        step([kn_ref], [vn_ref], bias_ref[2], nm_ref[:, :page])
        o_ref[...] = acc_scr[...] / l_scr[...]


def _dsa_sample(pt, layer, qb, iq, iw, cache_k, cache_v, cache_idx_t, k_new, v_new, ik_new, tab):
    b, tn, n_heads = qb.shape[:3]
    n_pages = pt.shape[1]
    page = cache_k.shape[2]
    n_sel = min(DSA_TOPK, (n_pages * page + tn) // 4)
    group = min(SMALL_PAGE_GROUP, n_pages)
    assert n_pages % group == 0
    n_steps = n_pages // group + 1
    width = n_steps * group * page
    pad = ((0, 0), (0, page - tn), (0, 0))
    qi = np.arange(tn)[:, None]
    ci = np.arange(page)[None, :]
    new_mask = jnp.asarray(np.where(ci <= qi, 0.0, NEG_INF).astype(np.float32))
    n_iq = N_IDX_HEADS * tn
    iq_rows = jnp.swapaxes(iq, 1, 2).reshape(b, n_iq, IDX_DIM).astype(BF16)
    w_rows = (jnp.swapaxes(iw, 1, 2).astype(F32) * N_IDX_HEADS ** -0.5).reshape(b, n_iq, 1)
    step_cols = pl.BlockSpec((None, tn, group * page), lambda bi, p, pt_: (bi, 0, p))
    scores = _paged_call(
        functools.partial(_dsa_score_body, n_pages=n_pages, group=group, n_new=tn),
        pt, n_steps,
        (iq_rows, w_rows, new_mask, _transposed_page(ik_new, page)) + (cache_idx_t,) * group,
        [_per_batch(n_iq, IDX_DIM), _per_batch(n_iq, 1), _whole(new_mask.shape), _per_batch(IDX_DIM, page)]
        + _page_specs(layer, n_pages, group, IDX_DIM, page),
        jax.ShapeDtypeStruct((b, tn, width), F32), step_cols, [], "dsa_score")
    negmask = pl.pallas_call(
        functools.partial(_dsa_select_body, n_new=b * tn, n_sel=n_sel, chunk=page),
        grid=(1,),
        in_specs=[pl.BlockSpec((b * tn, width), lambda i: (0, 0))],
        out_specs=pl.BlockSpec((b * tn, width), lambda i: (0, 0)),
        out_shape=jax.ShapeDtypeStruct((b * tn, width), F32),
        scratch_shapes=[pltpu.VMEM((b * tn, width), jnp.int32)],
        compiler_params=_params("arbitrary"),
        name="dsa_select",
    )(scores.reshape(b * tn, width)).reshape(b, tn, width)
    rows = n_heads * tn
    q_rows = jnp.swapaxes(qb, 1, 2).reshape(b, rows, HEAD_DIM).astype(BF16)
    bias = _sample_bias(tab, tn, page).reshape(3, rows, page)
    out = _paged_call(
        functools.partial(_dsa_sample_body, n_pages=n_pages, group=group, n_heads=n_heads),
        pt, n_steps,
        (q_rows, bias, negmask, jnp.pad(k_new, pad), jnp.pad(v_new, pad)) + (cache_k,) * group + (cache_v,) * group,
        [_per_batch(rows, HEAD_DIM), _whole(bias.shape), step_cols, _per_batch(page, HEAD_DIM),
         _per_batch(page, HEAD_DIM)]
        + _page_specs(layer, n_pages, group, page, HEAD_DIM) + _page_specs(layer, n_pages, group, page, HEAD_DIM),
        jax.ShapeDtypeStruct((b, rows, HEAD_DIM), F32), _per_batch(rows, HEAD_DIM),
        _flash_scratch(rows, HEAD_DIM), "dsa_sample")
    return _heads_first(out, b, n_heads, tn)


def _pad_rows(w, n):
    return jnp.pad(w, ((0, 0),) * (w.ndim - 2) + ((0, n - w.shape[-2]), (0, 0)))


def _mixer_even(h_in, g, w_in16, past, pt, layer_e, lam_vec, g_subln, lam_init, tab, b, t):
    d = h_in.shape[1]
    n_a = n_b = (d // HEAD_DIM) // 2
    wa, wb = n_a * HEAD_DIM, n_b * HEAD_DIM
    wi = N_IDX_HEADS * IDX_DIM
    c_qa, c_ka, c_va, c_qb = 0, wa, 2 * wa, 3 * wa
    c_iq = c_qb + wb
    c_small = c_iq + wi
    c_kb, c_vb, c_ik = c_small, c_small + HEAD_DIM, c_small + 2 * HEAD_DIM
    tn = COL_TILE
    assert wa % tn == 0 and c_small % tn == 0 and w_in16.shape[-2] == c_small + tn
    ka_t = past is None and t % min(b * t, 2 * ROW_TILE) == 0
    ka, va, small, p16 = _rms_matmul(
        h_in, g, w_in16, (layer_e,), emit16=True, w_transposed=True, rows_per_batch=t,
        f32_groups=((c_ka // tn, wa // tn, ka_t), (c_va // tn, wa // tn), (c_small // tn, 1)))
    p16 = p16.reshape(b, t, -1)
    va, small = va.reshape(b, t, wa), small.reshape(b, t, tn)
    if ka_t:
        a_k = jnp.transpose(ka.reshape(b, n_a, 2, DH_A, t), (0, 4, 1, 2, 3))
    else:
        ka = ka.reshape(b, t, wa)
        a_k = ka.reshape(b, t, n_a, 2, DH_A)
    kb = small[..., :HEAD_DIM]
    vb = small[..., HEAD_DIM:2 * HEAD_DIM]
    ik = small[..., 2 * HEAD_DIM:2 * HEAD_DIM + IDX_DIM]
    rows = (a_k, va.reshape(b, t, n_a, HEAD_DIM), kb, vb, ik)
    tab_a, tab_b = tab[:, :n_a], tab[:, n_a:]
    if past is None:
        o_a = _diff_prompt(p16, c_qa, c_ka, c_va, n_a, tab_a, lam_vec, g_subln, lam_init)
        iq_t = jnp.swapaxes(p16[..., c_iq:c_small].reshape(b, t, N_IDX_HEADS, IDX_DIM), 1, 2)
        o_b = _dsa_prompt(p16, small, iq_t, c_qb, c_kb, c_vb, c_ik, 2 * HEAD_DIM, n_b, tab_b,
                          min(DSA_TOPK, t // 4))
    else:
        cache_a_k, cache_a_v, cache_b_k, cache_b_v, cache_b_idx = past
        qa = p16[..., c_qa:c_qa + wa].reshape(b, t, n_a, 2, DH_A)
        o_a = _diff_sample(pt, layer_e, qa, cache_a_k, cache_a_v, ka, va, tab_a, lam_vec, g_subln, lam_init)
        qb = p16[..., c_qb:c_qb + wb].reshape(b, t, n_b, HEAD_DIM)
        iq = p16[..., c_iq:c_small].reshape(b, t, N_IDX_HEADS, IDX_DIM)
        iw = small[..., 2 * HEAD_DIM + IDX_DIM:2 * HEAD_DIM + IDX_DIM + N_IDX_HEADS]
        o_b = _dsa_sample(pt, layer_e, qb, iq, iw, cache_b_k, cache_b_v, cache_b_idx, kb, vb, ik, tab_b)
    return (o_a.reshape(b * t, wa), o_b.reshape(b * t, wb)), rows


def _mixer_odd(h_in, g, w_in16, past, pt, layer_o, b_forget, tab, b, t):
    d = h_in.shape[1]
    n_c = n_d = (d // HEAD_DIM) // 2
    wc, wd = n_c * HEAD_DIM, n_d * HEAD_DIM
    c_qc, c_kc, c_vc, c_qd = 0, wc, 2 * wc, 3 * wc
    c_kd = c_qd + wd
    c_vd = c_kd + wd
    c_fc = c_vd + wd
    tn = COL_TILE
    assert wc % tn == 0 and wd % tn == 0 and w_in16.shape[-2] == c_fc + tn
    groups = tuple((c // tn, wc // tn) for c in (c_kc, c_vc, c_qd, c_kd, c_vd)) + ((c_fc // tn, 1),)
    kc, vc, qd32, kd, vd, small, p16 = _rms_matmul(h_in, g, w_in16, (layer_o,), emit16=True, w_transposed=True,
                                                   f32_groups=groups)
    p16 = p16.reshape(b, t, -1)
    kc, vc, qd32, kd, vd = (a.reshape(b, t, wc) for a in (kc, vc, qd32, kd, vd))
    log_f = jax.nn.log_sigmoid(small.reshape(b, t, tn)[..., :n_c] + b_forget.astype(F32))
    rows = (kc.reshape(b, t, n_c, HEAD_DIM), vc.reshape(b, t, n_c, HEAD_DIM), log_f,
            kd.reshape(b, t, n_d, HEAD_DIM), vd.reshape(b, t, n_d, HEAD_DIM))
    tab_d = tab[:, n_c:]
    if past is None:
        o_c = _fox_prompt(p16, c_qc, c_kc, c_vc, n_c, jnp.cumsum(log_f, axis=1))
        o_d = _moba_prompt(p16, qd32, kd, c_qd, c_kd, c_vd, n_d, tab_d)
    else:
        cache_c_k, cache_c_v, cache_c_logf, cache_d_k, cache_d_v = past
        n_pages = pt.shape[1]
        page = cache_c_logf.shape[2]
        logf_past = cache_c_logf[layer_o][pt].reshape(b, n_pages * page, n_c)
        cum = jnp.cumsum(jnp.concatenate([logf_past, log_f], axis=1).astype(F32), axis=1)
        cum_q = cum[:, n_pages * page:]
        qc = p16[..., c_qc:c_qc + wc].reshape(b, t, n_c, HEAD_DIM)
        o_c = _fox_sample(pt, layer_o, qc, cache_c_k, cache_c_v, kc, vc, cum_q, cum[:, :n_pages * page], cum_q)
        o_d = _moba_sample(pt, layer_o, qd32.reshape(b, t, n_d, HEAD_DIM), cache_d_k, cache_d_v, kd, vd, tab_d)
    return (o_c.reshape(b * t, wc), o_d.reshape(b * t, wd)), rows


def _run_trunk(x, past_even, past_odd, pt, mem_kv, prm):
    b, t, d = x.shape
    x = x.reshape(b * t, d)
    depth = prm['norm_g'].shape[0]
    rows_even, rows_odd = [], []
    for layer in range(depth):
        g = prm['norm_g'][layer]
        wg, wu, wd = prm['w_ffn_gate'], prm['w_ffn_up'], prm['w_ffn_down']
        x = _ffn(x, g[NG_FFN1_PRE], g[NG_FFN1_POST], wg, wu, wd, (layer, 0))
        if layer % 2 == 0:
            e = layer // 2
            lam_init = 0.8 - 0.6 * math.exp(-0.3 * layer)
            parts, rows = _mixer_even(x, g[NG_MIX_PRE], prm['w_in_even'], past_even, pt, e,
                                      prm['diff_lambda'][e].astype(F32), prm['g_subln'][e].astype(F32), lam_init,
                                      prm['t5_table'], b, t)
            rows_even.append(rows)
            x = _out_proj(x, g[NG_MIX_POST], parts, prm['w_out_even'], (e,))
        else:
            o = layer // 2
            parts, rows = _mixer_odd(x, g[NG_MIX_PRE], prm['w_in_odd'], past_odd, pt, o,
                                     prm['b_forget'][o], prm['t5_table'], b, t)
            rows_odd.append(rows)
            x = _out_proj(x, g[NG_MIX_POST], parts, prm['w_out_odd'], (o,))
        mk, mv = mem_kv[layer]
        q = _rms_matmul(x, g[NG_X_PRE], prm['w_xq'], (layer,))
        o_x = _cross_attend(q.reshape(b, t, -1), mk, mv)
        x = _out_proj(x, g[NG_X_POST], [o_x.reshape(b * t, -1)], prm['w_xo'], (layer,))
        x = _ffn(x, g[NG_FFN2_PRE], g[NG_FFN2_POST], wg, wu, wd, (layer, 1))
    return x.reshape(b, t, d), rows_even, rows_odd


def kernel(x_prompt, x_sample, cache_a_k, cache_a_v, cache_b_k, cache_b_v, cache_b_idx, cache_c_k, cache_c_v, cache_c_logf, cache_d_k, cache_d_v, cache_mem_k, cache_mem_v, page_table, mem_prompt, t5_table, norm_g, w_ffn_gate, w_ffn_up, w_ffn_down, w_xq, w_xk, w_xv, w_xo, w_in_even, w_out_even, diff_lambda, g_subln, w_in_odd, w_out_odd, b_forget):
    depth = norm_g.shape[0]
    d_model = x_prompt.shape[-1]
    n_c = (d_model // HEAD_DIM) // 2
    half = n_c * HEAD_DIM
    wi = N_IDX_HEADS * IDX_DIM
    e_cut = 4 * half
    w_e = jnp.swapaxes(w_in_even, 1, 2)
    w_in_even_r = jnp.concatenate([w_e[:, :e_cut], w_e[:, e_cut + 2 * HEAD_DIM:e_cut + 2 * HEAD_DIM + wi],
                                   w_e[:, e_cut:e_cut + 2 * HEAD_DIM], w_e[:, e_cut + 2 * HEAD_DIM + wi:]], axis=1)
    c_fc = 3 * half
    w_o = jnp.swapaxes(w_in_odd, 1, 2)
    w_in_odd_r = jnp.concatenate([w_o[:, :c_fc], w_o[:, c_fc + n_c:], w_o[:, c_fc:c_fc + n_c]], axis=1)
    prm = {
        't5_table': t5_table.astype(F32), 'norm_g': norm_g.astype(F32),
        'w_ffn_gate': w_ffn_gate.astype(BF16), 'w_ffn_up': w_ffn_up.astype(BF16), 'w_ffn_down': w_ffn_down.astype(BF16),
        'w_xq': w_xq.astype(BF16), 'w_xo': w_xo.astype(BF16),
        'w_in_even': _pad_rows(w_in_even_r, e_cut + wi + COL_TILE).astype(BF16),
        'w_out_even': w_out_even.astype(BF16),
        'w_in_odd': _pad_rows(w_in_odd_r, 6 * half + COL_TILE).astype(BF16),
        'w_out_odd': w_out_odd.astype(BF16),
        'diff_lambda': diff_lambda, 'g_subln': g_subln, 'b_forget': b_forget,
    }
    b_p, n_mem, _ = mem_prompt.shape
    hx_w = w_xk.shape[-1]

    mem_kv_p, mem_k_out, mem_v_out = [], [], []
    for l in range(depth):
        w_kv = jnp.concatenate([w_xk[l], w_xv[l]], axis=-1).astype(BF16)
        kv = _rms_matmul(mem_prompt.reshape(b_p * n_mem, d_model), norm_g[l, NG_MEM].astype(F32), w_kv)
        mk = kv[:, :hx_w].reshape(b_p, n_mem, hx_w)
        mv = kv[:, hx_w:].reshape(b_p, n_mem, hx_w)
        mem_kv_p.append((mk, mv))
        mem_k_out.append(mk.reshape(b_p, n_mem, hx_w // HEAD_DIM, HEAD_DIM))
        mem_v_out.append(mv.reshape(b_p, n_mem, hx_w // HEAD_DIM, HEAD_DIM))
    y_prompt, ev_p, od_p = _run_trunk(x_prompt.astype(F32), None, None, None, mem_kv_p, prm)

    def rows_pages(c):
        return c.reshape(c.shape[:2] + (c.shape[2] * c.shape[3], c.shape[4]))

    def transposed_pages(c):
        c = c.reshape(c.shape[:3] + (-1,))
        return jnp.swapaxes(c, 2, 3)

    past_even = (transposed_pages(cache_a_k), rows_pages(cache_a_v), cache_b_k, cache_b_v,
                 transposed_pages(cache_b_idx))
    past_odd = (rows_pages(cache_c_k), rows_pages(cache_c_v), cache_c_logf, rows_pages(cache_d_k),
                rows_pages(cache_d_v))
    b_s = x_sample.shape[0]
    mem_kv_s = [(cache_mem_k[l].reshape(b_s, n_mem, hx_w), cache_mem_v[l].reshape(b_s, n_mem, hx_w))
                for l in range(depth)]
    y_sample, ev_s, od_s = _run_trunk(x_sample.astype(F32), past_even, past_odd, page_table.astype(jnp.int32),
                                      mem_kv_s, prm)

    def stack(rows, i):
        return jnp.stack([r[i] for r in rows])

    out = [y_prompt, y_sample]
    out += [stack(ev_p, i) for i in range(5)] + [stack(od_p, i) for i in range(5)]
    out += [jnp.stack(mem_k_out), jnp.stack(mem_v_out)]
    out += [stack(ev_s, i) for i in range(5)] + [stack(od_s, i) for i in range(5)]
    return tuple(out)
```

```python
import functools
import math

import numpy as np
import jax
import jax.numpy as jnp
from jax import lax
from jax.experimental import pallas as pl
from jax.experimental.pallas import tpu as pltpu

F32 = jnp.float32
BF16 = jnp.bfloat16
NEG_INF = float("-inf")

HEAD_DIM = 128
DH_A = HEAD_DIM // 2
N_IDX_HEADS = 16
IDX_DIM = 64
DSA_TOPK = 256
MOBA_BLOCK = 256
MOBA_TOPK = 3
N_BUCKETS = 32
T5_MAX_EXACT = N_BUCKETS // 2
T5_MAX_DIST = 128
RMS_EPS = 1e-6
NG_FFN1_PRE, NG_FFN1_POST, NG_MIX_PRE, NG_MIX_POST = 0, 1, 2, 3
NG_X_PRE, NG_X_POST, NG_FFN2_PRE, NG_FFN2_POST, NG_MEM = 4, 5, 6, 7, 8

LANES = 128
ROW_TILE = 512
COL_TILE = 512
VMEM_LIMIT = 56 * 1024 * 1024

NT_DIMS = (((1,), (1,)), ((), ()))


def _params(*sem):
    return pltpu.CompilerParams(dimension_semantics=sem, vmem_limit_bytes=VMEM_LIMIT)


def _rms(x, g):
    return x * lax.rsqrt(jnp.mean(x * x, axis=-1, keepdims=True) + RMS_EPS) * g


def _bucket_np(dist):
    n = np.maximum(dist, 0)
    n_f = np.maximum(n, 1).astype(np.float32)
    large = T5_MAX_EXACT + (np.log(n_f / np.float32(T5_MAX_EXACT)) / np.float32(math.log(T5_MAX_DIST / T5_MAX_EXACT))
                            * np.float32(N_BUCKETS - T5_MAX_EXACT)).astype(np.int32)
    return np.where(n < T5_MAX_EXACT, n, np.minimum(large, N_BUCKETS - 1)).astype(np.int32)


def _rel_bias(tab, dists):
    onehot = (_bucket_np(dists)[:, None] == np.arange(N_BUCKETS)[None, :]).astype(np.float32)
    rel = jnp.sum(jnp.asarray(onehot)[:, :, None] * tab[None].astype(F32), axis=1)
    return jnp.where(jnp.asarray(dists >= 0)[:, None], rel, NEG_INF).T


def _toeplitz(u, n_rows, n_cols):
    h = u.shape[0]
    period = n_rows + n_cols
    w = jnp.concatenate([u[:, :n_cols][:, ::-1], jnp.zeros((h, 1), u.dtype), u[:, n_cols:][:, ::-1]], axis=1)
    flat = jnp.tile(w, (1, n_rows))[:, :n_rows * (period - 1)]
    return flat.reshape(h, n_rows, period - 1)[:, :, :n_cols]


def _dist_tile(tab, n_rows, n_cols, offset):
    dists = np.arange(n_rows + n_cols - 1) - (n_cols - 1) + offset
    return _toeplitz(_rel_bias(tab, dists), n_rows, n_cols)


def _far_tile(tab, n_rows, n_cols):
    return jnp.broadcast_to(tab[N_BUCKETS - 1].astype(F32)[:, None, None], (tab.shape[1], n_rows, n_cols))


def _sample_bias(tab, n_new, page):
    assert page + 1 >= T5_MAX_DIST
    return jnp.stack([_far_tile(tab, n_new, page), _dist_tile(tab, n_new, page, page), _dist_tile(tab, n_new, page, 0)])


def _expand_heads(tile):
    h, r, c = tile.shape
    same = jnp.asarray(np.eye(h, dtype=bool))[:, None, None, :]
    return jnp.where(same, tile[:, :, :, None], NEG_INF).reshape(h * r, c * h)


def _rms_matmul_body(x_ref, g_ref, w_ref, *rest, groups, emit16, w_transposed):
    outs, h_scr = rest[:-1], rest[-1]
    j = pl.program_id(1)

    @pl.when(j == 0)
    def _():
        h_scr[...] = _rms(x_ref[...], g_ref[...]).astype(BF16)

    if w_transposed:
        y = lax.dot_general(h_scr[...], w_ref[...], NT_DIMS, preferred_element_type=F32)
    else:
        y = jnp.dot(h_scr[...], w_ref[...], preferred_element_type=F32)
    for (first, count, transposed), o_ref in zip(groups, outs):
        @pl.when((j >= first) & (j < first + count))
        def _():
            o_ref[...] = y.T if transposed else y
    if emit16:
        outs[len(groups)][...] = y.astype(BF16)


def _stacked(lead, *block):
    return (None,) * len(lead) + tuple(block), tuple(lead)


def _rms_matmul(x, g, w16, lead=(), *, f32_groups=None, emit16=False, w_transposed=False, rows_per_batch=None):
    m, d = x.shape
    n = w16.shape[-2] if w_transposed else w16.shape[-1]
    tm = min(m, 2 * ROW_TILE)
    tn = min(n, COL_TILE)
    assert m % tm == 0 and n % tn == 0
    groups = tuple(tuple(grp) + (False,) * (3 - len(grp)) for grp in (f32_groups or ((0, n // tn),)))
    out_shape, out_specs = [], []
    for first, count, transposed in groups:
        col = lambda j, first=first, count=count: jnp.clip(j - first, 0, count - 1)
        if transposed:
            assert rows_per_batch % tm == 0
            per_b = rows_per_batch // tm
            out_shape.append(jax.ShapeDtypeStruct((m // rows_per_batch, count * tn, rows_per_batch), F32))
            out_specs.append(pl.BlockSpec((None, tn, tm), lambda i, j, col=col: (i // per_b, col(j), i % per_b)))
        else:
            out_shape.append(jax.ShapeDtypeStruct((m, count * tn), F32))
            out_specs.append(pl.BlockSpec((tm, tn), lambda i, j, col=col: (i, col(j))))
    if emit16:
        out_shape.append(jax.ShapeDtypeStruct((m, n), BF16))
        out_specs.append(pl.BlockSpec((tm, tn), lambda i, j: (i, j)))
    w_block, w_lead = _stacked(lead, *((tn, d) if w_transposed else (d, tn)))
    w_index = (lambda i, j: w_lead + (j, 0)) if w_transposed else (lambda i, j: w_lead + (0, j))
    res = pl.pallas_call(
        functools.partial(_rms_matmul_body, groups=groups, emit16=emit16, w_transposed=w_transposed),
        grid=(m // tm, n // tn),
        in_specs=[pl.BlockSpec((tm, d), lambda i, j: (i, 0)),
                  pl.BlockSpec((1, d), lambda i, j: (0, 0)),
                  pl.BlockSpec(w_block, w_index)],
        out_specs=out_specs,
        out_shape=out_shape,
        scratch_shapes=[pltpu.VMEM((tm, d), BF16)],
        compiler_params=_params("parallel", "arbitrary"),
        name="rms_matmul",
    )(x, g.reshape(1, d), w16)
    return res if len(res) > 1 else res[0]


def _ffn_body(x_ref, gpre_ref, gpost_ref, wg_ref, wu_ref, wd_ref, o_ref, h_scr, acc_scr):
    j = pl.program_id(1)

    @pl.when(j == 0)
    def _():
        h_scr[...] = _rms(x_ref[...], gpre_ref[...]).astype(BF16)
        acc_scr[...] = jnp.zeros_like(acc_scr)

    h = h_scr[...]
    gate = jnp.dot(h, wg_ref[...], preferred_element_type=F32)
    up = jnp.dot(h, wu_ref[...], preferred_element_type=F32)
    act = (gate * jax.nn.sigmoid(gate) * up).astype(BF16)
    acc_scr[...] += jnp.dot(act, wd_ref[...], preferred_element_type=F32)

    @pl.when(j == pl.num_programs(1) - 1)
    def _():
        o_ref[...] = x_ref[...] + 0.5 * _rms(acc_scr[...], gpost_ref[...])


def _ffn(x, g_pre, g_post, wg16, wu16, wd16, lead):
    m, d = x.shape
    ff = wg16.shape[-1]
    tm = min(m, ROW_TILE)
    tf = min(ff, COL_TILE)
    assert m % tm == 0 and ff % tf == 0
    up_block, w_lead = _stacked(lead, d, tf)
    down_block, _ = _stacked(lead, tf, d)
    return pl.pallas_call(
        _ffn_body,
        grid=(m // tm, ff // tf),
        in_specs=[pl.BlockSpec((tm, d), lambda i, j: (i, 0)),
                  pl.BlockSpec((1, d), lambda i, j: (0, 0)),
                  pl.BlockSpec((1, d), lambda i, j: (0, 0)),
                  pl.BlockSpec(up_block, lambda i, j: w_lead + (0, j)),
                  pl.BlockSpec(up_block, lambda i, j: w_lead + (0, j)),
                  pl.BlockSpec(down_block, lambda i, j: w_lead + (j, 0))],
        out_specs=pl.BlockSpec((tm, d), lambda i, j: (i, 0)),
        out_shape=jax.ShapeDtypeStruct((m, d), F32),
        scratch_shapes=[pltpu.VMEM((tm, d), BF16), pltpu.VMEM((tm, d), F32)],
        compiler_params=_params("parallel", "arbitrary"),
        name="ffn",
    )(x, g_pre.reshape(1, d), g_post.reshape(1, d), wg16, wu16, wd16)


def _out_body(*refs, n_parts):
    x_ref, g_ref = refs[0], refs[1]
    o_refs = refs[2:2 + n_parts]
    w_refs = refs[2 + n_parts:2 + 2 * n_parts]
    out_ref = refs[-1]
    y = None
    for o_ref, w_ref in zip(o_refs, w_refs):
        t = jnp.dot(o_ref[...].astype(BF16), w_ref[...], preferred_element_type=F32)
        y = t if y is None else y + t
    out_ref[...] = x_ref[...] + _rms(y, g_ref[...])


def _out_proj(x, g, parts, w16, lead):
    m, d = x.shape
    tm = min(m, ROW_TILE)
    k = parts[0].shape[1]
    assert m % tm == 0 and all(p.shape[1] == k for p in parts) and len(parts) * k == w16.shape[-2]
    w_block, w_lead = _stacked(lead, k, d)
    in_specs = [pl.BlockSpec((tm, d), lambda i: (i, 0)), pl.BlockSpec((1, d), lambda i: (0, 0))]
    in_specs += [pl.BlockSpec((tm, k), lambda i: (i, 0)) for _ in parts]
    in_specs += [pl.BlockSpec(w_block, lambda i, n=n: w_lead + (n, 0)) for n in range(len(parts))]
    return pl.pallas_call(
        functools.partial(_out_body, n_parts=len(parts)),
        grid=(m // tm,),
        in_specs=in_specs,
        out_specs=pl.BlockSpec((tm, d), lambda i: (i, 0)),
        out_shape=jax.ShapeDtypeStruct((m, d), F32),
        compiler_params=_params("parallel"),
        name="out_proj",
    )(x, g.reshape(1, d), *parts, *([w16] * len(parts)))


def _cross_body(q_ref, k_ref, v_ref, o_ref, *, n_heads):
    q = q_ref[...].astype(BF16)
    k = k_ref[...].astype(BF16)
    v = v_ref[...].astype(BF16)
    for h in range(n_heads):
        sl = slice(h * HEAD_DIM, (h + 1) * HEAD_DIM)
        s = lax.dot_general(q[:, sl], k[:, sl], NT_DIMS, preferred_element_type=F32) * HEAD_DIM ** -0.5
        p = jnp.exp(s - jnp.max(s, axis=-1, keepdims=True))
        l = jnp.sum(p, axis=-1, keepdims=True)
        o = jnp.dot(p.astype(BF16), v[:, sl], preferred_element_type=F32) / l
        o_ref[:, sl] = o.astype(o_ref.dtype)


def _cross_attend(q, mem_k, mem_v):
    b, t, w = q.shape
    n_mem = mem_k.shape[1]
    tq = min(t, ROW_TILE)
    assert t % tq == 0
    return pl.pallas_call(
        functools.partial(_cross_body, n_heads=w // HEAD_DIM),
        grid=(b, t // tq),
        in_specs=[pl.BlockSpec((None, tq, w), lambda bi, i: (bi, i, 0)),
                  pl.BlockSpec((None, n_mem, w), lambda bi, i: (bi, 0, 0)),
                  pl.BlockSpec((None, n_mem, w), lambda bi, i: (bi, 0, 0))],
        out_specs=pl.BlockSpec((None, tq, w), lambda bi, i: (bi, i, 0)),
        out_shape=jax.ShapeDtypeStruct((b, t, w), F32),
        compiler_params=_params("parallel", "parallel"),
        name="cross_attend",
    )(q, mem_k, mem_v)


def _diff_lambda(lam_ref, lam_init):
    lv = lam_ref[...]
    return (jnp.exp(jnp.sum(lv[0:1] * lv[1:2], axis=-1, keepdims=True))
            - jnp.exp(jnp.sum(lv[2:3] * lv[3:4], axis=-1, keepdims=True)) + lam_init)


def _fold_lanes(x, op):
    parts = [x[:, c * LANES:(c + 1) * LANES] for c in range(x.shape[1] // LANES)]
    return functools.reduce(op, parts)


def _masked_softmax_pv(n_chunks, scores, values, s_scr):
    n_maps, _, rows, _ = s_scr.shape

    def first(j, mx):
        out = []
        for a, s in enumerate(scores(j)):
            s_scr[a, j] = s
            out.append(jnp.maximum(mx[a], _fold_lanes(s, jnp.maximum)))
        return tuple(out)

    mx = lax.fori_loop(0, n_chunks, first, tuple(jnp.full((rows, LANES), NEG_INF, F32) for _ in range(n_maps)))
    m = [jnp.max(x, axis=-1, keepdims=True) for x in mx]

    def second(j, carry):
        v = values(j)
        out = []
        for a in range(n_maps):
            l, acc = carry[a]
            p = jnp.exp(s_scr[a, j] - m[a])
            out.append((l + _fold_lanes(p, jnp.add), acc + jnp.dot(p.astype(BF16), v, preferred_element_type=F32)))
        return tuple(out)

    init = tuple((jnp.zeros((rows, LANES), F32), jnp.zeros((rows, HEAD_DIM), F32)) for _ in range(n_maps))
    res = lax.fori_loop(0, n_chunks, second, init)
    return [acc / jnp.sum(l, axis=-1, keepdims=True) for l, acc in res]


def _prompt_tiles(t):
    tq = min(t, 256)
    tk = min(t, 1024)
    assert t % tk == 0 and tk % tq == 0
    return tq, tk


def _chunk_bias(tab, tq, tk):
    assert tq >= T5_MAX_DIST and tk % tq == 0
    n_far = tk // tq + 1
    n_heads = tab.shape[1]
    far = _far_tile(tab, tq, (n_far - 1) * tq)
    near = _dist_tile(tab, tq, 2 * tq, tq)
    beyond = jnp.full((n_heads, tq, tk - tq), NEG_INF, F32)
    return jnp.concatenate([far, near, beyond], axis=2)


def _bias_window(bias_ref, k, tq, tk):
    n_far = tk // tq + 1
    start = pl.multiple_of((n_far - jnp.minimum(k, n_far)) * tq, tq)
    return bias_ref[:, pl.ds(start, tk)]


def _diff_prompt_body(q_ref, k_ref, v_ref, bias_ref, lam_ref, g_ref, o_ref, s_scr, *, tq, tk, lam_init):
    i = pl.program_id(2)
    ratio = tk // tq
    q = q_ref[...]
    q0, q1 = q[:, :DH_A], q[:, DH_A:]
    scale = DH_A ** -0.5

    def scores(j):
        ks = k_ref[pl.ds(pl.multiple_of(j * tk, tk), tk), :]
        bt = _bias_window(bias_ref, i - ratio * j, tq, tk)
        return (lax.dot_general(q0, ks[:, :DH_A], NT_DIMS, preferred_element_type=F32) * scale + bt,
                lax.dot_general(q1, ks[:, DH_A:], NT_DIMS, preferred_element_type=F32) * scale + bt)

    def values(j):
        return v_ref[pl.ds(pl.multiple_of(j * tk, tk), tk), :]

    o0, o1 = _masked_softmax_pv(i // ratio + 1, scores, values, s_scr)
    o = o0 - _diff_lambda(lam_ref, lam_init) * o1
    o_ref[...] = (_rms(o, g_ref[...]) * (1.0 - lam_init)).astype(o_ref.dtype)


def _diff_prompt(p16, col_q, col_k, col_v, n_heads, tab, lam_vec, g_subln, lam_init):
    b, t, _ = p16.shape
    tq, tk = _prompt_tiles(t)
    bias = _chunk_bias(tab, tq, tk)
    cq, ck, cv = col_q // HEAD_DIM, col_k // HEAD_DIM, col_v // HEAD_DIM
    return pl.pallas_call(
        functools.partial(_diff_prompt_body, tq=tq, tk=tk, lam_init=lam_init),
        grid=(b, n_heads, t // tq),
        in_specs=[pl.BlockSpec((None, tq, HEAD_DIM), lambda bi, h, i: (bi, i, cq + h)),
                  pl.BlockSpec((None, t, HEAD_DIM), lambda bi, h, i: (bi, 0, ck + h)),
                  pl.BlockSpec((None, t, HEAD_DIM), lambda bi, h, i: (bi, 0, cv + h)),
                  pl.BlockSpec((None,) + bias.shape[1:], lambda bi, h, i: (h, 0, 0)),
                  pl.BlockSpec(lam_vec.shape, lambda bi, h, i: (0, 0)),
                  pl.BlockSpec((1, HEAD_DIM), lambda bi, h, i: (0, 0))],
        out_specs=pl.BlockSpec((None, tq, HEAD_DIM), lambda bi, h, i: (bi, i, h)),
        out_shape=jax.ShapeDtypeStruct((b, t, n_heads * HEAD_DIM), BF16),
        scratch_shapes=[pltpu.VMEM((2, t // tk, tq, tk), F32)],
        compiler_params=_params("parallel", "parallel", "arbitrary"),
        name="diff_prompt",
    )(p16, p16, p16, bias, lam_vec, g_subln.reshape(1, HEAD_DIM))


def _fox_prompt_body(q_ref, k_ref, v_ref, cum_ref, cumt_ref, o_ref, s_scr, *, tq, tk):
    h = pl.program_id(1)
    i = pl.program_id(2)
    q = q_ref[...]
    cum = cum_ref[...]
    lane = lax.broadcasted_iota(jnp.int32, cum.shape, 1)
    cq = jnp.sum(jnp.where(lane == h, cum, 0.0), axis=-1, keepdims=True)
    ahead = lax.broadcasted_iota(jnp.int32, (tq, tk), 1) - lax.broadcasted_iota(jnp.int32, (tq, tk), 0)

    def scores(j):
        start = pl.multiple_of(j * tk, tk)
        ck = cumt_ref[:, pl.ds(start, tk)]
        s = lax.dot_general(q, k_ref[pl.ds(start, tk), :], NT_DIMS, preferred_element_type=F32)
        s = s * HEAD_DIM ** -0.5 + (cq - ck)
        return (jnp.where(ahead <= i * tq - j * tk, s, NEG_INF),)

    def values(j):
        return v_ref[pl.ds(pl.multiple_of(j * tk, tk), tk), :]

    o, = _masked_softmax_pv(i // (tk // tq) + 1, scores, values, s_scr)
    o_ref[...] = o.astype(o_ref.dtype)


def _fox_prompt(p16, col_q, col_k, col_v, n_heads, cum):
    b, t, _ = p16.shape
    tq, tk = _prompt_tiles(t)
    cq, ck, cv = col_q // HEAD_DIM, col_k // HEAD_DIM, col_v // HEAD_DIM
    cum_t = jnp.swapaxes(cum, 1, 2).reshape(b, n_heads, 1, t)
    return pl.pallas_call(
        functools.partial(_fox_prompt_body, tq=tq, tk=tk),
        grid=(b, n_heads, t // tq),
        in_specs=[pl.BlockSpec((None, tq, HEAD_DIM), lambda bi, h, i: (bi, i, cq + h)),
                  pl.BlockSpec((None, t, HEAD_DIM), lambda bi, h, i: (bi, 0, ck + h)),
                  pl.BlockSpec((None, t, HEAD_DIM), lambda bi, h, i: (bi, 0, cv + h)),
                  pl.BlockSpec((None, tq, n_heads), lambda bi, h, i: (bi, i, 0)),
                  pl.BlockSpec((None, None, 1, t), lambda bi, h, i: (bi, h, 0, 0))],
        out_specs=pl.BlockSpec((None, tq, HEAD_DIM), lambda bi, h, i: (bi, i, h)),
        out_shape=jax.ShapeDtypeStruct((b, t, n_heads * HEAD_DIM), BF16),
        scratch_shapes=[pltpu.VMEM((1, t // tk, tq, tk), F32)],
        compiler_params=_params("parallel", "parallel", "arbitrary"),
        name="fox_prompt",
    )(p16, p16, p16, cum, cum_t)


def _top_blocks(gate, n_top, limit, axis):
    n_blk = gate.shape[axis]
    blk_id = lax.broadcasted_iota(jnp.int32, gate.shape, axis).astype(F32)
    chosen = jnp.zeros(gate.shape, F32)
    g = gate
    for _ in range(n_top):
        mx = jnp.max(g, axis=axis, keepdims=True)
        idx = jnp.min(jnp.where(g == mx, blk_id, float(n_blk)), axis=axis, keepdims=True)
        pick = blk_id == idx
        chosen = jnp.where(pick & (idx < limit), 1.0, chosen)
        g = jnp.where(pick, NEG_INF, g)
    return chosen


def _top_blocks_negmask(gate, n_top, limit):
    return jnp.where(_top_blocks(gate, n_top, limit, 1) > 0.0, 0.0, NEG_INF)


def _moba_prompt_body(q_ref, k_ref, v_ref, q32_ref, k32_ref, bias_ref, o_ref, keep_scr, s_scr, *, n_blk, tk):
    blk = MOBA_BLOCK
    ratio = tk // blk
    i = pl.program_id(2)

    @pl.when(i == 0)
    def _():
        kmean = jnp.mean(k32_ref[...].reshape(n_blk, blk, HEAD_DIM), axis=1)
        gate = lax.dot_general(kmean, q32_ref[...], NT_DIMS, preferred_element_type=F32,
                               precision=lax.Precision.HIGHEST)
        blk_id = lax.broadcasted_iota(jnp.int32, gate.shape, 0)
        own = lax.broadcasted_iota(jnp.int32, gate.shape, 1) // blk
        gate = jnp.where(blk_id < own, gate, NEG_INF)
        chosen = _top_blocks(gate, min(MOBA_TOPK, n_blk), own[0:1].astype(F32), 0)
        keep_scr[...] = jnp.where(blk_id == own, 1.0, chosen)

    keep_t = keep_scr[:, pl.ds(pl.multiple_of(i * blk, blk), blk)].astype(BF16)
    eye = jnp.where(lax.broadcasted_iota(jnp.int32, (blk, blk), 0) == lax.broadcasted_iota(jnp.int32, (blk, blk), 1),
                    1.0, 0.0).astype(BF16)
    sel = jnp.where(lax.dot_general(eye, keep_t, NT_DIMS, preferred_element_type=F32) > 0.5, 0.0, NEG_INF)
    lane = lax.broadcasted_iota(jnp.int32, sel.shape, 1)
    q = q_ref[...]

    def scores(j):
        ks = k_ref[pl.ds(pl.multiple_of(j * tk, tk), tk), :]
        s = lax.dot_general(q, ks, NT_DIMS, preferred_element_type=F32) * HEAD_DIM ** -0.5
        s = s + _bias_window(bias_ref, i - ratio * j, blk, tk)
        parts = []
        for c in range(ratio):
            keep = jnp.min(jnp.where(lane == j * ratio + c, sel, 0.0), axis=-1, keepdims=True)
            parts.append(s[:, c * blk:(c + 1) * blk] + keep)
        return (parts[0] if ratio == 1 else jnp.concatenate(parts, axis=1),)

    def values(j):
        return v_ref[pl.ds(pl.multiple_of(j * tk, tk), tk), :]

    o, = _masked_softmax_pv(i // ratio + 1, scores, values, s_scr)
    o_ref[...] = o.astype(o_ref.dtype)


def _moba_prompt(p16, q32, k32, col_q, col_k, col_v, n_heads, tab):
    b, t, _ = p16.shape
    blk = MOBA_BLOCK
    assert t % blk == 0
    tk = min(t, 4 * blk)
    assert t % tk == 0
    bias = _chunk_bias(tab, blk, tk)
    cq, ck, cv = col_q // HEAD_DIM, col_k // HEAD_DIM, col_v // HEAD_DIM
    return pl.pallas_call(
        functools.partial(_moba_prompt_body, n_blk=t // blk, tk=tk),
        grid=(b, n_heads, t // blk),
        in_specs=[pl.BlockSpec((None, blk, HEAD_DIM), lambda bi, h, i: (bi, i, cq + h)),
                  pl.BlockSpec((None, t, HEAD_DIM), lambda bi, h, i: (bi, 0, ck + h)),
                  pl.BlockSpec((None, t, HEAD_DIM), lambda bi, h, i: (bi, 0, cv + h)),
                  pl.BlockSpec((None, t, HEAD_DIM), lambda bi, h, i: (bi, 0, h)),
                  pl.BlockSpec((None, t, HEAD_DIM), lambda bi, h, i: (bi, 0, h)),
                  pl.BlockSpec((None,) + bias.shape[1:], lambda bi, h, i: (h, 0, 0))],
        out_specs=pl.BlockSpec((None, blk, HEAD_DIM), lambda bi, h, i: (bi, i, h)),
        out_shape=jax.ShapeDtypeStruct((b, t, n_heads * HEAD_DIM), BF16),
        scratch_shapes=[pltpu.VMEM((t // blk, t), F32), pltpu.VMEM((1, t // tk, blk, tk), F32)],
        compiler_params=_params("parallel", "parallel", "arbitrary"),
        name="moba_prompt",
    )(p16, p16, p16, q32, k32, bias)


KEY_SIGN = -2 ** 31
KEY_OF_NEG_INF = -2139095041


def _order_key(score):
    bits = pltpu.bitcast(score, jnp.int32)
    key = jnp.where(bits < 0, bits ^ 0x7FFFFFFF, bits)
    return jnp.where(score == 0.0, 0, key)


def _kth_largest_key(count_ge, n_rows, k):
    def bit_body(b, ans):
        cand = ans | jnp.left_shift(jnp.int32(1), 31 - b)
        cnt, = count_ge([cand ^ KEY_SIGN])
        return jnp.where(cnt >= k, cand, ans)

    ans = lax.fori_loop(0, 32, bit_body, jnp.zeros((n_rows, 1), jnp.int32))
    return ans ^ KEY_SIGN


def _dsa_prompt_body(iq_ref, ikw_ref, kidx_ref, qb_ref, kb_ref, vb_ref, bias_ref, o_ref, key_scr, nm_scr, w_scr,
                     s_scr, *, tq, tk, n_heads, n_sel):
    i = pl.program_id(1)
    n_chunks = i + 1
    iq = iq_ref[...].reshape(N_IDX_HEADS * tq, IDX_DIM)
    w = ikw_ref[:, IDX_DIM:IDX_DIM + N_IDX_HEADS] * N_IDX_HEADS ** -0.5 * IDX_DIM ** -0.5
    for n in range(N_IDX_HEADS):
        w_scr[n] = jnp.broadcast_to(w[:, n:n + 1], (tq, tq))
    row = lax.broadcasted_iota(jnp.int32, (tq, tq), 0)
    col = lax.broadcasted_iota(jnp.int32, (tq, tq), 1)

    def score_body(j, _):
        kc = kidx_ref[pl.ds(pl.multiple_of(j * tq, tq), tq), :][:, :IDX_DIM]
        rel = jnp.maximum(lax.dot_general(iq, kc, NT_DIMS, preferred_element_type=F32), 0.0)
        rel = rel.reshape(N_IDX_HEADS, tq, tq)
        sc = w_scr[0] * rel[0]
        for n in range(1, N_IDX_HEADS):
            sc = sc + w_scr[n] * rel[n]
        sc = jnp.where((j < i) | (col <= row), sc, NEG_INF)
        key_scr[j] = _order_key(sc)
        return 0

    lax.fori_loop(0, n_chunks, score_body, 0)

    def count_many(preds):
        def body(j, accs):
            k = key_scr[j]
            return tuple(acc + jnp.where(pred(k), 1.0, 0.0) for acc, pred in zip(accs, preds))
        accs = lax.fori_loop(0, n_chunks, body, tuple(jnp.zeros((tq, tq), F32) for _ in preds))
        return [jnp.sum(acc, axis=-1, keepdims=True) for acc in accs]

    def count(pred):
        return count_many([pred])[0]

    thr = _kth_largest_key(lambda ts: count_many([(lambda k, t=t: k >= t) for t in ts]), tq, n_sel)
    cnt_ge = count(lambda k: k >= thr)
    tie = jnp.max(jnp.where((cnt_ge > n_sel) & (thr > KEY_OF_NEG_INF), 1.0, 0.0)) > 0.0

    @pl.when(jnp.logical_not(tie))
    def _():
        def body(j, _):
            nm_scr[j] = jnp.where(key_scr[j] >= thr, 0.0, NEG_INF)
            return 0
        lax.fori_loop(0, n_chunks, body, 0)

    @pl.when(tie)
    def _():
        allow = n_sel - count(lambda k: k > thr)
        tri = jnp.where(row <= col, 1.0, 0.0).astype(BF16)

        def body(j, before):
            k = key_scr[j]
            eq = jnp.where(k == thr, 1.0, 0.0)
            rank = jnp.dot(eq.astype(BF16), tri, preferred_element_type=F32) + before
            keep = jnp.where(k > thr, 1.0, jnp.where(rank <= allow, eq, 0.0))
            nm_scr[j] = jnp.where(keep > 0.0, 0.0, NEG_INF)
            return before + jnp.sum(eq, axis=-1, keepdims=True)
        lax.fori_loop(0, n_chunks, body, jnp.zeros((tq, 1), F32))

    qb = qb_ref[...]
    qs = jnp.concatenate([qb[:, h * HEAD_DIM:(h + 1) * HEAD_DIM] for h in range(n_heads)], axis=0)

    ratio = tk // tq
    n_wide = i // ratio + 1

    def clear(j, _):
        nm_scr[j] = jnp.zeros((tq, tq), F32)
        return 0

    lax.fori_loop(n_chunks, n_wide * ratio, clear, 0)

    def scores(j):
        rows = pl.ds(pl.multiple_of(j * tk, tk), tk)
        s = lax.dot_general(qs, kb_ref[rows, :], NT_DIMS, preferred_element_type=F32) * HEAD_DIM ** -0.5
        nm = jnp.concatenate([nm_scr[j * ratio + c] for c in range(ratio)], axis=1) if ratio > 1 else nm_scr[j]
        return (s + _bias_window(bias_ref, i - ratio * j, tq, tk) + jnp.tile(nm, (n_heads, 1)),)

    def values(j):
        return vb_ref[pl.ds(pl.multiple_of(j * tk, tk), tk), :]

    o, = _masked_softmax_pv(n_wide, scores, values, s_scr)
    for h in range(n_heads):
        o_ref[:, h * HEAD_DIM:(h + 1) * HEAD_DIM] = o[h * tq:(h + 1) * tq].astype(o_ref.dtype)


def _dsa_prompt(p16, small32, iq_t, col_qb, col_kb, col_vb, col_ik, col_ik32, n_heads, tab, n_sel):
    b, t, _ = p16.shape
    tq = min(t, 128)
    tk = min(t, 4 * tq)
    assert t % tk == 0 and tk % tq == 0
    bias = _chunk_bias(tab, tq, tk)
    bias = bias.reshape(n_heads * tq, bias.shape[-1])
    qw = n_heads * HEAD_DIM
    assert col_qb % qw == 0 and col_ik % LANES == 0 and col_ik32 % LANES == 0
    return pl.pallas_call(
        functools.partial(_dsa_prompt_body, tq=tq, tk=tk, n_heads=n_heads, n_sel=n_sel),
        grid=(b, t // tq),
        in_specs=[pl.BlockSpec((None, N_IDX_HEADS, tq, IDX_DIM), lambda bi, i: (bi, 0, i, 0)),
                  pl.BlockSpec((None, tq, LANES), lambda bi, i: (bi, i, col_ik32 // LANES)),
                  pl.BlockSpec((None, t, LANES), lambda bi, i: (bi, 0, col_ik // LANES)),
                  pl.BlockSpec((None, tq, qw), lambda bi, i: (bi, i, col_qb // qw)),
                  pl.BlockSpec((None, t, HEAD_DIM), lambda bi, i: (bi, 0, col_kb // HEAD_DIM)),
                  pl.BlockSpec((None, t, HEAD_DIM), lambda bi, i: (bi, 0, col_vb // HEAD_DIM)),
                  pl.BlockSpec(bias.shape, lambda bi, i: (0, 0))],
        out_specs=pl.BlockSpec((None, tq, qw), lambda bi, i: (bi, i, 0)),
        out_shape=jax.ShapeDtypeStruct((b, t, qw), BF16),
        scratch_shapes=[pltpu.VMEM((t // tq, tq, tq), jnp.int32), pltpu.VMEM((t // tq, tq, tq), F32),
                        pltpu.VMEM((N_IDX_HEADS, tq, tq), F32), pltpu.VMEM((1, t // tk, n_heads * tq, tk), F32)],
        compiler_params=_params("parallel", "arbitrary"),
        name="dsa_prompt",
    )(iq_t, small32, p16, p16, p16, p16, bias)


def _block_diag_rows(q, dtype):
    b, tn, g, d = q.shape
    eye = jnp.eye(g, dtype=q.dtype)
    return jnp.einsum('btgd,gk->bgtkd', q, eye).reshape(b, g * tn, g * d).astype(dtype)


def _page_specs(layer, n_pages, group, rows, width):
    def spec(g):
        return pl.BlockSpec((None, None, rows, width),
                            lambda bi, p, pt: (layer, pt[bi, jnp.minimum(p * group + g, n_pages - 1)], 0, 0))
    return [spec(g) for g in range(group)]


def _past_bias_index(page_idx, n_pages):
    return jnp.clip(page_idx - (n_pages - 2), 0, 1)


def _paged_call(body, pt, n_steps, operands, in_specs, out_shape, out_spec, scratch, name):
    return pl.pallas_call(
        body,
        grid_spec=pltpu.PrefetchScalarGridSpec(
            num_scalar_prefetch=1, grid=(pt.shape[0], n_steps),
            in_specs=in_specs, out_specs=out_spec, scratch_shapes=scratch),
        out_shape=out_shape,
        compiler_params=_params("parallel", "arbitrary"),
        name=name,
    )(pt, *operands)


def _per_batch(rows, width):
    return pl.BlockSpec((None, rows, width), lambda bi, p, pt: (bi, 0, 0))


def _whole(shape):
    return pl.BlockSpec(shape, lambda bi, p, pt: (0,) * len(shape))


def _rows_page(x, page):
    b, tn, w = x.shape
    return jnp.pad(x, ((0, 0), (0, page - tn), (0, 0))).reshape(b, page * (w // HEAD_DIM), HEAD_DIM)


def _transposed_page(x, page):
    return jnp.pad(jnp.swapaxes(x, 1, 2), ((0, 0), (0, 0), (0, page - x.shape[1])))


def _flash_scratch(rows, width):
    return [pltpu.VMEM((rows, 1), F32), pltpu.VMEM((rows, 1), F32), pltpu.VMEM((rows, width), F32)]


def _flash_init(m_scr, l_scr, acc_scr):
    m_scr[...] = jnp.full(m_scr.shape, NEG_INF, F32)
    l_scr[...] = jnp.zeros(l_scr.shape, F32)
    acc_scr[...] = jnp.zeros(acc_scr.shape, F32)


def _flash_update(s, pv, m_scr, l_scr, acc_scr):
    m = m_scr[...]
    m_new = jnp.maximum(m, jnp.max(s, axis=-1, keepdims=True))
    m_safe = jnp.where(m_new == NEG_INF, 0.0, m_new)
    p = jnp.exp(s - m_safe)
    alpha = jnp.exp(m - m_safe)
    l_scr[...] = alpha * l_scr[...] + jnp.sum(p, axis=-1, keepdims=True)
    acc_scr[...] = alpha * acc_scr[...] + pv(p.astype(BF16))
    m_scr[...] = m_new


def _cat16(refs, axis):
    parts = [r[...].astype(BF16) for r in refs]
    return parts[0] if len(parts) == 1 else jnp.concatenate(parts, axis=axis)


SAMPLE_PAGE_GROUP = 16
SMALL_PAGE_GROUP = 16
MERGE_UNROLL = 8

def _heads_first(o, b, n_heads, tn):
    return jnp.swapaxes(o.reshape(b, n_heads, tn, HEAD_DIM), 1, 2).reshape(b, tn, n_heads * HEAD_DIM)


def _diff_sample_body(pt_ref, wq_ref, bias_ref, lam_ref, g_ref, kn_ref, vn_ref, *rest,
                      n_pages, group, page, n_new, n_heads, lam_init):
    k_refs, v_refs = rest[:group], rest[group:2 * group]
    o_ref, m_scr, l_scr, acc_scr = rest[2 * group:]
    p = pl.program_id(1)
    rows_h = 2 * n_new

    @pl.when(p == 0)
    def _():
        _flash_init(m_scr, l_scr, acc_scr)

    def step(ks, vs, bias):
        s = jnp.dot(wq_ref[...], _cat16(ks, 1), preferred_element_type=F32) * DH_A ** -0.5 + bias

        def pv(p16):
            outs = []
            for h in range(n_heads):
                vh = [v[pl.ds(h, page, stride=n_heads), :].astype(BF16) for v in vs]
                vh = vh[0] if len(vh) == 1 else jnp.concatenate(vh, axis=0)
                outs.append(jnp.dot(p16[h * rows_h:(h + 1) * rows_h], vh, preferred_element_type=F32))
            return jnp.concatenate(outs, axis=0)

        _flash_update(s, pv, m_scr, l_scr, acc_scr)

    @pl.when(p < n_pages // group)
    def _():
        tiles = [bias_ref[_past_bias_index(p * group + g, n_pages)] for g in range(group)]
        step(k_refs, v_refs, jnp.concatenate(tiles, axis=1))

    @pl.when(p == n_pages // group)
    def _():
        step([kn_ref], [vn_ref], bias_ref[2])
        lam = _diff_lambda(lam_ref, lam_init)
        on = acc_scr[...] / l_scr[...]
        for h in range(n_heads):
            r0 = h * rows_h
            o = on[r0:r0 + n_new] - lam * on[r0 + n_new:r0 + rows_h]
            o_ref[h * n_new:(h + 1) * n_new, :] = _rms(o, g_ref[...]) * (1.0 - lam_init)


def _diff_sample(pt, layer, q, cache_kt, cache_v, k_new, v_new, tab, lam_vec, g_subln, lam_init):
    b, tn, n_heads = q.shape[:3]
    n_pages = pt.shape[1]
    width, page = cache_kt.shape[2:]
    group = min(SAMPLE_PAGE_GROUP, n_pages)
    assert n_pages % group == 0
    wq = _block_diag_rows(q.reshape(b, tn, 2 * n_heads, DH_A), BF16)
    rows = 2 * n_heads * tn
    bias = _sample_bias(tab, tn, page)
    bias = jnp.broadcast_to(bias[:, :, None], (3, n_heads, 2, tn, page)).reshape(3, rows, page)
    out = _paged_call(
        functools.partial(_diff_sample_body, n_pages=n_pages, group=group, page=page, n_new=tn, n_heads=n_heads,
                          lam_init=lam_init),
        pt, n_pages // group + 1,
        (wq, bias, lam_vec, g_subln.reshape(1, HEAD_DIM), _transposed_page(k_new, page), _rows_page(v_new, page))
        + (cache_kt,) * group + (cache_v,) * group,
        [_per_batch(rows, width), _whole(bias.shape), _whole(lam_vec.shape), _whole((1, HEAD_DIM)),
         _per_batch(width, page), _per_batch(page * n_heads, HEAD_DIM)]
        + _page_specs(layer, n_pages, group, width, page) + _page_specs(layer, n_pages, group, page * n_heads, HEAD_DIM),
        jax.ShapeDtypeStruct((b, n_heads * tn, HEAD_DIM), F32), _per_batch(n_heads * tn, HEAD_DIM),
        _flash_scratch(rows, HEAD_DIM), "diff_sample")
    return _heads_first(out, b, n_heads, tn)


def _fox_sample_body(pt_ref, q_ref, cq_ref, ckp_ref, ckn_ref, hm_ref, nm_ref, kn_ref, vn_ref, *rest,
                     n_pages, group):
    k_refs, v_refs = rest[:group], rest[group:2 * group]
    o_ref, m_scr, l_scr, acc_scr = rest[2 * group:]
    p = pl.program_id(1)

    @pl.when(p == 0)
    def _():
        _flash_init(m_scr, l_scr, acc_scr)

    def step(ks, vs, ck, mask):
        s = lax.dot_general(q_ref[...], _cat16(ks, 0), NT_DIMS, preferred_element_type=F32)
        s = s * HEAD_DIM ** -0.5 + (cq_ref[...] - ck) + mask
        _flash_update(s, lambda p16: jnp.dot(p16, _cat16(vs, 0), preferred_element_type=F32), m_scr, l_scr, acc_scr)

    @pl.when(p < n_pages // group)
    def _():
        step(k_refs, v_refs, ckp_ref[...], hm_ref[...])

    @pl.when(p == n_pages // group)
    def _():
        step([kn_ref], [vn_ref], ckn_ref[...], nm_ref[...])
        o_ref[...] = acc_scr[...] / l_scr[...]


def _fox_sample(pt, layer, q, cache_k, cache_v, k_new, v_new, cum_q, cum_past, cum_new):
    b, tn, n_heads = q.shape[:3]
    n_pages = pt.shape[1]
    prow = cache_k.shape[2]
    page = prow // n_heads
    group = min(SAMPLE_PAGE_GROUP, n_pages)
    assert n_pages % group == 0
    rows = n_heads * tn
    q_rows = jnp.swapaxes(q, 1, 2).reshape(b, rows, HEAD_DIM).astype(BF16)
    cq = jnp.swapaxes(cum_q, 1, 2).reshape(b, rows, 1)
    ck_past = cum_past.reshape(b, n_pages // group, 1, group * prow)
    ck_new = jnp.pad(cum_new, ((0, 0), (0, page - tn), (0, 0))).reshape(b, 1, prow)
    head_mask = _expand_heads(jnp.zeros((n_heads, tn, page), F32))
    qi = np.arange(tn)[:, None]
    ci = np.arange(page)[None, :]
    causal = np.broadcast_to(np.where(ci <= qi, 0.0, NEG_INF).astype(np.float32), (n_heads, tn, page))
    new_mask = _expand_heads(jnp.asarray(causal))
    out = _paged_call(
        functools.partial(_fox_sample_body, n_pages=n_pages, group=group),
        pt, n_pages // group + 1,
        (q_rows, cq, ck_past, ck_new, jnp.tile(head_mask, (1, group)), new_mask,
         _rows_page(k_new, page), _rows_page(v_new, page)) + (cache_k,) * group + (cache_v,) * group,
        [_per_batch(rows, HEAD_DIM), _per_batch(rows, 1),
         pl.BlockSpec((None, None, 1, group * prow),
                      lambda bi, p, pt_: (bi, jnp.minimum(p, n_pages // group - 1), 0, 0)),
         _per_batch(1, prow), _whole((rows, group * prow)), _whole((rows, prow)),
         _per_batch(prow, HEAD_DIM), _per_batch(prow, HEAD_DIM)]
        + _page_specs(layer, n_pages, group, prow, HEAD_DIM) + _page_specs(layer, n_pages, group, prow, HEAD_DIM),
        jax.ShapeDtypeStruct((b, rows, HEAD_DIM), F32), _per_batch(rows, HEAD_DIM),
        _flash_scratch(rows, HEAD_DIM), "fox_sample")
    return _heads_first(out, b, n_heads, tn)


def _moba_blocks_body(pt_ref, q_ref, bias_ref, *rest, n_pages, group, pages_per_block, n_heads):
    k_refs, v_refs = rest[:group], rest[group:2 * group]
    acc_ref, stat_ref, kmean_ref = rest[2 * group:]
    p = pl.program_id(1)
    lane = lax.broadcasted_iota(jnp.int32, stat_ref.shape[1:], 1)
    for blk in range(group // pages_per_block):
        pages = range(blk * pages_per_block, (blk + 1) * pages_per_block)
        parts, total = [], None
        for g in pages:
            k = k_refs[g][...]
            s = lax.dot_general(q_ref[...], k.astype(BF16), NT_DIMS, preferred_element_type=F32) * HEAD_DIM ** -0.5
            parts.append(s + bias_ref[_past_bias_index(p * group + g, n_pages)])
            ksum = jnp.sum(k.reshape(k.shape[0] // n_heads, n_heads, HEAD_DIM), axis=0)
            total = ksum if total is None else total + ksum
        s = parts[0] if len(parts) == 1 else jnp.concatenate(parts, axis=1)
        m = jnp.max(s, axis=-1, keepdims=True)
        prob = jnp.exp(s - m)
        acc_ref[blk] = jnp.dot(prob.astype(BF16), _cat16([v_refs[g] for g in pages], 0), preferred_element_type=F32)
        stat_ref[blk] = jnp.where(lane < LANES // 2, m, jnp.sum(prob, axis=-1, keepdims=True))
        kmean_ref[blk] = total * (1.0 / MOBA_BLOCK)


def _moba_merge_body(wq_ref, kmean_ref, acc_ref, stat_ref, q_ref, kn_ref, vn_ref, bias_ref, o_ref, *, n_blk):
    gate = lax.dot_general(wq_ref[...], kmean_ref[...], NT_DIMS, preferred_element_type=F32,
                           precision=lax.Precision.HIGHEST)
    sel = _top_blocks_negmask(gate, min(MOBA_TOPK, n_blk), float(n_blk))
    lane = lax.broadcasted_iota(jnp.int32, sel.shape, 1)
    s = lax.dot_general(q_ref[...], kn_ref[...].astype(BF16), NT_DIMS, preferred_element_type=F32)
    s = s * HEAD_DIM ** -0.5 + bias_ref[...]
    m_own = jnp.max(s, axis=-1, keepdims=True)
    prob = jnp.exp(s - m_own)

    def block_max(blk):
        return stat_ref[blk][:, 0:1] + jnp.min(jnp.where(lane == blk, sel, 0.0), axis=-1, keepdims=True)

    unroll = math.gcd(n_blk, MERGE_UNROLL)

    def max_step(g, m):
        return functools.reduce(jnp.maximum, [block_max(g * unroll + u) for u in range(unroll)], m)

    m_all = lax.fori_loop(0, n_blk // unroll, max_step, m_own)

    def merge(g, carry):
        num, den = carry
        for u in range(unroll):
            blk = g * unroll + u
            w = jnp.exp(block_max(blk) - m_all)
            num = num + w * acc_ref[blk]
            den = den + w * stat_ref[blk][:, LANES // 2:LANES // 2 + 1]
        return num, den

    w_own = jnp.exp(m_own - m_all)
    num, den = lax.fori_loop(
        0, n_blk // unroll, merge,
        (w_own * jnp.dot(prob.astype(BF16), vn_ref[...].astype(BF16), preferred_element_type=F32),
         w_own * jnp.sum(prob, axis=-1, keepdims=True)))
    o_ref[...] = num / den


def _moba_sample(pt, layer, q, cache_k, cache_v, k_new, v_new, tab):
    b, tn, n_heads = q.shape[:3]
    n_pages = pt.shape[1]
    prow = cache_k.shape[2]
    page = prow // n_heads
    width = n_heads * HEAD_DIM
    assert MOBA_BLOCK % page == 0 and (n_pages * page) % MOBA_BLOCK == 0 and tn < MOBA_BLOCK
    ppb = MOBA_BLOCK // page
    n_blk = n_pages // ppb
    group = min(SAMPLE_PAGE_GROUP, n_pages)
    assert n_pages % group == 0 and group % ppb == 0
    bps = group // ppb
    rows = n_heads * tn
    bias = _sample_bias(tab, tn, page)
    bias = jnp.stack([_expand_heads(bias[i]) for i in range(3)])
    q_rows = jnp.swapaxes(q, 1, 2).reshape(b, rows, HEAD_DIM).astype(BF16)
    per_block = lambda r: pl.BlockSpec((None, bps, r, HEAD_DIM), lambda bi, p, pt_: (bi, p, 0, 0))
    acc, stat, kmean = _paged_call(
        functools.partial(_moba_blocks_body, n_pages=n_pages, group=group, pages_per_block=ppb, n_heads=n_heads),
        pt, n_pages // group,
        (q_rows, bias[:2]) + (cache_k,) * group + (cache_v,) * group,
        [_per_batch(rows, HEAD_DIM), _whole((2,) + bias.shape[1:])]
        + _page_specs(layer, n_pages, group, prow, HEAD_DIM) + _page_specs(layer, n_pages, group, prow, HEAD_DIM),
        [jax.ShapeDtypeStruct((b, n_blk, rows, HEAD_DIM), F32), jax.ShapeDtypeStruct((b, n_blk, rows, LANES), F32),
         jax.ShapeDtypeStruct((b, n_blk, n_heads, HEAD_DIM), F32)],
        [per_block(rows), per_block(rows), per_block(n_heads)], [], "moba_blocks")
    whole_b = lambda *shape: pl.BlockSpec((None,) + shape, lambda bi: (bi,) + (0,) * len(shape))
    out = pl.pallas_call(
        functools.partial(_moba_merge_body, n_blk=n_blk),
        grid=(b,),
        in_specs=[whole_b(rows, width), whole_b(n_blk, width), whole_b(n_blk, rows, HEAD_DIM),
                  whole_b(n_blk, rows, LANES), whole_b(rows, HEAD_DIM), whole_b(prow, HEAD_DIM),
                  whole_b(prow, HEAD_DIM), pl.BlockSpec((rows, prow), lambda bi: (0, 0))],
        out_specs=whole_b(rows, HEAD_DIM),
        out_shape=jax.ShapeDtypeStruct((b, rows, HEAD_DIM), F32),
        compiler_params=_params("parallel"),
        name="moba_merge",
    )(_block_diag_rows(q, F32), kmean.reshape(b, n_blk, width), acc, stat, q_rows,
      _rows_page(k_new, page), _rows_page(v_new, page), bias[2])
    return _heads_first(out, b, n_heads, tn)


def _dsa_score_body(pt_ref, iq_ref, w_ref, mask_ref, kn_ref, *rest, n_pages, group, n_new):
    k_refs, o_ref = rest[:group], rest[group]
    p = pl.program_id(1)

    def score(kt16):
        rel = jnp.dot(iq_ref[...], kt16, preferred_element_type=F32)
        rel = jnp.maximum(rel * IDX_DIM ** -0.5, 0.0) * w_ref[...]
        return jnp.sum(rel.reshape(N_IDX_HEADS, n_new, rel.shape[1]), axis=0)

    @pl.when(p < n_pages // group)
    def _():
        o_ref[...] = score(_cat16(k_refs, 1))

    @pl.when(p == n_pages // group)
    def _():
        page = kn_ref.shape[1]
        o_ref[...] = jnp.full(o_ref.shape, NEG_INF, F32)
        o_ref[:, :page] = score(kn_ref[...].astype(BF16)) + mask_ref[...]


def _dsa_select_body(sc_ref, o_ref, key_scr, *, n_new, n_sel, chunk):
    width = sc_ref.shape[1]
    key_scr[...] = _order_key(sc_ref[...])

    def count(pred):
        return jnp.sum(jnp.where(pred(key_scr[...]), 1.0, 0.0), axis=-1, keepdims=True)

    thr = _kth_largest_key(lambda ts: [count(lambda k, t=t: k >= t) for t in ts], n_new, n_sel)
    cnt_ge = count(lambda k: k >= thr)
    tie = jnp.max(jnp.where((cnt_ge > n_sel) & (thr > KEY_OF_NEG_INF), 1.0, 0.0)) > 0.0

    @pl.when(jnp.logical_not(tie))
    def _():
        o_ref[...] = jnp.where(key_scr[...] >= thr, 0.0, NEG_INF)

    @pl.when(tie)
    def _():
        allow = n_sel - count(lambda k: k > thr)
        r = lax.broadcasted_iota(jnp.int32, (chunk, chunk), 0)
        c = lax.broadcasted_iota(jnp.int32, (chunk, chunk), 1)
        tri = jnp.where(r <= c, 1.0, 0.0).astype(BF16)

        def body(j, before):
            cols = pl.ds(pl.multiple_of(j * chunk, chunk), chunk)
            k = key_scr[:, cols]
            eq = jnp.where(k == thr, 1.0, 0.0)
            rank = jnp.dot(eq.astype(BF16), tri, preferred_element_type=F32) + before
            keep = jnp.where(k > thr, 1.0, jnp.where(rank <= allow, eq, 0.0))
            o_ref[:, cols] = jnp.where(keep > 0.0, 0.0, NEG_INF)
            return before + jnp.sum(eq, axis=-1, keepdims=True)
        lax.fori_loop(0, width // chunk, body, jnp.zeros((n_new, 1), F32))


def _dsa_sample_body(pt_ref, q_ref, bias_ref, nm_ref, kn_ref, vn_ref, *rest, n_pages, group, n_heads):
    k_refs, v_refs = rest[:group], rest[group:2 * group]
    o_ref, m_scr, l_scr, acc_scr = rest[2 * group:]
    p = pl.program_id(1)

    @pl.when(p == 0)
    def _():
        _flash_init(m_scr, l_scr, acc_scr)

    def step(ks, vs, bias, nm):
        s = lax.dot_general(q_ref[...], _cat16(ks, 0), NT_DIMS, preferred_element_type=F32)
        s = s * HEAD_DIM ** -0.5 + bias + jnp.tile(nm, (n_heads, 1))
        _flash_update(s, lambda p16: jnp.dot(p16, _cat16(vs, 0), preferred_element_type=F32), m_scr, l_scr, acc_scr)

    @pl.when(p < n_pages // group)
    def _():
        tiles = [bias_ref[_past_bias_index(p * group + g, n_pages)] for g in range(group)]
        step(k_refs, v_refs, jnp.concatenate(tiles, axis=1), nm_ref[...])

    @pl.when(p == n_pages // group)
    def _():
        page = kn_ref.shape[0]
        step([kn_ref], [vn_ref], bias_ref[2], nm_ref[:, :page])
        o_ref[...] = acc_scr[...] / l_scr[...]


def _dsa_sample(pt, layer, qb, iq, iw, cache_k, cache_v, cache_idx_t, k_new, v_new, ik_new, tab):
    b, tn, n_heads = qb.shape[:3]
    n_pages = pt.shape[1]
    page = cache_k.shape[2]
    n_sel = min(DSA_TOPK, (n_pages * page + tn) // 4)
    group = min(SMALL_PAGE_GROUP, n_pages)
    assert n_pages % group == 0
    n_steps = n_pages // group + 1
    width = n_steps * group * page
    pad = ((0, 0), (0, page - tn), (0, 0))
    qi = np.arange(tn)[:, None]
    ci = np.arange(page)[None, :]
    new_mask = jnp.asarray(np.where(ci <= qi, 0.0, NEG_INF).astype(np.float32))
    n_iq = N_IDX_HEADS * tn
    iq_rows = jnp.swapaxes(iq, 1, 2).reshape(b, n_iq, IDX_DIM).astype(BF16)
    w_rows = (jnp.swapaxes(iw, 1, 2).astype(F32) * N_IDX_HEADS ** -0.5).reshape(b, n_iq, 1)
    step_cols = pl.BlockSpec((None, tn, group * page), lambda bi, p, pt_: (bi, 0, p))
    scores = _paged_call(
        functools.partial(_dsa_score_body, n_pages=n_pages, group=group, n_new=tn),
        pt, n_steps,
        (iq_rows, w_rows, new_mask, _transposed_page(ik_new, page)) + (cache_idx_t,) * group,
        [_per_batch(n_iq, IDX_DIM), _per_batch(n_iq, 1), _whole(new_mask.shape), _per_batch(IDX_DIM, page)]
        + _page_specs(layer, n_pages, group, IDX_DIM, page),
        jax.ShapeDtypeStruct((b, tn, width), F32), step_cols, [], "dsa_score")
    negmask = pl.pallas_call(
        functools.partial(_dsa_select_body, n_new=b * tn, n_sel=n_sel, chunk=page),
        grid=(1,),
        in_specs=[pl.BlockSpec((b * tn, width), lambda i: (0, 0))],
        out_specs=pl.BlockSpec((b * tn, width), lambda i: (0, 0)),
        out_shape=jax.ShapeDtypeStruct((b * tn, width), F32),
        scratch_shapes=[pltpu.VMEM((b * tn, width), jnp.int32)],
        compiler_params=_params("arbitrary"),
        name="dsa_select",
    )(scores.reshape(b * tn, width)).reshape(b, tn, width)
    rows = n_heads * tn
    q_rows = jnp.swapaxes(qb, 1, 2).reshape(b, rows, HEAD_DIM).astype(BF16)
    bias = _sample_bias(tab, tn, page).reshape(3, rows, page)
    out = _paged_call(
        functools.partial(_dsa_sample_body, n_pages=n_pages, group=group, n_heads=n_heads),
        pt, n_steps,
        (q_rows, bias, negmask, jnp.pad(k_new, pad), jnp.pad(v_new, pad)) + (cache_k,) * group + (cache_v,) * group,
        [_per_batch(rows, HEAD_DIM), _whole(bias.shape), step_cols, _per_batch(page, HEAD_DIM),
         _per_batch(page, HEAD_DIM)]
        + _page_specs(layer, n_pages, group, page, HEAD_DIM) + _page_specs(layer, n_pages, group, page, HEAD_DIM),
        jax.ShapeDtypeStruct((b, rows, HEAD_DIM), F32), _per_batch(rows, HEAD_DIM),
        _flash_scratch(rows, HEAD_DIM), "dsa_sample")
    return _heads_first(out, b, n_heads, tn)


def _pad_rows(w, n):
    return jnp.pad(w, ((0, 0),) * (w.ndim - 2) + ((0, n - w.shape[-2]), (0, 0)))


def _mixer_even(h_in, g, w_in16, past, pt, layer_e, lam_vec, g_subln, lam_init, tab, b, t):
    d = h_in.shape[1]
    n_a = n_b = (d // HEAD_DIM) // 2
    wa, wb = n_a * HEAD_DIM, n_b * HEAD_DIM
    wi = N_IDX_HEADS * IDX_DIM
    c_qa, c_ka, c_va, c_qb = 0, wa, 2 * wa, 3 * wa
    c_iq = c_qb + wb
    c_small = c_iq + wi
    c_kb, c_vb, c_ik = c_small, c_small + HEAD_DIM, c_small + 2 * HEAD_DIM
    tn = COL_TILE
    assert wa % tn == 0 and c_small % tn == 0 and w_in16.shape[-2] == c_small + tn
    ka_t = past is None and t % min(b * t, 2 * ROW_TILE) == 0
    ka, va, small, p16 = _rms_matmul(
        h_in, g, w_in16, (layer_e,), emit16=True, w_transposed=True, rows_per_batch=t,
        f32_groups=((c_ka // tn, wa // tn, ka_t), (c_va // tn, wa // tn), (c_small // tn, 1)))
    p16 = p16.reshape(b, t, -1)
    va, small = va.reshape(b, t, wa), small.reshape(b, t, tn)
    if ka_t:
        a_k = jnp.transpose(ka.reshape(b, n_a, 2, DH_A, t), (0, 4, 1, 2, 3))
    else:
        ka = ka.reshape(b, t, wa)
        a_k = ka.reshape(b, t, n_a, 2, DH_A)
    kb = small[..., :HEAD_DIM]
    vb = small[..., HEAD_DIM:2 * HEAD_DIM]
    ik = small[..., 2 * HEAD_DIM:2 * HEAD_DIM + IDX_DIM]
    rows = (a_k, va.reshape(b, t, n_a, HEAD_DIM), kb, vb, ik)
    tab_a, tab_b = tab[:, :n_a], tab[:, n_a:]
    if past is None:
        o_a = _diff_prompt(p16, c_qa, c_ka, c_va, n_a, tab_a, lam_vec, g_subln, lam_init)
        iq_t = jnp.swapaxes(p16[..., c_iq:c_small].reshape(b, t, N_IDX_HEADS, IDX_DIM), 1, 2)
        o_b = _dsa_prompt(p16, small, iq_t, c_qb, c_kb, c_vb, c_ik, 2 * HEAD_DIM, n_b, tab_b,
                          min(DSA_TOPK, t // 4))
    else:
        cache_a_k, cache_a_v, cache_b_k, cache_b_v, cache_b_idx = past
        qa = p16[..., c_qa:c_qa + wa].reshape(b, t, n_a, 2, DH_A)
        o_a = _diff_sample(pt, layer_e, qa, cache_a_k, cache_a_v, ka, va, tab_a, lam_vec, g_subln, lam_init)
        qb = p16[..., c_qb:c_qb + wb].reshape(b, t, n_b, HEAD_DIM)
        iq = p16[..., c_iq:c_small].reshape(b, t, N_IDX_HEADS, IDX_DIM)
        iw = small[..., 2 * HEAD_DIM + IDX_DIM:2 * HEAD_DIM + IDX_DIM + N_IDX_HEADS]
        o_b = _dsa_sample(pt, layer_e, qb, iq, iw, cache_b_k, cache_b_v, cache_b_idx, kb, vb, ik, tab_b)
    return (o_a.reshape(b * t, wa), o_b.reshape(b * t, wb)), rows


def _mixer_odd(h_in, g, w_in16, past, pt, layer_o, b_forget, tab, b, t):
    d = h_in.shape[1]
    n_c = n_d = (d // HEAD_DIM) // 2
    wc, wd = n_c * HEAD_DIM, n_d * HEAD_DIM
    c_qc, c_kc, c_vc, c_qd = 0, wc, 2 * wc, 3 * wc
    c_kd = c_qd + wd
    c_vd = c_kd + wd
    c_fc = c_vd + wd
    tn = COL_TILE
    assert wc % tn == 0 and wd % tn == 0 and w_in16.shape[-2] == c_fc + tn
    groups = tuple((c // tn, wc // tn) for c in (c_kc, c_vc, c_qd, c_kd, c_vd)) + ((c_fc // tn, 1),)
    kc, vc, qd32, kd, vd, small, p16 = _rms_matmul(h_in, g, w_in16, (layer_o,), emit16=True, w_transposed=True,
                                                   f32_groups=groups)
    p16 = p16.reshape(b, t, -1)
    kc, vc, qd32, kd, vd = (a.reshape(b, t, wc) for a in (kc, vc, qd32, kd, vd))
    log_f = jax.nn.log_sigmoid(small.reshape(b, t, tn)[..., :n_c] + b_forget.astype(F32))
    rows = (kc.reshape(b, t, n_c, HEAD_DIM), vc.reshape(b, t, n_c, HEAD_DIM), log_f,
            kd.reshape(b, t, n_d, HEAD_DIM), vd.reshape(b, t, n_d, HEAD_DIM))
    tab_d = tab[:, n_c:]
    if past is None:
        o_c = _fox_prompt(p16, c_qc, c_kc, c_vc, n_c, jnp.cumsum(log_f, axis=1))
        o_d = _moba_prompt(p16, qd32, kd, c_qd, c_kd, c_vd, n_d, tab_d)
    else:
        cache_c_k, cache_c_v, cache_c_logf, cache_d_k, cache_d_v = past
        n_pages = pt.shape[1]
        page = cache_c_logf.shape[2]
        logf_past = cache_c_logf[layer_o][pt].reshape(b, n_pages * page, n_c)
        cum = jnp.cumsum(jnp.concatenate([logf_past, log_f], axis=1).astype(F32), axis=1)
        cum_q = cum[:, n_pages * page:]
        qc = p16[..., c_qc:c_qc + wc].reshape(b, t, n_c, HEAD_DIM)
        o_c = _fox_sample(pt, layer_o, qc, cache_c_k, cache_c_v, kc, vc, cum_q, cum[:, :n_pages * page], cum_q)
        o_d = _moba_sample(pt, layer_o, qd32.reshape(b, t, n_d, HEAD_DIM), cache_d_k, cache_d_v, kd, vd, tab_d)
    return (o_c.reshape(b * t, wc), o_d.reshape(b * t, wd)), rows


def _run_trunk(x, past_even, past_odd, pt, mem_kv, prm):
    b, t, d = x.shape
    x = x.reshape(b * t, d)
    depth = prm['norm_g'].shape[0]
    rows_even, rows_odd = [], []
    for layer in range(depth):
        g = prm['norm_g'][layer]
        wg, wu, wd = prm['w_ffn_gate'], prm['w_ffn_up'], prm['w_ffn_down']
        x = _ffn(x, g[NG_FFN1_PRE], g[NG_FFN1_POST], wg, wu, wd, (layer, 0))
        if layer % 2 == 0:
            e = layer // 2
            lam_init = 0.8 - 0.6 * math.exp(-0.3 * layer)
            parts, rows = _mixer_even(x, g[NG_MIX_PRE], prm['w_in_even'], past_even, pt, e,
                                      prm['diff_lambda'][e].astype(F32), prm['g_subln'][e].astype(F32), lam_init,
                                      prm['t5_table'], b, t)
            rows_even.append(rows)
            x = _out_proj(x, g[NG_MIX_POST], parts, prm['w_out_even'], (e,))
        else:
            o = layer // 2
            parts, rows = _mixer_odd(x, g[NG_MIX_PRE], prm['w_in_odd'], past_odd, pt, o,
                                     prm['b_forget'][o], prm['t5_table'], b, t)
            rows_odd.append(rows)
            x = _out_proj(x, g[NG_MIX_POST], parts, prm['w_out_odd'], (o,))
        mk, mv = mem_kv[layer]
        q = _rms_matmul(x, g[NG_X_PRE], prm['w_xq'], (layer,))
        o_x = _cross_attend(q.reshape(b, t, -1), mk, mv)
        x = _out_proj(x, g[NG_X_POST], [o_x.reshape(b * t, -1)], prm['w_xo'], (layer,))
        x = _ffn(x, g[NG_FFN2_PRE], g[NG_FFN2_POST], wg, wu, wd, (layer, 1))
    return x.reshape(b, t, d), rows_even, rows_odd


def kernel(x_prompt, x_sample, cache_a_k, cache_a_v, cache_b_k, cache_b_v, cache_b_idx, cache_c_k, cache_c_v, cache_c_logf, cache_d_k, cache_d_v, cache_mem_k, cache_mem_v, page_table, mem_prompt, t5_table, norm_g, w_ffn_gate, w_ffn_up, w_ffn_down, w_xq, w_xk, w_xv, w_xo, w_in_even, w_out_even, diff_lambda, g_subln, w_in_odd, w_out_odd, b_forget):
    depth = norm_g.shape[0]
    d_model = x_prompt.shape[-1]
    n_c = (d_model // HEAD_DIM) // 2
    half = n_c * HEAD_DIM
    wi = N_IDX_HEADS * IDX_DIM
    e_cut = 4 * half
    w_e = jnp.swapaxes(w_in_even, 1, 2)
    w_in_even_r = jnp.concatenate([w_e[:, :e_cut], w_e[:, e_cut + 2 * HEAD_DIM:e_cut + 2 * HEAD_DIM + wi],
                                   w_e[:, e_cut:e_cut + 2 * HEAD_DIM], w_e[:, e_cut + 2 * HEAD_DIM + wi:]], axis=1)
    c_fc = 3 * half
    w_o = jnp.swapaxes(w_in_odd, 1, 2)
    w_in_odd_r = jnp.concatenate([w_o[:, :c_fc], w_o[:, c_fc + n_c:], w_o[:, c_fc:c_fc + n_c]], axis=1)
    prm = {
        't5_table': t5_table.astype(F32), 'norm_g': norm_g.astype(F32),
        'w_ffn_gate': w_ffn_gate.astype(BF16), 'w_ffn_up': w_ffn_up.astype(BF16), 'w_ffn_down': w_ffn_down.astype(BF16),
        'w_xq': w_xq.astype(BF16), 'w_xo': w_xo.astype(BF16),
        'w_in_even': _pad_rows(w_in_even_r, e_cut + wi + COL_TILE).astype(BF16),
        'w_out_even': w_out_even.astype(BF16),
        'w_in_odd': _pad_rows(w_in_odd_r, 6 * half + COL_TILE).astype(BF16),
        'w_out_odd': w_out_odd.astype(BF16),
        'diff_lambda': diff_lambda, 'g_subln': g_subln, 'b_forget': b_forget,
    }
    b_p, n_mem, _ = mem_prompt.shape
    hx_w = w_xk.shape[-1]

    mem_kv_p, mem_k_out, mem_v_out = [], [], []
    for l in range(depth):
        w_kv = jnp.concatenate([w_xk[l], w_xv[l]], axis=-1).astype(BF16)
        kv = _rms_matmul(mem_prompt.reshape(b_p * n_mem, d_model), norm_g[l, NG_MEM].astype(F32), w_kv)
        mk = kv[:, :hx_w].reshape(b_p, n_mem, hx_w)
        mv = kv[:, hx_w:].reshape(b_p, n_mem, hx_w)
        mem_kv_p.append((mk, mv))
        mem_k_out.append(mk.reshape(b_p, n_mem, hx_w // HEAD_DIM, HEAD_DIM))
        mem_v_out.append(mv.reshape(b_p, n_mem, hx_w // HEAD_DIM, HEAD_DIM))
    y_prompt, ev_p, od_p = _run_trunk(x_prompt.astype(F32), None, None, None, mem_kv_p, prm)

    def rows_pages(c):
        return c.reshape(c.shape[:2] + (c.shape[2] * c.shape[3], c.shape[4]))

    def transposed_pages(c):
        c = c.reshape(c.shape[:3] + (-1,))
        return jnp.swapaxes(c, 2, 3)

    past_even = (transposed_pages(cache_a_k), rows_pages(cache_a_v), cache_b_k, cache_b_v,
                 transposed_pages(cache_b_idx))
    past_odd = (rows_pages(cache_c_k), rows_pages(cache_c_v), cache_c_logf, rows_pages(cache_d_k),
                rows_pages(cache_d_v))
    b_s = x_sample.shape[0]
    mem_kv_s = [(cache_mem_k[l].reshape(b_s, n_mem, hx_w), cache_mem_v[l].reshape(b_s, n_mem, hx_w))
                for l in range(depth)]
    y_sample, ev_s, od_s = _run_trunk(x_sample.astype(F32), past_even, past_odd, page_table.astype(jnp.int32),
                                      mem_kv_s, prm)

    def stack(rows, i):
        return jnp.stack([r[i] for r in rows])

    out = [y_prompt, y_sample]
    out += [stack(ev_p, i) for i in range(5)] + [stack(od_p, i) for i in range(5)]
    out += [jnp.stack(mem_k_out), jnp.stack(mem_v_out)]
    out += [stack(ev_s, i) for i in range(5)] + [stack(od_s, i) for i in range(5)]
    return tuple(out)
```
